```python
import math
import jax, jax.numpy as jnp
from jax import lax
import numpy as np

D_MODEL = 1024
BATCH = 8
SEQ = 2048
DEPTH = 4
DEC_BATCH = 128
DEC_SEQ = 1
PAST_LEN = 8192
PAGE_SIZE = 128

N_META = 16
N_EVEN = (DEPTH + 1) // 2
N_ODD = DEPTH // 2
CHUNK = 128
RET_HEADS = 4
RET_DK = 64
RET_DV = 128
S5_WIDTH = 512
S5_GROUP = 16
S5_GROUPS = S5_WIDTH // S5_GROUP
S5_STATE = 64
RWKV_HEAD = 64
RWKV_WIDTH = 512
RWKV_HEADS = RWKV_WIDTH // RWKV_HEAD
LORA_W = 64
LORA_A = 64
LORA_G = 128
RWKV_PROJ = 3 * RWKV_WIDTH + LORA_W + LORA_A + LORA_G
RWKV_GN_EPS = 64e-5
MLA_HEADS = 8
Q_LORA = 384
KV_LORA = 256
QK_NOPE = 64
QK_ROPE = 32
V_HEAD = 64
MLA_WIDTH = MLA_HEADS * V_HEAD
MLA_SCALE = (QK_NOPE + QK_ROPE) ** -0.5
ROPE_THETA = 10000.0
N_EGROUPS = 4
E_PER_GROUP = 4
N_EXPERTS = N_EGROUPS * E_PER_GROUP
FINE_TOPK = 2
D_EXPERT = 256
DN_ALPHA = (2 * DEPTH) ** 0.25
DN_BETA = (8 * DEPTH) ** -0.25
EVEN_IN = 2 * RET_HEADS * RET_DK + 2 * RET_HEADS * RET_DV + S5_WIDTH
EVEN_OUT = RET_HEADS * RET_DV + S5_WIDTH
ODD_IN = RWKV_PROJ + Q_LORA + KV_LORA + QK_ROPE
ODD_OUT = RWKV_WIDTH + MLA_WIDTH

kernel_name = 'hybrid_retention_s5_rwkv7_mla_hmoe_step'


def layer_norm(x, g, b, eps=1e-5):
    xf = x.astype(jnp.float32)
    mu = jnp.mean(xf, -1, keepdims=True)
    var = jnp.mean(jnp.square(xf - mu), -1, keepdims=True)
    return ((xf - mu) * lax.rsqrt(var + eps)).astype(x.dtype) * g + b


def rms_norm(x, g, eps=1e-6):
    xf = x.astype(jnp.float32)
    return (xf * lax.rsqrt(jnp.mean(jnp.square(xf), -1, keepdims=True) + eps)).astype(x.dtype) * g


def head_norm(x, g, b, eps):
    xf = x.astype(jnp.float32)
    mu = jnp.mean(xf, -1, keepdims=True)
    var = jnp.mean(jnp.square(xf - mu), -1, keepdims=True)
    y = ((xf - mu) * lax.rsqrt(var + eps)).astype(g.dtype)
    return y.reshape(*x.shape[:-2], -1) * g + b


def rope(x, pos):
    d = x.shape[-1]
    inv = ROPE_THETA ** (-jnp.arange(0, d, 2, dtype=jnp.float32) / d)
    ang = pos.astype(jnp.float32)[:, None] * inv[None, :]
    shape = (ang.shape[0],) + (1,) * (x.ndim - 3) + (d // 2,)
    cos = jnp.cos(ang).reshape(shape)
    sin = jnp.sin(ang).reshape(shape)
    x1, x2 = x[..., : d // 2], x[..., d // 2:]
    return jnp.concatenate([x1 * cos - x2 * sin, x1 * sin + x2 * cos], -1).astype(x.dtype)


def retention_chunk(S, q, k, v, log_g):
    L = q.shape[1]
    idx = jnp.arange(L, dtype=jnp.float32)
    diff = idx[:, None] - idx[None, :]
    dmask = jnp.where(diff >= 0, jnp.exp(log_g[:, None, None] * jnp.maximum(diff, 0.0)), 0.0)
    scores = jnp.einsum('bihd,bjhd->bhij', q, k) * dmask
    inner = jnp.einsum('bhij,bjhv->bihv', scores, v)
    cross = jnp.einsum('bihd,bhdv->bihv', q, S) * jnp.exp(log_g[None, None, :, None] * (idx + 1.0)[None, :, None, None])
    kdec = k * jnp.exp(log_g[None, None, :, None] * (L - 1.0 - idx)[None, :, None, None])
    S_new = jnp.exp(log_g * L)[None, :, None, None] * S + jnp.einsum('bjhd,bjhv->bhdv', kdec, v)
    return S_new, inner + cross


def retention(q, k, v, S0, n_lead):
    log_g = jnp.log1p(-jnp.exp2(-5.0 - jnp.arange(RET_HEADS, dtype=jnp.float32)))
    S, o_lead = retention_chunk(S0.astype(jnp.float32), q[:, :n_lead], k[:, :n_lead], v[:, :n_lead], log_g)
    B, T = q.shape[:2]
    rest = T - n_lead
    if rest == 0:
        return o_lead, S

    def to_chunks(t):
        return jnp.moveaxis(t[:, n_lead:].reshape(B, rest // CHUNK, CHUNK, *t.shape[2:]), 1, 0)

    S, o = lax.scan(lambda s, qkv: retention_chunk(s, qkv[0], qkv[1], qkv[2], log_g), S,
                    (to_chunks(q), to_chunks(k), to_chunks(v)))
    o = jnp.moveaxis(o, 0, 1).reshape(B, rest, RET_HEADS, RET_DV)
    return jnp.concatenate([o_lead.astype(o.dtype), o], 1), S


def s5_layer(u, h_re0, h_im0, P, i):
    B, T, _ = u.shape
    f32 = jnp.float32
    ug = u.reshape(B, T, S5_GROUPS, S5_GROUP).astype(f32)
    a_re = P['s5_a_re'][i].astype(f32)
    a_im = P['s5_a_im'][i].astype(f32)
    dt = jnp.exp(P['s5_log_dt'][i].astype(f32))[:, None]
    mag = jnp.exp(dt * a_re)
    ab_re = mag * jnp.cos(dt * a_im)
    ab_im = mag * jnp.sin(dt * a_im)
    den = a_re * a_re + a_im * a_im
    n_re = ab_re - 1.0
    f_re = (n_re * a_re + ab_im * a_im) / den
    f_im = (ab_im * a_re - n_re * a_im) / den
    b_re = P['s5_b_re'][i].astype(f32)
    b_im = P['s5_b_im'][i].astype(f32)
    bb_re = f_re[..., None] * b_re - f_im[..., None] * b_im
    bb_im = f_re[..., None] * b_im + f_im[..., None] * b_re
    bu_re = jnp.einsum('btgc,gpc->btgp', ug, bb_re)
    bu_im = jnp.einsum('btgc,gpc->btgp', ug, bb_im)
    h_re0 = h_re0.astype(f32)
    h_im0 = h_im0.astype(f32)
    bu_re = bu_re.at[:, 0].add(ab_re * h_re0 - ab_im * h_im0)
    bu_im = bu_im.at[:, 0].add(ab_re * h_im0 + ab_im * h_re0)
    A_re = jnp.broadcast_to(ab_re, bu_re.shape)
    A_im = jnp.broadcast_to(ab_im, bu_im.shape)

    def combine(e1, e2):
        a1r, a1i, b1r, b1i = e1
        a2r, a2i, b2r, b2i = e2
        return (a1r * a2r - a1i * a2i, a1r * a2i + a1i * a2r,
                a2r * b1r - a2i * b1i + b2r, a2r * b1i + a2i * b1r + b2i)

    _, _, h_re, h_im = lax.associative_scan(combine, (A_re, A_im, bu_re, bu_im), axis=1)
    y = (jnp.einsum('btgp,gcp->btgc', h_re, P['s5_c_re'][i]) - jnp.einsum('btgp,gcp->btgc', h_im, P['s5_c_im'][i])
         + P['s5_d'][i].reshape(S5_GROUPS, S5_GROUP) * ug).reshape(B, T, S5_WIDTH)
    z = jax.nn.gelu(y)
    out = z * jax.nn.sigmoid(z @ P['s5_w_glu'][i] + P['s5_b_glu'][i])
    return out, h_re[:, -1], h_im[:, -1]


def even_mixer(x, pos, n_lead, S0, h_re0, h_im0, P, i):
    B, T, _ = x.shape
    p = x @ P['ev_w_in'][i]
    o1 = RET_HEADS * RET_DK
    o2 = 2 * o1
    o3 = o2 + RET_HEADS * RET_DV
    o4 = o3 + RET_HEADS * RET_DV
    q = rope(p[..., :o1].reshape(B, T, RET_HEADS, RET_DK), pos) * RET_DK ** -0.5
    k = rope(p[..., o1:o2].reshape(B, T, RET_HEADS, RET_DK), pos)
    v = p[..., o2:o3].reshape(B, T, RET_HEADS, RET_DV)
    gate = p[..., o3:o4]
    u = p[..., o4:]
    o, S = retention(q, k, v, S0, n_lead)
    ret = jax.nn.silu(gate) * head_norm(o, P['ret_gn_g'][i], P['ret_gn_b'][i], 1e-5)
    s5_out, h_re, h_im = s5_layer(u, h_re0, h_im0, P, i)
    mix = jnp.concatenate([ret.astype(x.dtype), s5_out.astype(x.dtype)], -1) @ P['ev_w_out'][i]
    return mix, S, h_re, h_im


def wkv7(r, w, k, v, a, b, S0):
    f32 = jnp.float32

    def step(S, inp):
        r_t, w_t, k_t, v_t, a_t, b_t = inp
        sa = jnp.einsum('bhij,bhj->bhi', S, a_t)
        S = S * w_t[:, :, None, :] + sa[..., None] * b_t[:, :, None, :] + v_t[..., None] * k_t[:, :, None, :]
        return S, jnp.einsum('bhij,bhj->bhi', S, r_t)

    xs = tuple(jnp.moveaxis(t.astype(f32), 1, 0) for t in (r, w, k, v, a, b))
    S, ys = lax.scan(step, S0.astype(f32), xs)
    return jnp.moveaxis(ys, 0, 1), S


def mla_block(q_lat, q_rope, q_pos, c, kr, k_pos):
    s = jnp.einsum('bqhc,bkc->bhqk', q_lat, c) + jnp.einsum('bqhr,bkr->bhqk', q_rope, kr)
    s = s.astype(jnp.float32) * MLA_SCALE
    s = jnp.where(k_pos[None, None, None, :] <= q_pos[None, None, :, None], s, -jnp.inf)
    pr = jax.nn.softmax(s, axis=-1).astype(c.dtype)
    return jnp.einsum('bhqk,bkc->bqhc', pr, c)


def mla_attend(q_lat, q_rope, q_pos, c, kr, k_pos):
    B, T = q_lat.shape[:2]
    if T <= CHUNK:
        return mla_block(q_lat, q_rope, q_pos, c, kr, k_pos)
    nb = -(-T // CHUNK)
    pad = nb * CHUNK - T

    def blocks(t):
        t = jnp.pad(t, ((0, 0), (0, pad)) + ((0, 0),) * (t.ndim - 2))
        return jnp.moveaxis(t.reshape(B, nb, CHUNK, *t.shape[2:]), 1, 0)

    pb = jnp.pad(q_pos, (0, pad), mode='edge').reshape(nb, CHUNK)
    o = lax.map(lambda a: mla_block(a[0], a[1], a[2], c, kr, k_pos), (blocks(q_lat), blocks(q_rope), pb))
    return jnp.moveaxis(o, 0, 1).reshape(B, nb * CHUNK, *o.shape[3:])[:, :T]


def odd_mixer(x, pos, S0, shift0, past_c, past_kr, P, i):
    B, T, _ = x.shape
    W = RWKV_WIDTH
    p = x @ P['od_w_in'][i]
    pr = p[..., :RWKV_PROJ]
    qa = p[..., RWKV_PROJ:RWKV_PROJ + Q_LORA]
    ckv = p[..., RWKV_PROJ + Q_LORA:RWKV_PROJ + Q_LORA + KV_LORA]
    kr = p[..., RWKV_PROJ + Q_LORA + KV_LORA:]
    prev = jnp.concatenate([shift0[:, None].astype(pr.dtype), pr[:, :-1]], 1)
    xm = pr + (prev - pr) * P['rwkv_mu'][i]
    r = xm[..., :W]
    k = xm[..., W:2 * W]
    v = xm[..., 2 * W:3 * W]
    xw = xm[..., 3 * W:3 * W + LORA_W]
    xa = xm[..., 3 * W + LORA_W:3 * W + LORA_W + LORA_A]
    xg = xm[..., 3 * W + LORA_W + LORA_A:]
    w = -jax.nn.softplus(-(P['rwkv_w0'][i] + jnp.tanh(xw) @ P['rwkv_w2'][i])) - 0.5
    decay = jnp.exp(-jnp.exp(w.astype(jnp.float32)))
    a = jax.nn.sigmoid(P['rwkv_a0'][i] + xa @ P['rwkv_a2'][i])
    g = jax.nn.sigmoid(xg) @ P['rwkv_g2'][i]

    def hs(t):
        return t.reshape(B, T, RWKV_HEADS, RWKV_HEAD)

    kk = hs((k * P['rwkv_k_k'][i]).astype(jnp.float32))
    kk = kk / jnp.maximum(jnp.sqrt(jnp.sum(kk * kk, -1, keepdims=True)), 1e-12)
    k = k * (1.0 + (a - 1.0) * P['rwkv_k_a'][i])
    y, S = wkv7(hs(r), hs(decay), hs(k), hs(v), -kk, kk * hs(a), S0)
    y = head_norm(y, P['rwkv_gn_g'][i], P['rwkv_gn_b'][i], RWKV_GN_EPS)
    bonus = jnp.sum(hs(r) * hs(k) * P['rwkv_r_k'][i], -1, keepdims=True) * hs(v)
    y_c = (y + bonus.reshape(B, T, W)) * g
    qa = rms_norm(qa, P['mla_q_norm'][i])
    q = (qa @ P['mla_w_qb'][i]).reshape(B, T, MLA_HEADS, QK_NOPE + QK_ROPE)
    q_nope = q[..., :QK_NOPE]
    q_rope = rope(q[..., QK_NOPE:], pos)
    c = rms_norm(ckv, P['mla_kv_norm'][i])
    kr = rope(kr, pos)
    q_lat = jnp.einsum('bthn,chn->bthc', q_nope, P['mla_w_uk'][i])
    if past_c is None:
        c_all, kr_all, k_pos = c, kr, pos
    else:
        c_all = jnp.concatenate([past_c.astype(c.dtype), c], 1)
        kr_all = jnp.concatenate([past_kr.astype(kr.dtype), kr], 1)
        k_pos = jnp.arange(c_all.shape[1], dtype=jnp.int32)
    o_lat = mla_attend(q_lat, q_rope, pos, c_all, kr_all, k_pos)
    y_d = jnp.einsum('bthc,chv->bthv', o_lat, P['mla_w_uv'][i]).reshape(B, T, MLA_WIDTH)
    mix = jnp.concatenate([y_c.astype(x.dtype), y_d.astype(x.dtype)], -1) @ P['od_w_out'][i]
    return mix, S, pr[:, -1], c, kr


def hier_moe(x, P, l):
    B, T, D = x.shape
    xf = x.reshape(B * T, D)
    lc = (xf @ P['moe_w_coarse'][l] + P['moe_b_coarse'][l]).astype(jnp.float32)
    pc = jax.nn.softmax(lc, -1)
    gsel = jnp.argmax(lc, -1)
    gate_c = jnp.take_along_axis(pc, gsel[:, None], 1)
    lf = jnp.einsum('nd,gde->nge', xf, P['moe_w_fine'][l]) + P['moe_b_fine'][l]
    lf = jnp.take_along_axis(lf, gsel[:, None, None], 1)[:, 0].astype(jnp.float32)
    tv, ti = lax.top_k(lf, FINE_TOPK)
    wf = jax.nn.softmax(tv, -1) * gate_c
    eid = gsel[:, None] * E_PER_GROUP + ti
    comb = jnp.sum(jax.nn.one_hot(eid, N_EXPERTS, dtype=jnp.float32) * wf[..., None], 1)
    h = jax.nn.silu(jnp.einsum('nd,edf->nef', xf, P['moe_w_gate'][l])) * jnp.einsum('nd,edf->nef', xf, P['moe_w_up'][l])
    y = jnp.einsum('nef,efd->nd', h * comb[..., None].astype(h.dtype), P['moe_w_down'][l])
    return y.reshape(B, T, D)


def trunk(x, pos, n_lead, ret0, s5re0, s5im0, wkv0, shift0, past, P):
    ret_l, s5re_l, s5im_l, wkv_l, shift_l, kv_l, kr_l = [], [], [], [], [], [], []
    for layer in range(DEPTH):
        i = layer // 2
        if layer % 2 == 0:
            mix, S, h_re, h_im = even_mixer(x, pos, n_lead, ret0[i], s5re0[i], s5im0[i], P, i)
            ret_l.append(S)
            s5re_l.append(h_re)
            s5im_l.append(h_im)
        else:
            if past is None:
                past_c = None
                past_kr = None
            else:
                cache_kv, cache_krope, page_table = past
                nb = page_table.shape[0]
                past_c = cache_kv[page_table, i].reshape(nb, -1, KV_LORA)
                past_kr = cache_krope[page_table, i].reshape(nb, -1, QK_ROPE)
            mix, S, sh, c, kr = odd_mixer(x, pos, wkv0[i], shift0[i], past_c, past_kr, P, i)
            wkv_l.append(S)
            shift_l.append(sh)
            kv_l.append(c)
            kr_l.append(kr)
        x = layer_norm(DN_ALPHA * x + mix.astype(x.dtype), P['ln1_g'][layer], P['ln1_b'][layer])
        x = layer_norm(DN_ALPHA * x + hier_moe(x, P, layer).astype(x.dtype), P['ln2_g'][layer], P['ln2_b'][layer])
    new = (jnp.stack(ret_l), jnp.stack(s5re_l), jnp.stack(s5im_l), jnp.stack(wkv_l), jnp.stack(shift_l),
           jnp.stack(kv_l, 1), jnp.stack(kr_l, 1))
    return x, new


def setup_inputs(seed: int = 0) -> dict:
    key = jax.random.key(seed)
    ks = iter(jax.random.split(key, 80))
    f32 = jnp.float32

    def nrm(shape, scale=1.0):
        return jax.random.normal(next(ks), shape, f32) * scale

    def unif(shape, lo, hi):
        return jax.random.uniform(next(ks), shape, f32, lo, hi)

    n_pages = PAST_LEN // PAGE_SIZE
    n_pool = -(-5 * DEC_BATCH * n_pages // 4)
    NE, NO = N_EVEN, N_ODD
    inp = {}
    inp['x_prompt'] = nrm((BATCH, SEQ, D_MODEL))
    inp['x_sample'] = nrm((DEC_BATCH, DEC_SEQ, D_MODEL))
    inp['state_ret'] = nrm((NE, DEC_BATCH, RET_HEADS, RET_DK, RET_DV), 0.5)
    inp['state_s5_re'] = nrm((NE, DEC_BATCH, S5_GROUPS, S5_STATE), 0.1)
    inp['state_s5_im'] = nrm((NE, DEC_BATCH, S5_GROUPS, S5_STATE), 0.1)
    inp['state_wkv'] = nrm((NO, DEC_BATCH, RWKV_HEADS, RWKV_HEAD, RWKV_HEAD), 0.1)
    inp['state_shift'] = nrm((NO, DEC_BATCH, RWKV_PROJ))
    inp['cache_kv'] = nrm((n_pool, NO, PAGE_SIZE, KV_LORA))
    inp['cache_krope'] = nrm((n_pool, NO, PAGE_SIZE, QK_ROPE))
    inp['page_table'] = jax.random.permutation(next(ks), n_pool)[:DEC_BATCH * n_pages].reshape(DEC_BATCH, n_pages).astype(jnp.int32)
    inp['meta_tokens'] = nrm((N_META, D_MODEL))
    inp['ev_w_in'] = nrm((NE, D_MODEL, EVEN_IN), D_MODEL ** -0.5)
    inp['ret_gn_g'] = 1.0 + nrm((NE, RET_HEADS * RET_DV), 0.02)
    inp['ret_gn_b'] = nrm((NE, RET_HEADS * RET_DV), 0.02)
    inp['s5_a_re'] = -0.5 + nrm((NE, S5_GROUPS, S5_STATE), 0.01)
    inp['s5_a_im'] = jnp.pi * jnp.arange(S5_STATE, dtype=f32) + nrm((NE, S5_GROUPS, S5_STATE), 0.01)
    inp['s5_log_dt'] = unif((NE, S5_GROUPS), math.log(1e-3), math.log(1e-1))
    inp['s5_b_re'] = nrm((NE, S5_GROUPS, S5_STATE, S5_GROUP), (2 * S5_GROUP) ** -0.5)
    inp['s5_b_im'] = nrm((NE, S5_GROUPS, S5_STATE, S5_GROUP), (2 * S5_GROUP) ** -0.5)
    inp['s5_c_re'] = nrm((NE, S5_GROUPS, S5_GROUP, S5_STATE), S5_STATE ** -0.5)
    inp['s5_c_im'] = nrm((NE, S5_GROUPS, S5_GROUP, S5_STATE), S5_STATE ** -0.5)
    inp['s5_d'] = nrm((NE, S5_WIDTH))
    inp['s5_w_glu'] = nrm((NE, S5_WIDTH, S5_WIDTH), S5_WIDTH ** -0.5)
    inp['s5_b_glu'] = nrm((NE, S5_WIDTH), 0.02)
    inp['ev_w_out'] = nrm((NE, EVEN_OUT, D_MODEL), EVEN_OUT ** -0.5 * DN_BETA)
    inp['od_w_in'] = nrm((NO, D_MODEL, ODD_IN), D_MODEL ** -0.5)
    inp['rwkv_mu'] = unif((NO, RWKV_PROJ), 0.0, 1.0)
    inp['rwkv_w0'] = unif((NO, RWKV_WIDTH), -6.0, -1.0)
    inp['rwkv_w2'] = nrm((NO, LORA_W, RWKV_WIDTH), 0.3 * LORA_W ** -0.5)
    inp['rwkv_a0'] = nrm((NO, RWKV_WIDTH), 0.1)
    inp['rwkv_a2'] = nrm((NO, LORA_A, RWKV_WIDTH), LORA_A ** -0.5)
    inp['rwkv_g2'] = nrm((NO, LORA_G, RWKV_WIDTH), LORA_G ** -0.5)
    inp['rwkv_k_k'] = 0.85 + nrm((NO, RWKV_WIDTH), 0.02)
    inp['rwkv_k_a'] = 1.0 + nrm((NO, RWKV_WIDTH), 0.02)
    inp['rwkv_r_k'] = nrm((NO, RWKV_HEADS, RWKV_HEAD), 0.1)
    inp['rwkv_gn_g'] = 1.0 + nrm((NO, RWKV_WIDTH), 0.02)
    inp['rwkv_gn_b'] = nrm((NO, RWKV_WIDTH), 0.02)
    inp['mla_q_norm'] = 1.0 + nrm((NO, Q_LORA), 0.02)
    inp['mla_w_qb'] = nrm((NO, Q_LORA, MLA_HEADS * (QK_NOPE + QK_ROPE)), Q_LORA ** -0.5)
    inp['mla_kv_norm'] = 1.0 + nrm((NO, KV_LORA), 0.02)
    inp['mla_w_uk'] = nrm((NO, KV_LORA, MLA_HEADS, QK_NOPE), KV_LORA ** -0.5)
    inp['mla_w_uv'] = nrm((NO, KV_LORA, MLA_HEADS, V_HEAD), KV_LORA ** -0.5)
    inp['od_w_out'] = nrm((NO, ODD_OUT, D_MODEL), ODD_OUT ** -0.5 * DN_BETA)
    inp['ln1_g'] = 1.0 + nrm((DEPTH, D_MODEL), 0.02)
    inp['ln1_b'] = nrm((DEPTH, D_MODEL), 0.02)
    inp['ln2_g'] = 1.0 + nrm((DEPTH, D_MODEL), 0.02)
    inp['ln2_b'] = nrm((DEPTH, D_MODEL), 0.02)
    inp['moe_w_coarse'] = nrm((DEPTH, D_MODEL, N_EGROUPS), D_MODEL ** -0.5)
    inp['moe_b_coarse'] = nrm((DEPTH, N_EGROUPS), 0.01)
    inp['moe_w_fine'] = nrm((DEPTH, N_EGROUPS, D_MODEL, E_PER_GROUP), D_MODEL ** -0.5)
    inp['moe_b_fine'] = nrm((DEPTH, N_EGROUPS, E_PER_GROUP), 0.01)
    inp['moe_w_gate'] = nrm((DEPTH, N_EXPERTS, D_MODEL, D_EXPERT), D_MODEL ** -0.5)
    inp['moe_w_up'] = nrm((DEPTH, N_EXPERTS, D_MODEL, D_EXPERT), D_MODEL ** -0.5)
    inp['moe_w_down'] = nrm((DEPTH, N_EXPERTS, D_EXPERT, D_MODEL), D_EXPERT ** -0.5 * DN_BETA)
    return inp


def reference(x_prompt, x_sample, state_ret, state_s5_re, state_s5_im, state_wkv, state_shift,
              cache_kv, cache_krope, page_table, meta_tokens,
              ev_w_in, ret_gn_g, ret_gn_b, s5_a_re, s5_a_im, s5_log_dt, s5_b_re, s5_b_im, s5_c_re, s5_c_im,
              s5_d, s5_w_glu, s5_b_glu, ev_w_out,
              od_w_in, rwkv_mu, rwkv_w0, rwkv_w2, rwkv_a0, rwkv_a2, rwkv_g2, rwkv_k_k, rwkv_k_a, rwkv_r_k,
              rwkv_gn_g, rwkv_gn_b, mla_q_norm, mla_w_qb, mla_kv_norm, mla_w_uk, mla_w_uv, od_w_out,
              ln1_g, ln1_b, ln2_g, ln2_b, moe_w_coarse, moe_b_coarse, moe_w_fine, moe_b_fine,
              moe_w_gate, moe_w_up, moe_w_down):
    P = dict(ev_w_in=ev_w_in, ret_gn_g=ret_gn_g, ret_gn_b=ret_gn_b, s5_a_re=s5_a_re, s5_a_im=s5_a_im,
             s5_log_dt=s5_log_dt, s5_b_re=s5_b_re, s5_b_im=s5_b_im, s5_c_re=s5_c_re, s5_c_im=s5_c_im,
             s5_d=s5_d, s5_w_glu=s5_w_glu, s5_b_glu=s5_b_glu, ev_w_out=ev_w_out,
             od_w_in=od_w_in, rwkv_mu=rwkv_mu, rwkv_w0=rwkv_w0, rwkv_w2=rwkv_w2, rwkv_a0=rwkv_a0,
             rwkv_a2=rwkv_a2, rwkv_g2=rwkv_g2, rwkv_k_k=rwkv_k_k, rwkv_k_a=rwkv_k_a, rwkv_r_k=rwkv_r_k,
             rwkv_gn_g=rwkv_gn_g, rwkv_gn_b=rwkv_gn_b, mla_q_norm=mla_q_norm, mla_w_qb=mla_w_qb,
             mla_kv_norm=mla_kv_norm, mla_w_uk=mla_w_uk, mla_w_uv=mla_w_uv, od_w_out=od_w_out,
             ln1_g=ln1_g, ln1_b=ln1_b, ln2_g=ln2_g, ln2_b=ln2_b, moe_w_coarse=moe_w_coarse,
             moe_b_coarse=moe_b_coarse, moe_w_fine=moe_w_fine, moe_b_fine=moe_b_fine,
             moe_w_gate=moe_w_gate, moe_w_up=moe_w_up, moe_w_down=moe_w_down)
    f32 = jnp.float32
    B = x_prompt.shape[0]
    xp = jnp.concatenate([jnp.broadcast_to(meta_tokens[None].astype(x_prompt.dtype), (B, N_META, D_MODEL)), x_prompt], 1)
    pos_p = jnp.arange(xp.shape[1], dtype=jnp.int32)
    ret0 = jnp.zeros((N_EVEN, B, RET_HEADS, RET_DK, RET_DV), f32)
    s50 = jnp.zeros((N_EVEN, B, S5_GROUPS, S5_STATE), f32)
    wkv0 = jnp.zeros((N_ODD, B, RWKV_HEADS, RWKV_HEAD, RWKV_HEAD), f32)
    sh0 = jnp.zeros((N_ODD, B, RWKV_PROJ), x_prompt.dtype)
    yp, (ret_p, s5re_p, s5im_p, wkv_p, shift_p, kv_p, kr_p) = trunk(
        xp, pos_p, N_META, ret0, s50, s50, wkv0, sh0, None, P)
    pos_s = PAST_LEN + jnp.arange(x_sample.shape[1], dtype=jnp.int32)
    ys, (ret_s, s5re_s, s5im_s, wkv_s, shift_s, kv_s, kr_s) = trunk(
        x_sample, pos_s, x_sample.shape[1], state_ret, state_s5_re, state_s5_im, state_wkv, state_shift,
        (cache_kv, cache_krope, page_table), P)
    return (yp[:, N_META:], ys, ret_p, ret_s, s5re_p, s5re_s, s5im_p, s5im_s, wkv_p, wkv_s,
            shift_p, shift_s, kv_p, kv_s, kr_p, kr_s)
```

```python
import functools
import math

import jax
import jax.numpy as jnp
from jax import lax
from jax.experimental import pallas as pl
from jax.experimental.pallas import tpu as pltpu

F32 = jnp.float32
BF16 = jnp.bfloat16

D_MODEL = 1024
N_META = 16
PAGE_SIZE = 128
RET_HEADS, RET_DK, RET_DV = 4, 64, 128
S5_WIDTH, S5_GROUP, S5_STATE = 512, 16, 64
S5_GROUPS = S5_WIDTH // S5_GROUP
S5_CH = S5_GROUPS * S5_STATE
RWKV_HEAD, RWKV_WIDTH = 64, 512
RWKV_HEADS = RWKV_WIDTH // RWKV_HEAD
LORA_W, LORA_A, LORA_G = 64, 64, 128
RWKV_PROJ = 3 * RWKV_WIDTH + LORA_W + LORA_A + LORA_G
RWKV_GN_EPS = 64e-5
MLA_HEADS, Q_LORA, KV_LORA, QK_NOPE, QK_ROPE, V_HEAD = 8, 384, 256, 64, 32, 64
MLA_WIDTH = MLA_HEADS * V_HEAD
MLA_SCALE = (QK_NOPE + QK_ROPE) ** -0.5
ROPE_THETA = 10000.0
N_EGROUPS, E_PER_GROUP, D_EXPERT = 4, 4, 256
N_EXPERTS = N_EGROUPS * E_PER_GROUP
DEPTH = 4
DN_ALPHA = (2 * DEPTH) ** 0.25
BP = 8

VMEM_LIMIT = 56 * 1024 * 1024
WKV_L = 32
WKV_HG = 4
MLA_TILE = 416
SEQ_TILE = 344

NT = (((1,), (1,)), ((), ()))
TN = (((0,), (0,)), ((), ()))
NN = (((1,), (0,)), ((), ()))


def _cp(*sem):
    return pltpu.CompilerParams(dimension_semantics=sem, vmem_limit_bytes=VMEM_LIMIT)


def _div_tile(n, cap, mult=8):
    best = None
    for d in range(mult, min(n, cap) + 1, mult):
        if n % d == 0:
            best = d
    assert best is not None, (n, cap, mult)
    return best


def _dg(a, b, dims):
    return lax.dot_general(a, b, dims, preferred_element_type=F32)


def _bdot(a, b, dims=NN):
    return _dg(a.astype(BF16), b.astype(BF16), dims)


def _dot3(a, b, dims=NN):
    ah = a.astype(BF16)
    al = (a - ah.astype(F32)).astype(BF16)
    bh = b.astype(BF16)
    bl = (b - bh.astype(F32)).astype(BF16)
    return _dg(ah, bh, dims) + (_dg(ah, bl, dims) + _dg(al, bh, dims))


def _layer_norm(z, g, b, eps=1e-5):
    mu = jnp.mean(z, -1, keepdims=True)
    zc = z - mu
    var = jnp.mean(zc * zc, -1, keepdims=True)
    return zc * lax.rsqrt(var + eps) * g + b


def _rope(x, cos, sin_signed, half):
    w = x.shape[-1]
    lane = lax.broadcasted_iota(jnp.int32, x.shape, 1)
    first = (lane % (2 * half)) < half
    other = jnp.where(first, pltpu.roll(x, w - half, 1), pltpu.roll(x, half, 1))
    return x * cos + other * sin_signed


def _rope_tables(pos, dim, reps):
    inv = ROPE_THETA ** (-jnp.arange(0, dim, 2, dtype=F32) / dim)
    ang = pos.astype(F32)[:, None] * inv[None, :]
    cos = jnp.cos(ang)
    sin = jnp.sin(ang)
    cos = jnp.tile(jnp.concatenate([cos, cos], -1), (1, reps))
    sin = jnp.tile(jnp.concatenate([-sin, sin], -1), (1, reps))
    return cos, sin


def _mm_kernel(x_ref, w_ref, o_ref):
    o_ref[...] = _dg(x_ref[...].astype(BF16), w_ref[...], NN)


def _matmul(x, w, tm, tn):
    n, k = x.shape
    m = w.shape[1]
    return pl.pallas_call(
        _mm_kernel,
        grid=(n // tm, m // tn),
        in_specs=[pl.BlockSpec((tm, k), lambda i, j: (i, 0)), pl.BlockSpec((k, tn), lambda i, j: (0, j))],
        out_specs=pl.BlockSpec((tm, tn), lambda i, j: (i, j)),
        out_shape=jax.ShapeDtypeStruct((n, m), F32),
        compiler_params=_cp("parallel", "arbitrary"),
        name="proj_in",
    )(x, w)


def _proj_ln_kernel(a1_ref, a2_ref, x_ref, w1_ref, w2_ref, g_ref, b_ref, o_ref):
    mix = _dg(a1_ref[...].astype(BF16), w1_ref[...], NN) + _dg(a2_ref[...].astype(BF16), w2_ref[...], NN)
    z = DN_ALPHA * x_ref[...] + mix
    o_ref[...] = _layer_norm(z, g_ref[...], b_ref[...])


def _proj_ln(a1, a2, x, w_out, g, b, tm):
    n = x.shape[0]
    k1, k2 = a1.shape[1], a2.shape[1]
    w1 = w_out[:k1].astype(BF16)
    w2 = w_out[k1:].astype(BF16)
    row = lambda i: (i, 0)
    fix = lambda i: (0, 0)
    return pl.pallas_call(
        _proj_ln_kernel,
        grid=(n // tm,),
        in_specs=[pl.BlockSpec((tm, k1), row), pl.BlockSpec((tm, k2), row), pl.BlockSpec((tm, D_MODEL), row),
                  pl.BlockSpec((k1, D_MODEL), fix), pl.BlockSpec((k2, D_MODEL), fix),
                  pl.BlockSpec((1, D_MODEL), fix), pl.BlockSpec((1, D_MODEL), fix)],
        out_specs=pl.BlockSpec((tm, D_MODEL), row),
        out_shape=jax.ShapeDtypeStruct((n, D_MODEL), F32),
        compiler_params=_cp("parallel"),
        name="proj_out_ln",
    )(a1, a2, x, w1, w2, g.reshape(1, -1), b.reshape(1, -1))


def _router_kernel(x_ref, w_ref, b_ref, o_ref):
    lt = _bdot(w_ref[...], x_ref[...], NT) + b_ref[...]
    lc = [lt[g:g + 1] for g in range(N_EGROUPS)]
    m = functools.reduce(jnp.maximum, lc)
    taken = jnp.zeros_like(m) > 1.0
    sel = []
    for g in range(N_EGROUPS):
        s = jnp.logical_and(lc[g] == m, jnp.logical_not(taken))
        sel.append(s)
        taken = jnp.logical_or(taken, s)
    gate_c = 1.0 / functools.reduce(lambda a, c: a + c, [jnp.exp(v - m) for v in lc])
    lf = []
    for e in range(E_PER_GROUP):
        v = jnp.zeros_like(m)
        for g in range(N_EGROUPS):
            r = N_EGROUPS + g * E_PER_GROUP + e
            v = jnp.where(sel[g], lt[r:r + 1], v)
        lf.append(v)
    pick = []
    for e in range(E_PER_GROUP):
        rank = jnp.zeros_like(m)
        for e2 in range(E_PER_GROUP):
            if e2 == e:
                continue
            ahead = (lf[e2] > lf[e]) if e2 > e else (lf[e2] >= lf[e])
            rank = rank + jnp.where(ahead, 1.0, 0.0)
        pick.append(rank < 2.0)
    t1 = functools.reduce(jnp.maximum, lf)
    ex = [jnp.where(pick[e], jnp.exp(lf[e] - t1), 0.0) for e in range(E_PER_GROUP)]
    scale = gate_c / functools.reduce(lambda a, c: a + c, ex)
    rows = []
    for g in range(N_EGROUPS):
        for e in range(E_PER_GROUP):
            rows.append(jnp.where(sel[g], ex[e] * scale, 0.0))
    o_ref[...] = jnp.concatenate(rows, axis=0)


def _router(x, w_coarse, b_coarse, w_fine, b_fine, tm):
    n = x.shape[0]
    nr = 32
    wt = jnp.concatenate([w_coarse.T, jnp.transpose(w_fine, (0, 2, 1)).reshape(N_EXPERTS, D_MODEL)], 0)
    wt = jnp.pad(wt, ((0, nr - wt.shape[0]), (0, 0)))
    bt = jnp.pad(jnp.concatenate([b_coarse, b_fine.reshape(-1)]), (0, nr - N_EGROUPS - N_EXPERTS)).reshape(nr, 1)
    comb_t = pl.pallas_call(
        _router_kernel,
        grid=(n // tm,),
        in_specs=[pl.BlockSpec((tm, D_MODEL), lambda i: (i, 0)), pl.BlockSpec((nr, D_MODEL), lambda i: (0, 0)),
                  pl.BlockSpec((nr, 1), lambda i: (0, 0))],
        out_specs=pl.BlockSpec((N_EXPERTS, tm), lambda i: (0, i)),
        out_shape=jax.ShapeDtypeStruct((N_EXPERTS, n), F32),
        compiler_params=_cp("parallel"),
        name="router",
    )(x, wt, bt)
    return comb_t.T


def _moe_kernel(x_ref, comb_ref, wg_ref, wu_ref, wd_ref, g_ref, b_ref, o_ref, xb_scr, acc_scr):
    e = pl.program_id(1)

    @pl.when(e == 0)
    def _():
        xb_scr[...] = x_ref[...].astype(BF16)
        acc_scr[...] = jnp.zeros_like(acc_scr)

    xb = xb_scr[...]
    hg = _dg(xb, wg_ref[0], NN)
    hu = _dg(xb, wu_ref[0], NN)
    comb = comb_ref[...]
    lane = lax.broadcasted_iota(jnp.int32, comb.shape, 1)
    ce = jnp.sum(jnp.where(lane == e, comb, 0.0), axis=1, keepdims=True)
    h = (hg * jax.nn.sigmoid(hg)) * hu * ce
    acc_scr[...] += _dg(h.astype(BF16), wd_ref[0], NN)

    @pl.when(e == pl.num_programs(1) - 1)
    def _():
        z = DN_ALPHA * x_ref[...] + acc_scr[...]
        o_ref[...] = _layer_norm(z, g_ref[...], b_ref[...])


def _moe_ln(x, comb, wg, wu, wd, g, b, tm):
    n = x.shape[0]
    row = lambda i, e: (i, 0)
    fix = lambda i, e: (0, 0)
    return pl.pallas_call(
        _moe_kernel,
        grid=(n // tm, N_EXPERTS),
        in_specs=[pl.BlockSpec((tm, D_MODEL), row), pl.BlockSpec((tm, N_EXPERTS), row),
                  pl.BlockSpec((1, D_MODEL, D_EXPERT), lambda i, e: (e, 0, 0)),
                  pl.BlockSpec((1, D_MODEL, D_EXPERT), lambda i, e: (e, 0, 0)),
                  pl.BlockSpec((1, D_EXPERT, D_MODEL), lambda i, e: (e, 0, 0)),
                  pl.BlockSpec((1, D_MODEL), fix), pl.BlockSpec((1, D_MODEL), fix)],
        out_specs=pl.BlockSpec((tm, D_MODEL), row),
        out_shape=jax.ShapeDtypeStruct((n, D_MODEL), F32),
        scratch_shapes=[pltpu.VMEM((tm, D_MODEL), BF16), pltpu.VMEM((tm, D_MODEL), F32)],
        compiler_params=_cp("parallel", "arbitrary"),
        name="moe_ln",
    )(x, comb, wg, wu, wd, g.reshape(1, -1), b.reshape(1, -1))


def _ret_kernel(q_ref, k_ref, v_ref, gate_ref, cos_ref, sin_ref, dmask_ref, cdec_ref, kdec_ref, sdec_ref,
                gng_ref, gnb_ref, s0_ref, o_ref, sout_ref, s_scr, *, L, Lblk):
    c = pl.program_id(1)
    HK, HV = RET_HEADS * RET_DK, RET_HEADS * RET_DV

    @pl.when(c == 0)
    def _():
        rows = []
        for h in range(RET_HEADS):
            pieces = [s0_ref[0, h] if h2 == h else jnp.zeros((RET_DK, RET_DV), F32) for h2 in range(RET_HEADS)]
            rows.append(jnp.concatenate(pieces, axis=1))
        s_scr[...] = jnp.concatenate(rows, axis=0)

    def load(ref, w):
        x = ref[0] if len(ref.shape) == 3 else ref[...]
        if Lblk < L:
            rid = lax.broadcasted_iota(jnp.int32, (L, w), 0)
            x = jnp.where(rid < Lblk, jnp.broadcast_to(x, (L, w)), 0.0)
        return x

    cos, sin = load(cos_ref, HK), load(sin_ref, HK)
    q = _rope(load(q_ref, HK), cos, sin, RET_DK // 2) * RET_DK ** -0.5
    k = _rope(load(k_ref, HK), cos, sin, RET_DK // 2)
    v = load(v_ref, HV)
    gate = load(gate_ref, HV)
    qb, kb, vb = q.astype(BF16), k.astype(BF16), v.astype(BF16)
    sbd = s_scr[...]
    cross = _dg(qb, sbd.astype(BF16), NN) * cdec_ref[...]
    lane = lax.broadcasted_iota(jnp.int32, (L, HK), 1)
    outs = []
    for h in range(RET_HEADS):
        qh = jnp.where(lane // RET_DK == h, qb, jnp.zeros_like(qb))
        sc = _dg(qh, kb, NT) * dmask_ref[h]
        sl = slice(h * RET_DV, (h + 1) * RET_DV)
        o = _dg(sc.astype(BF16), vb[:, sl], NN) + cross[:, sl]
        mu = jnp.mean(o, -1, keepdims=True)
        oc = o - mu
        var = jnp.mean(oc * oc, -1, keepdims=True)
        outs.append(oc * lax.rsqrt(var + 1e-5))
    on = jnp.concatenate(outs, axis=1) * gng_ref[...] + gnb_ref[...]
    res = (gate * jax.nn.sigmoid(gate)) * on
    if len(o_ref.shape) == 3:
        o_ref[0] = res[:Lblk]
    else:
        o_ref[...] = res
    upd = _dg((k * kdec_ref[...]).astype(BF16), vb, TN)
    ri = lax.broadcasted_iota(jnp.int32, (HK, HV), 0) // RET_DK
    ci = lax.broadcasted_iota(jnp.int32, (HK, HV), 1) // RET_DV
    s_new = sdec_ref[...] * sbd + jnp.where(ri == ci, upd, 0.0)
    s_scr[...] = s_new

    @pl.when(c == pl.num_programs(1) - 1)
    def _():
        for h in range(RET_HEADS):
            sout_ref[0, h] = s_new[h * RET_DK:(h + 1) * RET_DK, h * RET_DV:(h + 1) * RET_DV]


def _retention(p2d, nb, t_len, row0, pos, s0, gn_g, gn_b, out_prev, n_rows):
    HK, HV = RET_HEADS * RET_DK, RET_HEADS * RET_DV
    if t_len >= 8:
        L = Lblk = _div_tile(t_len, SEQ_TILE)
    else:
        assert t_len == 1
        L, Lblk = 8, 1
    nc = t_len // Lblk
    assert row0 % Lblk == 0
    r0 = row0 // Lblk
    log_g = jnp.log1p(-jnp.exp2(-5.0 - jnp.arange(RET_HEADS, dtype=F32)))
    idx = jnp.arange(L, dtype=F32)
    valid = (jnp.arange(L) < Lblk)
    diff = idx[:, None] - idx[None, :]
    dmask = jnp.where(diff >= 0, jnp.exp(log_g[:, None, None] * jnp.maximum(diff, 0.0)), 0.0)
    cdec = jnp.repeat(jnp.exp(log_g[None, :] * (idx + 1.0)[:, None]), RET_DV, axis=1)
    kdec = jnp.repeat(jnp.exp(log_g[None, :] * (Lblk - 1.0 - idx)[:, None]), RET_DK, axis=1)
    kdec = jnp.where(valid[:, None], kdec, 0.0)
    sdec = jnp.broadcast_to(jnp.repeat(jnp.exp(log_g * Lblk), RET_DK)[:, None], (HK, HV))
    cos, sin = _rope_tables(pos, RET_DK, RET_HEADS)
    ein = 2 * HK + 2 * HV + S5_WIDTH
    qb, vb = ein // HK, ein // HV
    fix2 = lambda b, c: (0, 0)
    kern = functools.partial(_ret_kernel, L=L, Lblk=Lblk)
    if Lblk == 1:
        p2d = p2d.reshape(p2d.shape[0], 1, p2d.shape[1])

        def rspec(w, col):
            return pl.BlockSpec((1, 1, w), lambda b, c: (r0 + c, 0, col(b)))
    else:
        def rspec(w, col):
            return pl.BlockSpec((Lblk, w), lambda b, c: (r0 + c, col(b)))
    in_specs = [rspec(HK, lambda b: b * qb), rspec(HK, lambda b: b * qb + 1),
                rspec(HV, lambda b: b * vb + 1), rspec(HV, lambda b: b * vb + 2),
                pl.BlockSpec((Lblk, HK), lambda b, c: (c, 0)), pl.BlockSpec((Lblk, HK), lambda b, c: (c, 0)),
                pl.BlockSpec((RET_HEADS, L, L), lambda b, c: (0, 0, 0)),
                pl.BlockSpec((L, HV), fix2), pl.BlockSpec((L, HK), fix2), pl.BlockSpec((HK, HV), fix2),
                pl.BlockSpec((1, HV), fix2), pl.BlockSpec((1, HV), fix2),
                pl.BlockSpec((1, RET_HEADS, RET_DK, RET_DV), lambda b, c: (b, 0, 0, 0))]
    args = [p2d, p2d, p2d, p2d, cos, sin, dmask, cdec, kdec, sdec, gn_g.reshape(1, -1), gn_b.reshape(1, -1), s0]
    aliases = {}
    if out_prev is not None:
        in_specs.append(pl.BlockSpec(memory_space=pl.ANY))
        args.append(out_prev)
        aliases = {len(args) - 1: 0}
        kern = functools.partial(_drop_alias_arg, kern, len(args) - 1)
    out, s_new = pl.pallas_call(
        kern,
        grid=(nb, nc),
        in_specs=in_specs,
        out_specs=[rspec(HV, lambda b: b),
                   pl.BlockSpec((1, RET_HEADS, RET_DK, RET_DV), lambda b, c: (b, 0, 0, 0))],
        out_shape=[jax.ShapeDtypeStruct((n_rows // nb, 1, nb * HV) if Lblk == 1 else (n_rows // nb, nb * HV), F32),
                   jax.ShapeDtypeStruct((nb, RET_HEADS, RET_DK, RET_DV), F32)],
        scratch_shapes=[pltpu.VMEM((HK, HV), F32)],
        input_output_aliases=aliases,
        compiler_params=_cp("parallel", "arbitrary"),
        name="retention",
    )(*args)
    return out, s_new


def _drop_alias_arg(kern, idx, *refs):
    return kern(*(refs[:idx] + refs[idx + 1:]))


def _s5_kernel(u_ref, bb_ref, cc_ref, lam_ref, d_ref, wglu_ref, bglu_ref, h0_ref, o_ref, hout_ref, bu_scr, h_scr,
               *, bs, lt):
    i = pl.program_id(0)

    @pl.when(i == 0)
    def _():
        h_scr[...] = h0_ref[...]

    u = u_ref[...]
    bu_scr[...] = _dg(u.astype(BF16), bb_ref[...], NN)
    CW = 512
    for j in range(S5_CH // CW):
        re_sl = pl.ds(j * CW, CW)
        im_sl = pl.ds(S5_CH + j * CW, CW)
        lr = jnp.broadcast_to(lam_ref[:, j * CW:(j + 1) * CW], (bs, CW))
        li = jnp.broadcast_to(lam_ref[:, S5_CH + j * CW:S5_CH + (j + 1) * CW], (bs, CW))

        def step(t, carry):
            hr, hi = carry
            rows = pl.ds(pl.multiple_of(t * bs, 8), bs)
            nr = lr * hr - li * hi + bu_scr[rows, re_sl]
            ni = lr * hi + li * hr + bu_scr[rows, im_sl]
            bu_scr[rows, re_sl] = nr
            bu_scr[rows, im_sl] = ni
            return nr, ni

        hr, hi = lax.fori_loop(0, lt, step, (h_scr[:, re_sl], h_scr[:, im_sl]))
        h_scr[:, re_sl] = hr
        h_scr[:, im_sl] = hi
    y = _dg(bu_scr[...].astype(BF16), cc_ref[...], NN) + d_ref[...] * u
    z = 0.5 * y * (1.0 + jnp.tanh(math.sqrt(2.0 / math.pi) * (y + 0.044715 * (y * y * y))))
    gl = _dg(z.astype(BF16), wglu_ref[...], NN) + bglu_ref[...]
    o_ref[...] = z * jax.nn.sigmoid(gl)

    @pl.when(i == pl.num_programs(0) - 1)
    def _():
        hout_ref[...] = h_scr[...]


def _s5(p, row0, bs, t_len, h0, prm, li, out_prev):
    n = p.shape[0]
    f32 = F32
    a_re, a_im = prm['s5_a_re'][li].astype(f32), prm['s5_a_im'][li].astype(f32)
    dt = jnp.exp(prm['s5_log_dt'][li].astype(f32))[:, None]
    mag = jnp.exp(dt * a_re)
    ab_re, ab_im = mag * jnp.cos(dt * a_im), mag * jnp.sin(dt * a_im)
    den = a_re * a_re + a_im * a_im
    n_re = ab_re - 1.0
    f_re = (n_re * a_re + ab_im * a_im) / den
    f_im = (ab_im * a_re - n_re * a_im) / den
    b_re, b_im = prm['s5_b_re'][li].astype(f32), prm['s5_b_im'][li].astype(f32)
    bb_re = f_re[..., None] * b_re - f_im[..., None] * b_im
    bb_im = f_re[..., None] * b_im + f_im[..., None] * b_re
    eye = jnp.eye(S5_GROUPS, dtype=f32)

    def blockdiag_in(w):
        return jnp.einsum('gpc,gh->gchp', w, eye).reshape(S5_WIDTH, S5_CH)

    def blockdiag_out(w):
        return jnp.einsum('gcp,gh->gphc', w, eye).reshape(S5_CH, S5_WIDTH)

    bbw = jnp.concatenate([blockdiag_in(bb_re), blockdiag_in(bb_im)], 1).astype(BF16)
    ccw = jnp.concatenate([blockdiag_out(prm['s5_c_re'][li]), -blockdiag_out(prm['s5_c_im'][li])], 0).astype(BF16)
    lam = jnp.concatenate([ab_re.reshape(1, -1), ab_im.reshape(1, -1)], 1)
    lt = _div_tile(t_len, max(1, SEQ_TILE // bs), 1)
    rows = bs * lt
    assert row0 % rows == 0
    r0 = row0 // rows
    ein = p.shape[1]
    ucol = ein // S5_WIDTH - 1
    fix = lambda i: (0, 0)
    in_specs = [pl.BlockSpec((rows, S5_WIDTH), lambda i: (r0 + i, ucol)),
                pl.BlockSpec((S5_WIDTH, 2 * S5_CH), fix), pl.BlockSpec((2 * S5_CH, S5_WIDTH), fix),
                pl.BlockSpec((1, 2 * S5_CH), fix), pl.BlockSpec((1, S5_WIDTH), fix),
                pl.BlockSpec((S5_WIDTH, S5_WIDTH), fix), pl.BlockSpec((1, S5_WIDTH), fix),
                pl.BlockSpec((bs, 2 * S5_CH), fix)]
    args = [p, bbw, ccw, lam, prm['s5_d'][li].reshape(1, -1), prm['s5_w_glu'][li].astype(BF16),
            prm['s5_b_glu'][li].reshape(1, -1), h0]
    kern = functools.partial(_s5_kernel, bs=bs, lt=lt)
    aliases = {}
    if out_prev is not None:
        in_specs.append(pl.BlockSpec(memory_space=pl.ANY))
        args.append(out_prev)
        aliases = {len(args) - 1: 0}
        kern = functools.partial(_drop_alias_arg, kern, len(args) - 1)
    out, h_new = pl.pallas_call(
        kern,
        grid=(t_len // lt,),
        in_specs=in_specs,
        out_specs=[pl.BlockSpec((rows, S5_WIDTH), lambda i: (r0 + i, 0)), pl.BlockSpec((bs, 2 * S5_CH), fix)],
        out_shape=[jax.ShapeDtypeStruct((n, S5_WIDTH), F32), jax.ShapeDtypeStruct((bs, 2 * S5_CH), F32)],
        scratch_shapes=[pltpu.VMEM((rows, 2 * S5_CH), F32), pltpu.VMEM((bs, 2 * S5_CH), F32)],
        input_output_aliases=aliases,
        compiler_params=_cp("arbitrary"),
        name="s5",
    )(*args)
    return out, h_new


def _head_sum(x, ones_bd):
    return _dot3(x, ones_bd)


def _rwkv_prep_body(pr, prev, mu_ref, w0_ref, w2_ref, a0_ref, a2_ref, g2_ref, kk_ref, ka_ref, ones_ref,
                    r_ref, lw_ref, k_ref, v_ref, a_ref, b_ref, g_ref):
    W = RWKV_WIDTH
    xm = pr + (prev - pr) * mu_ref[...]
    r = xm[:, :W]
    k = xm[:, W:2 * W]
    v = xm[:, 2 * W:3 * W]
    xwa = xm[:, 3 * W:3 * W + LORA_W + LORA_A]
    xg = xm[:, 3 * W + LORA_W + LORA_A:]
    wl = w0_ref[...] + _bdot(jnp.tanh(xwa), w2_ref[...])
    sp = jnp.maximum(-wl, 0.0) + jnp.log(1.0 + jnp.exp(-jnp.abs(wl)))
    w = -sp - 0.5
    lw = -jnp.exp(w)
    a = jax.nn.sigmoid(a0_ref[...] + _bdot(xwa, a2_ref[...]))
    g = _bdot(jax.nn.sigmoid(xg), g2_ref[...])
    kk = k * kk_ref[...]
    nrm = jnp.sqrt(_head_sum(kk * kk, ones_ref[...]))
    kk = kk / jnp.maximum(nrm, 1e-12)
    k2 = k * (1.0 + (a - 1.0) * ka_ref[...])
    r_ref[...] = r
    lw_ref[...] = lw
    k_ref[...] = k2
    v_ref[...] = v
    a_ref[...] = -kk
    b_ref[...] = kk * a
    g_ref[...] = g


def _rwkv_prep_prompt_kernel(pr_ref, prev8_ref, *rest):
    pr = pr_ref[...]
    first = jnp.where(pl.program_id(0) == 0, 0.0, 1.0)
    prev = jnp.concatenate([prev8_ref[...] * first, pr[:-BP]], axis=0)
    _rwkv_prep_body(pr, prev, *rest)


def _rwkv_prep_sample_kernel(pr_ref, prev_ref, *rest):
    _rwkv_prep_body(pr_ref[...], prev_ref[...], *rest[:9], *rest[9 + 7:])


def _rwkv_prep(p, n_prompt, shift_s, prm, li):
    n = p.shape[0]
    ns = n - n_prompt
    W = RWKV_WIDTH
    zeros64 = jnp.zeros((LORA_W, W), F32)
    w2p = jnp.concatenate([prm['rwkv_w2'][li], zeros64], 0).astype(BF16)
    a2p = jnp.concatenate([zeros64, prm['rwkv_a2'][li]], 0).astype(BF16)
    hid = jnp.arange(W) // RWKV_HEAD
    ones_bd = (hid[:, None] == hid[None, :]).astype(F32)
    consts = [prm['rwkv_mu'][li].reshape(1, -1), prm['rwkv_w0'][li].reshape(1, -1), w2p,
              prm['rwkv_a0'][li].reshape(1, -1), a2p, prm['rwkv_g2'][li].astype(BF16),
              prm['rwkv_k_k'][li].reshape(1, -1), prm['rwkv_k_a'][li].reshape(1, -1), ones_bd]
    fix = lambda i: (0, 0)
    cspecs = [pl.BlockSpec(c.shape, fix) for c in consts]
    outs_shape = [jax.ShapeDtypeStruct((n, W), F32)] * 7
    tm = _div_tile(n_prompt, 700)
    outs = pl.pallas_call(
        _rwkv_prep_prompt_kernel,
        grid=(n_prompt // tm,),
        in_specs=[pl.BlockSpec((tm, RWKV_PROJ), lambda i: (i, 0)),
                  pl.BlockSpec((BP, RWKV_PROJ), lambda i: (jnp.maximum(i * (tm // BP) - 1, 0), 0))] + cspecs,
        out_specs=[pl.BlockSpec((tm, W), lambda i: (i, 0))] * 7,
        out_shape=outs_shape,
        compiler_params=_cp("parallel"),
        name="rwkv_prep_prompt",
    )(p, p, *consts)
    assert n_prompt % ns == 0
    sb = n_prompt // ns
    nin = 2 + len(consts)
    outs = pl.pallas_call(
        _rwkv_prep_sample_kernel,
        grid=(1,),
        in_specs=[pl.BlockSpec((ns, RWKV_PROJ), lambda i: (sb, 0)), pl.BlockSpec((ns, RWKV_PROJ), fix)] + cspecs
        + [pl.BlockSpec(memory_space=pl.ANY)] * 7,
        out_specs=[pl.BlockSpec((ns, W), lambda i: (sb, 0))] * 7,
        out_shape=outs_shape,
        input_output_aliases={nin + j: j for j in range(7)},
        compiler_params=_cp("arbitrary"),
        name="rwkv_prep_sample",
    )(p, shift_s, *consts, *outs)
    return outs


def _wkv_chunk_kernel(r_ref, lw_ref, k_ref, v_ref, a_ref, b_ref, y_ref, gout_ref, g_scr, *, t_len):
    c = pl.program_id(1)
    L, HG = WKV_L, WKV_HG
    NG = RWKV_HEADS // HG
    WD = HG * RWKV_HEAD
    M = HG * L

    @pl.when(c == 0)
    def _():
        g_scr[...] = jnp.zeros_like(g_scr)

    row = lax.broadcasted_iota(jnp.int32, (L, RWKV_WIDTH), 0) + c * L
    valid = row < t_len

    def ld(ref):
        return jnp.where(valid, ref[...], 0.0)

    r, lw, k, v, a, b = ld(r_ref), ld(lw_ref), ld(k_ref), ld(v_ref), ld(a_ref), ld(b_ref)
    ti = lax.broadcasted_iota(jnp.int32, (L, L), 0)
    tj = lax.broadcasted_iota(jnp.int32, (L, L), 1)
    tri = jnp.where(ti >= tj, 1.0, 0.0)
    cum = _dot3(tri, lw)
    dec = jnp.exp(cum)
    inv = jnp.exp(-cum)
    at = a * jnp.exp(cum - lw)
    bt = b * inv
    kt = k * inv
    rt = r * dec
    rh = lax.broadcasted_iota(jnp.int32, (M, WD), 0) // L
    lh = lax.broadcasted_iota(jnp.int32, (M, WD), 1) // RWKV_HEAD
    bd = rh == lh
    mi = lax.broadcasted_iota(jnp.int32, (M, M), 0)
    mj = lax.broadcasted_iota(jnp.int32, (M, M), 1)
    same = (mi // L) == (mj // L)
    m_strict = jnp.logical_and(same, (mj % L) < (mi % L))
    m_incl = jnp.logical_and(same, (mj % L) <= (mi % L))
    eye = jnp.where(mi == mj, 1.0, 0.0)

    def stack(x):
        return jnp.where(bd, jnp.concatenate([x] * HG, axis=0), 0.0)

    for grp in range(NG):
        sl = slice(grp * WD, (grp + 1) * WD)
        a_s, r_s, b_s, k_s, v_s = stack(at[:, sl]), stack(rt[:, sl]), stack(bt[:, sl]), stack(kt[:, sl]), stack(v[:, sl])
        g = g_scr[grp]
        ab = jnp.where(m_strict, _dot3(a_s, b_s, NT), 0.0)
        ak = jnp.where(m_strict, _dot3(a_s, k_s, NT), 0.0)
        rb = jnp.where(m_incl, _dot3(r_s, b_s, NT), 0.0)
        rk = jnp.where(m_incl, _dot3(r_s, k_s, NT), 0.0)
        tinv = eye + ab
        pw = ab
        for _ in range(int(math.log2(L)) - 1):
            pw = _dot3(pw, pw)
            tinv = tinv + _dot3(tinv, pw)
        z = _dot3(a_s, g, NT) + _dot3(ak, v_s)
        u = _dot3(tinv, z)
        ybd = _dot3(r_s, g, NT) + _dot3(rb, u) + _dot3(rk, v_s)
        y = ybd[0:L]
        for h in range(1, HG):
            y = y + ybd[h * L:(h + 1) * L]
        y_ref[:, sl] = y
        g_new = (g + _dot3(u, b_s, TN) + _dot3(v_s, k_s, TN)) * dec[L - 1:L, sl]
        g_scr[grp] = g_new

        @pl.when(c == pl.num_programs(1) - 1)
        def _():
            gout_ref[0, grp] = g_new


def _wkv_prompt(seq, nb, t_len, n_rows):
    L = WKV_L
    NG = RWKV_HEADS // WKV_HG
    WD = WKV_HG * RWKV_HEAD
    nc = -(-t_len // L)
    views = [x.reshape(n_rows // nb, nb * RWKV_WIDTH) for x in seq]
    spec = pl.BlockSpec((L, RWKV_WIDTH), lambda b, c: (c, b))
    y, g = pl.pallas_call(
        functools.partial(_wkv_chunk_kernel, t_len=t_len),
        grid=(nb, nc),
        in_specs=[spec] * 6,
        out_specs=[spec, pl.BlockSpec((1, NG, WD, WD), lambda b, c: (b, 0, 0, 0))],
        out_shape=[jax.ShapeDtypeStruct((n_rows // nb, nb * RWKV_WIDTH), F32),
                   jax.ShapeDtypeStruct((nb, NG, WD, WD), F32)],
        scratch_shapes=[pltpu.VMEM((NG, WD, WD), F32)],
        compiler_params=_cp("parallel", "arbitrary"),
        name="wkv_chunk",
    )(*views)
    g = g.reshape(nb, NG, WKV_HG, RWKV_HEAD, WKV_HG, RWKV_HEAD)
    idx = jnp.arange(WKV_HG)
    s = g[:, :, idx, :, idx, :]
    s = jnp.transpose(s, (1, 2, 0, 3, 4)).reshape(nb, RWKV_HEADS, RWKV_HEAD, RWKV_HEAD)
    return y.reshape(n_rows, RWKV_WIDTH), s


def _wkv_step_kernel(r_ref, lw_ref, k_ref, v_ref, a_ref, b_ref, s_ref, yprev_ref, y_ref, sout_ref):
    del yprev_ref
    H, D = RWKV_HEADS, RWKV_HEAD
    ri = lax.broadcasted_iota(jnp.int32, (H * D, D), 0) % D
    ci = lax.broadcasted_iota(jnp.int32, (H * D, D), 1)
    eye = jnp.where(ri == ci, 1.0, 0.0)

    def expand(x):
        return jnp.concatenate([jnp.broadcast_to(x[:, h * D:(h + 1) * D], (D, D)) for h in range(H)], axis=0)

    for j in range(BP):
        row = lambda ref: ref[j:j + 1, :]
        s = s_ref[j].reshape(H * D, D)
        w_e = jnp.exp(expand(row(lw_ref)))
        v_col = jnp.sum(expand(row(v_ref)) * eye, axis=1, keepdims=True)
        sa = jnp.sum(s * expand(row(a_ref)), axis=1, keepdims=True)
        s_new = s * w_e + sa * expand(row(b_ref)) + v_col * expand(row(k_ref))
        y_col = jnp.sum(s_new * expand(row(r_ref)), axis=1, keepdims=True)
        ye = y_col * eye
        y_row = jnp.concatenate([jnp.sum(ye[h * D:(h + 1) * D], axis=0, keepdims=True) for h in range(H)], axis=1)
        y_ref[j:j + 1, :] = y_row
        sout_ref[j] = s_new.reshape(H, D, D)


def _wkv_sample(seq, s0, y_prev, n_prompt):
    n = y_prev.shape[0]
    ns = n - n_prompt
    rb = n_prompt // BP
    spec = pl.BlockSpec((BP, RWKV_WIDTH), lambda i: (rb + i, 0))
    sspec = pl.BlockSpec((BP, RWKV_HEADS, RWKV_HEAD, RWKV_HEAD), lambda i: (i, 0, 0, 0))
    y, s = pl.pallas_call(
        _wkv_step_kernel,
        grid=(ns // BP,),
        in_specs=[spec] * 6 + [sspec, pl.BlockSpec(memory_space=pl.ANY)],
        out_specs=[spec, sspec],
        out_shape=[jax.ShapeDtypeStruct((n, RWKV_WIDTH), F32),
                   jax.ShapeDtypeStruct((ns, RWKV_HEADS, RWKV_HEAD, RWKV_HEAD), F32)],
        input_output_aliases={7: 0},
        compiler_params=_cp("parallel"),
        name="wkv_step",
    )(*seq, s0, y_prev)
    return y, s


def _rwkv_post_kernel(y_ref, r_ref, k_ref, v_ref, g_ref, gng_ref, gnb_ref, rk_ref, ones_ref, o_ref):
    y = y_ref[...]
    ones = ones_ref[...]
    mu = _head_sum(y, ones) * (1.0 / RWKV_HEAD)
    yc = y - mu
    var = _head_sum(yc * yc, ones) * (1.0 / RWKV_HEAD)
    yn = yc * lax.rsqrt(var + RWKV_GN_EPS) * gng_ref[...] + gnb_ref[...]
    v = v_ref[...]
    bonus = _head_sum(r_ref[...] * k_ref[...] * rk_ref[...], ones) * v
    o_ref[...] = (yn + bonus) * g_ref[...]


def _rwkv_post(y, r, k, v, g, prm, li, tm):
    n = y.shape[0]
    W = RWKV_WIDTH
    hid = jnp.arange(W) // RWKV_HEAD
    ones_bd = (hid[:, None] == hid[None, :]).astype(F32)
    row = pl.BlockSpec((tm, W), lambda i: (i, 0))
    vec = pl.BlockSpec((1, W), lambda i: (0, 0))
    return pl.pallas_call(
        _rwkv_post_kernel,
        grid=(n // tm,),
        in_specs=[row] * 5 + [vec] * 3 + [pl.BlockSpec((W, W), lambda i: (0, 0))],
        out_specs=row,
        out_shape=jax.ShapeDtypeStruct((n, W), F32),
        compiler_params=_cp("parallel"),
        name="rwkv_post",
    )(y, r, k, v, g, prm['rwkv_gn_g'][li].reshape(1, -1), prm['rwkv_gn_b'][li].reshape(1, -1),
      prm['rwkv_r_k'][li].reshape(1, -1), ones_bd)


def _mla_prep_kernel(ckv_ref, qa_ref, kr_ref, cos_ref, sin_ref, qn_ref, kvn_ref, wqb_ref, wuk_ref,
                     qlat_ref, qrope_ref, c_ref, kr8_ref):
    qa = qa_ref[:, :Q_LORA]
    qa = qa * lax.rsqrt(jnp.mean(qa * qa, -1, keepdims=True) + 1e-6) * qn_ref[...]
    q = _bdot(qa, wqb_ref[...])
    nope_w = MLA_HEADS * QK_NOPE
    cos, sin = cos_ref[...], sin_ref[...]
    qrope_ref[...] = _rope(q[:, nope_w:], cos, sin, QK_ROPE // 2)
    qlat_ref[...] = _bdot(q[:, :nope_w], wuk_ref[...])
    ckv = ckv_ref[...]
    c_ref[...] = ckv * lax.rsqrt(jnp.mean(ckv * ckv, -1, keepdims=True) + 1e-6) * kvn_ref[...]
    kr8_ref[...] = _rope(kr_ref[...], cos, sin, QK_ROPE // 2)


def _mla_prep(p, pos_rows, prm, li, tm):
    n = p.shape[0]
    rw = MLA_HEADS * QK_ROPE
    cos, sin = _rope_tables(pos_rows, QK_ROPE, MLA_HEADS)
    wqb = prm['mla_w_qb'][li].reshape(Q_LORA, MLA_HEADS, QK_NOPE + QK_ROPE)
    wqb = jnp.concatenate([wqb[:, :, :QK_NOPE].reshape(Q_LORA, -1), wqb[:, :, QK_NOPE:].reshape(Q_LORA, -1)], 1)
    eye = jnp.eye(MLA_HEADS, dtype=F32)
    wuk = jnp.einsum('chn,hg->hngc', prm['mla_w_uk'][li], eye).reshape(MLA_HEADS * QK_NOPE, MLA_HEADS * KV_LORA)
    fix = lambda i: (0, 0)
    c0 = RWKV_PROJ // KV_LORA
    q0 = (RWKV_PROJ + KV_LORA) // 512
    k0 = (RWKV_PROJ + KV_LORA + 512) // rw
    return pl.pallas_call(
        _mla_prep_kernel,
        grid=(n // tm,),
        in_specs=[pl.BlockSpec((tm, KV_LORA), lambda i: (i, c0)), pl.BlockSpec((tm, 512), lambda i: (i, q0)),
                  pl.BlockSpec((tm, rw), lambda i: (i, k0)),
                  pl.BlockSpec((tm, rw), lambda i: (i, 0)), pl.BlockSpec((tm, rw), lambda i: (i, 0)),
                  pl.BlockSpec((1, Q_LORA), fix), pl.BlockSpec((1, KV_LORA), fix),
                  pl.BlockSpec(wqb.shape, fix), pl.BlockSpec(wuk.shape, fix)],
        out_specs=[pl.BlockSpec((tm, MLA_HEADS * KV_LORA), lambda i: (i, 0)), pl.BlockSpec((tm, rw), lambda i: (i, 0)),
                   pl.BlockSpec((tm, KV_LORA), lambda i: (i, 0)), pl.BlockSpec((tm, rw), lambda i: (i, 0))],
        out_shape=[jax.ShapeDtypeStruct((n, MLA_HEADS * KV_LORA), F32), jax.ShapeDtypeStruct((n, rw), F32),
                   jax.ShapeDtypeStruct((n, KV_LORA), F32), jax.ShapeDtypeStruct((n, rw), F32)],
        compiler_params=_cp("parallel"),
        name="mla_prep",
    )(p, p, p, cos, sin, prm['mla_q_norm'][li].reshape(1, -1), prm['mla_kv_norm'][li].reshape(1, -1),
      wqb.astype(BF16), wuk.astype(BF16))


def _mla_attn_kernel(qlat_ref, qrope_ref, c_ref, kr_ref, wuv_ref, o_ref, m_scr, l_scr, acc_scr, *, lq):
    qi, ki = pl.program_id(1), pl.program_id(2)
    H = MLA_HEADS

    @pl.when(ki == 0)
    def _():
        m_scr[...] = jnp.full_like(m_scr, -1e30)
        l_scr[...] = jnp.zeros_like(l_scr)
        acc_scr[...] = jnp.zeros_like(acc_scr)

    @pl.when(ki <= qi)
    def _():
        ql = qlat_ref[...]
        qs = jnp.concatenate([ql[:, h * KV_LORA:(h + 1) * KV_LORA] for h in range(H)], axis=0).astype(BF16)
        qr = qrope_ref[...].astype(BF16)
        lane = lax.broadcasted_iota(jnp.int32, qr.shape, 1) // QK_ROPE
        qrs = jnp.concatenate([jnp.where(lane == h, qr, jnp.zeros_like(qr)) for h in range(H)], axis=0)
        cb = c_ref[...].astype(BF16)
        s = (_dg(qs, cb, NT) + _dg(qrs, kr_ref[...].astype(BF16), NT)) * MLA_SCALE
        qpos = qi * lq + lax.broadcasted_iota(jnp.int32, s.shape, 0) % lq
        kpos = ki * lq + lax.broadcasted_iota(jnp.int32, s.shape, 1)
        s = jnp.where(kpos <= qpos, s, -1e30)
        m_old = m_scr[...]
        m_new = jnp.maximum(m_old, jnp.max(s, axis=1, keepdims=True))
        alpha = jnp.exp(m_old - m_new)
        p = jnp.exp(s - m_new)
        l_scr[...] = alpha * l_scr[...] + jnp.sum(p, axis=1, keepdims=True)
        acc_scr[...] = alpha * acc_scr[...] + _dg(p.astype(BF16), cb, NN)
        m_scr[...] = m_new

    @pl.when(ki == pl.num_programs(2) - 1)
    def _():
        o = acc_scr[...] / l_scr[...]
        ol = jnp.concatenate([o[h * lq:(h + 1) * lq] for h in range(H)], axis=1)
        o_ref[...] = _bdot(ol, wuv_ref[...])


def _mla_attn_prompt(qlat, qrope, c, kr8, w_uv, nb):
    n = qlat.shape[0]
    tv = n // nb
    lq = _div_tile(tv, MLA_TILE)
    nq = tv // lq
    eye = jnp.eye(MLA_HEADS, dtype=F32)
    wuv = jnp.einsum('chv,hg->hcgv', w_uv, eye).reshape(MLA_HEADS * KV_LORA, MLA_WIDTH).astype(BF16)
    rw = MLA_HEADS * QK_ROPE
    v = lambda x: x.reshape(tv, nb * x.shape[1])
    kmap = lambda b, qi, ki: (jnp.minimum(ki, qi), b)
    out = pl.pallas_call(
        functools.partial(_mla_attn_kernel, lq=lq),
        grid=(nb, nq, nq),
        in_specs=[pl.BlockSpec((lq, MLA_HEADS * KV_LORA), lambda b, qi, ki: (qi, b)),
                  pl.BlockSpec((lq, rw), lambda b, qi, ki: (qi, b)),
                  pl.BlockSpec((lq, KV_LORA), kmap), pl.BlockSpec((lq, rw), kmap),
                  pl.BlockSpec(wuv.shape, lambda b, qi, ki: (0, 0))],
        out_specs=pl.BlockSpec((lq, MLA_WIDTH), lambda b, qi, ki: (qi, b)),
        out_shape=jax.ShapeDtypeStruct((tv, nb * MLA_WIDTH), F32),
        scratch_shapes=[pltpu.VMEM((MLA_HEADS * lq, 1), F32), pltpu.VMEM((MLA_HEADS * lq, 1), F32),
                        pltpu.VMEM((MLA_HEADS * lq, KV_LORA), F32)],
        compiler_params=_cp("parallel", "parallel", "arbitrary"),
        name="mla_attn",
    )(v(qlat), v(qrope), v(c), v(kr8), wuv)
    return out.reshape(n, MLA_WIDTH)


def _mla_decode_kernel(pt_ref, qlat_ref, qrope_ref, cnew_ref, krnew_ref, kv_ref, kr_ref, tile_ref, wuv_ref,
                       yprev_ref, o_ref, m_scr, l_scr, acc_scr):
    del pt_ref, yprev_ref
    b, pg = pl.program_id(0), pl.program_id(1)
    H = MLA_HEADS
    j = b % BP
    rid = lax.broadcasted_iota(jnp.int32, (BP, 1), 0)

    def pick(ref):
        return jnp.sum(jnp.where(rid == j, ref[...], 0.0), axis=0, keepdims=True)

    ql = pick(qlat_ref)
    qs = jnp.concatenate([ql[:, h * KV_LORA:(h + 1) * KV_LORA] for h in range(H)], axis=0).astype(BF16)
    qr = jnp.broadcast_to(pick(qrope_ref), (H, H * QK_ROPE))
    hl = lax.broadcasted_iota(jnp.int32, qr.shape, 1) // QK_ROPE
    hr = lax.broadcasted_iota(jnp.int32, qr.shape, 0)
    qrs = jnp.where(hl == hr, qr, 0.0).astype(BF16)

    @pl.when(pg == 0)
    def _():
        cn = pick(cnew_ref).astype(BF16).astype(F32)
        kn = pick(krnew_ref).astype(BF16).astype(F32)
        s0 = jnp.sum(qs.astype(F32) * cn, axis=1, keepdims=True) + jnp.sum(qrs.astype(F32) * kn, axis=1, keepdims=True)
        m_scr[...] = s0 * MLA_SCALE
        l_scr[...] = jnp.ones_like(l_scr)
        acc_scr[...] = jnp.broadcast_to(cn, acc_scr.shape)

    cb = kv_ref[0, 0].astype(BF16)
    kr8 = _dg(kr_ref[0, 0].astype(BF16), tile_ref[...], NN).astype(BF16)
    s = (_dg(qs, cb, NT) + _dg(qrs, kr8, NT)) * MLA_SCALE
    m_old = m_scr[...]
    m_new = jnp.maximum(m_old, jnp.max(s, axis=1, keepdims=True))
    alpha = jnp.exp(m_old - m_new)
    p = jnp.exp(s - m_new)
    l_new = alpha * l_scr[...] + jnp.sum(p, axis=1, keepdims=True)
    acc_new = alpha * acc_scr[...] + _dg(p.astype(BF16), cb, NN)
    l_scr[...] = l_new
    acc_scr[...] = acc_new
    m_scr[...] = m_new

    @pl.when(pg == pl.num_programs(1) - 1)
    def _():
        o = acc_new / l_new
        ol = jnp.concatenate([o[h:h + 1] for h in range(H)], axis=1)
        y = _bdot(jnp.broadcast_to(ol, (BP, ol.shape[1])), wuv_ref[...])
        o_ref[pl.ds(j, 1), :] = y[0:1]


def _mla_decode(qlat, qrope, c, kr8, cache_kv, cache_kr, page_table, w_uv, li, y_prev, n_prompt):
    n = qlat.shape[0]
    ns = n - n_prompt
    n_pages = page_table.shape[1]
    rb = n_prompt // BP
    rw = MLA_HEADS * QK_ROPE
    eye = jnp.eye(MLA_HEADS, dtype=F32)
    wuv = jnp.einsum('chv,hg->hcgv', w_uv, eye).reshape(MLA_HEADS * KV_LORA, MLA_WIDTH).astype(BF16)
    tile = jnp.tile(jnp.eye(QK_ROPE, dtype=F32), (1, MLA_HEADS)).astype(BF16)
    rowblk = lambda w: pl.BlockSpec((BP, w), lambda b, pg, pt: (rb + b // BP, 0))
    grid_spec = pltpu.PrefetchScalarGridSpec(
        num_scalar_prefetch=1,
        grid=(ns, n_pages),
        in_specs=[rowblk(MLA_HEADS * KV_LORA), rowblk(rw), rowblk(KV_LORA), rowblk(rw),
                  pl.BlockSpec((1, 1, PAGE_SIZE, KV_LORA), lambda b, pg, pt: (pt[b, pg], li, 0, 0)),
                  pl.BlockSpec((1, 1, PAGE_SIZE, QK_ROPE), lambda b, pg, pt: (pt[b, pg], li, 0, 0)),
                  pl.BlockSpec(tile.shape, lambda b, pg, pt: (0, 0)),
                  pl.BlockSpec(wuv.shape, lambda b, pg, pt: (0, 0)),
                  pl.BlockSpec(memory_space=pl.ANY)],
        out_specs=rowblk(MLA_WIDTH),
        scratch_shapes=[pltpu.VMEM((MLA_HEADS, 1), F32), pltpu.VMEM((MLA_HEADS, 1), F32),
                        pltpu.VMEM((MLA_HEADS, KV_LORA), F32)],
    )
    return pl.pallas_call(
        _mla_decode_kernel,
        grid_spec=grid_spec,
        out_shape=jax.ShapeDtypeStruct((n, MLA_WIDTH), F32),
        input_output_aliases={9: 0},
        compiler_params=_cp("arbitrary", "arbitrary"),
        name="mla_decode",
    )(page_table, qlat, qrope, c, kr8, cache_kv, cache_kr, tile, wuv, y_prev)


def _forward(x_prompt, x_sample, state_ret, state_s5_re, state_s5_im, state_wkv, state_shift, cache_kv,
             cache_krope, page_table, meta_tokens, prm):
    nb, seq, _ = x_prompt.shape
    assert nb == BP
    ns = x_sample.shape[0]
    assert x_sample.shape[1] == 1 and ns % BP == 0
    tq = seq + N_META
    n_prompt = tq * nb
    n = n_prompt + ns
    assert n_prompt % ns == 0
    past_len = page_table.shape[1] * PAGE_SIZE
    n_even, n_odd = state_ret.shape[0], state_wkv.shape[0]

    meta = jnp.broadcast_to(meta_tokens[:, None, :], (N_META, nb, D_MODEL))
    x = jnp.concatenate([meta, jnp.transpose(x_prompt, (1, 0, 2))], 0).reshape(n_prompt, D_MODEL)
    x = jnp.concatenate([x, x_sample.reshape(ns, D_MODEL)], 0)
    tm = _div_tile(n, 832)
    tm_r = _div_tile(n, 1664, 128)
    pos_p = jnp.arange(tq, dtype=jnp.int32)
    pos_s = jnp.full((1,), past_len, jnp.int32)
    pos_rows = jnp.concatenate([jnp.repeat(pos_p, nb), jnp.full((ns,), past_len, jnp.int32)])

    HK, HV = RET_HEADS * RET_DK, RET_HEADS * RET_DV
    ret_p, ret_s, s5_p, s5_s, wkv_p, wkv_s, sh_p, sh_s, kv_l, kr_l = [], [], [], [], [], [], [], [], [], []
    for layer in range(DEPTH):
        li = layer // 2
        if layer % 2 == 0:
            w_in = prm['ev_w_in'][li].astype(BF16)
            p = _matmul(x, w_in, tm, _div_tile(w_in.shape[1], 1024, 128))
            ein = p.shape[1]
            zero_s = jnp.zeros((nb, RET_HEADS, RET_DK, RET_DV), F32)
            ret, s_p = _retention(p.reshape(n // nb, nb * ein), nb, tq, 0, pos_p, zero_s,
                                  prm['ret_gn_g'][li], prm['ret_gn_b'][li], None, n)
            ret, s_s = _retention(p.reshape(n // ns, ns * ein), ns, 1, n_prompt // ns, pos_s, state_ret[li],
                                  prm['ret_gn_g'][li], prm['ret_gn_b'][li], ret.reshape(n // ns, 1, ns * HV), n)
            ret = ret.reshape(n, HV)
            s5o, h_p = _s5(p, 0, nb, tq, jnp.zeros((nb, 2 * S5_CH), F32), prm, li, None)
            h0 = jnp.concatenate([state_s5_re[li].reshape(ns, S5_CH), state_s5_im[li].reshape(ns, S5_CH)], 1)
            s5o, h_s = _s5(p, n_prompt, ns, 1, h0, prm, li, s5o)
            ret_p.append(s_p)
            ret_s.append(s_s)
            s5_p.append(h_p)
            s5_s.append(h_s)
            x = _proj_ln(ret, s5o, x, prm['ev_w_out'][li], prm['ln1_g'][layer], prm['ln1_b'][layer], tm)
        else:
            w = prm['od_w_in'][li]
            o1 = RWKV_PROJ
            o2 = o1 + Q_LORA
            o3 = o2 + KV_LORA
            w_in = jnp.concatenate([w[:, :o1], w[:, o2:o3], w[:, o1:o2], jnp.zeros((D_MODEL, 512 - Q_LORA), F32),
                                    jnp.tile(w[:, o3:], (1, MLA_HEADS))], 1).astype(BF16)
            p = _matmul(x, w_in, tm, _div_tile(w_in.shape[1], 1408, 128))
            pr_p = p[n_prompt - nb:n_prompt, :RWKV_PROJ]
            pr_s = p[n_prompt:, :RWKV_PROJ]
            sh_p.append(pr_p)
            sh_s.append(pr_s)
            r, lw, k, v, a, b, g = _rwkv_prep(p, n_prompt, state_shift[li], prm, li)
            seqs = (r, lw, k, v, a, b)
            y, g_p = _wkv_prompt(seqs, nb, tq, n)
            y, g_s = _wkv_sample(seqs, state_wkv[li], y, n_prompt)
            wkv_p.append(g_p)
            wkv_s.append(g_s)
            y_c = _rwkv_post(y, r, k, v, g, prm, li, tm)
            qlat, qrope, c, kr8 = _mla_prep(p, pos_rows, prm, li, tm)
            kv_l.append(c)
            kr_l.append(kr8[:, :QK_ROPE])
            y_d = _mla_attn_prompt(qlat, qrope, c, kr8, prm['mla_w_uv'][li], nb)
            y_d = _mla_decode(qlat, qrope, c, kr8, cache_kv, cache_krope, page_table, prm['mla_w_uv'][li], li,
                              y_d, n_prompt)
            x = _proj_ln(y_c, y_d, x, prm['od_w_out'][li], prm['ln1_g'][layer], prm['ln1_b'][layer], tm)
        comb = _router(x, prm['moe_w_coarse'][layer], prm['moe_b_coarse'][layer], prm['moe_w_fine'][layer],
                       prm['moe_b_fine'][layer], tm_r)
        x = _moe_ln(x, comb, prm['moe_w_gate'][layer].astype(BF16), prm['moe_w_up'][layer].astype(BF16),
                    prm['moe_w_down'][layer].astype(BF16), prm['ln2_g'][layer], prm['ln2_b'][layer], tm)

    def tm_to_bm(a):
        return jnp.transpose(a.reshape(tq, nb, a.shape[-1]), (1, 0, 2))

    y_p = tm_to_bm(x[:n_prompt])[:, N_META:]
    y_s = x[n_prompt:].reshape(ns, 1, D_MODEL)

    def s5_state(hs, part):
        return jnp.stack([h[:, part * S5_CH:(part + 1) * S5_CH].reshape(-1, S5_GROUPS, S5_STATE) for h in hs])

    kv_p = jnp.stack([tm_to_bm(c[:n_prompt]) for c in kv_l], 1)
    kv_s = jnp.stack([c[n_prompt:].reshape(ns, 1, KV_LORA) for c in kv_l], 1)
    kr_p = jnp.stack([tm_to_bm(c[:n_prompt]) for c in kr_l], 1)
    kr_s = jnp.stack([c[n_prompt:].reshape(ns, 1, QK_ROPE) for c in kr_l], 1)
    return (y_p, y_s, jnp.stack(ret_p), jnp.stack(ret_s), s5_state(s5_p, 0), s5_state(s5_s, 0),
            s5_state(s5_p, 1), s5_state(s5_s, 1), jnp.stack(wkv_p), jnp.stack(wkv_s),
            jnp.stack(sh_p), jnp.stack(sh_s), kv_p, kv_s, kr_p, kr_s)


def kernel(x_prompt, x_sample, state_ret, state_s5_re, state_s5_im, state_wkv, state_shift, cache_kv, cache_krope, page_table, meta_tokens, ev_w_in, ret_gn_g, ret_gn_b, s5_a_re, s5_a_im, s5_log_dt, s5_b_re, s5_b_im, s5_c_re, s5_c_im, s5_d, s5_w_glu, s5_b_glu, ev_w_out, od_w_in, rwkv_mu, rwkv_w0, rwkv_w2, rwkv_a0, rwkv_a2, rwkv_g2, rwkv_k_k, rwkv_k_a, rwkv_r_k, rwkv_gn_g, rwkv_gn_b, mla_q_norm, mla_w_qb, mla_kv_norm, mla_w_uk, mla_w_uv, od_w_out, ln1_g, ln1_b, ln2_g, ln2_b, moe_w_coarse, moe_b_coarse, moe_w_fine, moe_b_fine, moe_w_gate, moe_w_up, moe_w_down):
    prm = dict(ev_w_in=ev_w_in, ret_gn_g=ret_gn_g, ret_gn_b=ret_gn_b, s5_a_re=s5_a_re, s5_a_im=s5_a_im,
               s5_log_dt=s5_log_dt, s5_b_re=s5_b_re, s5_b_im=s5_b_im, s5_c_re=s5_c_re, s5_c_im=s5_c_im,
               s5_d=s5_d, s5_w_glu=s5_w_glu, s5_b_glu=s5_b_glu, ev_w_out=ev_w_out,
               od_w_in=od_w_in, rwkv_mu=rwkv_mu, rwkv_w0=rwkv_w0, rwkv_w2=rwkv_w2, rwkv_a0=rwkv_a0,
               rwkv_a2=rwkv_a2, rwkv_g2=rwkv_g2, rwkv_k_k=rwkv_k_k, rwkv_k_a=rwkv_k_a, rwkv_r_k=rwkv_r_k,
               rwkv_gn_g=rwkv_gn_g, rwkv_gn_b=rwkv_gn_b, mla_q_norm=mla_q_norm, mla_w_qb=mla_w_qb,
               mla_kv_norm=mla_kv_norm, mla_w_uk=mla_w_uk, mla_w_uv=mla_w_uv, od_w_out=od_w_out,
               ln1_g=ln1_g, ln1_b=ln1_b, ln2_g=ln2_g, ln2_b=ln2_b, moe_w_coarse=moe_w_coarse,
               moe_b_coarse=moe_b_coarse, moe_w_fine=moe_w_fine, moe_b_fine=moe_b_fine,
               moe_w_gate=moe_w_gate, moe_w_up=moe_w_up, moe_w_down=moe_w_down)
    return _forward(x_prompt, x_sample, state_ret, state_s5_re, state_s5_im, state_wkv, state_shift, cache_kv,
                    cache_krope, page_table, meta_tokens, prm)
```

```python
import functools
import math

import jax
import jax.numpy as jnp
from jax import lax
from jax.experimental import pallas as pl
from jax.experimental.pallas import tpu as pltpu

F32 = jnp.float32
BF16 = jnp.bfloat16

D_MODEL = 1024
N_META = 16
PAGE_SIZE = 128
RET_HEADS, RET_DK, RET_DV = 4, 64, 128
S5_WIDTH, S5_GROUP, S5_STATE = 512, 16, 64
S5_GROUPS = S5_WIDTH // S5_GROUP
S5_CH = S5_GROUPS * S5_STATE
RWKV_HEAD, RWKV_WIDTH = 64, 512
RWKV_HEADS = RWKV_WIDTH // RWKV_HEAD
LORA_W, LORA_A, LORA_G = 64, 64, 128
RWKV_PROJ = 3 * RWKV_WIDTH + LORA_W + LORA_A + LORA_G
RWKV_GN_EPS = 64e-5
MLA_HEADS, Q_LORA, KV_LORA, QK_NOPE, QK_ROPE, V_HEAD = 8, 384, 256, 64, 32, 64
MLA_WIDTH = MLA_HEADS * V_HEAD
MLA_SCALE = (QK_NOPE + QK_ROPE) ** -0.5
ROPE_THETA = 10000.0
N_EGROUPS, E_PER_GROUP, D_EXPERT = 4, 4, 256
N_EXPERTS = N_EGROUPS * E_PER_GROUP
DEPTH = 4
DN_ALPHA = (2 * DEPTH) ** 0.25
BP = 8

VMEM_LIMIT = 56 * 1024 * 1024
WKV_L = 32
WKV_HG = 4
WKV_NB = 2
MLA_TILE = 416
SEQ_TILE = 344

NT = (((1,), (1,)), ((), ()))
TN = (((0,), (0,)), ((), ()))
NN = (((1,), (0,)), ((), ()))


def _cp(*sem):
    return pltpu.CompilerParams(dimension_semantics=sem, vmem_limit_bytes=VMEM_LIMIT)


def _div_tile(n, cap, mult=8):
    best = None
    for d in range(mult, min(n, cap) + 1, mult):
        if n % d == 0:
            best = d
    assert best is not None, (n, cap, mult)
    return best


def _dg(a, b, dims):
    return lax.dot_general(a, b, dims, preferred_element_type=F32)


def _bdot(a, b, dims=NN):
    return _dg(a.astype(BF16), b.astype(BF16), dims)


def _dot3(a, b, dims=NN):
    ah = a.astype(BF16)
    al = (a - ah.astype(F32)).astype(BF16)
    bh = b.astype(BF16)
    bl = (b - bh.astype(F32)).astype(BF16)
    return _dg(ah, bh, dims) + (_dg(ah, bl, dims) + _dg(al, bh, dims))


def _layer_norm(z, g, b, eps=1e-5):
    mu = jnp.mean(z, -1, keepdims=True)
    zc = z - mu
    var = jnp.mean(zc * zc, -1, keepdims=True)
    return zc * lax.rsqrt(var + eps) * g + b


def _rope(x, cos, sin_signed, half):
    w = x.shape[-1]
    lane = lax.broadcasted_iota(jnp.int32, x.shape, 1)
    first = (lane % (2 * half)) < half
    other = jnp.where(first, pltpu.roll(x, w - half, 1), pltpu.roll(x, half, 1))
    return x * cos + other * sin_signed


def _rope_tables(pos, dim, reps):
    inv = ROPE_THETA ** (-jnp.arange(0, dim, 2, dtype=F32) / dim)
    ang = pos.astype(F32)[:, None] * inv[None, :]
    cos = jnp.cos(ang)
    sin = jnp.sin(ang)
    cos = jnp.tile(jnp.concatenate([cos, cos], -1), (1, reps))
    sin = jnp.tile(jnp.concatenate([-sin, sin], -1), (1, reps))
    return cos, sin


def _mm_kernel(x_ref, w_ref, o_ref):
    o_ref[...] = _dg(x_ref[...].astype(BF16), w_ref[...], NN)


def _matmul(x, w, tm, tn):
    n, k = x.shape
    m = w.shape[1]
    return pl.pallas_call(
        _mm_kernel,
        grid=(n // tm, m // tn),
        in_specs=[pl.BlockSpec((tm, k), lambda i, j: (i, 0)), pl.BlockSpec((k, tn), lambda i, j: (0, j))],
        out_specs=pl.BlockSpec((tm, tn), lambda i, j: (i, j)),
        out_shape=jax.ShapeDtypeStruct((n, m), F32),
        compiler_params=_cp("parallel", "arbitrary"),
        name="proj_in",
    )(x, w)


def _proj_ln_kernel(a1_ref, a2_ref, x_ref, w1_ref, w2_ref, g_ref, b_ref, o_ref):
    mix = _dg(a1_ref[...].astype(BF16), w1_ref[...], NN) + _dg(a2_ref[...].astype(BF16), w2_ref[...], NN)
    z = DN_ALPHA * x_ref[...] + mix
    o_ref[...] = _layer_norm(z, g_ref[...], b_ref[...])


def _proj_ln(a1, a2, x, w_out, g, b, tm):
    n = x.shape[0]
    k1, k2 = a1.shape[1], a2.shape[1]
    w1 = w_out[:k1].astype(BF16)
    w2 = w_out[k1:].astype(BF16)
    row = lambda i: (i, 0)
    fix = lambda i: (0, 0)
    return pl.pallas_call(
        _proj_ln_kernel,
        grid=(n // tm,),
        in_specs=[pl.BlockSpec((tm, k1), row), pl.BlockSpec((tm, k2), row), pl.BlockSpec((tm, D_MODEL), row),
                  pl.BlockSpec((k1, D_MODEL), fix), pl.BlockSpec((k2, D_MODEL), fix),
                  pl.BlockSpec((1, D_MODEL), fix), pl.BlockSpec((1, D_MODEL), fix)],
        out_specs=pl.BlockSpec((tm, D_MODEL), row),
        out_shape=jax.ShapeDtypeStruct((n, D_MODEL), F32),
        compiler_params=_cp("parallel"),
        name="proj_out_ln",
    )(a1, a2, x, w1, w2, g.reshape(1, -1), b.reshape(1, -1))


def _router_kernel(x_ref, w_ref, b_ref, o_ref):
    lt = _bdot(w_ref[...], x_ref[...], NT) + b_ref[...]
    lc = [lt[g:g + 1] for g in range(N_EGROUPS)]
    m = functools.reduce(jnp.maximum, lc)
    taken = jnp.zeros_like(m) > 1.0
    sel = []
    for g in range(N_EGROUPS):
        s = jnp.logical_and(lc[g] == m, jnp.logical_not(taken))
        sel.append(s)
        taken = jnp.logical_or(taken, s)
    gate_c = 1.0 / functools.reduce(lambda a, c: a + c, [jnp.exp(v - m) for v in lc])
    lf = []
    for e in range(E_PER_GROUP):
        v = jnp.zeros_like(m)
        for g in range(N_EGROUPS):
            r = N_EGROUPS + g * E_PER_GROUP + e
            v = jnp.where(sel[g], lt[r:r + 1], v)
        lf.append(v)
    pick = []
    for e in range(E_PER_GROUP):
        rank = jnp.zeros_like(m)
        for e2 in range(E_PER_GROUP):
            if e2 == e:
                continue
            ahead = (lf[e2] > lf[e]) if e2 > e else (lf[e2] >= lf[e])
            rank = rank + jnp.where(ahead, 1.0, 0.0)
        pick.append(rank < 2.0)
    t1 = functools.reduce(jnp.maximum, lf)
    ex = [jnp.where(pick[e], jnp.exp(lf[e] - t1), 0.0) for e in range(E_PER_GROUP)]
    scale = gate_c / functools.reduce(lambda a, c: a + c, ex)
    rows = []
    for g in range(N_EGROUPS):
        for e in range(E_PER_GROUP):
            rows.append(jnp.where(sel[g], ex[e] * scale, 0.0))
    o_ref[...] = jnp.concatenate(rows, axis=0)


def _router(x, w_coarse, b_coarse, w_fine, b_fine, tm):
    n = x.shape[0]
    nr = 32
    wt = jnp.concatenate([w_coarse.T, jnp.transpose(w_fine, (0, 2, 1)).reshape(N_EXPERTS, D_MODEL)], 0)
    wt = jnp.pad(wt, ((0, nr - wt.shape[0]), (0, 0)))
    bt = jnp.pad(jnp.concatenate([b_coarse, b_fine.reshape(-1)]), (0, nr - N_EGROUPS - N_EXPERTS)).reshape(nr, 1)
    comb_t = pl.pallas_call(
        _router_kernel,
        grid=(n // tm,),
        in_specs=[pl.BlockSpec((tm, D_MODEL), lambda i: (i, 0)), pl.BlockSpec((nr, D_MODEL), lambda i: (0, 0)),
                  pl.BlockSpec((nr, 1), lambda i: (0, 0))],
        out_specs=pl.BlockSpec((N_EXPERTS, tm), lambda i: (0, i)),
        out_shape=jax.ShapeDtypeStruct((N_EXPERTS, n), F32),
        compiler_params=_cp("parallel"),
        name="router",
    )(x, wt, bt)
    return comb_t.T


def _moe_kernel(x_ref, comb_ref, wg_ref, wu_ref, wd_ref, g_ref, b_ref, o_ref, xb_scr, acc_scr):
    e = pl.program_id(1)

    @pl.when(e == 0)
    def _():
        xb_scr[...] = x_ref[...].astype(BF16)
        acc_scr[...] = jnp.zeros_like(acc_scr)

    xb = xb_scr[...]
    hg = _dg(xb, wg_ref[0], NN)
    hu = _dg(xb, wu_ref[0], NN)
    comb = comb_ref[...]
    lane = lax.broadcasted_iota(jnp.int32, comb.shape, 1)
    ce = jnp.sum(jnp.where(lane == e, comb, 0.0), axis=1, keepdims=True)
    h = (hg * jax.nn.sigmoid(hg)) * hu * ce
    acc_scr[...] += _dg(h.astype(BF16), wd_ref[0], NN)

    @pl.when(e == pl.num_programs(1) - 1)
    def _():
        z = DN_ALPHA * x_ref[...] + acc_scr[...]
        o_ref[...] = _layer_norm(z, g_ref[...], b_ref[...])


def _moe_ln(x, comb, wg, wu, wd, g, b, tm):
    n = x.shape[0]
    row = lambda i, e: (i, 0)
    fix = lambda i, e: (0, 0)
    return pl.pallas_call(
        _moe_kernel,
        grid=(n // tm, N_EXPERTS),
        in_specs=[pl.BlockSpec((tm, D_MODEL), row), pl.BlockSpec((tm, N_EXPERTS), row),
                  pl.BlockSpec((1, D_MODEL, D_EXPERT), lambda i, e: (e, 0, 0)),
                  pl.BlockSpec((1, D_MODEL, D_EXPERT), lambda i, e: (e, 0, 0)),
                  pl.BlockSpec((1, D_EXPERT, D_MODEL), lambda i, e: (e, 0, 0)),
                  pl.BlockSpec((1, D_MODEL), fix), pl.BlockSpec((1, D_MODEL), fix)],
        out_specs=pl.BlockSpec((tm, D_MODEL), row),
        out_shape=jax.ShapeDtypeStruct((n, D_MODEL), F32),
        scratch_shapes=[pltpu.VMEM((tm, D_MODEL), BF16), pltpu.VMEM((tm, D_MODEL), F32)],
        compiler_params=_cp("parallel", "arbitrary"),
        name="moe_ln",
    )(x, comb, wg, wu, wd, g.reshape(1, -1), b.reshape(1, -1))


def _ret_kernel(q_ref, k_ref, v_ref, gate_ref, cos_ref, sin_ref, dmask_ref, cdec_ref, kdec_ref, sdec_ref,
                gng_ref, gnb_ref, s0_ref, o_ref, sout_ref, s_scr, *, L, Lblk):
    c = pl.program_id(1)
    HK, HV = RET_HEADS * RET_DK, RET_HEADS * RET_DV

    @pl.when(c == 0)
    def _():
        rows = []
        for h in range(RET_HEADS):
            pieces = [s0_ref[0, h] if h2 == h else jnp.zeros((RET_DK, RET_DV), F32) for h2 in range(RET_HEADS)]
            rows.append(jnp.concatenate(pieces, axis=1))
        s_scr[...] = jnp.concatenate(rows, axis=0)

    def load(ref, w):
        x = ref[0] if len(ref.shape) == 3 else ref[...]
        if Lblk < L:
            rid = lax.broadcasted_iota(jnp.int32, (L, w), 0)
            x = jnp.where(rid < Lblk, jnp.broadcast_to(x, (L, w)), 0.0)
        return x

    cos, sin = load(cos_ref, HK), load(sin_ref, HK)
    q = _rope(load(q_ref, HK), cos, sin, RET_DK // 2) * RET_DK ** -0.5
    k = _rope(load(k_ref, HK), cos, sin, RET_DK // 2)
    v = load(v_ref, HV)
    gate = load(gate_ref, HV)
    qb, kb, vb = q.astype(BF16), k.astype(BF16), v.astype(BF16)
    sbd = s_scr[...]
    cross = _dg(qb, sbd.astype(BF16), NN) * cdec_ref[...]
    lane = lax.broadcasted_iota(jnp.int32, (L, HK), 1)
    outs = []
    for h in range(RET_HEADS):
        qh = jnp.where(lane // RET_DK == h, qb, jnp.zeros_like(qb))
        sc = _dg(qh, kb, NT) * dmask_ref[h]
        sl = slice(h * RET_DV, (h + 1) * RET_DV)
        o = _dg(sc.astype(BF16), vb[:, sl], NN) + cross[:, sl]
        mu = jnp.mean(o, -1, keepdims=True)
        oc = o - mu
        var = jnp.mean(oc * oc, -1, keepdims=True)
        outs.append(oc * lax.rsqrt(var + 1e-5))
    on = jnp.concatenate(outs, axis=1) * gng_ref[...] + gnb_ref[...]
    res = (gate * jax.nn.sigmoid(gate)) * on
    if len(o_ref.shape) == 3:
        o_ref[0] = res[:Lblk]
    else:
        o_ref[...] = res
    upd = _dg((k * kdec_ref[...]).astype(BF16), vb, TN)
    ri = lax.broadcasted_iota(jnp.int32, (HK, HV), 0) // RET_DK
    ci = lax.broadcasted_iota(jnp.int32, (HK, HV), 1) // RET_DV
    s_new = sdec_ref[...] * sbd + jnp.where(ri == ci, upd, 0.0)
    s_scr[...] = s_new

    @pl.when(c == pl.num_programs(1) - 1)
    def _():
        for h in range(RET_HEADS):
            sout_ref[0, h] = s_new[h * RET_DK:(h + 1) * RET_DK, h * RET_DV:(h + 1) * RET_DV]


def _retention(p2d, nb, t_len, row0, pos, s0, gn_g, gn_b, out_prev, n_rows):
    HK, HV = RET_HEADS * RET_DK, RET_HEADS * RET_DV
    if t_len >= 8:
        L = Lblk = _div_tile(t_len, SEQ_TILE)
    else:
        assert t_len == 1
        L, Lblk = 8, 1
    nc = t_len // Lblk
    assert row0 % Lblk == 0
    r0 = row0 // Lblk
    log_g = jnp.log1p(-jnp.exp2(-5.0 - jnp.arange(RET_HEADS, dtype=F32)))
    idx = jnp.arange(L, dtype=F32)
    valid = (jnp.arange(L) < Lblk)
    diff = idx[:, None] - idx[None, :]
    dmask = jnp.where(diff >= 0, jnp.exp(log_g[:, None, None] * jnp.maximum(diff, 0.0)), 0.0)
    cdec = jnp.repeat(jnp.exp(log_g[None, :] * (idx + 1.0)[:, None]), RET_DV, axis=1)
    kdec = jnp.repeat(jnp.exp(log_g[None, :] * (Lblk - 1.0 - idx)[:, None]), RET_DK, axis=1)
    kdec = jnp.where(valid[:, None], kdec, 0.0)
    sdec = jnp.broadcast_to(jnp.repeat(jnp.exp(log_g * Lblk), RET_DK)[:, None], (HK, HV))
    cos, sin = _rope_tables(pos, RET_DK, RET_HEADS)
    ein = 2 * HK + 2 * HV + S5_WIDTH
    qb, vb = ein // HK, ein // HV
    fix2 = lambda b, c: (0, 0)
    kern = functools.partial(_ret_kernel, L=L, Lblk=Lblk)
    if Lblk == 1:
        p2d = p2d.reshape(p2d.shape[0], 1, p2d.shape[1])

        def rspec(w, col):
            return pl.BlockSpec((1, 1, w), lambda b, c: (r0 + c, 0, col(b)))
    else:
        def rspec(w, col):
            return pl.BlockSpec((Lblk, w), lambda b, c: (r0 + c, col(b)))
    in_specs = [rspec(HK, lambda b: b * qb), rspec(HK, lambda b: b * qb + 1),
                rspec(HV, lambda b: b * vb + 1), rspec(HV, lambda b: b * vb + 2),
                pl.BlockSpec((Lblk, HK), lambda b, c: (c, 0)), pl.BlockSpec((Lblk, HK), lambda b, c: (c, 0)),
                pl.BlockSpec((RET_HEADS, L, L), lambda b, c: (0, 0, 0)),
                pl.BlockSpec((L, HV), fix2), pl.BlockSpec((L, HK), fix2), pl.BlockSpec((HK, HV), fix2),
                pl.BlockSpec((1, HV), fix2), pl.BlockSpec((1, HV), fix2),
                pl.BlockSpec((1, RET_HEADS, RET_DK, RET_DV), lambda b, c: (b, 0, 0, 0))]
    args = [p2d, p2d, p2d, p2d, cos, sin, dmask, cdec, kdec, sdec, gn_g.reshape(1, -1), gn_b.reshape(1, -1), s0]
    aliases = {}
    if out_prev is not None:
        in_specs.append(pl.BlockSpec(memory_space=pl.ANY))
        args.append(out_prev)
        aliases = {len(args) - 1: 0}
        kern = functools.partial(_drop_alias_arg, kern, len(args) - 1)
    out, s_new = pl.pallas_call(
        kern,
        grid=(nb, nc),
        in_specs=in_specs,
        out_specs=[rspec(HV, lambda b: b),
                   pl.BlockSpec((1, RET_HEADS, RET_DK, RET_DV), lambda b, c: (b, 0, 0, 0))],
        out_shape=[jax.ShapeDtypeStruct((n_rows // nb, 1, nb * HV) if Lblk == 1 else (n_rows // nb, nb * HV), F32),
                   jax.ShapeDtypeStruct((nb, RET_HEADS, RET_DK, RET_DV), F32)],
        scratch_shapes=[pltpu.VMEM((HK, HV), F32)],
        input_output_aliases=aliases,
        compiler_params=_cp("parallel", "arbitrary"),
        name="retention",
    )(*args)
    return out, s_new


def _drop_alias_arg(kern, idx, *refs):
    return kern(*(refs[:idx] + refs[idx + 1:]))


def _s5_kernel(u_ref, bb_ref, cc_ref, lam_ref, d_ref, wglu_ref, bglu_ref, h0_ref, o_ref, hout_ref, bu_scr, h_scr,
               *, bs, lt):
    i = pl.program_id(0)

    @pl.when(i == 0)
    def _():
        h_scr[...] = h0_ref[...]

    u = u_ref[...]
    bu_scr[...] = _dg(u.astype(BF16), bb_ref[...], NN)
    CW = 512
    for j in range(S5_CH // CW):
        re_sl = pl.ds(j * CW, CW)
        im_sl = pl.ds(S5_CH + j * CW, CW)
        lr = jnp.broadcast_to(lam_ref[:, j * CW:(j + 1) * CW], (bs, CW))
        li = jnp.broadcast_to(lam_ref[:, S5_CH + j * CW:S5_CH + (j + 1) * CW], (bs, CW))

        def step(t, carry):
            hr, hi = carry
            rows = pl.ds(pl.multiple_of(t * bs, 8), bs)
            nr = lr * hr - li * hi + bu_scr[rows, re_sl]
            ni = lr * hi + li * hr + bu_scr[rows, im_sl]
            bu_scr[rows, re_sl] = nr
            bu_scr[rows, im_sl] = ni
            return nr, ni

        hr, hi = lax.fori_loop(0, lt, step, (h_scr[:, re_sl], h_scr[:, im_sl]))
        h_scr[:, re_sl] = hr
        h_scr[:, im_sl] = hi
    y = _dg(bu_scr[...].astype(BF16), cc_ref[...], NN) + d_ref[...] * u
    z = 0.5 * y * (1.0 + jnp.tanh(math.sqrt(2.0 / math.pi) * (y + 0.044715 * (y * y * y))))
    gl = _dg(z.astype(BF16), wglu_ref[...], NN) + bglu_ref[...]
    o_ref[...] = z * jax.nn.sigmoid(gl)

    @pl.when(i == pl.num_programs(0) - 1)
    def _():
        hout_ref[...] = h_scr[...]


def _s5(p, row0, bs, t_len, h0, prm, li, out_prev):
    n = p.shape[0]
    f32 = F32
    a_re, a_im = prm['s5_a_re'][li].astype(f32), prm['s5_a_im'][li].astype(f32)
    dt = jnp.exp(prm['s5_log_dt'][li].astype(f32))[:, None]
    mag = jnp.exp(dt * a_re)
    ab_re, ab_im = mag * jnp.cos(dt * a_im), mag * jnp.sin(dt * a_im)
    den = a_re * a_re + a_im * a_im
    n_re = ab_re - 1.0
    f_re = (n_re * a_re + ab_im * a_im) / den
    f_im = (ab_im * a_re - n_re * a_im) / den
    b_re, b_im = prm['s5_b_re'][li].astype(f32), prm['s5_b_im'][li].astype(f32)
    bb_re = f_re[..., None] * b_re - f_im[..., None] * b_im
    bb_im = f_re[..., None] * b_im + f_im[..., None] * b_re
    eye = jnp.eye(S5_GROUPS, dtype=f32)

    def blockdiag_in(w):
        return jnp.einsum('gpc,gh->gchp', w, eye).reshape(S5_WIDTH, S5_CH)

    def blockdiag_out(w):
        return jnp.einsum('gcp,gh->gphc', w, eye).reshape(S5_CH, S5_WIDTH)

    bbw = jnp.concatenate([blockdiag_in(bb_re), blockdiag_in(bb_im)], 1).astype(BF16)
    ccw = jnp.concatenate([blockdiag_out(prm['s5_c_re'][li]), -blockdiag_out(prm['s5_c_im'][li])], 0).astype(BF16)
    lam = jnp.concatenate([ab_re.reshape(1, -1), ab_im.reshape(1, -1)], 1)
    lt = _div_tile(t_len, max(1, SEQ_TILE // bs), 1)
    rows = bs * lt
    assert row0 % rows == 0
    r0 = row0 // rows
    ein = p.shape[1]
    ucol = ein // S5_WIDTH - 1
    fix = lambda i: (0, 0)
    in_specs = [pl.BlockSpec((rows, S5_WIDTH), lambda i: (r0 + i, ucol)),
                pl.BlockSpec((S5_WIDTH, 2 * S5_CH), fix), pl.BlockSpec((2 * S5_CH, S5_WIDTH), fix),
                pl.BlockSpec((1, 2 * S5_CH), fix), pl.BlockSpec((1, S5_WIDTH), fix),
                pl.BlockSpec((S5_WIDTH, S5_WIDTH), fix), pl.BlockSpec((1, S5_WIDTH), fix),
                pl.BlockSpec((bs, 2 * S5_CH), fix)]
    args = [p, bbw, ccw, lam, prm['s5_d'][li].reshape(1, -1), prm['s5_w_glu'][li].astype(BF16),
            prm['s5_b_glu'][li].reshape(1, -1), h0]
    kern = functools.partial(_s5_kernel, bs=bs, lt=lt)
    aliases = {}
    if out_prev is not None:
        in_specs.append(pl.BlockSpec(memory_space=pl.ANY))
        args.append(out_prev)
        aliases = {len(args) - 1: 0}
        kern = functools.partial(_drop_alias_arg, kern, len(args) - 1)
    out, h_new = pl.pallas_call(
        kern,
        grid=(t_len // lt,),
        in_specs=in_specs,
        out_specs=[pl.BlockSpec((rows, S5_WIDTH), lambda i: (r0 + i, 0)), pl.BlockSpec((bs, 2 * S5_CH), fix)],
        out_shape=[jax.ShapeDtypeStruct((n, S5_WIDTH), F32), jax.ShapeDtypeStruct((bs, 2 * S5_CH), F32)],
        scratch_shapes=[pltpu.VMEM((rows, 2 * S5_CH), F32), pltpu.VMEM((bs, 2 * S5_CH), F32)],
        input_output_aliases=aliases,
        compiler_params=_cp("arbitrary"),
        name="s5",
    )(*args)
    return out, h_new


def _head_sum(x, ones_bd):
    return _dot3(x, ones_bd)


def _rwkv_prep_body(pr, prev, mu_ref, w0_ref, w2_ref, a0_ref, a2_ref, g2_ref, kk_ref, ka_ref, ones_ref,
                    r_ref, lw_ref, k_ref, v_ref, a_ref, b_ref, g_ref):
    W = RWKV_WIDTH
    xm = pr + (prev - pr) * mu_ref[...]
    r = xm[:, :W]
    k = xm[:, W:2 * W]
    v = xm[:, 2 * W:3 * W]
    xwa = xm[:, 3 * W:3 * W + LORA_W + LORA_A]
    xg = xm[:, 3 * W + LORA_W + LORA_A:]
    wl = w0_ref[...] + _bdot(jnp.tanh(xwa), w2_ref[...])
    sp = jnp.maximum(-wl, 0.0) + jnp.log(1.0 + jnp.exp(-jnp.abs(wl)))
    w = -sp - 0.5
    lw = -jnp.exp(w)
    a = jax.nn.sigmoid(a0_ref[...] + _bdot(xwa, a2_ref[...]))
    g = _bdot(jax.nn.sigmoid(xg), g2_ref[...])
    kk = k * kk_ref[...]
    nrm = jnp.sqrt(_head_sum(kk * kk, ones_ref[...]))
    kk = kk / jnp.maximum(nrm, 1e-12)
    k2 = k * (1.0 + (a - 1.0) * ka_ref[...])
    r_ref[...] = r
    lw_ref[...] = lw
    k_ref[...] = k2
    v_ref[...] = v
    a_ref[...] = -kk
    b_ref[...] = kk * a
    g_ref[...] = g


def _rwkv_prep_prompt_kernel(pr_ref, prev8_ref, *rest):
    pr = pr_ref[...]
    first = jnp.where(pl.program_id(0) == 0, 0.0, 1.0)
    prev = jnp.concatenate([prev8_ref[...] * first, pr[:-BP]], axis=0)
    _rwkv_prep_body(pr, prev, *rest)


def _rwkv_prep_sample_kernel(pr_ref, prev_ref, *rest):
    _rwkv_prep_body(pr_ref[...], prev_ref[...], *rest[:9], *rest[9 + 7:])


def _rwkv_prep(p, n_prompt, shift_s, prm, li):
    n = p.shape[0]
    ns = n - n_prompt
    W = RWKV_WIDTH
    zeros64 = jnp.zeros((LORA_W, W), F32)
    w2p = jnp.concatenate([prm['rwkv_w2'][li], zeros64], 0).astype(BF16)
    a2p = jnp.concatenate([zeros64, prm['rwkv_a2'][li]], 0).astype(BF16)
    hid = jnp.arange(W) // RWKV_HEAD
    ones_bd = (hid[:, None] == hid[None, :]).astype(F32)
    consts = [prm['rwkv_mu'][li].reshape(1, -1), prm['rwkv_w0'][li].reshape(1, -1), w2p,
              prm['rwkv_a0'][li].reshape(1, -1), a2p, prm['rwkv_g2'][li].astype(BF16),
              prm['rwkv_k_k'][li].reshape(1, -1), prm['rwkv_k_a'][li].reshape(1, -1), ones_bd]
    fix = lambda i: (0, 0)
    cspecs = [pl.BlockSpec(c.shape, fix) for c in consts]
    outs_shape = [jax.ShapeDtypeStruct((n, W), F32)] * 7
    tm = _div_tile(n_prompt, 700)
    outs = pl.pallas_call(
        _rwkv_prep_prompt_kernel,
        grid=(n_prompt // tm,),
        in_specs=[pl.BlockSpec((tm, RWKV_PROJ), lambda i: (i, 0)),
                  pl.BlockSpec((BP, RWKV_PROJ), lambda i: (jnp.maximum(i * (tm // BP) - 1, 0), 0))] + cspecs,
        out_specs=[pl.BlockSpec((tm, W), lambda i: (i, 0))] * 7,
        out_shape=outs_shape,
        compiler_params=_cp("parallel"),
        name="rwkv_prep_prompt",
    )(p, p, *consts)
    assert n_prompt % ns == 0
    sb = n_prompt // ns
    nin = 2 + len(consts)
    outs = pl.pallas_call(
        _rwkv_prep_sample_kernel,
        grid=(1,),
        in_specs=[pl.BlockSpec((ns, RWKV_PROJ), lambda i: (sb, 0)), pl.BlockSpec((ns, RWKV_PROJ), fix)] + cspecs
        + [pl.BlockSpec(memory_space=pl.ANY)] * 7,
        out_specs=[pl.BlockSpec((ns, W), lambda i: (sb, 0))] * 7,
        out_shape=outs_shape,
        input_output_aliases={nin + j: j for j in range(7)},
        compiler_params=_cp("arbitrary"),
        name="rwkv_prep_sample",
    )(p, shift_s, *consts, *outs)
    return outs


def _wkv_chunk_kernel(r_ref, lw_ref, k_ref, v_ref, a_ref, b_ref, y_ref, gout_ref, g_scr, *, t_len):
    c = pl.program_id(1)
    L, HG = WKV_L, WKV_HG
    NG = RWKV_HEADS // HG
    WD = HG * RWKV_HEAD
    M = HG * L

    @pl.when(c == 0)
    def _():
        g_scr[...] = jnp.zeros_like(g_scr)

    width = WKV_NB * RWKV_WIDTH
    row = lax.broadcasted_iota(jnp.int32, (L, width), 0) + c * L
    valid = row < t_len

    def ld(ref):
        return jnp.where(valid, ref[...], 0.0)

    r, lw, k, v, a, b = ld(r_ref), ld(lw_ref), ld(k_ref), ld(v_ref), ld(a_ref), ld(b_ref)
    ti = lax.broadcasted_iota(jnp.int32, (L, L), 0)
    tj = lax.broadcasted_iota(jnp.int32, (L, L), 1)
    tri = jnp.where(ti >= tj, 1.0, 0.0)
    cum = _dot3(tri, lw)
    dec = jnp.exp(cum)
    inv = jnp.exp(-cum)
    at = a * jnp.exp(cum - lw)
    bt = b * inv
    kt = k * inv
    rt = r * dec
    rh = lax.broadcasted_iota(jnp.int32, (M, WD), 0) // L
    lh = lax.broadcasted_iota(jnp.int32, (M, WD), 1) // RWKV_HEAD
    bd = rh == lh
    mi = lax.broadcasted_iota(jnp.int32, (M, M), 0)
    mj = lax.broadcasted_iota(jnp.int32, (M, M), 1)
    same = (mi // L) == (mj // L)
    m_strict = jnp.logical_and(same, (mj % L) < (mi % L))
    m_incl = jnp.logical_and(same, (mj % L) <= (mi % L))
    eye = jnp.where(mi == mj, 1.0, 0.0)

    def stack(x):
        return jnp.where(bd, jnp.concatenate([x] * HG, axis=0), 0.0)

    g_last = []
    for grp in range(WKV_NB * NG):
        sl = slice(grp * WD, (grp + 1) * WD)
        a_s, r_s, b_s, k_s, v_s = stack(at[:, sl]), stack(rt[:, sl]), stack(bt[:, sl]), stack(kt[:, sl]), stack(v[:, sl])
        g = g_scr[grp]
        ab = jnp.where(m_strict, _dot3(a_s, b_s, NT), 0.0)
        ak = jnp.where(m_strict, _bdot(a_s, k_s, NT), 0.0)
        rbk = _bdot(r_s, jnp.concatenate([b_s, k_s], axis=0), NT)
        rb = jnp.where(m_incl, rbk[:, :M], 0.0)
        rk = jnp.where(m_incl, rbk[:, M:], 0.0)
        tinv = eye + ab
        pw = ab
        for _ in range(int(math.log2(L)) - 1):
            pw = _dot3(pw, pw)
            tinv = tinv + _dot3(tinv, pw)
        z = _bdot(a_s, g, NT) + _bdot(ak, v_s)
        u = _dot3(tinv, z)
        ybd = _bdot(r_s, g, NT) + _bdot(rb, u) + _bdot(rk, v_s)
        y = ybd[0:L]
        for h in range(1, HG):
            y = y + ybd[h * L:(h + 1) * L]
        y_ref[:, sl] = y
        upd = _bdot(jnp.concatenate([u, v_s], axis=0), jnp.concatenate([b_s, k_s], axis=0), TN)
        g_new = (g + upd) * dec[L - 1:L, sl]
        g_scr[grp] = g_new
        g_last.append(g_new)

    @pl.when(c == pl.num_programs(1) - 1)
    def _():
        for grp, g_new in enumerate(g_last):
            gout_ref[grp // NG, grp % NG] = g_new


def _wkv_prompt(seq, nb, t_len, n_rows):
    L = WKV_L
    NG = RWKV_HEADS // WKV_HG
    WD = WKV_HG * RWKV_HEAD
    nc = -(-t_len // L)
    views = [x.reshape(n_rows // nb, nb * RWKV_WIDTH) for x in seq]
    spec = pl.BlockSpec((L, WKV_NB * RWKV_WIDTH), lambda b, c: (c, b))
    y, g = pl.pallas_call(
        functools.partial(_wkv_chunk_kernel, t_len=t_len),
        grid=(nb // WKV_NB, nc),
        in_specs=[spec] * 6,
        out_specs=[spec, pl.BlockSpec((WKV_NB, NG, WD, WD), lambda b, c: (b, 0, 0, 0))],
        out_shape=[jax.ShapeDtypeStruct((n_rows // nb, nb * RWKV_WIDTH), F32),
                   jax.ShapeDtypeStruct((nb, NG, WD, WD), F32)],
        scratch_shapes=[pltpu.VMEM((WKV_NB * NG, WD, WD), F32)],
        compiler_params=_cp("parallel", "arbitrary"),
        name="wkv_chunk",
    )(*views)
    g = g.reshape(nb, NG, WKV_HG, RWKV_HEAD, WKV_HG, RWKV_HEAD)
    idx = jnp.arange(WKV_HG)
    s = g[:, :, idx, :, idx, :]
    s = jnp.transpose(s, (1, 2, 0, 3, 4)).reshape(nb, RWKV_HEADS, RWKV_HEAD, RWKV_HEAD)
    return y.reshape(n_rows, RWKV_WIDTH), s


def _wkv_step_kernel(r_ref, lw_ref, k_ref, v_ref, a_ref, b_ref, s_ref, yprev_ref, y_ref, sout_ref):
    del yprev_ref
    H, D = RWKV_HEADS, RWKV_HEAD
    ri = lax.broadcasted_iota(jnp.int32, (H * D, D), 0) % D
    ci = lax.broadcasted_iota(jnp.int32, (H * D, D), 1)
    eye = jnp.where(ri == ci, 1.0, 0.0)

    def expand(x):
        return jnp.concatenate([jnp.broadcast_to(x[:, h * D:(h + 1) * D], (D, D)) for h in range(H)], axis=0)

    for j in range(BP):
        row = lambda ref: ref[j:j + 1, :]
        s = s_ref[j].reshape(H * D, D)
        w_e = jnp.exp(expand(row(lw_ref)))
        v_col = jnp.sum(expand(row(v_ref)) * eye, axis=1, keepdims=True)
        sa = jnp.sum(s * expand(row(a_ref)), axis=1, keepdims=True)
        s_new = s * w_e + sa * expand(row(b_ref)) + v_col * expand(row(k_ref))
        y_col = jnp.sum(s_new * expand(row(r_ref)), axis=1, keepdims=True)
        ye = y_col * eye
        y_row = jnp.concatenate([jnp.sum(ye[h * D:(h + 1) * D], axis=0, keepdims=True) for h in range(H)], axis=1)
        y_ref[j:j + 1, :] = y_row
        sout_ref[j] = s_new.reshape(H, D, D)


def _wkv_sample(seq, s0, y_prev, n_prompt):
    n = y_prev.shape[0]
    ns = n - n_prompt
    rb = n_prompt // BP
    spec = pl.BlockSpec((BP, RWKV_WIDTH), lambda i: (rb + i, 0))
    sspec = pl.BlockSpec((BP, RWKV_HEADS, RWKV_HEAD, RWKV_HEAD), lambda i: (i, 0, 0, 0))
    y, s = pl.pallas_call(
        _wkv_step_kernel,
        grid=(ns // BP,),
        in_specs=[spec] * 6 + [sspec, pl.BlockSpec(memory_space=pl.ANY)],
        out_specs=[spec, sspec],
        out_shape=[jax.ShapeDtypeStruct((n, RWKV_WIDTH), F32),
                   jax.ShapeDtypeStruct((ns, RWKV_HEADS, RWKV_HEAD, RWKV_HEAD), F32)],
        input_output_aliases={7: 0},
        compiler_params=_cp("parallel"),
        name="wkv_step",
    )(*seq, s0, y_prev)
    return y, s


def _rwkv_post_kernel(y_ref, r_ref, k_ref, v_ref, g_ref, gng_ref, gnb_ref, rk_ref, ones_ref, o_ref):
    y = y_ref[...]
    ones = ones_ref[...]
    mu = _head_sum(y, ones) * (1.0 / RWKV_HEAD)
    yc = y - mu
    var = _head_sum(yc * yc, ones) * (1.0 / RWKV_HEAD)
    yn = yc * lax.rsqrt(var + RWKV_GN_EPS) * gng_ref[...] + gnb_ref[...]
    v = v_ref[...]
    bonus = _head_sum(r_ref[...] * k_ref[...] * rk_ref[...], ones) * v
    o_ref[...] = (yn + bonus) * g_ref[...]


def _rwkv_post(y, r, k, v, g, prm, li, tm):
    n = y.shape[0]
    W = RWKV_WIDTH
    hid = jnp.arange(W) // RWKV_HEAD
    ones_bd = (hid[:, None] == hid[None, :]).astype(F32)
    row = pl.BlockSpec((tm, W), lambda i: (i, 0))
    vec = pl.BlockSpec((1, W), lambda i: (0, 0))
    return pl.pallas_call(
        _rwkv_post_kernel,
        grid=(n // tm,),
        in_specs=[row] * 5 + [vec] * 3 + [pl.BlockSpec((W, W), lambda i: (0, 0))],
        out_specs=row,
        out_shape=jax.ShapeDtypeStruct((n, W), F32),
        compiler_params=_cp("parallel"),
        name="rwkv_post",
    )(y, r, k, v, g, prm['rwkv_gn_g'][li].reshape(1, -1), prm['rwkv_gn_b'][li].reshape(1, -1),
      prm['rwkv_r_k'][li].reshape(1, -1), ones_bd)


def _mla_prep_kernel(ckv_ref, qa_ref, kr_ref, cos_ref, sin_ref, qn_ref, kvn_ref, wqb_ref, wuk_ref,
                     qlat_ref, qrope_ref, c_ref, kr8_ref):
    qa = qa_ref[:, :Q_LORA]
    qa = qa * lax.rsqrt(jnp.mean(qa * qa, -1, keepdims=True) + 1e-6) * qn_ref[...]
    q = _bdot(qa, wqb_ref[...])
    nope_w = MLA_HEADS * QK_NOPE
    cos, sin = cos_ref[...], sin_ref[...]
    qrope_ref[...] = _rope(q[:, nope_w:], cos, sin, QK_ROPE // 2)
    qlat_ref[...] = _bdot(q[:, :nope_w], wuk_ref[...])
    ckv = ckv_ref[...]
    c_ref[...] = ckv * lax.rsqrt(jnp.mean(ckv * ckv, -1, keepdims=True) + 1e-6) * kvn_ref[...]
    kr8_ref[...] = _rope(kr_ref[...], cos, sin, QK_ROPE // 2)


def _mla_prep(p, pos_rows, prm, li, tm):
    n = p.shape[0]
    rw = MLA_HEADS * QK_ROPE
    cos, sin = _rope_tables(pos_rows, QK_ROPE, MLA_HEADS)
    wqb = prm['mla_w_qb'][li].reshape(Q_LORA, MLA_HEADS, QK_NOPE + QK_ROPE)
    wqb = jnp.concatenate([wqb[:, :, :QK_NOPE].reshape(Q_LORA, -1), wqb[:, :, QK_NOPE:].reshape(Q_LORA, -1)], 1)
    eye = jnp.eye(MLA_HEADS, dtype=F32)
    wuk = jnp.einsum('chn,hg->hngc', prm['mla_w_uk'][li], eye).reshape(MLA_HEADS * QK_NOPE, MLA_HEADS * KV_LORA)
    fix = lambda i: (0, 0)
    c0 = RWKV_PROJ // KV_LORA
    q0 = (RWKV_PROJ + KV_LORA) // 512
    k0 = (RWKV_PROJ + KV_LORA + 512) // rw
    return pl.pallas_call(
        _mla_prep_kernel,
        grid=(n // tm,),
        in_specs=[pl.BlockSpec((tm, KV_LORA), lambda i: (i, c0)), pl.BlockSpec((tm, 512), lambda i: (i, q0)),
                  pl.BlockSpec((tm, rw), lambda i: (i, k0)),
                  pl.BlockSpec((tm, rw), lambda i: (i, 0)), pl.BlockSpec((tm, rw), lambda i: (i, 0)),
                  pl.BlockSpec((1, Q_LORA), fix), pl.BlockSpec((1, KV_LORA), fix),
                  pl.BlockSpec(wqb.shape, fix), pl.BlockSpec(wuk.shape, fix)],
        out_specs=[pl.BlockSpec((tm, MLA_HEADS * KV_LORA), lambda i: (i, 0)), pl.BlockSpec((tm, rw), lambda i: (i, 0)),
                   pl.BlockSpec((tm, KV_LORA), lambda i: (i, 0)), pl.BlockSpec((tm, rw), lambda i: (i, 0))],
        out_shape=[jax.ShapeDtypeStruct((n, MLA_HEADS * KV_LORA), F32), jax.ShapeDtypeStruct((n, rw), F32),
                   jax.ShapeDtypeStruct((n, KV_LORA), F32), jax.ShapeDtypeStruct((n, rw), F32)],
        compiler_params=_cp("parallel"),
        name="mla_prep",
    )(p, p, p, cos, sin, prm['mla_q_norm'][li].reshape(1, -1), prm['mla_kv_norm'][li].reshape(1, -1),
      wqb.astype(BF16), wuk.astype(BF16))


def _mla_attn_kernel(qlat_ref, qrope_ref, c_ref, kr_ref, wuv_ref, o_ref, m_scr, l_scr, acc_scr, *, lq):
    qi, ki = pl.program_id(1), pl.program_id(2)
    H = MLA_HEADS

    @pl.when(ki == 0)
    def _():
        m_scr[...] = jnp.full_like(m_scr, -1e30)
        l_scr[...] = jnp.zeros_like(l_scr)
        acc_scr[...] = jnp.zeros_like(acc_scr)

    @pl.when(ki <= qi)
    def _():
        ql = qlat_ref[...]
        qs = jnp.concatenate([ql[:, h * KV_LORA:(h + 1) * KV_LORA] for h in range(H)], axis=0).astype(BF16)
        qr = qrope_ref[...].astype(BF16)
        lane = lax.broadcasted_iota(jnp.int32, qr.shape, 1) // QK_ROPE
        qrs = jnp.concatenate([jnp.where(lane == h, qr, jnp.zeros_like(qr)) for h in range(H)], axis=0)
        cb = c_ref[...].astype(BF16)
        s = (_dg(qs, cb, NT) + _dg(qrs, kr_ref[...].astype(BF16), NT)) * MLA_SCALE
        qpos = qi * lq + lax.broadcasted_iota(jnp.int32, s.shape, 0) % lq
        kpos = ki * lq + lax.broadcasted_iota(jnp.int32, s.shape, 1)
        s = jnp.where(kpos <= qpos, s, -1e30)
        m_old = m_scr[...]
        m_new = jnp.maximum(m_old, jnp.max(s, axis=1, keepdims=True))
        alpha = jnp.exp(m_old - m_new)
        p = jnp.exp(s - m_new)
        l_scr[...] = alpha * l_scr[...] + jnp.sum(p, axis=1, keepdims=True)
        acc_scr[...] = alpha * acc_scr[...] + _dg(p.astype(BF16), cb, NN)
        m_scr[...] = m_new

    @pl.when(ki == pl.num_programs(2) - 1)
    def _():
        o = acc_scr[...] / l_scr[...]
        ol = jnp.concatenate([o[h * lq:(h + 1) * lq] for h in range(H)], axis=1)
        o_ref[...] = _bdot(ol, wuv_ref[...])


def _mla_attn_prompt(qlat, qrope, c, kr8, w_uv, nb):
    n = qlat.shape[0]
    tv = n // nb
    lq = _div_tile(tv, MLA_TILE)
    nq = tv // lq
    eye = jnp.eye(MLA_HEADS, dtype=F32)
    wuv = jnp.einsum('chv,hg->hcgv', w_uv, eye).reshape(MLA_HEADS * KV_LORA, MLA_WIDTH).astype(BF16)
    rw = MLA_HEADS * QK_ROPE
    v = lambda x: x.reshape(tv, nb * x.shape[1])
    kmap = lambda b, qi, ki: (jnp.minimum(ki, qi), b)
    out = pl.pallas_call(
        functools.partial(_mla_attn_kernel, lq=lq),
        grid=(nb, nq, nq),
        in_specs=[pl.BlockSpec((lq, MLA_HEADS * KV_LORA), lambda b, qi, ki: (qi, b)),
                  pl.BlockSpec((lq, rw), lambda b, qi, ki: (qi, b)),
                  pl.BlockSpec((lq, KV_LORA), kmap), pl.BlockSpec((lq, rw), kmap),
                  pl.BlockSpec(wuv.shape, lambda b, qi, ki: (0, 0))],
        out_specs=pl.BlockSpec((lq, MLA_WIDTH), lambda b, qi, ki: (qi, b)),
        out_shape=jax.ShapeDtypeStruct((tv, nb * MLA_WIDTH), F32),
        scratch_shapes=[pltpu.VMEM((MLA_HEADS * lq, 1), F32), pltpu.VMEM((MLA_HEADS * lq, 1), F32),
                        pltpu.VMEM((MLA_HEADS * lq, KV_LORA), F32)],
        compiler_params=_cp("parallel", "parallel", "arbitrary"),
        name="mla_attn",
    )(v(qlat), v(qrope), v(c), v(kr8), wuv)
    return out.reshape(n, MLA_WIDTH)


def _mla_decode_kernel(pt_ref, qlat_ref, qrope_ref, cnew_ref, krnew_ref, kv_hbm, kr_hbm, tile_ref, wuv_ref,
                       yprev_ref, o_ref, kv_buf, kr_buf, kb_scr, s_scr, sem, *, li, n_pages, ch):
    del yprev_ref
    b = pl.program_id(0)
    slot = b % 2
    H = MLA_HEADS

    def page_copies(page, sl, pg):
        return (pltpu.make_async_copy(kv_hbm.at[page, li], kv_buf.at[sl, pg], sem.at[0, sl]),
                pltpu.make_async_copy(kr_hbm.at[page, li], kr_buf.at[sl, pg], sem.at[1, sl]))

    def start_fetch(bb, sl):
        def body(pg, carry):
            for cp in page_copies(pt_ref[bb, pg], sl, pg):
                cp.start()
            return carry
        lax.fori_loop(0, n_pages, body, 0)

    def wait_fetch(sl):
        def body(pg, carry):
            for cp in page_copies(0, sl, pg):
                cp.wait()
            return carry
        lax.fori_loop(0, n_pages, body, 0)

    @pl.when(b == 0)
    def _():
        start_fetch(0, 0)

    @pl.when(b + 1 < pl.num_programs(0))
    def _():
        start_fetch(b + 1, 1 - slot)

    j = b % BP
    rid = lax.broadcasted_iota(jnp.int32, (BP, 1), 0)

    def pick(ref):
        return jnp.sum(jnp.where(rid == j, ref[...], 0.0), axis=0, keepdims=True)

    ql = pick(qlat_ref)
    qs = jnp.concatenate([ql[:, h * KV_LORA:(h + 1) * KV_LORA] for h in range(H)], axis=0).astype(BF16)
    qr = jnp.broadcast_to(pick(qrope_ref), (H, H * QK_ROPE))
    hl = lax.broadcasted_iota(jnp.int32, qr.shape, 1) // QK_ROPE
    hr = lax.broadcasted_iota(jnp.int32, qr.shape, 0)
    qrs = jnp.where(hl == hr, qr, 0.0).astype(BF16)

    qr32 = _dg(qrs, tile_ref[...], NT).astype(BF16)
    cn = pick(cnew_ref).astype(BF16).astype(F32)
    kn = pick(krnew_ref).astype(BF16).astype(F32)
    s_self = (jnp.sum(qs.astype(F32) * cn, axis=1, keepdims=True)
              + jnp.sum(qrs.astype(F32) * kn, axis=1, keepdims=True)) * MLA_SCALE

    wait_fetch(slot)
    cw = ch * PAGE_SIZE
    for ci in range(n_pages // ch):
        kc = kv_buf[slot, ci * ch:(ci + 1) * ch].reshape(cw, KV_LORA).astype(BF16)
        kb_scr[ci * cw:(ci + 1) * cw, :] = kc
        krc = kr_buf[slot, ci * ch:(ci + 1) * ch].reshape(cw, QK_ROPE).astype(BF16)
        s_scr[:, ci * cw:(ci + 1) * cw] = (_dg(qs, kc, NT) + _dg(qr32, krc, NT)) * MLA_SCALE
    s = s_scr[...]
    m = jnp.maximum(jnp.max(s, axis=1, keepdims=True), s_self)
    e = jnp.exp(s - m)
    e_self = jnp.exp(s_self - m)
    rl = 1.0 / (jnp.sum(e, axis=1, keepdims=True) + e_self)
    p = (e * rl).astype(BF16)
    acc = (e_self * rl).astype(BF16).astype(F32) * cn
    for ci in range(n_pages // ch):
        acc = acc + _dg(p[:, ci * cw:(ci + 1) * cw], kb_scr[ci * cw:(ci + 1) * cw, :], NN)
    ol = jnp.concatenate([acc[h:h + 1] for h in range(H)], axis=1)
    y = _bdot(jnp.broadcast_to(ol, (BP, ol.shape[1])), wuv_ref[...])
    o_ref[pl.ds(j, 1), :] = y[0:1]


def _mla_decode(qlat, qrope, c, kr8, cache_kv, cache_kr, page_table, w_uv, li, y_prev, n_prompt):
    n = qlat.shape[0]
    ns = n - n_prompt
    n_pages = page_table.shape[1]
    rb = n_prompt // BP
    rw = MLA_HEADS * QK_ROPE
    eye = jnp.eye(MLA_HEADS, dtype=F32)
    wuv = jnp.einsum('chv,hg->hcgv', w_uv, eye).reshape(MLA_HEADS * KV_LORA, MLA_WIDTH).astype(BF16)
    tile = jnp.tile(jnp.eye(QK_ROPE, dtype=F32), (1, MLA_HEADS)).astype(BF16)
    ch = _div_tile(n_pages, 8, 1)
    rowblk = lambda w: pl.BlockSpec((BP, w), lambda b, pt: (rb + b // BP, 0))
    grid_spec = pltpu.PrefetchScalarGridSpec(
        num_scalar_prefetch=1,
        grid=(ns,),
        in_specs=[rowblk(MLA_HEADS * KV_LORA), rowblk(rw), rowblk(KV_LORA), rowblk(rw),
                  pl.BlockSpec(memory_space=pl.ANY), pl.BlockSpec(memory_space=pl.ANY),
                  pl.BlockSpec(tile.shape, lambda b, pt: (0, 0)),
                  pl.BlockSpec(wuv.shape, lambda b, pt: (0, 0)),
                  pl.BlockSpec(memory_space=pl.ANY)],
        out_specs=rowblk(MLA_WIDTH),
        scratch_shapes=[pltpu.VMEM((2, n_pages, PAGE_SIZE, KV_LORA), F32),
                        pltpu.VMEM((2, n_pages, PAGE_SIZE, QK_ROPE), F32),
                        pltpu.VMEM((n_pages * PAGE_SIZE, KV_LORA), BF16),
                        pltpu.VMEM((MLA_HEADS, n_pages * PAGE_SIZE), F32),
                        pltpu.SemaphoreType.DMA((2, 2))],
    )
    return pl.pallas_call(
        functools.partial(_mla_decode_kernel, li=li, n_pages=n_pages, ch=ch),
        grid_spec=grid_spec,
        out_shape=jax.ShapeDtypeStruct((n, MLA_WIDTH), F32),
        input_output_aliases={9: 0},
        compiler_params=_cp("arbitrary"),
        name="mla_decode",
    )(page_table, qlat, qrope, c, kr8, cache_kv, cache_kr, tile, wuv, y_prev)


def _forward(x_prompt, x_sample, state_ret, state_s5_re, state_s5_im, state_wkv, state_shift, cache_kv,
             cache_krope, page_table, meta_tokens, prm):
    nb, seq, _ = x_prompt.shape
    assert nb == BP
    ns = x_sample.shape[0]
    assert x_sample.shape[1] == 1 and ns % BP == 0
    tq = seq + N_META
    n_prompt = tq * nb
    n = n_prompt + ns
    assert n_prompt % ns == 0
    past_len = page_table.shape[1] * PAGE_SIZE
    n_even, n_odd = state_ret.shape[0], state_wkv.shape[0]

    meta = jnp.broadcast_to(meta_tokens[:, None, :], (N_META, nb, D_MODEL))
    x = jnp.concatenate([meta, jnp.transpose(x_prompt, (1, 0, 2))], 0).reshape(n_prompt, D_MODEL)
    x = jnp.concatenate([x, x_sample.reshape(ns, D_MODEL)], 0)
    tm = _div_tile(n, 832)
    tm_r = _div_tile(n, 1664, 128)
    pos_p = jnp.arange(tq, dtype=jnp.int32)
    pos_s = jnp.full((1,), past_len, jnp.int32)
    pos_rows = jnp.concatenate([jnp.repeat(pos_p, nb), jnp.full((ns,), past_len, jnp.int32)])

    HK, HV = RET_HEADS * RET_DK, RET_HEADS * RET_DV
    ret_p, ret_s, s5_p, s5_s, wkv_p, wkv_s, sh_p, sh_s, kv_l, kr_l = [], [], [], [], [], [], [], [], [], []
    for layer in range(DEPTH):
        li = layer // 2
        if layer % 2 == 0:
            w_in = prm['ev_w_in'][li].astype(BF16)
            p = _matmul(x, w_in, tm, _div_tile(w_in.shape[1], 1024, 128))
            ein = p.shape[1]
            zero_s = jnp.zeros((nb, RET_HEADS, RET_DK, RET_DV), F32)
            ret, s_p = _retention(p.reshape(n // nb, nb * ein), nb, tq, 0, pos_p, zero_s,
                                  prm['ret_gn_g'][li], prm['ret_gn_b'][li], None, n)
            ret, s_s = _retention(p.reshape(n // ns, ns * ein), ns, 1, n_prompt // ns, pos_s, state_ret[li],
                                  prm['ret_gn_g'][li], prm['ret_gn_b'][li], ret.reshape(n // ns, 1, ns * HV), n)
            ret = ret.reshape(n, HV)
            s5o, h_p = _s5(p, 0, nb, tq, jnp.zeros((nb, 2 * S5_CH), F32), prm, li, None)
            h0 = jnp.concatenate([state_s5_re[li].reshape(ns, S5_CH), state_s5_im[li].reshape(ns, S5_CH)], 1)
            s5o, h_s = _s5(p, n_prompt, ns, 1, h0, prm, li, s5o)
            ret_p.append(s_p)
            ret_s.append(s_s)
            s5_p.append(h_p)
            s5_s.append(h_s)
            x = _proj_ln(ret, s5o, x, prm['ev_w_out'][li], prm['ln1_g'][layer], prm['ln1_b'][layer], tm)
        else:
            w = prm['od_w_in'][li]
            o1 = RWKV_PROJ
            o2 = o1 + Q_LORA
            o3 = o2 + KV_LORA
            w_in = jnp.concatenate([w[:, :o1], w[:, o2:o3], w[:, o1:o2], jnp.zeros((D_MODEL, 512 - Q_LORA), F32),
                                    jnp.tile(w[:, o3:], (1, MLA_HEADS))], 1).astype(BF16)
            p = _matmul(x, w_in, tm, _div_tile(w_in.shape[1], 1408, 128))
            pr_p = p[n_prompt - nb:n_prompt, :RWKV_PROJ]
            pr_s = p[n_prompt:, :RWKV_PROJ]
            sh_p.append(pr_p)
            sh_s.append(pr_s)
            r, lw, k, v, a, b, g = _rwkv_prep(p, n_prompt, state_shift[li], prm, li)
            seqs = (r, lw, k, v, a, b)
            y, g_p = _wkv_prompt(seqs, nb, tq, n)
            y, g_s = _wkv_sample(seqs, state_wkv[li], y, n_prompt)
            wkv_p.append(g_p)
            wkv_s.append(g_s)
            y_c = _rwkv_post(y, r, k, v, g, prm, li, tm)
            qlat, qrope, c, kr8 = _mla_prep(p, pos_rows, prm, li, tm)
            kv_l.append(c)
            kr_l.append(kr8[:, :QK_ROPE])
            y_d = _mla_attn_prompt(qlat, qrope, c, kr8, prm['mla_w_uv'][li], nb)
            y_d = _mla_decode(qlat, qrope, c, kr8, cache_kv, cache_krope, page_table, prm['mla_w_uv'][li], li,
                              y_d, n_prompt)
            x = _proj_ln(y_c, y_d, x, prm['od_w_out'][li], prm['ln1_g'][layer], prm['ln1_b'][layer], tm)
        comb = _router(x, prm['moe_w_coarse'][layer], prm['moe_b_coarse'][layer], prm['moe_w_fine'][layer],
                       prm['moe_b_fine'][layer], tm_r)
        x = _moe_ln(x, comb, prm['moe_w_gate'][layer].astype(BF16), prm['moe_w_up'][layer].astype(BF16),
                    prm['moe_w_down'][layer].astype(BF16), prm['ln2_g'][layer], prm['ln2_b'][layer], tm)

    def tm_to_bm(a):
        return jnp.transpose(a.reshape(tq, nb, a.shape[-1]), (1, 0, 2))

    y_p = tm_to_bm(x[:n_prompt])[:, N_META:]
    y_s = x[n_prompt:].reshape(ns, 1, D_MODEL)

    def s5_state(hs, part):
        return jnp.stack([h[:, part * S5_CH:(part + 1) * S5_CH].reshape(-1, S5_GROUPS, S5_STATE) for h in hs])

    kv_p = jnp.stack([tm_to_bm(c[:n_prompt]) for c in kv_l], 1)
    kv_s = jnp.stack([c[n_prompt:].reshape(ns, 1, KV_LORA) for c in kv_l], 1)
    kr_p = jnp.stack([tm_to_bm(c[:n_prompt]) for c in kr_l], 1)
    kr_s = jnp.stack([c[n_prompt:].reshape(ns, 1, QK_ROPE) for c in kr_l], 1)
    return (y_p, y_s, jnp.stack(ret_p), jnp.stack(ret_s), s5_state(s5_p, 0), s5_state(s5_s, 0),
            s5_state(s5_p, 1), s5_state(s5_s, 1), jnp.stack(wkv_p), jnp.stack(wkv_s),
            jnp.stack(sh_p), jnp.stack(sh_s), kv_p, kv_s, kr_p, kr_s)


def kernel(x_prompt, x_sample, state_ret, state_s5_re, state_s5_im, state_wkv, state_shift, cache_kv, cache_krope, page_table, meta_tokens, ev_w_in, ret_gn_g, ret_gn_b, s5_a_re, s5_a_im, s5_log_dt, s5_b_re, s5_b_im, s5_c_re, s5_c_im, s5_d, s5_w_glu, s5_b_glu, ev_w_out, od_w_in, rwkv_mu, rwkv_w0, rwkv_w2, rwkv_a0, rwkv_a2, rwkv_g2, rwkv_k_k, rwkv_k_a, rwkv_r_k, rwkv_gn_g, rwkv_gn_b, mla_q_norm, mla_w_qb, mla_kv_norm, mla_w_uk, mla_w_uv, od_w_out, ln1_g, ln1_b, ln2_g, ln2_b, moe_w_coarse, moe_b_coarse, moe_w_fine, moe_b_fine, moe_w_gate, moe_w_up, moe_w_down):
    prm = dict(ev_w_in=ev_w_in, ret_gn_g=ret_gn_g, ret_gn_b=ret_gn_b, s5_a_re=s5_a_re, s5_a_im=s5_a_im,
               s5_log_dt=s5_log_dt, s5_b_re=s5_b_re, s5_b_im=s5_b_im, s5_c_re=s5_c_re, s5_c_im=s5_c_im,
               s5_d=s5_d, s5_w_glu=s5_w_glu, s5_b_glu=s5_b_glu, ev_w_out=ev_w_out,
               od_w_in=od_w_in, rwkv_mu=rwkv_mu, rwkv_w0=rwkv_w0, rwkv_w2=rwkv_w2, rwkv_a0=rwkv_a0,
               rwkv_a2=rwkv_a2, rwkv_g2=rwkv_g2, rwkv_k_k=rwkv_k_k, rwkv_k_a=rwkv_k_a, rwkv_r_k=rwkv_r_k,
               rwkv_gn_g=rwkv_gn_g, rwkv_gn_b=rwkv_gn_b, mla_q_norm=mla_q_norm, mla_w_qb=mla_w_qb,
               mla_kv_norm=mla_kv_norm, mla_w_uk=mla_w_uk, mla_w_uv=mla_w_uv, od_w_out=od_w_out,
               ln1_g=ln1_g, ln1_b=ln1_b, ln2_g=ln2_g, ln2_b=ln2_b, moe_w_coarse=moe_w_coarse,
               moe_b_coarse=moe_b_coarse, moe_w_fine=moe_w_fine, moe_b_fine=moe_b_fine,
               moe_w_gate=moe_w_gate, moe_w_up=moe_w_up, moe_w_down=moe_w_down)
    return _forward(x_prompt, x_sample, state_ret, state_s5_re, state_s5_im, state_wkv, state_shift, cache_kv,
                    cache_krope, page_table, meta_tokens, prm)
```

```python
import functools
import math

import jax
import jax.numpy as jnp
from jax import lax
from jax.experimental import pallas as pl
from jax.experimental.pallas import tpu as pltpu

F32 = jnp.float32
BF16 = jnp.bfloat16

D_MODEL = 1024
N_META = 16
PAGE_SIZE = 128
RET_HEADS, RET_DK, RET_DV = 4, 64, 128
S5_WIDTH, S5_GROUP, S5_STATE = 512, 16, 64
S5_GROUPS = S5_WIDTH // S5_GROUP
S5_CH = S5_GROUPS * S5_STATE
RWKV_HEAD, RWKV_WIDTH = 64, 512
RWKV_HEADS = RWKV_WIDTH // RWKV_HEAD
LORA_W, LORA_A, LORA_G = 64, 64, 128
RWKV_PROJ = 3 * RWKV_WIDTH + LORA_W + LORA_A + LORA_G
RWKV_GN_EPS = 64e-5
MLA_HEADS, Q_LORA, KV_LORA, QK_NOPE, QK_ROPE, V_HEAD = 8, 384, 256, 64, 32, 64
MLA_WIDTH = MLA_HEADS * V_HEAD
MLA_SCALE = (QK_NOPE + QK_ROPE) ** -0.5
ROPE_THETA = 10000.0
N_EGROUPS, E_PER_GROUP, D_EXPERT = 4, 4, 256
N_EXPERTS = N_EGROUPS * E_PER_GROUP
DEPTH = 4
DN_ALPHA = (2 * DEPTH) ** 0.25
NSLAB = 8
SUB = 8
LANES = 128

VMEM_LIMIT = 56 * 1024 * 1024
WKV_L = 32
WKV_HG = 4
WKV_NB = 2
ROW_TILE = 1056
SEQ_TILE = 416
S5_TILE = 80
NEG = -1e30

NT = (((1,), (1,)), ((), ()))
TN = (((0,), (0,)), ((), ()))
NN = (((1,), (0,)), ((), ()))


def _cp(*sem):
    return pltpu.CompilerParams(dimension_semantics=sem, vmem_limit_bytes=VMEM_LIMIT)


def _div_tile(n, cap, mult=8):
    best = None
    for d in range(mult, min(n, cap) + 1, mult):
        if n % d == 0:
            best = d
    assert best is not None, (n, cap, mult)
    return best


def _dg(a, b, dims):
    return lax.dot_general(a, b, dims, preferred_element_type=F32)


def _bdot(a, b, dims=NN):
    return _dg(a.astype(BF16), b.astype(BF16), dims)


def _dot3(a, b, dims=NN):
    ah = a.astype(BF16)
    al = (a - ah.astype(F32)).astype(BF16)
    bh = b.astype(BF16)
    bl = (b - bh.astype(F32)).astype(BF16)
    return _dg(ah, bh, dims) + (_dg(ah, bl, dims) + _dg(al, bh, dims))


def _rb(x):
    return x.astype(BF16).astype(F32)


def _layer_norm(z, g, b, eps=1e-5):
    mu = jnp.mean(z, -1, keepdims=True)
    zc = z - mu
    var = jnp.mean(zc * zc, -1, keepdims=True)
    return zc * lax.rsqrt(var + eps) * g + b


def _rope(x, cos, sin_signed, half):
    w = x.shape[-1]
    lane = lax.broadcasted_iota(jnp.int32, x.shape, 1)
    first = (lane % (2 * half)) < half
    other = jnp.where(first, pltpu.roll(x, w - half, 1), pltpu.roll(x, half, 1))
    return x * cos + other * sin_signed


def _rope_tables(pos, dim, reps):
    inv = ROPE_THETA ** (-jnp.arange(0, dim, 2, dtype=F32) / dim)
    ang = pos.astype(F32)[:, None] * inv[None, :]
    cos = jnp.cos(ang)
    sin = jnp.sin(ang)
    cos = jnp.tile(jnp.concatenate([cos, cos], -1), (1, reps))
    sin = jnp.tile(jnp.concatenate([-sin, sin], -1), (1, reps))
    return cos, sin


def _drop_arg(kern, idx, *refs):
    return kern(*(refs[:idx] + refs[idx + 1:]))


class _Geom:
    def __init__(self, seq, ns, n_pages):
        self.tq = seq + N_META
        self.padf = (-self.tq) % WKV_L
        self.tp = self.padf + self.tq
        assert ns % (NSLAB * SUB) == 0
        self.ns = ns
        self.sps = ns // NSLAB
        self.slab = self.tp + self.sps
        self.n = NSLAB * self.slab
        self.past = n_pages * PAGE_SIZE
        self.tm = _div_tile(self.slab, ROW_TILE)
        self.tps = self.slab // self.tm

    def v3(self, x):
        return x.reshape(NSLAB, self.slab, x.shape[-1])

    def sample_blk(self, rows):
        per = self.sps // rows
        return lambda k: (k // per, self.tp // rows + k % per)


def _mm_kernel(x_ref, w_ref, o_ref):
    o_ref[...] = _dg(x_ref[...].astype(BF16), w_ref[...], NN)


def _matmul(x, w, tm, tn):
    n, k = x.shape
    m = w.shape[1]
    return pl.pallas_call(
        _mm_kernel,
        grid=(n // tm, m // tn),
        in_specs=[pl.BlockSpec((tm, k), lambda i, j: (i, 0)), pl.BlockSpec((k, tn), lambda i, j: (0, j))],
        out_specs=pl.BlockSpec((tm, tn), lambda i, j: (i, j)),
        out_shape=jax.ShapeDtypeStruct((n, m), F32),
        compiler_params=_cp("parallel", "arbitrary"),
        name="proj_in",
    )(x, w)


def _proj_ln_kernel(a1_ref, a2_ref, x_ref, w1_ref, w2_ref, g_ref, b_ref, o_ref):
    mix = _dg(a1_ref[...].astype(BF16), w1_ref[...], NN) + _dg(a2_ref[...].astype(BF16), w2_ref[...], NN)
    z = DN_ALPHA * x_ref[...] + mix
    o_ref[...] = _layer_norm(z, g_ref[...], b_ref[...])


def _proj_ln(a1, a2, x, w_out, g, b, tm):
    n = x.shape[0]
    k1, k2 = a1.shape[1], a2.shape[1]
    w1 = w_out[:k1].astype(BF16)
    w2 = w_out[k1:].astype(BF16)
    row = lambda i: (i, 0)
    fix = lambda i: (0, 0)
    return pl.pallas_call(
        _proj_ln_kernel,
        grid=(n // tm,),
        in_specs=[pl.BlockSpec((tm, k1), row), pl.BlockSpec((tm, k2), row), pl.BlockSpec((tm, D_MODEL), row),
                  pl.BlockSpec((k1, D_MODEL), fix), pl.BlockSpec((k2, D_MODEL), fix),
                  pl.BlockSpec((1, D_MODEL), fix), pl.BlockSpec((1, D_MODEL), fix)],
        out_specs=pl.BlockSpec((tm, D_MODEL), row),
        out_shape=jax.ShapeDtypeStruct((n, D_MODEL), F32),
        compiler_params=_cp("parallel"),
        name="proj_out_ln",
    )(a1, a2, x, w1, w2, g.reshape(1, -1), b.reshape(1, -1))


ROUTE0 = N_EGROUPS


def _route(lt):
    lane = lax.broadcasted_iota(jnp.int32, lt.shape, 1)
    lanef = lane.astype(F32)
    coarse = lane < N_EGROUPS
    lc = jnp.where(coarse, lt, NEG)
    m = jnp.max(lc, axis=1, keepdims=True)
    gsel = jnp.min(jnp.where(lc == m, lanef, 1e9), axis=1, keepdims=True)
    gate_c = 1.0 / jnp.sum(jnp.where(coarse, jnp.exp(lc - m), 0.0), axis=1, keepdims=True)
    grp = ((lane - ROUTE0) // E_PER_GROUP).astype(F32)
    lf = jnp.where(grp == gsel, lt, NEG)
    t1 = jnp.max(lf, axis=1, keepdims=True)
    i1 = jnp.min(jnp.where(lf == t1, lanef, 1e9), axis=1, keepdims=True)
    lf2 = jnp.where(lanef == i1, NEG, lf)
    t2 = jnp.max(lf2, axis=1, keepdims=True)
    i2 = jnp.min(jnp.where(lf2 == t2, lanef, 1e9), axis=1, keepdims=True)
    e2 = jnp.exp(t2 - t1)
    w1 = gate_c / (1.0 + e2)
    return jnp.where(lanef == i1, w1, 0.0) + jnp.where(lanef == i2, w1 * e2, 0.0)


def _moe_kernel(x_ref, wr_ref, br_ref, wg_ref, wu_ref, wd_ref, g_ref, b_ref, o_ref, xb_scr, comb_scr, acc_scr):
    e = pl.program_id(1)

    @pl.when(e == 0)
    def _():
        xb = x_ref[...].astype(BF16)
        xb_scr[...] = xb
        comb_scr[...] = _route(_dg(xb, wr_ref[...], NN) + br_ref[...])
        acc_scr[...] = jnp.zeros_like(acc_scr)

    xb = xb_scr[...]
    hg = _dg(xb, wg_ref[0], NN)
    hu = _dg(xb, wu_ref[0], NN)
    comb = comb_scr[...]
    lane = lax.broadcasted_iota(jnp.int32, comb.shape, 1)
    ce = jnp.sum(jnp.where(lane == e + ROUTE0, comb, 0.0), axis=1, keepdims=True)
    h = (hg * jax.nn.sigmoid(hg)) * hu * ce
    acc_scr[...] += _dg(h.astype(BF16), wd_ref[0], NN)

    @pl.when(e == pl.num_programs(1) - 1)
    def _():
        z = DN_ALPHA * x_ref[...] + acc_scr[...]
        o_ref[...] = _layer_norm(z, g_ref[...], b_ref[...])


def _moe_ln(x, prm, layer, tm):
    n = x.shape[0]
    wr = jnp.concatenate([prm['moe_w_coarse'][layer],
                          jnp.transpose(prm['moe_w_fine'][layer], (1, 0, 2)).reshape(D_MODEL, N_EXPERTS)], 1)
    wr = jnp.pad(wr, ((0, 0), (0, LANES - wr.shape[1]))).astype(BF16)
    br = jnp.pad(jnp.concatenate([prm['moe_b_coarse'][layer], prm['moe_b_fine'][layer].reshape(-1)]),
                 (0, LANES - N_EGROUPS - N_EXPERTS)).reshape(1, LANES)
    row = lambda i, e: (i, 0)
    fix = lambda i, e: (0, 0)
    exp3 = lambda i, e: (e, 0, 0)
    return pl.pallas_call(
        _moe_kernel,
        grid=(n // tm, N_EXPERTS),
        in_specs=[pl.BlockSpec((tm, D_MODEL), row), pl.BlockSpec((D_MODEL, LANES), fix), pl.BlockSpec((1, LANES), fix),
                  pl.BlockSpec((1, D_MODEL, D_EXPERT), exp3), pl.BlockSpec((1, D_MODEL, D_EXPERT), exp3),
                  pl.BlockSpec((1, D_EXPERT, D_MODEL), exp3),
                  pl.BlockSpec((1, D_MODEL), fix), pl.BlockSpec((1, D_MODEL), fix)],
        out_specs=pl.BlockSpec((tm, D_MODEL), row),
        out_shape=jax.ShapeDtypeStruct((n, D_MODEL), F32),
        scratch_shapes=[pltpu.VMEM((tm, D_MODEL), BF16), pltpu.VMEM((tm, LANES), F32), pltpu.VMEM((tm, D_MODEL), F32)],
        compiler_params=_cp("parallel", "arbitrary"),
        name="moe_ln",
    )(x, wr, br, prm['moe_w_gate'][layer].astype(BF16), prm['moe_w_up'][layer].astype(BF16),
      prm['moe_w_down'][layer].astype(BF16), prm['ln2_g'][layer].reshape(1, -1), prm['ln2_b'][layer].reshape(1, -1))


def _ret_log_g():
    return [math.log1p(-2.0 ** (-5.0 - h)) for h in range(RET_HEADS)]


def _ret_kernel(q_ref, k_ref, v_ref, gate_ref, cos_ref, sin_ref, dmask_ref, cdec_ref, kdec_ref, sdec_ref,
                gng_ref, gnb_ref, o_ref, sout_ref, s_scr, *, L, padf):
    c = pl.program_id(1)
    HK, HV = RET_HEADS * RET_DK, RET_HEADS * RET_DV

    @pl.when(c == 0)
    def _():
        s_scr[...] = jnp.zeros_like(s_scr)

    cos, sin = cos_ref[...], sin_ref[...]
    q = _rope(q_ref[...], cos, sin, RET_DK // 2) * RET_DK ** -0.5
    k = _rope(k_ref[...], cos, sin, RET_DK // 2)
    real = (lax.broadcasted_iota(jnp.int32, (L, 1), 0) + c * L) >= padf
    k = jnp.where(real, k, 0.0)
    v = jnp.where(real, v_ref[...], 0.0)
    gate = gate_ref[...]
    qb, kb, vb = q.astype(BF16), k.astype(BF16), v.astype(BF16)
    sbd = s_scr[...]
    cross = _dg(qb, sbd.astype(BF16), NN) * cdec_ref[...]
    lane = lax.broadcasted_iota(jnp.int32, (L, HK), 1)
    outs = []
    for h in range(RET_HEADS):
        qh = jnp.where(lane // RET_DK == h, qb, jnp.zeros_like(qb))
        sc = _dg(qh, kb, NT) * dmask_ref[h]
        sl = slice(h * RET_DV, (h + 1) * RET_DV)
        o = _dg(sc.astype(BF16), vb[:, sl], NN) + cross[:, sl]
        mu = jnp.mean(o, -1, keepdims=True)
        oc = o - mu
        var = jnp.mean(oc * oc, -1, keepdims=True)
        outs.append(oc * lax.rsqrt(var + 1e-5))
    on = jnp.concatenate(outs, axis=1) * gng_ref[...] + gnb_ref[...]
    o_ref[...] = (gate * jax.nn.sigmoid(gate)) * on
    upd = _dg((k * kdec_ref[...]).astype(BF16), vb, TN)
    ri = lax.broadcasted_iota(jnp.int32, (HK, HV), 0) // RET_DK
    ci = lax.broadcasted_iota(jnp.int32, (HK, HV), 1) // RET_DV
    s_new = sdec_ref[...] * sbd + jnp.where(ri == ci, upd, 0.0)
    s_scr[...] = s_new

    @pl.when(c == pl.num_programs(1) - 1)
    def _():
        for h in range(RET_HEADS):
            sout_ref[h] = s_new[h * RET_DK:(h + 1) * RET_DK, h * RET_DV:(h + 1) * RET_DV]


def _retention_prompt(p3, geo, gn_g, gn_b):
    HK, HV = RET_HEADS * RET_DK, RET_HEADS * RET_DV
    L = _div_tile(geo.tp, SEQ_TILE)
    log_g = jnp.asarray(_ret_log_g(), F32)
    idx = jnp.arange(L, dtype=F32)
    diff = idx[:, None] - idx[None, :]
    dmask = jnp.where(diff >= 0, jnp.exp(log_g[:, None, None] * jnp.maximum(diff, 0.0)), 0.0)
    cdec = jnp.repeat(jnp.exp(log_g[None, :] * (idx + 1.0)[:, None]), RET_DV, axis=1)
    kdec = jnp.repeat(jnp.exp(log_g[None, :] * (L - 1.0 - idx)[:, None]), RET_DK, axis=1)
    sdec = jnp.broadcast_to(jnp.repeat(jnp.exp(log_g * L), RET_DK)[:, None], (HK, HV))
    cos, sin = _rope_tables(jnp.arange(geo.tp, dtype=jnp.int32) - geo.padf, RET_DK, RET_HEADS)
    fix2 = lambda b, c: (0, 0)
    tab = pl.BlockSpec((L, HK), lambda b, c: (c, 0))
    return pl.pallas_call(
        functools.partial(_ret_kernel, L=L, padf=geo.padf),
        grid=(NSLAB, geo.tp // L),
        in_specs=[pl.BlockSpec((None, L, HK), lambda b, c: (b, c, 0)), pl.BlockSpec((None, L, HK), lambda b, c: (b, c, 1)),
                  pl.BlockSpec((None, L, HV), lambda b, c: (b, c, 1)), pl.BlockSpec((None, L, HV), lambda b, c: (b, c, 2)),
                  tab, tab, pl.BlockSpec((RET_HEADS, L, L), lambda b, c: (0, 0, 0)),
                  pl.BlockSpec((L, HV), fix2), pl.BlockSpec((L, HK), fix2), pl.BlockSpec((HK, HV), fix2),
                  pl.BlockSpec((1, HV), fix2), pl.BlockSpec((1, HV), fix2)],
        out_specs=[pl.BlockSpec((None, L, HV), lambda b, c: (b, c, 0)),
                   pl.BlockSpec((None, RET_HEADS, RET_DK, RET_DV), lambda b, c: (b, 0, 0, 0))],
        out_shape=[jax.ShapeDtypeStruct((NSLAB, geo.slab, HV), F32),
                   jax.ShapeDtypeStruct((NSLAB, RET_HEADS, RET_DK, RET_DV), F32)],
        scratch_shapes=[pltpu.VMEM((HK, HV), F32)],
        compiler_params=_cp("parallel", "arbitrary"),
        name="retention",
    )(p3, p3, p3, p3, cos, sin, dmask, cdec, kdec, sdec, gn_g.reshape(1, -1), gn_b.reshape(1, -1))


def _ret_step_kernel(q_ref, k_ref, v_ref, gate_ref, cos_ref, sin_ref, gng_ref, gnb_ref, s_ref, o_ref, sout_ref):
    cos, sin = cos_ref[...], sin_ref[...]
    qb = _rb(_rope(q_ref[...], cos, sin, RET_DK // 2) * RET_DK ** -0.5)
    kb = _rb(_rope(k_ref[...], cos, sin, RET_DK // 2))
    vb = _rb(v_ref[...])
    gate = gate_ref[...]
    ri = lax.broadcasted_iota(jnp.int32, (RET_DK, RET_DK), 0)
    ci = lax.broadcasted_iota(jnp.int32, (RET_DK, RET_DK), 1)
    eye = jnp.where(ri == ci, 1.0, 0.0)

    def col(x):
        return jnp.sum(eye * jnp.broadcast_to(x, (RET_DK, RET_DK)), axis=1, keepdims=True)

    for j in range(SUB):
        outs = []
        for h, lg in enumerate(_ret_log_g()):
            g = math.exp(lg)
            qh = qb[j:j + 1, h * RET_DK:(h + 1) * RET_DK]
            kh = kb[j:j + 1, h * RET_DK:(h + 1) * RET_DK]
            vh = vb[j:j + 1, h * RET_DV:(h + 1) * RET_DV]
            s0 = s_ref[j, h]
            qk = _rb(jnp.sum(qh * kh, axis=1, keepdims=True))
            o = qk * vh + g * jnp.sum(col(qh) * _rb(s0), axis=0, keepdims=True)
            sout_ref[j, h] = g * s0 + col(kh) * vh
            mu = jnp.mean(o, -1, keepdims=True)
            oc = o - mu
            var = jnp.mean(oc * oc, -1, keepdims=True)
            outs.append(oc * lax.rsqrt(var + 1e-5))
        on = jnp.concatenate(outs, axis=1) * gng_ref[...] + gnb_ref[...]
        gj = gate[j:j + 1]
        o_ref[j:j + 1, :] = (gj * jax.nn.sigmoid(gj)) * on


def _retention_sample(p3, geo, s0, gn_g, gn_b, out_prev):
    HK, HV = RET_HEADS * RET_DK, RET_HEADS * RET_DV
    cos, sin = _rope_tables(jnp.full((1,), geo.past, jnp.int32), RET_DK, RET_HEADS)
    sb = geo.sample_blk(SUB)
    fix = lambda k: (0, 0)
    st = pl.BlockSpec((SUB, RET_HEADS, RET_DK, RET_DV), lambda k: (k, 0, 0, 0))
    return pl.pallas_call(
        functools.partial(_drop_arg, _ret_step_kernel, 9),
        grid=(geo.ns // SUB,),
        in_specs=[pl.BlockSpec((None, SUB, HK), lambda k: sb(k) + (0,)), pl.BlockSpec((None, SUB, HK), lambda k: sb(k) + (1,)),
                  pl.BlockSpec((None, SUB, HV), lambda k: sb(k) + (1,)), pl.BlockSpec((None, SUB, HV), lambda k: sb(k) + (2,)),
                  pl.BlockSpec((1, HK), fix), pl.BlockSpec((1, HK), fix),
                  pl.BlockSpec((1, HV), fix), pl.BlockSpec((1, HV), fix), st, pl.BlockSpec(memory_space=pl.ANY)],
        out_specs=[pl.BlockSpec((None, SUB, HV), lambda k: sb(k) + (0,)), st],
        out_shape=[jax.ShapeDtypeStruct(out_prev.shape, F32), jax.ShapeDtypeStruct(s0.shape, F32)],
        input_output_aliases={9: 0},
        compiler_params=_cp("parallel"),
        name="retention_step",
    )(p3, p3, p3, p3, cos, sin, gn_g.reshape(1, -1), gn_b.reshape(1, -1), s0, out_prev)


def _s5_params(prm, li):
    f32 = F32
    a_re, a_im = prm['s5_a_re'][li].astype(f32), prm['s5_a_im'][li].astype(f32)
    dt = jnp.exp(prm['s5_log_dt'][li].astype(f32))[:, None]
    mag = jnp.exp(dt * a_re)
    ab_re, ab_im = mag * jnp.cos(dt * a_im), mag * jnp.sin(dt * a_im)
    den = a_re * a_re + a_im * a_im
    n_re = ab_re - 1.0
    f_re = (n_re * a_re + ab_im * a_im) / den
    f_im = (ab_im * a_re - n_re * a_im) / den
    b_re, b_im = prm['s5_b_re'][li].astype(f32), prm['s5_b_im'][li].astype(f32)
    bb_re = f_re[..., None] * b_re - f_im[..., None] * b_im
    bb_im = f_re[..., None] * b_im + f_im[..., None] * b_re
    eye = jnp.eye(S5_GROUPS, dtype=f32)

    def blockdiag_in(w):
        return jnp.einsum('gpc,gh->gchp', w, eye).reshape(S5_WIDTH, S5_CH)

    def blockdiag_out(w):
        return jnp.einsum('gcp,gh->gphc', w, eye).reshape(S5_CH, S5_WIDTH)

    bbw = jnp.concatenate([blockdiag_in(bb_re), blockdiag_in(bb_im)], 1).astype(BF16)
    ccw = jnp.concatenate([blockdiag_out(prm['s5_c_re'][li]), -blockdiag_out(prm['s5_c_im'][li])], 0).astype(BF16)
    lam = jnp.concatenate([ab_re.reshape(1, -1), ab_im.reshape(1, -1)], 1)
    return [bbw, ccw, lam, prm['s5_d'][li].reshape(1, -1), prm['s5_w_glu'][li].astype(BF16),
            prm['s5_b_glu'][li].reshape(1, -1)]


def _s5_specs(fix):
    return [pl.BlockSpec((S5_WIDTH, 2 * S5_CH), fix), pl.BlockSpec((2 * S5_CH, S5_WIDTH), fix),
            pl.BlockSpec((1, 2 * S5_CH), fix), pl.BlockSpec((1, S5_WIDTH), fix),
            pl.BlockSpec((S5_WIDTH, S5_WIDTH), fix), pl.BlockSpec((1, S5_WIDTH), fix)]


def _s5_out(h, u, cc_ref, d_ref, wglu_ref, bglu_ref):
    y = _dg(h.astype(BF16), cc_ref[...], NN) + d_ref[...] * u
    z = 0.5 * y * (1.0 + jnp.tanh(math.sqrt(2.0 / math.pi) * (y + 0.044715 * (y * y * y))))
    return z * jax.nn.sigmoid(_dg(z.astype(BF16), wglu_ref[...], NN) + bglu_ref[...])


S5_NJ = 2 * S5_CH // LANES
S5_PAIRS = 4


def _s5_prompt_kernel(u_ref, bb_ref, cc_ref, lam_ref, d_ref, wglu_ref, bglu_ref, o_ref, hout_ref, sc, h_scr,
                      *, lt, pitch, padf):
    i = pl.program_id(0)

    @pl.when(i == 0)
    def _():
        h_scr[...] = jnp.zeros_like(h_scr)

    real = (lax.broadcasted_iota(jnp.int32, (lt, 1), 0) + i * lt) >= padf
    for b in range(NSLAB):
        bu = _dg(jnp.where(real, u_ref[b], 0.0).astype(BF16), bb_ref[...], NN)
        for j in range(S5_NJ):
            sc[j, b * pitch:b * pitch + lt, :] = bu[:, j * LANES:(j + 1) * LANES]
    half = S5_NJ // 2
    for j0 in range(0, half, S5_PAIRS):
        js = list(range(j0, j0 + S5_PAIRS))
        lr = [jnp.broadcast_to(lam_ref[:, j * LANES:(j + 1) * LANES], (NSLAB, LANES)) for j in js]
        li = [jnp.broadcast_to(lam_ref[:, S5_CH + j * LANES:S5_CH + (j + 1) * LANES], (NSLAB, LANES)) for j in js]

        def step(t, carry):
            hr, hi = carry
            rows = pl.ds(t, NSLAB, stride=pitch)
            nr, ni = [], []
            for q, j in enumerate(js):
                r = lr[q] * hr[q] - li[q] * hi[q] + sc[j, rows, :]
                m = lr[q] * hi[q] + li[q] * hr[q] + sc[half + j, rows, :]
                sc[j, rows, :] = r
                sc[half + j, rows, :] = m
                nr.append(r)
                ni.append(m)
            return tuple(nr), tuple(ni)

        init = (tuple(h_scr[:, j * LANES:(j + 1) * LANES] for j in js),
                tuple(h_scr[:, S5_CH + j * LANES:S5_CH + (j + 1) * LANES] for j in js))
        hr, hi = lax.fori_loop(0, lt, step, init)
        for q, j in enumerate(js):
            h_scr[:, j * LANES:(j + 1) * LANES] = hr[q]
            h_scr[:, S5_CH + j * LANES:S5_CH + (j + 1) * LANES] = hi[q]
    for b in range(NSLAB):
        h = jnp.concatenate([sc[j, b * pitch:b * pitch + lt, :] for j in range(S5_NJ)], axis=1)
        o_ref[b] = _s5_out(h, u_ref[b], cc_ref, d_ref, wglu_ref, bglu_ref)

    @pl.when(i == pl.num_programs(0) - 1)
    def _():
        hout_ref[...] = h_scr[...]


def _s5_prompt(p3, geo, prm, li):
    lt = _div_tile(geo.tp, S5_TILE, 16)
    pitch = lt + SUB
    ucol = p3.shape[-1] // S5_WIDTH - 1
    fix = lambda i: (0, 0)
    return pl.pallas_call(
        functools.partial(_s5_prompt_kernel, lt=lt, pitch=pitch, padf=geo.padf),
        grid=(geo.tp // lt,),
        in_specs=[pl.BlockSpec((NSLAB, lt, S5_WIDTH), lambda i: (0, i, ucol))] + _s5_specs(fix),
        out_specs=[pl.BlockSpec((NSLAB, lt, S5_WIDTH), lambda i: (0, i, 0)), pl.BlockSpec((NSLAB, 2 * S5_CH), fix)],
        out_shape=[jax.ShapeDtypeStruct((NSLAB, geo.slab, S5_WIDTH), F32), jax.ShapeDtypeStruct((NSLAB, 2 * S5_CH), F32)],
        scratch_shapes=[pltpu.VMEM((S5_NJ, NSLAB * pitch, LANES), F32), pltpu.VMEM((NSLAB, 2 * S5_CH), F32)],
        compiler_params=_cp("arbitrary"),
        name="s5",
    )(p3, *_s5_params(prm, li))


def _s5_step_kernel(u_ref, bb_ref, cc_ref, lam_ref, d_ref, wglu_ref, bglu_ref, h0_ref, o_ref, hout_ref):
    u = u_ref[...]
    bu = _dg(u.astype(BF16), bb_ref[...], NN)
    h0 = h0_ref[...]
    lr, li = lam_ref[:, :S5_CH], lam_ref[:, S5_CH:]
    hr, hi = h0[:, :S5_CH], h0[:, S5_CH:]
    h = jnp.concatenate([lr * hr - li * hi + bu[:, :S5_CH], lr * hi + li * hr + bu[:, S5_CH:]], axis=1)
    hout_ref[...] = h
    o_ref[...] = _s5_out(h, u, cc_ref, d_ref, wglu_ref, bglu_ref)


def _s5_sample(p3, geo, h0, prm, li, out_prev):
    rows = geo.sps
    sb = geo.sample_blk(rows)
    ucol = p3.shape[-1] // S5_WIDTH - 1
    fix = lambda k: (0, 0)
    st = pl.BlockSpec((rows, 2 * S5_CH), lambda k: (k, 0))
    return pl.pallas_call(
        functools.partial(_drop_arg, _s5_step_kernel, 8),
        grid=(NSLAB,),
        in_specs=[pl.BlockSpec((None, rows, S5_WIDTH), lambda k: sb(k) + (ucol,))] + _s5_specs(fix)
        + [st, pl.BlockSpec(memory_space=pl.ANY)],
        out_specs=[pl.BlockSpec((None, rows, S5_WIDTH), lambda k: sb(k) + (0,)), st],
        out_shape=[jax.ShapeDtypeStruct(out_prev.shape, F32), jax.ShapeDtypeStruct(h0.shape, F32)],
        input_output_aliases={8: 0},
        compiler_params=_cp("parallel"),
        name="s5_step",
    )(p3, *_s5_params(prm, li), h0, out_prev)


def _head_sum(x, ones_bd):
    return _dot3(x, ones_bd)


def _rwkv_prep_body(pr, prev, mu_ref, w0_ref, w2_ref, a0_ref, a2_ref, g2_ref, kk_ref, ka_ref, ones_ref,
                    r_ref, lw_ref, k_ref, v_ref, a_ref, b_ref, g_ref):
    W = RWKV_WIDTH
    xm = pr + (prev - pr) * mu_ref[...]
    r = xm[:, :W]
    k = xm[:, W:2 * W]
    v = xm[:, 2 * W:3 * W]
    xwa = xm[:, 3 * W:3 * W + LORA_W + LORA_A]
    xg = xm[:, 3 * W + LORA_W + LORA_A:]
    wl = w0_ref[...] + _bdot(jnp.tanh(xwa), w2_ref[...])
    sp = jnp.maximum(-wl, 0.0) + jnp.log(1.0 + jnp.exp(-jnp.abs(wl)))
    w = -sp - 0.5
    lw = -jnp.exp(w)
    a = jax.nn.sigmoid(a0_ref[...] + _bdot(xwa, a2_ref[...]))
    g = _bdot(jax.nn.sigmoid(xg), g2_ref[...])
    kk = k * kk_ref[...]
    nrm = jnp.sqrt(_head_sum(kk * kk, ones_ref[...]))
    kk = kk / jnp.maximum(nrm, 1e-12)
    k2 = k * (1.0 + (a - 1.0) * ka_ref[...])
    r_ref[...] = r
    lw_ref[...] = lw
    k_ref[...] = k2
    v_ref[...] = v
    a_ref[...] = -kk
    b_ref[...] = kk * a
    g_ref[...] = g


def _rwkv_prep_prompt_kernel(pr_ref, prev8_ref, *rest, tm, padf):
    pr = pr_ref[...]
    t = lax.broadcasted_iota(jnp.int32, (tm, 1), 0)
    last = jnp.broadcast_to(prev8_ref[SUB - 1:SUB, :], pr.shape)
    prev = jnp.where(t == 0, last, pltpu.roll(pr, 1, 0))
    first = (t + pl.program_id(1) * tm) == padf
    prev = jnp.where(first, 0.0, prev)
    _rwkv_prep_body(pr, prev, *rest)


def _rwkv_prep_sample_kernel(pr_ref, prev_ref, *rest):
    _rwkv_prep_body(pr_ref[...], prev_ref[...], *rest[:9], *rest[9 + 7:])


def _rwkv_prep(p3, geo, shift_s, prm, li):
    W = RWKV_WIDTH
    zeros64 = jnp.zeros((LORA_W, W), F32)
    w2p = jnp.concatenate([prm['rwkv_w2'][li], zeros64], 0).astype(BF16)
    a2p = jnp.concatenate([zeros64, prm['rwkv_a2'][li]], 0).astype(BF16)
    hid = jnp.arange(W) // RWKV_HEAD
    ones_bd = (hid[:, None] == hid[None, :]).astype(F32)
    consts = [prm['rwkv_mu'][li].reshape(1, -1), prm['rwkv_w0'][li].reshape(1, -1), w2p,
              prm['rwkv_a0'][li].reshape(1, -1), a2p, prm['rwkv_g2'][li].astype(BF16),
              prm['rwkv_k_k'][li].reshape(1, -1), prm['rwkv_k_a'][li].reshape(1, -1), ones_bd]
    outs_shape = [jax.ShapeDtypeStruct((NSLAB, geo.slab, W), F32)] * 7
    tm = _div_tile(geo.tp, SEQ_TILE)
    outs = pl.pallas_call(
        functools.partial(_rwkv_prep_prompt_kernel, tm=tm, padf=geo.padf),
        grid=(NSLAB, geo.tp // tm),
        in_specs=[pl.BlockSpec((None, tm, RWKV_PROJ), lambda b, c: (b, c, 0)),
                  pl.BlockSpec((None, SUB, RWKV_PROJ), lambda b, c: (b, jnp.maximum(c * (tm // SUB) - 1, 0), 0))]
        + [pl.BlockSpec(x.shape, lambda b, c: (0, 0)) for x in consts],
        out_specs=[pl.BlockSpec((None, tm, W), lambda b, c: (b, c, 0))] * 7,
        out_shape=outs_shape,
        compiler_params=_cp("parallel", "parallel"),
        name="rwkv_prep_prompt",
    )(p3, p3, *consts)
    rows = geo.sps
    sb = geo.sample_blk(rows)
    nin = 2 + len(consts)
    outs = pl.pallas_call(
        _rwkv_prep_sample_kernel,
        grid=(NSLAB,),
        in_specs=[pl.BlockSpec((None, rows, RWKV_PROJ), lambda k: sb(k) + (0,)),
                  pl.BlockSpec((rows, RWKV_PROJ), lambda k: (k, 0))]
        + [pl.BlockSpec(x.shape, lambda k: (0, 0)) for x in consts] + [pl.BlockSpec(memory_space=pl.ANY)] * 7,
        out_specs=[pl.BlockSpec((None, rows, W), lambda k: sb(k) + (0,))] * 7,
        out_shape=outs_shape,
        input_output_aliases={nin + j: j for j in range(7)},
        compiler_params=_cp("parallel"),
        name="rwkv_prep_sample",
    )(p3, shift_s, *consts, *outs)
    return outs


def _wkv_chunk_kernel(r_ref, lw_ref, k_ref, v_ref, a_ref, b_ref, y_ref, gout_ref, g_scr, *, padf):
    c = pl.program_id(1)
    L, HG = WKV_L, WKV_HG
    NG = RWKV_HEADS // HG
    WD = HG * RWKV_HEAD
    M = HG * L

    @pl.when(c == 0)
    def _():
        g_scr[...] = jnp.zeros_like(g_scr)

    real = (lax.broadcasted_iota(jnp.int32, (L, 1), 0) + c * L) >= padf

    def ld(ref):
        return jnp.where(real, jnp.concatenate([ref[i] for i in range(WKV_NB)], axis=1), 0.0)

    r, lw, k, v, a, b = ld(r_ref), ld(lw_ref), ld(k_ref), ld(v_ref), ld(a_ref), ld(b_ref)
    ti = lax.broadcasted_iota(jnp.int32, (L, L), 0)
    tj = lax.broadcasted_iota(jnp.int32, (L, L), 1)
    tri = jnp.where(ti >= tj, 1.0, 0.0)
    cum = _dot3(tri, lw)
    dec = jnp.exp(cum)
    inv = jnp.exp(-cum)
    at = a * jnp.exp(cum - lw)
    bt = b * inv
    kt = k * inv
    rt = r * dec
    rh = lax.broadcasted_iota(jnp.int32, (M, WD), 0) // L
    lh = lax.broadcasted_iota(jnp.int32, (M, WD), 1) // RWKV_HEAD
    bd = rh == lh
    mi = lax.broadcasted_iota(jnp.int32, (M, M), 0)
    mj = lax.broadcasted_iota(jnp.int32, (M, M), 1)
    same = (mi // L) == (mj // L)
    m_strict = jnp.logical_and(same, (mj % L) < (mi % L))
    m_incl = jnp.logical_and(same, (mj % L) <= (mi % L))
    eye = jnp.where(mi == mj, 1.0, 0.0)

    def stack(x):
        return jnp.where(bd, jnp.concatenate([x] * HG, axis=0), 0.0)

    g_last = []
    for grp in range(WKV_NB * NG):
        sl = slice(grp * WD, (grp + 1) * WD)
        a_s, r_s, b_s, k_s, v_s = stack(at[:, sl]), stack(rt[:, sl]), stack(bt[:, sl]), stack(kt[:, sl]), stack(v[:, sl])
        g = g_scr[grp]
        ab = jnp.where(m_strict, _dot3(a_s, b_s, NT), 0.0)
        ak = jnp.where(m_strict, _bdot(a_s, k_s, NT), 0.0)
        rbk = _bdot(r_s, jnp.concatenate([b_s, k_s], axis=0), NT)
        rb = jnp.where(m_incl, rbk[:, :M], 0.0)
        rk = jnp.where(m_incl, rbk[:, M:], 0.0)
        tinv = eye + ab
        pw = ab
        for _ in range(int(math.log2(L)) - 1):
            pw = _dot3(pw, pw)
            tinv = tinv + _dot3(tinv, pw)
        z = _bdot(a_s, g, NT) + _bdot(ak, v_s)
        u = _dot3(tinv, z)
        ybd = _bdot(r_s, g, NT) + _bdot(rb, u) + _bdot(rk, v_s)
        y = ybd[0:L]
        for h in range(1, HG):
            y = y + ybd[h * L:(h + 1) * L]
        y_ref[grp // NG, :, (grp % NG) * WD:(grp % NG + 1) * WD] = y
        upd = _bdot(jnp.concatenate([u, v_s], axis=0), jnp.concatenate([b_s, k_s], axis=0), TN)
        g_new = (g + upd) * dec[L - 1:L, sl]
        g_scr[grp] = g_new
        g_last.append(g_new)

    @pl.when(c == pl.num_programs(1) - 1)
    def _():
        for grp, g_new in enumerate(g_last):
            gout_ref[grp // NG, grp % NG] = g_new


def _wkv_prompt(seq, geo):
    L = WKV_L
    NG = RWKV_HEADS // WKV_HG
    WD = WKV_HG * RWKV_HEAD
    spec = pl.BlockSpec((WKV_NB, L, RWKV_WIDTH), lambda b, c: (b, c, 0))
    y, g = pl.pallas_call(
        functools.partial(_wkv_chunk_kernel, padf=geo.padf),
        grid=(NSLAB // WKV_NB, geo.tp // L),
        in_specs=[spec] * 6,
        out_specs=[spec, pl.BlockSpec((WKV_NB, NG, WD, WD), lambda b, c: (b, 0, 0, 0))],
        out_shape=[jax.ShapeDtypeStruct((NSLAB, geo.slab, RWKV_WIDTH), F32),
                   jax.ShapeDtypeStruct((NSLAB, NG, WD, WD), F32)],
        scratch_shapes=[pltpu.VMEM((WKV_NB * NG, WD, WD), F32)],
        compiler_params=_cp("parallel", "arbitrary"),
        name="wkv_chunk",
    )(*seq)
    g = g.reshape(NSLAB, NG, WKV_HG, RWKV_HEAD, WKV_HG, RWKV_HEAD)
    idx = jnp.arange(WKV_HG)
    s = g[:, :, idx, :, idx, :]
    s = jnp.transpose(s, (1, 2, 0, 3, 4)).reshape(NSLAB, RWKV_HEADS, RWKV_HEAD, RWKV_HEAD)
    return y, s


def _wkv_step_kernel(r_ref, lw_ref, k_ref, v_ref, a_ref, b_ref, s_ref, y_ref, sout_ref):
    H, D = RWKV_HEADS, RWKV_HEAD
    ri = lax.broadcasted_iota(jnp.int32, (H * D, D), 0) % D
    ci = lax.broadcasted_iota(jnp.int32, (H * D, D), 1)
    eye = jnp.where(ri == ci, 1.0, 0.0)

    def expand(x):
        return jnp.concatenate([jnp.broadcast_to(x[:, h * D:(h + 1) * D], (D, D)) for h in range(H)], axis=0)

    for j in range(SUB):
        row = lambda ref: ref[j:j + 1, :]
        s = s_ref[j].reshape(H * D, D)
        w_e = jnp.exp(expand(row(lw_ref)))
        v_col = jnp.sum(expand(row(v_ref)) * eye, axis=1, keepdims=True)
        sa = jnp.sum(s * expand(row(a_ref)), axis=1, keepdims=True)
        s_new = s * w_e + sa * expand(row(b_ref)) + v_col * expand(row(k_ref))
        y_col = jnp.sum(s_new * expand(row(r_ref)), axis=1, keepdims=True)
        ye = y_col * eye
        y_row = jnp.concatenate([jnp.sum(ye[h * D:(h + 1) * D], axis=0, keepdims=True) for h in range(H)], axis=1)
        y_ref[j:j + 1, :] = y_row
        sout_ref[j] = s_new.reshape(H, D, D)


def _wkv_sample(seq, geo, s0, y_prev):
    sb = geo.sample_blk(SUB)
    spec = pl.BlockSpec((None, SUB, RWKV_WIDTH), lambda k: sb(k) + (0,))
    sspec = pl.BlockSpec((SUB, RWKV_HEADS, RWKV_HEAD, RWKV_HEAD), lambda k: (k, 0, 0, 0))
    return pl.pallas_call(
        functools.partial(_drop_arg, _wkv_step_kernel, 7),
        grid=(geo.ns // SUB,),
        in_specs=[spec] * 6 + [sspec, pl.BlockSpec(memory_space=pl.ANY)],
        out_specs=[spec, sspec],
        out_shape=[jax.ShapeDtypeStruct(y_prev.shape, F32), jax.ShapeDtypeStruct(s0.shape, F32)],
        input_output_aliases={7: 0},
        compiler_params=_cp("parallel"),
        name="wkv_step",
    )(*seq, s0, y_prev)


def _rwkv_post_kernel(y_ref, r_ref, k_ref, v_ref, g_ref, gng_ref, gnb_ref, rk_ref, ones_ref, o_ref):
    y = y_ref[...]
    ones = ones_ref[...]
    mu = _head_sum(y, ones) * (1.0 / RWKV_HEAD)
    yc = y - mu
    var = _head_sum(yc * yc, ones) * (1.0 / RWKV_HEAD)
    yn = yc * lax.rsqrt(var + RWKV_GN_EPS) * gng_ref[...] + gnb_ref[...]
    v = v_ref[...]
    bonus = _head_sum(r_ref[...] * k_ref[...] * rk_ref[...], ones) * v
    o_ref[...] = (yn + bonus) * g_ref[...]


def _rwkv_post(y, r, k, v, g, prm, li, tm):
    n = y.shape[0]
    W = RWKV_WIDTH
    hid = jnp.arange(W) // RWKV_HEAD
    ones_bd = (hid[:, None] == hid[None, :]).astype(F32)
    row = pl.BlockSpec((tm, W), lambda i: (i, 0))
    vec = pl.BlockSpec((1, W), lambda i: (0, 0))
    return pl.pallas_call(
        _rwkv_post_kernel,
        grid=(n // tm,),
        in_specs=[row] * 5 + [vec] * 3 + [pl.BlockSpec((W, W), lambda i: (0, 0))],
        out_specs=row,
        out_shape=jax.ShapeDtypeStruct((n, W), F32),
        compiler_params=_cp("parallel"),
        name="rwkv_post",
    )(y, r, k, v, g, prm['rwkv_gn_g'][li].reshape(1, -1), prm['rwkv_gn_b'][li].reshape(1, -1),
      prm['rwkv_r_k'][li].reshape(1, -1), ones_bd)


def _mla_prep_kernel(ckv_ref, qa_ref, kr_ref, cos_ref, sin_ref, qn_ref, kvn_ref, wqb_ref, wuk_ref,
                     qlat_ref, qrope_ref, c_ref, kr8_ref):
    qa = qa_ref[:, :Q_LORA]
    qa = qa * lax.rsqrt(jnp.mean(qa * qa, -1, keepdims=True) + 1e-6) * qn_ref[...]
    q = _bdot(qa, wqb_ref[...])
    nope_w = MLA_HEADS * QK_NOPE
    cos, sin = cos_ref[...], sin_ref[...]
    qrope_ref[...] = _rope(q[:, nope_w:], cos, sin, QK_ROPE // 2)
    qlat_ref[...] = _bdot(q[:, :nope_w], wuk_ref[...])
    ckv = ckv_ref[...]
    c_ref[...] = ckv * lax.rsqrt(jnp.mean(ckv * ckv, -1, keepdims=True) + 1e-6) * kvn_ref[...]
    kr8_ref[...] = _rope(kr_ref[...], cos, sin, QK_ROPE // 2)


def _mla_prep(p, geo, pos_slab, prm, li):
    n = p.shape[0]
    tm, tps = geo.tm, geo.tps
    rw = MLA_HEADS * QK_ROPE
    cos, sin = _rope_tables(pos_slab, QK_ROPE, MLA_HEADS)
    wqb = prm['mla_w_qb'][li].reshape(Q_LORA, MLA_HEADS, QK_NOPE + QK_ROPE)
    wqb = jnp.concatenate([wqb[:, :, :QK_NOPE].reshape(Q_LORA, -1), wqb[:, :, QK_NOPE:].reshape(Q_LORA, -1)], 1)
    eye = jnp.eye(MLA_HEADS, dtype=F32)
    wuk = jnp.einsum('chn,hg->hngc', prm['mla_w_uk'][li], eye).reshape(MLA_HEADS * QK_NOPE, MLA_HEADS * KV_LORA)
    fix = lambda i: (0, 0)
    c0 = RWKV_PROJ // KV_LORA
    q0 = (RWKV_PROJ + KV_LORA) // 512
    k0 = (RWKV_PROJ + KV_LORA + 512) // rw
    tab = pl.BlockSpec((tm, rw), lambda i: (i % tps, 0))
    return pl.pallas_call(
        _mla_prep_kernel,
        grid=(n // tm,),
        in_specs=[pl.BlockSpec((tm, KV_LORA), lambda i: (i, c0)), pl.BlockSpec((tm, 512), lambda i: (i, q0)),
                  pl.BlockSpec((tm, rw), lambda i: (i, k0)), tab, tab,
                  pl.BlockSpec((1, Q_LORA), fix), pl.BlockSpec((1, KV_LORA), fix),
                  pl.BlockSpec(wqb.shape, fix), pl.BlockSpec(wuk.shape, fix)],
        out_specs=[pl.BlockSpec((tm, MLA_HEADS * KV_LORA), lambda i: (i, 0)), pl.BlockSpec((tm, rw), lambda i: (i, 0)),
                   pl.BlockSpec((tm, KV_LORA), lambda i: (i, 0)), pl.BlockSpec((tm, rw), lambda i: (i, 0))],
        out_shape=[jax.ShapeDtypeStruct((n, MLA_HEADS * KV_LORA), F32), jax.ShapeDtypeStruct((n, rw), F32),
                   jax.ShapeDtypeStruct((n, KV_LORA), F32), jax.ShapeDtypeStruct((n, rw), F32)],
        compiler_params=_cp("parallel"),
        name="mla_prep",
    )(p, p, p, cos, sin, prm['mla_q_norm'][li].reshape(1, -1), prm['mla_kv_norm'][li].reshape(1, -1),
      wqb.astype(BF16), wuk.astype(BF16))


def _wuv_blockdiag(w_uv):
    eye = jnp.eye(MLA_HEADS, dtype=F32)
    return jnp.einsum('chv,hg->hcgv', w_uv, eye).reshape(MLA_HEADS * KV_LORA, MLA_WIDTH).astype(BF16)


def _mla_attn_kernel(qlat_ref, qrope_ref, c_ref, kr_ref, wuv_ref, o_ref, m_scr, l_scr, acc_scr, *, lq, padf):
    qi, ki = pl.program_id(1), pl.program_id(2)
    H = MLA_HEADS

    @pl.when(ki == 0)
    def _():
        m_scr[...] = jnp.full_like(m_scr, NEG)
        l_scr[...] = jnp.zeros_like(l_scr)
        acc_scr[...] = jnp.zeros_like(acc_scr)

    @pl.when(ki <= qi)
    def _():
        ql = qlat_ref[...]
        qs = jnp.concatenate([ql[:, h * KV_LORA:(h + 1) * KV_LORA] for h in range(H)], axis=0).astype(BF16)
        qr = qrope_ref[...].astype(BF16)
        lane = lax.broadcasted_iota(jnp.int32, qr.shape, 1) // QK_ROPE
        qrs = jnp.concatenate([jnp.where(lane == h, qr, jnp.zeros_like(qr)) for h in range(H)], axis=0)
        cb = c_ref[...].astype(BF16)
        s = (_dg(qs, cb, NT) + _dg(qrs, kr_ref[...].astype(BF16), NT)) * MLA_SCALE
        qpos = qi * lq + lax.broadcasted_iota(jnp.int32, s.shape, 0) % lq
        kpos = ki * lq + lax.broadcasted_iota(jnp.int32, s.shape, 1)
        s = jnp.where(jnp.logical_and(kpos <= qpos, kpos >= padf), s, NEG)
        m_old = m_scr[...]
        m_new = jnp.maximum(m_old, jnp.max(s, axis=1, keepdims=True))
        alpha = jnp.exp(m_old - m_new)
        p = jnp.exp(s - m_new)
        l_scr[...] = alpha * l_scr[...] + jnp.sum(p, axis=1, keepdims=True)
        acc_scr[...] = alpha * acc_scr[...] + _dg(p.astype(BF16), cb, NN)
        m_scr[...] = m_new

    @pl.when(ki == pl.num_programs(2) - 1)
    def _():
        o = acc_scr[...] / l_scr[...]
        ol = jnp.concatenate([o[h * lq:(h + 1) * lq] for h in range(H)], axis=1)
        o_ref[...] = _bdot(ol, wuv_ref[...])


def _mla_attn_prompt(qlat3, qrope3, c3, kr3, w_uv, geo):
    lq = _div_tile(geo.tp, SEQ_TILE)
    nq = geo.tp // lq
    wuv = _wuv_blockdiag(w_uv)
    rw = MLA_HEADS * QK_ROPE
    qmap = lambda b, qi, ki: (b, qi, 0)
    kmap = lambda b, qi, ki: (b, jnp.minimum(ki, qi), 0)
    return pl.pallas_call(
        functools.partial(_mla_attn_kernel, lq=lq, padf=geo.padf),
        grid=(NSLAB, nq, nq),
        in_specs=[pl.BlockSpec((None, lq, MLA_HEADS * KV_LORA), qmap), pl.BlockSpec((None, lq, rw), qmap),
                  pl.BlockSpec((None, lq, KV_LORA), kmap), pl.BlockSpec((None, lq, rw), kmap),
                  pl.BlockSpec(wuv.shape, lambda b, qi, ki: (0, 0))],
        out_specs=pl.BlockSpec((None, lq, MLA_WIDTH), qmap),
        out_shape=jax.ShapeDtypeStruct((NSLAB, geo.slab, MLA_WIDTH), F32),
        scratch_shapes=[pltpu.VMEM((MLA_HEADS * lq, 1), F32), pltpu.VMEM((MLA_HEADS * lq, 1), F32),
                        pltpu.VMEM((MLA_HEADS * lq, KV_LORA), F32)],
        compiler_params=_cp("parallel", "parallel", "arbitrary"),
        name="mla_attn",
    )(qlat3, qrope3, c3, kr3, wuv)


def _mla_decode_kernel(pt_ref, qlat_ref, qrope_ref, cnew_ref, krnew_ref, kv_hbm, kr_hbm, tile_ref, wuv_ref,
                       yprev_ref, o_ref, kv_buf, kr_buf, kb_scr, s_scr, sem, *, li, n_pages, ch):
    del yprev_ref
    b = pl.program_id(0)
    slot = b % 2
    H = MLA_HEADS

    def page_copies(page, sl, pg):
        return (pltpu.make_async_copy(kv_hbm.at[page, li], kv_buf.at[sl, pg], sem.at[0, sl]),
                pltpu.make_async_copy(kr_hbm.at[page, li], kr_buf.at[sl, pg], sem.at[1, sl]))

    def start_fetch(bb, sl):
        def body(pg, carry):
            for cp in page_copies(pt_ref[bb, pg], sl, pg):
                cp.start()
            return carry
        lax.fori_loop(0, n_pages, body, 0)

    def wait_fetch(sl):
        def body(pg, carry):
            for cp in page_copies(0, sl, pg):
                cp.wait()
            return carry
        lax.fori_loop(0, n_pages, body, 0)

    @pl.when(b == 0)
    def _():
        start_fetch(0, 0)

    @pl.when(b + 1 < pl.num_programs(0))
    def _():
        start_fetch(b + 1, 1 - slot)

    j = b % SUB
    rid = lax.broadcasted_iota(jnp.int32, (SUB, 1), 0)

    def pick(ref):
        return jnp.sum(jnp.where(rid == j, ref[...], 0.0), axis=0, keepdims=True)

    ql = pick(qlat_ref)
    qs = jnp.concatenate([ql[:, h * KV_LORA:(h + 1) * KV_LORA] for h in range(H)], axis=0).astype(BF16)
    qr = jnp.broadcast_to(pick(qrope_ref), (H, H * QK_ROPE))
    hl = lax.broadcasted_iota(jnp.int32, qr.shape, 1) // QK_ROPE
    hr = lax.broadcasted_iota(jnp.int32, qr.shape, 0)
    qrs = jnp.where(hl == hr, qr, 0.0).astype(BF16)
    qr32 = _dg(qrs, tile_ref[...], NT).astype(BF16)
    cn = _rb(pick(cnew_ref))
    kn = _rb(pick(krnew_ref))
    s_self = (jnp.sum(qs.astype(F32) * cn, axis=1, keepdims=True)
              + jnp.sum(qrs.astype(F32) * kn, axis=1, keepdims=True)) * MLA_SCALE

    wait_fetch(slot)
    cw = ch * PAGE_SIZE
    for ci in range(n_pages // ch):
        kc = kv_buf[slot, ci * ch:(ci + 1) * ch].reshape(cw, KV_LORA).astype(BF16)
        kb_scr[ci * cw:(ci + 1) * cw, :] = kc
        krc = kr_buf[slot, ci * ch:(ci + 1) * ch].reshape(cw, QK_ROPE).astype(BF16)
        s_scr[:, ci * cw:(ci + 1) * cw] = (_dg(qs, kc, NT) + _dg(qr32, krc, NT)) * MLA_SCALE
    s = s_scr[...]
    m = jnp.maximum(jnp.max(s, axis=1, keepdims=True), s_self)
    e = jnp.exp(s - m)
    e_self = jnp.exp(s_self - m)
    rl = 1.0 / (jnp.sum(e, axis=1, keepdims=True) + e_self)
    p = (e * rl).astype(BF16)
    acc = _rb(e_self * rl) * cn
    for ci in range(n_pages // ch):
        acc = acc + _dg(p[:, ci * cw:(ci + 1) * cw], kb_scr[ci * cw:(ci + 1) * cw, :], NN)
    ol = jnp.concatenate([acc[h:h + 1] for h in range(H)], axis=1)
    y = _bdot(jnp.broadcast_to(ol, (SUB, ol.shape[1])), wuv_ref[...])
    o_ref[pl.ds(j, 1), :] = y[0:1]


def _mla_decode(qlat, qrope, c, kr8, cache_kv, cache_kr, page_table, w_uv, li, y_prev, geo):
    n = qlat.shape[0]
    n_pages = page_table.shape[1]
    ch = _div_tile(n_pages, 8, 1)
    rw = MLA_HEADS * QK_ROPE
    wuv = _wuv_blockdiag(w_uv)
    tile = jnp.tile(jnp.eye(QK_ROPE, dtype=F32), (1, MLA_HEADS)).astype(BF16)

    def rowmap(b, pt):
        return ((b // geo.sps) * geo.slab + geo.tp + b % geo.sps) // SUB, 0

    rowblk = lambda w: pl.BlockSpec((SUB, w), rowmap)
    grid_spec = pltpu.PrefetchScalarGridSpec(
        num_scalar_prefetch=1,
        grid=(geo.ns,),
        in_specs=[rowblk(MLA_HEADS * KV_LORA), rowblk(rw), rowblk(KV_LORA), rowblk(rw),
                  pl.BlockSpec(memory_space=pl.ANY), pl.BlockSpec(memory_space=pl.ANY),
                  pl.BlockSpec(tile.shape, lambda b, pt: (0, 0)),
                  pl.BlockSpec(wuv.shape, lambda b, pt: (0, 0)),
                  pl.BlockSpec(memory_space=pl.ANY)],
        out_specs=rowblk(MLA_WIDTH),
        scratch_shapes=[pltpu.VMEM((2, n_pages, PAGE_SIZE, KV_LORA), F32),
                        pltpu.VMEM((2, n_pages, PAGE_SIZE, QK_ROPE), F32),
                        pltpu.VMEM((n_pages * PAGE_SIZE, KV_LORA), BF16),
                        pltpu.VMEM((MLA_HEADS, n_pages * PAGE_SIZE), F32),
                        pltpu.SemaphoreType.DMA((2, 2))],
    )
    return pl.pallas_call(
        functools.partial(_mla_decode_kernel, li=li, n_pages=n_pages, ch=ch),
        grid_spec=grid_spec,
        out_shape=jax.ShapeDtypeStruct((n, MLA_WIDTH), F32),
        input_output_aliases={9: 0},
        compiler_params=_cp("arbitrary"),
        name="mla_decode",
    )(page_table, qlat, qrope, c, kr8, cache_kv, cache_kr, tile, wuv, y_prev)


def _forward(x_prompt, x_sample, state_ret, state_s5_re, state_s5_im, state_wkv, state_shift, cache_kv,
             cache_krope, page_table, meta_tokens, prm):
    nb, seq, _ = x_prompt.shape
    assert nb == NSLAB and x_sample.shape[1] == 1
    ns = x_sample.shape[0]
    geo = _Geom(seq, ns, page_table.shape[1])
    n, tm, tp, padf, sps = geo.n, geo.tm, geo.tp, geo.padf, geo.sps

    meta = jnp.broadcast_to(meta_tokens[None], (nb, N_META, D_MODEL))
    x = jnp.concatenate([jnp.zeros((nb, padf, D_MODEL), F32), meta, x_prompt, x_sample.reshape(nb, sps, D_MODEL)], 1)
    x = x.reshape(n, D_MODEL)
    pos_slab = jnp.concatenate([jnp.arange(tp, dtype=jnp.int32) - padf, jnp.full((sps,), geo.past, jnp.int32)])

    ret_p, ret_s, s5_p, s5_s, wkv_p, wkv_s, sh_p, sh_s, kv_l, kr_l = [], [], [], [], [], [], [], [], [], []
    for layer in range(DEPTH):
        li = layer // 2
        if layer % 2 == 0:
            w_in = prm['ev_w_in'][li].astype(BF16)
            p = _matmul(x, w_in, tm, _div_tile(w_in.shape[1], 1024, 128))
            p3 = geo.v3(p)
            gng, gnb = prm['ret_gn_g'][li], prm['ret_gn_b'][li]
            ret, s_p = _retention_prompt(p3, geo, gng, gnb)
            ret, s_s = _retention_sample(p3, geo, state_ret[li], gng, gnb, ret)
            s5o, h_p = _s5_prompt(p3, geo, prm, li)
            h0 = jnp.concatenate([state_s5_re[li].reshape(ns, S5_CH), state_s5_im[li].reshape(ns, S5_CH)], 1)
            s5o, h_s = _s5_sample(p3, geo, h0, prm, li, s5o)
            ret_p.append(s_p)
            ret_s.append(s_s)
            s5_p.append(h_p)
            s5_s.append(h_s)
            x = _proj_ln(ret.reshape(n, -1), s5o.reshape(n, -1), x, prm['ev_w_out'][li], prm['ln1_g'][layer],
                         prm['ln1_b'][layer], tm)
        else:
            w = prm['od_w_in'][li]
            o1 = RWKV_PROJ
            o2 = o1 + Q_LORA
            o3 = o2 + KV_LORA
            w_in = jnp.concatenate([w[:, :o1], w[:, o2:o3], w[:, o1:o2], jnp.zeros((D_MODEL, 512 - Q_LORA), F32),
                                    jnp.tile(w[:, o3:], (1, MLA_HEADS))], 1).astype(BF16)
            p = _matmul(x, w_in, tm, _div_tile(w_in.shape[1], 1408, 128))
            p3 = geo.v3(p)
            sh_p.append(p3[:, tp - 1, :RWKV_PROJ])
            sh_s.append(p3[:, tp:, :RWKV_PROJ].reshape(ns, RWKV_PROJ))
            r, lw, k, v, a, b, g = _rwkv_prep(p3, geo, state_shift[li], prm, li)
            seqs = (r, lw, k, v, a, b)
            y, g_p = _wkv_prompt(seqs, geo)
            y, g_s = _wkv_sample(seqs, geo, state_wkv[li], y)
            wkv_p.append(g_p)
            wkv_s.append(g_s)
            flat = lambda t: t.reshape(n, t.shape[-1])
            y_c = _rwkv_post(flat(y), flat(r), flat(k), flat(v), flat(g), prm, li, tm)
            qlat, qrope, c, kr8 = _mla_prep(p, geo, pos_slab, prm, li)
            kv_l.append(geo.v3(c))
            kr_l.append(geo.v3(kr8)[:, :, :QK_ROPE])
            y_d = _mla_attn_prompt(geo.v3(qlat), geo.v3(qrope), geo.v3(c), geo.v3(kr8), prm['mla_w_uv'][li], geo)
            y_d = _mla_decode(qlat, qrope, c, kr8, cache_kv, cache_krope, page_table, prm['mla_w_uv'][li], li,
                              flat(y_d), geo)
            x = _proj_ln(y_c, y_d, x, prm['od_w_out'][li], prm['ln1_g'][layer], prm['ln1_b'][layer], tm)
        x = _moe_ln(x, prm, layer, tm)

    x3 = geo.v3(x)
    y_p = x3[:, padf + N_META:tp]
    y_s = x3[:, tp:].reshape(ns, 1, D_MODEL)

    def s5_state(hs, part):
        return jnp.stack([h[:, part * S5_CH:(part + 1) * S5_CH].reshape(-1, S5_GROUPS, S5_STATE) for h in hs])

    kv_p = jnp.stack([c[:, padf:tp] for c in kv_l], 1)
    kv_s = jnp.stack([c[:, tp:].reshape(ns, 1, KV_LORA) for c in kv_l], 1)
    kr_p = jnp.stack([c[:, padf:tp] for c in kr_l], 1)
    kr_s = jnp.stack([c[:, tp:].reshape(ns, 1, QK_ROPE) for c in kr_l], 1)
    return (y_p, y_s, jnp.stack(ret_p), jnp.stack(ret_s), s5_state(s5_p, 0), s5_state(s5_s, 0),
            s5_state(s5_p, 1), s5_state(s5_s, 1), jnp.stack(wkv_p), jnp.stack(wkv_s),
            jnp.stack(sh_p), jnp.stack(sh_s), kv_p, kv_s, kr_p, kr_s)


def kernel(x_prompt, x_sample, state_ret, state_s5_re, state_s5_im, state_wkv, state_shift, cache_kv, cache_krope, page_table, meta_tokens, ev_w_in, ret_gn_g, ret_gn_b, s5_a_re, s5_a_im, s5_log_dt, s5_b_re, s5_b_im, s5_c_re, s5_c_im, s5_d, s5_w_glu, s5_b_glu, ev_w_out, od_w_in, rwkv_mu, rwkv_w0, rwkv_w2, rwkv_a0, rwkv_a2, rwkv_g2, rwkv_k_k, rwkv_k_a, rwkv_r_k, rwkv_gn_g, rwkv_gn_b, mla_q_norm, mla_w_qb, mla_kv_norm, mla_w_uk, mla_w_uv, od_w_out, ln1_g, ln1_b, ln2_g, ln2_b, moe_w_coarse, moe_b_coarse, moe_w_fine, moe_b_fine, moe_w_gate, moe_w_up, moe_w_down):
    prm = dict(ev_w_in=ev_w_in, ret_gn_g=ret_gn_g, ret_gn_b=ret_gn_b, s5_a_re=s5_a_re, s5_a_im=s5_a_im,
               s5_log_dt=s5_log_dt, s5_b_re=s5_b_re, s5_b_im=s5_b_im, s5_c_re=s5_c_re, s5_c_im=s5_c_im,
               s5_d=s5_d, s5_w_glu=s5_w_glu, s5_b_glu=s5_b_glu, ev_w_out=ev_w_out,
               od_w_in=od_w_in, rwkv_mu=rwkv_mu, rwkv_w0=rwkv_w0, rwkv_w2=rwkv_w2, rwkv_a0=rwkv_a0,
               rwkv_a2=rwkv_a2, rwkv_g2=rwkv_g2, rwkv_k_k=rwkv_k_k, rwkv_k_a=rwkv_k_a, rwkv_r_k=rwkv_r_k,
               rwkv_gn_g=rwkv_gn_g, rwkv_gn_b=rwkv_gn_b, mla_q_norm=mla_q_norm, mla_w_qb=mla_w_qb,
               mla_kv_norm=mla_kv_norm, mla_w_uk=mla_w_uk, mla_w_uv=mla_w_uv, od_w_out=od_w_out,
               ln1_g=ln1_g, ln1_b=ln1_b, ln2_g=ln2_g, ln2_b=ln2_b, moe_w_coarse=moe_w_coarse,
               moe_b_coarse=moe_b_coarse, moe_w_fine=moe_w_fine, moe_b_fine=moe_b_fine,
               moe_w_gate=moe_w_gate, moe_w_up=moe_w_up, moe_w_down=moe_w_down)
    return _forward(x_prompt, x_sample, state_ret, state_s5_re, state_s5_im, state_wkv, state_shift, cache_kv,
                    cache_krope, page_table, meta_tokens, prm)
```

```python
import functools
import math

import jax
import jax.numpy as jnp
from jax import lax
from jax.experimental import pallas as pl
from jax.experimental.pallas import tpu as pltpu

F32 = jnp.float32
BF16 = jnp.bfloat16

D_MODEL = 1024
N_META = 16
PAGE_SIZE = 128
RET_HEADS, RET_DK, RET_DV = 4, 64, 128
S5_WIDTH, S5_GROUP, S5_STATE = 512, 16, 64
S5_GROUPS = S5_WIDTH // S5_GROUP
S5_CH = S5_GROUPS * S5_STATE
RWKV_HEAD, RWKV_WIDTH = 64, 512
RWKV_HEADS = RWKV_WIDTH // RWKV_HEAD
LORA_W, LORA_A, LORA_G = 64, 64, 128
RWKV_PROJ = 3 * RWKV_WIDTH + LORA_W + LORA_A + LORA_G
RWKV_GN_EPS = 64e-5
MLA_HEADS, Q_LORA, KV_LORA, QK_NOPE, QK_ROPE, V_HEAD = 8, 384, 256, 64, 32, 64
MLA_WIDTH = MLA_HEADS * V_HEAD
MLA_SCALE = (QK_NOPE + QK_ROPE) ** -0.5
ROPE_THETA = 10000.0
N_EGROUPS, E_PER_GROUP, D_EXPERT = 4, 4, 256
N_EXPERTS = N_EGROUPS * E_PER_GROUP
DEPTH = 4
DN_ALPHA = (2 * DEPTH) ** 0.25
NSLAB = 8
SUB = 8
LANES = 128

VMEM_LIMIT = 56 * 1024 * 1024
WKV_L = 32
WKV_HG = 4
WKV_NB = 4
ROW_TILE = 1056
SEQ_TILE = 416
S5_TILE = 80
NEG = -1e30

NT = (((1,), (1,)), ((), ()))
TN = (((0,), (0,)), ((), ()))
NN = (((1,), (0,)), ((), ()))


def _cp(*sem):
    return pltpu.CompilerParams(dimension_semantics=sem, vmem_limit_bytes=VMEM_LIMIT)


def _div_tile(n, cap, mult=8):
    best = None
    for d in range(mult, min(n, cap) + 1, mult):
        if n % d == 0:
            best = d
    assert best is not None, (n, cap, mult)
    return best


def _dg(a, b, dims):
    return lax.dot_general(a, b, dims, preferred_element_type=F32)


def _bdot(a, b, dims=NN):
    return _dg(a.astype(BF16), b.astype(BF16), dims)


def _dot3(a, b, dims=NN):
    ah = a.astype(BF16)
    al = (a - ah.astype(F32)).astype(BF16)
    bh = b.astype(BF16)
    bl = (b - bh.astype(F32)).astype(BF16)
    return _dg(ah, bh, dims) + (_dg(ah, bl, dims) + _dg(al, bh, dims))


def _dotc(a, b, dims=NN):
    return _bdot(a, b, dims)


def _rb(x):
    return x.astype(BF16).astype(F32)


def _layer_norm(z, g, b, eps=1e-5):
    mu = jnp.mean(z, -1, keepdims=True)
    zc = z - mu
    var = jnp.mean(zc * zc, -1, keepdims=True)
    return zc * lax.rsqrt(var + eps) * g + b


def _rope(x, cos, sin_signed, half):
    w = x.shape[-1]
    lane = lax.broadcasted_iota(jnp.int32, x.shape, 1)
    first = (lane % (2 * half)) < half
    other = jnp.where(first, pltpu.roll(x, w - half, 1), pltpu.roll(x, half, 1))
    return x * cos + other * sin_signed


def _rope_tables(pos, dim, reps):
    inv = ROPE_THETA ** (-jnp.arange(0, dim, 2, dtype=F32) / dim)
    ang = pos.astype(F32)[:, None] * inv[None, :]
    cos = jnp.cos(ang)
    sin = jnp.sin(ang)
    cos = jnp.tile(jnp.concatenate([cos, cos], -1), (1, reps))
    sin = jnp.tile(jnp.concatenate([-sin, sin], -1), (1, reps))
    return cos, sin


def _drop_arg(kern, idx, *refs):
    return kern(*(refs[:idx] + refs[idx + 1:]))


class _Geom:
    def __init__(self, seq, ns, n_pages):
        self.tq = seq + N_META
        self.padf = (-self.tq) % WKV_L
        self.tp = self.padf + self.tq
        assert ns % (NSLAB * SUB) == 0
        self.ns = ns
        self.sps = ns // NSLAB
        self.slab = self.tp + self.sps
        self.n = NSLAB * self.slab
        self.past = n_pages * PAGE_SIZE
        self.tm = _div_tile(self.slab, ROW_TILE)
        self.tps = self.slab // self.tm

    def v3(self, x):
        return x.reshape(NSLAB, self.slab, x.shape[-1])

    def sample_blk(self, rows):
        per = self.sps // rows
        return lambda k: (k // per, self.tp // rows + k % per)


def _mm_kernel(x_ref, w_ref, o_ref):
    o_ref[...] = _dg(x_ref[...].astype(BF16), w_ref[...], NN)


def _matmul(x, w, tm, tn):
    n, k = x.shape
    m = w.shape[1]
    return pl.pallas_call(
        _mm_kernel,
        grid=(n // tm, m // tn),
        in_specs=[pl.BlockSpec((tm, k), lambda i, j: (i, 0)), pl.BlockSpec((k, tn), lambda i, j: (0, j))],
        out_specs=pl.BlockSpec((tm, tn), lambda i, j: (i, j)),
        out_shape=jax.ShapeDtypeStruct((n, m), F32),
        compiler_params=_cp("parallel", "arbitrary"),
        name="proj_in",
    )(x, w)


def _proj_ln_kernel(a1_ref, a2_ref, x_ref, w1_ref, w2_ref, g_ref, b_ref, o_ref):
    mix = _dg(a1_ref[...].astype(BF16), w1_ref[...], NN) + _dg(a2_ref[...].astype(BF16), w2_ref[...], NN)
    z = DN_ALPHA * x_ref[...] + mix
    o_ref[...] = _layer_norm(z, g_ref[...], b_ref[...])


def _proj_ln(a1, a2, x, w_out, g, b, tm):
    n = x.shape[0]
    k1, k2 = a1.shape[1], a2.shape[1]
    w1 = w_out[:k1].astype(BF16)
    w2 = w_out[k1:].astype(BF16)
    row = lambda i: (i, 0)
    fix = lambda i: (0, 0)
    return pl.pallas_call(
        _proj_ln_kernel,
        grid=(n // tm,),
        in_specs=[pl.BlockSpec((tm, k1), row), pl.BlockSpec((tm, k2), row), pl.BlockSpec((tm, D_MODEL), row),
                  pl.BlockSpec((k1, D_MODEL), fix), pl.BlockSpec((k2, D_MODEL), fix),
                  pl.BlockSpec((1, D_MODEL), fix), pl.BlockSpec((1, D_MODEL), fix)],
        out_specs=pl.BlockSpec((tm, D_MODEL), row),
        out_shape=jax.ShapeDtypeStruct((n, D_MODEL), F32),
        compiler_params=_cp("parallel"),
        name="proj_out_ln",
    )(a1, a2, x, w1, w2, g.reshape(1, -1), b.reshape(1, -1))


ROUTE0 = N_EGROUPS


def _route(lt):
    lane = lax.broadcasted_iota(jnp.int32, lt.shape, 1)
    lanef = lane.astype(F32)
    coarse = lane < N_EGROUPS
    lc = jnp.where(coarse, lt, NEG)
    m = jnp.max(lc, axis=1, keepdims=True)
    gsel = jnp.min(jnp.where(lc == m, lanef, 1e9), axis=1, keepdims=True)
    gate_c = 1.0 / jnp.sum(jnp.where(coarse, jnp.exp(lc - m), 0.0), axis=1, keepdims=True)
    grp = ((lane - ROUTE0) // E_PER_GROUP).astype(F32)
    lf = jnp.where(grp == gsel, lt, NEG)
    t1 = jnp.max(lf, axis=1, keepdims=True)
    i1 = jnp.min(jnp.where(lf == t1, lanef, 1e9), axis=1, keepdims=True)
    lf2 = jnp.where(lanef == i1, NEG, lf)
    t2 = jnp.max(lf2, axis=1, keepdims=True)
    i2 = jnp.min(jnp.where(lf2 == t2, lanef, 1e9), axis=1, keepdims=True)
    e2 = jnp.exp(t2 - t1)
    w1 = gate_c / (1.0 + e2)
    return jnp.where(lanef == i1, w1, 0.0) + jnp.where(lanef == i2, w1 * e2, 0.0)


def _moe_kernel(x_ref, wr_ref, br_ref, wg_ref, wu_ref, wd_ref, g_ref, b_ref, o_ref, xb_scr, comb_scr, h_scr):
    e = pl.program_id(1)

    @pl.when(e == 0)
    def _():
        xb = x_ref[...].astype(BF16)
        xb_scr[...] = xb
        comb_scr[...] = _route(_dg(xb, wr_ref[...], NN) + br_ref[...])

    xb = xb_scr[...]
    hg = _dg(xb, wg_ref[0], NN)
    hu = _dg(xb, wu_ref[0], NN)
    comb = comb_scr[...]
    lane = lax.broadcasted_iota(jnp.int32, comb.shape, 1)
    ce = jnp.sum(jnp.where(lane == e + ROUTE0, comb, 0.0), axis=1, keepdims=True)
    hb = ((hg * jax.nn.sigmoid(hg)) * hu * ce).astype(BF16)
    for ee in range(N_EXPERTS):
        @pl.when(e == ee)
        def _():
            h_scr[:, ee * D_EXPERT:(ee + 1) * D_EXPERT] = hb

    @pl.when(e == pl.num_programs(1) - 1)
    def _():
        z = DN_ALPHA * x_ref[...] + _dg(h_scr[...], wd_ref[...], NN)
        o_ref[...] = _layer_norm(z, g_ref[...], b_ref[...])


def _moe_ln(x, prm, layer, tm):
    n = x.shape[0]
    wr = jnp.concatenate([prm['moe_w_coarse'][layer],
                          jnp.transpose(prm['moe_w_fine'][layer], (1, 0, 2)).reshape(D_MODEL, N_EXPERTS)], 1)
    wr = jnp.pad(wr, ((0, 0), (0, LANES - wr.shape[1]))).astype(BF16)
    br = jnp.pad(jnp.concatenate([prm['moe_b_coarse'][layer], prm['moe_b_fine'][layer].reshape(-1)]),
                 (0, LANES - N_EGROUPS - N_EXPERTS)).reshape(1, LANES)
    row = lambda i, e: (i, 0)
    fix = lambda i, e: (0, 0)
    exp3 = lambda i, e: (e, 0, 0)
    return pl.pallas_call(
        _moe_kernel,
        grid=(n // tm, N_EXPERTS),
        in_specs=[pl.BlockSpec((tm, D_MODEL), row), pl.BlockSpec((D_MODEL, LANES), fix), pl.BlockSpec((1, LANES), fix),
                  pl.BlockSpec((1, D_MODEL, D_EXPERT), exp3), pl.BlockSpec((1, D_MODEL, D_EXPERT), exp3),
                  pl.BlockSpec((N_EXPERTS * D_EXPERT, D_MODEL), fix),
                  pl.BlockSpec((1, D_MODEL), fix), pl.BlockSpec((1, D_MODEL), fix)],
        out_specs=pl.BlockSpec((tm, D_MODEL), row),
        out_shape=jax.ShapeDtypeStruct((n, D_MODEL), F32),
        scratch_shapes=[pltpu.VMEM((tm, D_MODEL), BF16), pltpu.VMEM((tm, LANES), F32),
                        pltpu.VMEM((tm, N_EXPERTS * D_EXPERT), BF16)],
        compiler_params=_cp("parallel", "arbitrary"),
        name="moe_ln",
    )(x, wr, br, prm['moe_w_gate'][layer].astype(BF16), prm['moe_w_up'][layer].astype(BF16),
      prm['moe_w_down'][layer].astype(BF16).reshape(N_EXPERTS * D_EXPERT, D_MODEL),
      prm['ln2_g'][layer].reshape(1, -1), prm['ln2_b'][layer].reshape(1, -1))


def _ret_log_g():
    return [math.log1p(-2.0 ** (-5.0 - h)) for h in range(RET_HEADS)]


def _ret_kernel(q_ref, k_ref, v_ref, gate_ref, cos_ref, sin_ref, dmask_ref, cdec_ref, kdec_ref, sdec_ref,
                gng_ref, gnb_ref, o_ref, sout_ref, s_scr, *, L, padf):
    c = pl.program_id(1)
    HK, HV = RET_HEADS * RET_DK, RET_HEADS * RET_DV

    @pl.when(c == 0)
    def _():
        s_scr[...] = jnp.zeros_like(s_scr)

    cos, sin = cos_ref[...], sin_ref[...]
    q = _rope(q_ref[...], cos, sin, RET_DK // 2) * RET_DK ** -0.5
    k = _rope(k_ref[...], cos, sin, RET_DK // 2)
    real = (lax.broadcasted_iota(jnp.int32, (L, 1), 0) + c * L) >= padf
    k = jnp.where(real, k, 0.0)
    v = jnp.where(real, v_ref[...], 0.0)
    gate = gate_ref[...]
    qb, kb, vb = q.astype(BF16), k.astype(BF16), v.astype(BF16)
    sbd = s_scr[...]
    cross = _dg(qb, sbd.astype(BF16), NN) * cdec_ref[...]
    lane = lax.broadcasted_iota(jnp.int32, (L, HK), 1)
    outs = []
    for h in range(RET_HEADS):
        qh = jnp.where(lane // RET_DK == h, qb, jnp.zeros_like(qb))
        sc = _dg(qh, kb, NT) * dmask_ref[h]
        sl = slice(h * RET_DV, (h + 1) * RET_DV)
        o = _dg(sc.astype(BF16), vb[:, sl], NN) + cross[:, sl]
        mu = jnp.mean(o, -1, keepdims=True)
        oc = o - mu
        var = jnp.mean(oc * oc, -1, keepdims=True)
        outs.append(oc * lax.rsqrt(var + 1e-5))
    on = jnp.concatenate(outs, axis=1) * gng_ref[...] + gnb_ref[...]
    o_ref[...] = (gate * jax.nn.sigmoid(gate)) * on
    upd = _dg((k * kdec_ref[...]).astype(BF16), vb, TN)
    ri = lax.broadcasted_iota(jnp.int32, (HK, HV), 0) // RET_DK
    ci = lax.broadcasted_iota(jnp.int32, (HK, HV), 1) // RET_DV
    s_new = sdec_ref[...] * sbd + jnp.where(ri == ci, upd, 0.0)
    s_scr[...] = s_new

    @pl.when(c == pl.num_programs(1) - 1)
    def _():
        for h in range(RET_HEADS):
            sout_ref[h] = s_new[h * RET_DK:(h + 1) * RET_DK, h * RET_DV:(h + 1) * RET_DV]


def _retention_prompt(p3, geo, gn_g, gn_b):
    HK, HV = RET_HEADS * RET_DK, RET_HEADS * RET_DV
    L = _div_tile(geo.tp, SEQ_TILE)
    log_g = jnp.asarray(_ret_log_g(), F32)
    idx = jnp.arange(L, dtype=F32)
    diff = idx[:, None] - idx[None, :]
    dmask = jnp.where(diff >= 0, jnp.exp(log_g[:, None, None] * jnp.maximum(diff, 0.0)), 0.0)
    cdec = jnp.repeat(jnp.exp(log_g[None, :] * (idx + 1.0)[:, None]), RET_DV, axis=1)
    kdec = jnp.repeat(jnp.exp(log_g[None, :] * (L - 1.0 - idx)[:, None]), RET_DK, axis=1)
    sdec = jnp.broadcast_to(jnp.repeat(jnp.exp(log_g * L), RET_DK)[:, None], (HK, HV))
    cos, sin = _rope_tables(jnp.arange(geo.tp, dtype=jnp.int32) - geo.padf, RET_DK, RET_HEADS)
    fix2 = lambda b, c: (0, 0)
    tab = pl.BlockSpec((L, HK), lambda b, c: (c, 0))
    return pl.pallas_call(
        functools.partial(_ret_kernel, L=L, padf=geo.padf),
        grid=(NSLAB, geo.tp // L),
        in_specs=[pl.BlockSpec((None, L, HK), lambda b, c: (b, c, 0)), pl.BlockSpec((None, L, HK), lambda b, c: (b, c, 1)),
                  pl.BlockSpec((None, L, HV), lambda b, c: (b, c, 1)), pl.BlockSpec((None, L, HV), lambda b, c: (b, c, 2)),
                  tab, tab, pl.BlockSpec((RET_HEADS, L, L), lambda b, c: (0, 0, 0)),
                  pl.BlockSpec((L, HV), fix2), pl.BlockSpec((L, HK), fix2), pl.BlockSpec((HK, HV), fix2),
                  pl.BlockSpec((1, HV), fix2), pl.BlockSpec((1, HV), fix2)],
        out_specs=[pl.BlockSpec((None, L, HV), lambda b, c: (b, c, 0)),
                   pl.BlockSpec((None, RET_HEADS, RET_DK, RET_DV), lambda b, c: (b, 0, 0, 0))],
        out_shape=[jax.ShapeDtypeStruct((NSLAB, geo.slab, HV), F32),
                   jax.ShapeDtypeStruct((NSLAB, RET_HEADS, RET_DK, RET_DV), F32)],
        scratch_shapes=[pltpu.VMEM((HK, HV), F32)],
        compiler_params=_cp("parallel", "arbitrary"),
        name="retention",
    )(p3, p3, p3, p3, cos, sin, dmask, cdec, kdec, sdec, gn_g.reshape(1, -1), gn_b.reshape(1, -1))


def _ret_step_kernel(q_ref, k_ref, v_ref, gate_ref, cos_ref, sin_ref, gng_ref, gnb_ref, s_ref, o_ref, sout_ref):
    cos, sin = cos_ref[...], sin_ref[...]
    qb = _rb(_rope(q_ref[...], cos, sin, RET_DK // 2) * RET_DK ** -0.5)
    kb = _rb(_rope(k_ref[...], cos, sin, RET_DK // 2))
    vb = _rb(v_ref[...])
    gate = gate_ref[...]
    ri = lax.broadcasted_iota(jnp.int32, (RET_DK, RET_DK), 0)
    ci = lax.broadcasted_iota(jnp.int32, (RET_DK, RET_DK), 1)
    eye = jnp.where(ri == ci, 1.0, 0.0)

    def col(x):
        return jnp.sum(eye * jnp.broadcast_to(x, (RET_DK, RET_DK)), axis=1, keepdims=True)

    for j in range(SUB):
        outs = []
        for h, lg in enumerate(_ret_log_g()):
            g = math.exp(lg)
            qh = qb[j:j + 1, h * RET_DK:(h + 1) * RET_DK]
            kh = kb[j:j + 1, h * RET_DK:(h + 1) * RET_DK]
            vh = vb[j:j + 1, h * RET_DV:(h + 1) * RET_DV]
            s0 = s_ref[j, h]
            qk = _rb(jnp.sum(qh * kh, axis=1, keepdims=True))
            o = qk * vh + g * jnp.sum(col(qh) * _rb(s0), axis=0, keepdims=True)
            sout_ref[j, h] = g * s0 + col(kh) * vh
            mu = jnp.mean(o, -1, keepdims=True)
            oc = o - mu
            var = jnp.mean(oc * oc, -1, keepdims=True)
            outs.append(oc * lax.rsqrt(var + 1e-5))
        on = jnp.concatenate(outs, axis=1) * gng_ref[...] + gnb_ref[...]
        gj = gate[j:j + 1]
        o_ref[j:j + 1, :] = (gj * jax.nn.sigmoid(gj)) * on


def _retention_sample(p3, geo, s0, gn_g, gn_b, out_prev):
    HK, HV = RET_HEADS * RET_DK, RET_HEADS * RET_DV
    cos, sin = _rope_tables(jnp.full((1,), geo.past, jnp.int32), RET_DK, RET_HEADS)
    sb = geo.sample_blk(SUB)
    fix = lambda k: (0, 0)
    st = pl.BlockSpec((SUB, RET_HEADS, RET_DK, RET_DV), lambda k: (k, 0, 0, 0))
    return pl.pallas_call(
        functools.partial(_drop_arg, _ret_step_kernel, 9),
        grid=(geo.ns // SUB,),
        in_specs=[pl.BlockSpec((None, SUB, HK), lambda k: sb(k) + (0,)), pl.BlockSpec((None, SUB, HK), lambda k: sb(k) + (1,)),
                  pl.BlockSpec((None, SUB, HV), lambda k: sb(k) + (1,)), pl.BlockSpec((None, SUB, HV), lambda k: sb(k) + (2,)),
                  pl.BlockSpec((1, HK), fix), pl.BlockSpec((1, HK), fix),
                  pl.BlockSpec((1, HV), fix), pl.BlockSpec((1, HV), fix), st, pl.BlockSpec(memory_space=pl.ANY)],
        out_specs=[pl.BlockSpec((None, SUB, HV), lambda k: sb(k) + (0,)), st],
        out_shape=[jax.ShapeDtypeStruct(out_prev.shape, F32), jax.ShapeDtypeStruct(s0.shape, F32)],
        input_output_aliases={9: 0},
        compiler_params=_cp("parallel"),
        name="retention_step",
    )(p3, p3, p3, p3, cos, sin, gn_g.reshape(1, -1), gn_b.reshape(1, -1), s0, out_prev)


def _s5_params(prm, li):
    f32 = F32
    a_re, a_im = prm['s5_a_re'][li].astype(f32), prm['s5_a_im'][li].astype(f32)
    dt = jnp.exp(prm['s5_log_dt'][li].astype(f32))[:, None]
    mag = jnp.exp(dt * a_re)
    ab_re, ab_im = mag * jnp.cos(dt * a_im), mag * jnp.sin(dt * a_im)
    den = a_re * a_re + a_im * a_im
    n_re = ab_re - 1.0
    f_re = (n_re * a_re + ab_im * a_im) / den
    f_im = (ab_im * a_re - n_re * a_im) / den
    b_re, b_im = prm['s5_b_re'][li].astype(f32), prm['s5_b_im'][li].astype(f32)
    bb_re = f_re[..., None] * b_re - f_im[..., None] * b_im
    bb_im = f_re[..., None] * b_im + f_im[..., None] * b_re
    eye = jnp.eye(S5_GROUPS, dtype=f32)

    def blockdiag_in(w):
        return jnp.einsum('gpc,gh->gchp', w, eye).reshape(S5_WIDTH, S5_CH)

    def blockdiag_out(w):
        return jnp.einsum('gcp,gh->gphc', w, eye).reshape(S5_CH, S5_WIDTH)

    bbw = jnp.concatenate([blockdiag_in(bb_re), blockdiag_in(bb_im)], 1).astype(BF16)
    ccw = jnp.concatenate([blockdiag_out(prm['s5_c_re'][li]), -blockdiag_out(prm['s5_c_im'][li])], 0).astype(BF16)
    lam = jnp.concatenate([ab_re.reshape(1, -1), ab_im.reshape(1, -1)], 1)
    return [bbw, ccw, lam, prm['s5_d'][li].reshape(1, -1), prm['s5_w_glu'][li].astype(BF16),
            prm['s5_b_glu'][li].reshape(1, -1)]


def _s5_specs(fix):
    return [pl.BlockSpec((S5_WIDTH, 2 * S5_CH), fix), pl.BlockSpec((2 * S5_CH, S5_WIDTH), fix),
            pl.BlockSpec((1, 2 * S5_CH), fix), pl.BlockSpec((1, S5_WIDTH), fix),
            pl.BlockSpec((S5_WIDTH, S5_WIDTH), fix), pl.BlockSpec((1, S5_WIDTH), fix)]


def _s5_out(h, u, cc_ref, d_ref, wglu_ref, bglu_ref):
    y = _dg(h.astype(BF16), cc_ref[...], NN) + d_ref[...] * u
    z = 0.5 * y * (1.0 + jnp.tanh(math.sqrt(2.0 / math.pi) * (y + 0.044715 * (y * y * y))))
    return z * jax.nn.sigmoid(_dg(z.astype(BF16), wglu_ref[...], NN) + bglu_ref[...])


S5_NJ = 2 * S5_CH // LANES
S5_PAIRS = 4


def _s5_prompt_kernel(u_ref, bb_ref, cc_ref, lam_ref, d_ref, wglu_ref, bglu_ref, o_ref, hout_ref, sc, h_scr,
                      *, lt, pitch, padf):
    i = pl.program_id(0)

    @pl.when(i == 0)
    def _():
        h_scr[...] = jnp.zeros_like(h_scr)

    real = (lax.broadcasted_iota(jnp.int32, (lt, 1), 0) + i * lt) >= padf
    for b in range(NSLAB):
        bu = _dg(jnp.where(real, u_ref[b], 0.0).astype(BF16), bb_ref[...], NN)
        for j in range(S5_NJ):
            sc[j, b * pitch:b * pitch + lt, :] = bu[:, j * LANES:(j + 1) * LANES]
    half = S5_NJ // 2
    for j0 in range(0, half, S5_PAIRS):
        js = list(range(j0, j0 + S5_PAIRS))
        lr = [jnp.broadcast_to(lam_ref[:, j * LANES:(j + 1) * LANES], (NSLAB, LANES)) for j in js]
        li = [jnp.broadcast_to(lam_ref[:, S5_CH + j * LANES:S5_CH + (j + 1) * LANES], (NSLAB, LANES)) for j in js]

        def step(t, carry):
            hr, hi = carry
            rows = pl.ds(t, NSLAB, stride=pitch)
            nr, ni = [], []
            for q, j in enumerate(js):
                r = lr[q] * hr[q] - li[q] * hi[q] + sc[j, rows, :]
                m = lr[q] * hi[q] + li[q] * hr[q] + sc[half + j, rows, :]
                sc[j, rows, :] = r
                sc[half + j, rows, :] = m
                nr.append(r)
                ni.append(m)
            return tuple(nr), tuple(ni)

        init = (tuple(h_scr[:, j * LANES:(j + 1) * LANES] for j in js),
                tuple(h_scr[:, S5_CH + j * LANES:S5_CH + (j + 1) * LANES] for j in js))
        hr, hi = lax.fori_loop(0, lt, step, init)
        for q, j in enumerate(js):
            h_scr[:, j * LANES:(j + 1) * LANES] = hr[q]
            h_scr[:, S5_CH + j * LANES:S5_CH + (j + 1) * LANES] = hi[q]
    for b in range(NSLAB):
        h = jnp.concatenate([sc[j, b * pitch:b * pitch + lt, :] for j in range(S5_NJ)], axis=1)
        o_ref[b] = _s5_out(h, u_ref[b], cc_ref, d_ref, wglu_ref, bglu_ref)

    @pl.when(i == pl.num_programs(0) - 1)
    def _():
        hout_ref[...] = h_scr[...]


def _s5_prompt(p3, geo, prm, li):
    lt = _div_tile(geo.tp, S5_TILE, 16)
    pitch = lt + SUB
    ucol = p3.shape[-1] // S5_WIDTH - 1
    fix = lambda i: (0, 0)
    return pl.pallas_call(
        functools.partial(_s5_prompt_kernel, lt=lt, pitch=pitch, padf=geo.padf),
        grid=(geo.tp // lt,),
        in_specs=[pl.BlockSpec((NSLAB, lt, S5_WIDTH), lambda i: (0, i, ucol))] + _s5_specs(fix),
        out_specs=[pl.BlockSpec((NSLAB, lt, S5_WIDTH), lambda i: (0, i, 0)), pl.BlockSpec((NSLAB, 2 * S5_CH), fix)],
        out_shape=[jax.ShapeDtypeStruct((NSLAB, geo.slab, S5_WIDTH), F32), jax.ShapeDtypeStruct((NSLAB, 2 * S5_CH), F32)],
        scratch_shapes=[pltpu.VMEM((S5_NJ, NSLAB * pitch, LANES), F32), pltpu.VMEM((NSLAB, 2 * S5_CH), F32)],
        compiler_params=_cp("arbitrary"),
        name="s5",
    )(p3, *_s5_params(prm, li))


def _s5_step_kernel(u_ref, bb_ref, cc_ref, lam_ref, d_ref, wglu_ref, bglu_ref, h0_ref, o_ref, hout_ref):
    u = u_ref[...]
    bu = _dg(u.astype(BF16), bb_ref[...], NN)
    h0 = h0_ref[...]
    lr, li = lam_ref[:, :S5_CH], lam_ref[:, S5_CH:]
    hr, hi = h0[:, :S5_CH], h0[:, S5_CH:]
    h = jnp.concatenate([lr * hr - li * hi + bu[:, :S5_CH], lr * hi + li * hr + bu[:, S5_CH:]], axis=1)
    hout_ref[...] = h
    o_ref[...] = _s5_out(h, u, cc_ref, d_ref, wglu_ref, bglu_ref)


def _s5_sample(p3, geo, h0, prm, li, out_prev):
    rows = geo.sps
    sb = geo.sample_blk(rows)
    ucol = p3.shape[-1] // S5_WIDTH - 1
    fix = lambda k: (0, 0)
    st = pl.BlockSpec((rows, 2 * S5_CH), lambda k: (k, 0))
    return pl.pallas_call(
        functools.partial(_drop_arg, _s5_step_kernel, 8),
        grid=(NSLAB,),
        in_specs=[pl.BlockSpec((None, rows, S5_WIDTH), lambda k: sb(k) + (ucol,))] + _s5_specs(fix)
        + [st, pl.BlockSpec(memory_space=pl.ANY)],
        out_specs=[pl.BlockSpec((None, rows, S5_WIDTH), lambda k: sb(k) + (0,)), st],
        out_shape=[jax.ShapeDtypeStruct(out_prev.shape, F32), jax.ShapeDtypeStruct(h0.shape, F32)],
        input_output_aliases={8: 0},
        compiler_params=_cp("parallel"),
        name="s5_step",
    )(p3, *_s5_params(prm, li), h0, out_prev)


def _head_sum(x, ones_bd):
    return _dot3(x, ones_bd)


def _rwkv_prep_body(pr, prev, mu_ref, w0_ref, w2_ref, a0_ref, a2_ref, g2_ref, kk_ref, ka_ref, ones_ref,
                    r_ref, lw_ref, k_ref, v_ref, a_ref, b_ref, g_ref):
    W = RWKV_WIDTH
    xm = pr + (prev - pr) * mu_ref[...]
    r = xm[:, :W]
    k = xm[:, W:2 * W]
    v = xm[:, 2 * W:3 * W]
    xwa = xm[:, 3 * W:3 * W + LORA_W + LORA_A]
    xg = xm[:, 3 * W + LORA_W + LORA_A:]
    wl = w0_ref[...] + _bdot(jnp.tanh(xwa), w2_ref[...])
    sp = jnp.maximum(-wl, 0.0) + jnp.log(1.0 + jnp.exp(-jnp.abs(wl)))
    w = -sp - 0.5
    lw = -jnp.exp(w)
    a = jax.nn.sigmoid(a0_ref[...] + _bdot(xwa, a2_ref[...]))
    g = _bdot(jax.nn.sigmoid(xg), g2_ref[...])
    kk = k * kk_ref[...]
    nrm = jnp.sqrt(_head_sum(kk * kk, ones_ref[...]))
    kk = kk / jnp.maximum(nrm, 1e-12)
    k2 = k * (1.0 + (a - 1.0) * ka_ref[...])
    r_ref[...] = r
    lw_ref[...] = lw
    k_ref[...] = k2
    v_ref[...] = v
    a_ref[...] = -kk
    b_ref[...] = kk * a
    g_ref[...] = g


def _rwkv_prep_prompt_kernel(pr_ref, prev8_ref, *rest, tm, padf):
    pr = pr_ref[...]
    t = lax.broadcasted_iota(jnp.int32, (tm, 1), 0)
    last = jnp.broadcast_to(prev8_ref[SUB - 1:SUB, :], pr.shape)
    prev = jnp.where(t == 0, last, pltpu.roll(pr, 1, 0))
    first = (t + pl.program_id(1) * tm) == padf
    prev = jnp.where(first, 0.0, prev)
    _rwkv_prep_body(pr, prev, *rest)


def _rwkv_prep_sample_kernel(pr_ref, prev_ref, *rest):
    _rwkv_prep_body(pr_ref[...], prev_ref[...], *rest[:9], *rest[9 + 7:])


def _rwkv_prep(p3, geo, shift_s, prm, li):
    W = RWKV_WIDTH
    zeros64 = jnp.zeros((LORA_W, W), F32)
    w2p = jnp.concatenate([prm['rwkv_w2'][li], zeros64], 0).astype(BF16)
    a2p = jnp.concatenate([zeros64, prm['rwkv_a2'][li]], 0).astype(BF16)
    hid = jnp.arange(W) // RWKV_HEAD
    ones_bd = (hid[:, None] == hid[None, :]).astype(F32)
    consts = [prm['rwkv_mu'][li].reshape(1, -1), prm['rwkv_w0'][li].reshape(1, -1), w2p,
              prm['rwkv_a0'][li].reshape(1, -1), a2p, prm['rwkv_g2'][li].astype(BF16),
              prm['rwkv_k_k'][li].reshape(1, -1), prm['rwkv_k_a'][li].reshape(1, -1), ones_bd]
    outs_shape = [jax.ShapeDtypeStruct((NSLAB, geo.slab, W), F32)] * 7
    tm = _div_tile(geo.tp, SEQ_TILE)
    outs = pl.pallas_call(
        functools.partial(_rwkv_prep_prompt_kernel, tm=tm, padf=geo.padf),
        grid=(NSLAB, geo.tp // tm),
        in_specs=[pl.BlockSpec((None, tm, RWKV_PROJ), lambda b, c: (b, c, 0)),
                  pl.BlockSpec((None, SUB, RWKV_PROJ), lambda b, c: (b, jnp.maximum(c * (tm // SUB) - 1, 0), 0))]
        + [pl.BlockSpec(x.shape, lambda b, c: (0, 0)) for x in consts],
        out_specs=[pl.BlockSpec((None, tm, W), lambda b, c: (b, c, 0))] * 7,
        out_shape=outs_shape,
        compiler_params=_cp("parallel", "parallel"),
        name="rwkv_prep_prompt",
    )(p3, p3, *consts)
    rows = geo.sps
    sb = geo.sample_blk(rows)
    nin = 2 + len(consts)
    outs = pl.pallas_call(
        _rwkv_prep_sample_kernel,
        grid=(NSLAB,),
        in_specs=[pl.BlockSpec((None, rows, RWKV_PROJ), lambda k: sb(k) + (0,)),
                  pl.BlockSpec((rows, RWKV_PROJ), lambda k: (k, 0))]
        + [pl.BlockSpec(x.shape, lambda k: (0, 0)) for x in consts] + [pl.BlockSpec(memory_space=pl.ANY)] * 7,
        out_specs=[pl.BlockSpec((None, rows, W), lambda k: sb(k) + (0,))] * 7,
        out_shape=outs_shape,
        input_output_aliases={nin + j: j for j in range(7)},
        compiler_params=_cp("parallel"),
        name="rwkv_prep_sample",
    )(p3, shift_s, *consts, *outs)
    return outs


def _wkv_chunk_kernel(r_ref, lw_ref, k_ref, v_ref, a_ref, b_ref, y_ref, gout_ref, g_scr, *, padf):
    c = pl.program_id(1)
    L, HG = WKV_L, WKV_HG
    NG = RWKV_HEADS // HG
    WD = HG * RWKV_HEAD
    M = HG * L

    @pl.when(c == 0)
    def _():
        g_scr[...] = jnp.zeros_like(g_scr)

    real = (lax.broadcasted_iota(jnp.int32, (L, 1), 0) + c * L) >= padf

    def ld(ref):
        return jnp.where(real, jnp.concatenate([ref[i] for i in range(WKV_NB)], axis=1), 0.0)

    r, lw, k, v, a, b = ld(r_ref), ld(lw_ref), ld(k_ref), ld(v_ref), ld(a_ref), ld(b_ref)
    ti = lax.broadcasted_iota(jnp.int32, (L, L), 0)
    tj = lax.broadcasted_iota(jnp.int32, (L, L), 1)
    tri = jnp.where(ti >= tj, 1.0, 0.0)
    cum = _dot3(tri, lw)
    dec = jnp.exp(cum)
    inv = jnp.exp(-cum)
    at = a * jnp.exp(cum - lw)
    bt = b * inv
    kt = k * inv
    rt = r * dec
    rh = lax.broadcasted_iota(jnp.int32, (M, WD), 0) // L
    lh = lax.broadcasted_iota(jnp.int32, (M, WD), 1) // RWKV_HEAD
    bd = rh == lh
    mi = lax.broadcasted_iota(jnp.int32, (M, M), 0)
    mj = lax.broadcasted_iota(jnp.int32, (M, M), 1)
    same = (mi // L) == (mj // L)
    m_strict = jnp.logical_and(same, (mj % L) < (mi % L))
    m_incl = jnp.logical_and(same, (mj % L) <= (mi % L))
    eye = jnp.where(mi == mj, 1.0, 0.0)

    def stack(x):
        return jnp.where(bd, jnp.concatenate([x] * HG, axis=0), 0.0)

    g_last = []
    for grp in range(WKV_NB * NG):
        sl = slice(grp * WD, (grp + 1) * WD)
        a_s, r_s, b_s, k_s, v_s = stack(at[:, sl]), stack(rt[:, sl]), stack(bt[:, sl]), stack(kt[:, sl]), stack(v[:, sl])
        g = g_scr[grp]
        bk_s = jnp.concatenate([b_s, k_s], axis=0).astype(BF16)
        prod = _bdot(jnp.concatenate([a_s, r_s], axis=0), bk_s, NT)
        ab = jnp.where(m_strict, prod[:M, :M], 0.0)
        ak = jnp.where(m_strict, prod[:M, M:], 0.0)
        rb = jnp.where(m_incl, prod[M:, :M], 0.0)
        rk = jnp.where(m_incl, prod[M:, M:], 0.0)
        tinv = eye + ab
        pw = ab
        for _ in range(int(math.log2(L)) - 1):
            pw = _dotc(pw, pw)
            tinv = tinv + _dotc(tinv, pw)
        z = _bdot(a_s, g, NT) + _bdot(ak, v_s)
        u = _dotc(tinv, z)
        ybd = _bdot(r_s, g, NT) + _bdot(rb, u) + _bdot(rk, v_s)
        y = ybd[0:L]
        for h in range(1, HG):
            y = y + ybd[h * L:(h + 1) * L]
        y_ref[grp // NG, :, (grp % NG) * WD:(grp % NG + 1) * WD] = y
        upd = _bdot(jnp.concatenate([u, v_s], axis=0), bk_s, TN)
        g_new = (g + upd) * dec[L - 1:L, sl]
        g_scr[grp] = g_new
        g_last.append(g_new)

    @pl.when(c == pl.num_programs(1) - 1)
    def _():
        for grp, g_new in enumerate(g_last):
            gout_ref[grp // NG, grp % NG] = g_new


def _wkv_prompt(seq, geo):
    L = WKV_L
    NG = RWKV_HEADS // WKV_HG
    WD = WKV_HG * RWKV_HEAD
    spec = pl.BlockSpec((WKV_NB, L, RWKV_WIDTH), lambda b, c: (b, c, 0))
    y, g = pl.pallas_call(
        functools.partial(_wkv_chunk_kernel, padf=geo.padf),
        grid=(NSLAB // WKV_NB, geo.tp // L),
        in_specs=[spec] * 6,
        out_specs=[spec, pl.BlockSpec((WKV_NB, NG, WD, WD), lambda b, c: (b, 0, 0, 0))],
        out_shape=[jax.ShapeDtypeStruct((NSLAB, geo.slab, RWKV_WIDTH), F32),
                   jax.ShapeDtypeStruct((NSLAB, NG, WD, WD), F32)],
        scratch_shapes=[pltpu.VMEM((WKV_NB * NG, WD, WD), F32)],
        compiler_params=_cp("parallel", "arbitrary"),
        name="wkv_chunk",
    )(*seq)
    g = g.reshape(NSLAB, NG, WKV_HG, RWKV_HEAD, WKV_HG, RWKV_HEAD)
    idx = jnp.arange(WKV_HG)
    s = g[:, :, idx, :, idx, :]
    s = jnp.transpose(s, (1, 2, 0, 3, 4)).reshape(NSLAB, RWKV_HEADS, RWKV_HEAD, RWKV_HEAD)
    return y, s


def _wkv_step_kernel(r_ref, lw_ref, k_ref, v_ref, a_ref, b_ref, s_ref, y_ref, sout_ref):
    H, D = RWKV_HEADS, RWKV_HEAD
    ri = lax.broadcasted_iota(jnp.int32, (H * D, D), 0) % D
    ci = lax.broadcasted_iota(jnp.int32, (H * D, D), 1)
    eye = jnp.where(ri == ci, 1.0, 0.0)

    def expand(x):
        return jnp.concatenate([jnp.broadcast_to(x[:, h * D:(h + 1) * D], (D, D)) for h in range(H)], axis=0)

    for j in range(SUB):
        row = lambda ref: ref[j:j + 1, :]
        s = s_ref[j].reshape(H * D, D)
        w_e = jnp.exp(expand(row(lw_ref)))
        v_col = jnp.sum(expand(row(v_ref)) * eye, axis=1, keepdims=True)
        sa = jnp.sum(s * expand(row(a_ref)), axis=1, keepdims=True)
        s_new = s * w_e + sa * expand(row(b_ref)) + v_col * expand(row(k_ref))
        y_col = jnp.sum(s_new * expand(row(r_ref)), axis=1, keepdims=True)
        ye = y_col * eye
        y_row = jnp.concatenate([jnp.sum(ye[h * D:(h + 1) * D], axis=0, keepdims=True) for h in range(H)], axis=1)
        y_ref[j:j + 1, :] = y_row
        sout_ref[j] = s_new.reshape(H, D, D)


def _wkv_sample(seq, geo, s0, y_prev):
    sb = geo.sample_blk(SUB)
    spec = pl.BlockSpec((None, SUB, RWKV_WIDTH), lambda k: sb(k) + (0,))
    sspec = pl.BlockSpec((SUB, RWKV_HEADS, RWKV_HEAD, RWKV_HEAD), lambda k: (k, 0, 0, 0))
    return pl.pallas_call(
        functools.partial(_drop_arg, _wkv_step_kernel, 7),
        grid=(geo.ns // SUB,),
        in_specs=[spec] * 6 + [sspec, pl.BlockSpec(memory_space=pl.ANY)],
        out_specs=[spec, sspec],
        out_shape=[jax.ShapeDtypeStruct(y_prev.shape, F32), jax.ShapeDtypeStruct(s0.shape, F32)],
        input_output_aliases={7: 0},
        compiler_params=_cp("parallel"),
        name="wkv_step",
    )(*seq, s0, y_prev)


def _rwkv_post_kernel(y_ref, r_ref, k_ref, v_ref, g_ref, gng_ref, gnb_ref, rk_ref, ones_ref, o_ref):
    y = y_ref[...]
    ones = ones_ref[...]
    mu = _head_sum(y, ones) * (1.0 / RWKV_HEAD)
    yc = y - mu
    var = _head_sum(yc * yc, ones) * (1.0 / RWKV_HEAD)
    yn = yc * lax.rsqrt(var + RWKV_GN_EPS) * gng_ref[...] + gnb_ref[...]
    v = v_ref[...]
    bonus = _head_sum(r_ref[...] * k_ref[...] * rk_ref[...], ones) * v
    o_ref[...] = (yn + bonus) * g_ref[...]


def _rwkv_post(y, r, k, v, g, prm, li, tm):
    n = y.shape[0]
    W = RWKV_WIDTH
    hid = jnp.arange(W) // RWKV_HEAD
    ones_bd = (hid[:, None] == hid[None, :]).astype(F32)
    row = pl.BlockSpec((tm, W), lambda i: (i, 0))
    vec = pl.BlockSpec((1, W), lambda i: (0, 0))
    return pl.pallas_call(
        _rwkv_post_kernel,
        grid=(n // tm,),
        in_specs=[row] * 5 + [vec] * 3 + [pl.BlockSpec((W, W), lambda i: (0, 0))],
        out_specs=row,
        out_shape=jax.ShapeDtypeStruct((n, W), F32),
        compiler_params=_cp("parallel"),
        name="rwkv_post",
    )(y, r, k, v, g, prm['rwkv_gn_g'][li].reshape(1, -1), prm['rwkv_gn_b'][li].reshape(1, -1),
      prm['rwkv_r_k'][li].reshape(1, -1), ones_bd)


def _mla_prep_kernel(ckv_ref, qa_ref, kr_ref, cos_ref, sin_ref, qn_ref, kvn_ref, wqb_ref, wuk_ref,
                     qlat_ref, qrope_ref, c_ref, kr8_ref):
    qa = qa_ref[:, :Q_LORA]
    qa = qa * lax.rsqrt(jnp.mean(qa * qa, -1, keepdims=True) + 1e-6) * qn_ref[...]
    q = _bdot(qa, wqb_ref[...])
    nope_w = MLA_HEADS * QK_NOPE
    cos, sin = cos_ref[...], sin_ref[...]
    qrope_ref[...] = _rope(q[:, nope_w:], cos, sin, QK_ROPE // 2)
    qlat_ref[...] = _bdot(q[:, :nope_w], wuk_ref[...])
    ckv = ckv_ref[...]
    c_ref[...] = ckv * lax.rsqrt(jnp.mean(ckv * ckv, -1, keepdims=True) + 1e-6) * kvn_ref[...]
    kr8_ref[...] = _rope(kr_ref[...], cos, sin, QK_ROPE // 2)


def _mla_prep(p, geo, pos_slab, prm, li):
    n = p.shape[0]
    tm, tps = geo.tm, geo.tps
    rw = MLA_HEADS * QK_ROPE
    cos, sin = _rope_tables(pos_slab, QK_ROPE, MLA_HEADS)
    wqb = prm['mla_w_qb'][li].reshape(Q_LORA, MLA_HEADS, QK_NOPE + QK_ROPE)
    wqb = jnp.concatenate([wqb[:, :, :QK_NOPE].reshape(Q_LORA, -1), wqb[:, :, QK_NOPE:].reshape(Q_LORA, -1)], 1)
    eye = jnp.eye(MLA_HEADS, dtype=F32)
    wuk = jnp.einsum('chn,hg->hngc', prm['mla_w_uk'][li], eye).reshape(MLA_HEADS * QK_NOPE, MLA_HEADS * KV_LORA)
    fix = lambda i: (0, 0)
    c0 = RWKV_PROJ // KV_LORA
    q0 = (RWKV_PROJ + KV_LORA) // 512
    k0 = (RWKV_PROJ + KV_LORA + 512) // rw
    tab = pl.BlockSpec((tm, rw), lambda i: (i % tps, 0))
    return pl.pallas_call(
        _mla_prep_kernel,
        grid=(n // tm,),
        in_specs=[pl.BlockSpec((tm, KV_LORA), lambda i: (i, c0)), pl.BlockSpec((tm, 512), lambda i: (i, q0)),
                  pl.BlockSpec((tm, rw), lambda i: (i, k0)), tab, tab,
                  pl.BlockSpec((1, Q_LORA), fix), pl.BlockSpec((1, KV_LORA), fix),
                  pl.BlockSpec(wqb.shape, fix), pl.BlockSpec(wuk.shape, fix)],
        out_specs=[pl.BlockSpec((tm, MLA_HEADS * KV_LORA), lambda i: (i, 0)), pl.BlockSpec((tm, rw), lambda i: (i, 0)),
                   pl.BlockSpec((tm, KV_LORA), lambda i: (i, 0)), pl.BlockSpec((tm, rw), lambda i: (i, 0))],
        out_shape=[jax.ShapeDtypeStruct((n, MLA_HEADS * KV_LORA), F32), jax.ShapeDtypeStruct((n, rw), F32),
                   jax.ShapeDtypeStruct((n, KV_LORA), F32), jax.ShapeDtypeStruct((n, rw), F32)],
        compiler_params=_cp("parallel"),
        name="mla_prep",
    )(p, p, p, cos, sin, prm['mla_q_norm'][li].reshape(1, -1), prm['mla_kv_norm'][li].reshape(1, -1),
      wqb.astype(BF16), wuk.astype(BF16))


def _wuv_blockdiag(w_uv):
    eye = jnp.eye(MLA_HEADS, dtype=F32)
    return jnp.einsum('chv,hg->hcgv', w_uv, eye).reshape(MLA_HEADS * KV_LORA, MLA_WIDTH).astype(BF16)


def _mla_attn_kernel(qlat_ref, qrope_ref, c_ref, kr_ref, wuv_ref, o_ref, qs_scr, qrs_scr, m_scr, l_scr, acc_scr,
                     *, lq, padf):
    qi, ki = pl.program_id(1), pl.program_id(2)
    H = MLA_HEADS

    @pl.when(ki == 0)
    def _():
        ql = qlat_ref[...]
        qs_scr[...] = jnp.concatenate([ql[:, h * KV_LORA:(h + 1) * KV_LORA] for h in range(H)], axis=0).astype(BF16)
        qr = qrope_ref[...].astype(BF16)
        lane = lax.broadcasted_iota(jnp.int32, qr.shape, 1) // QK_ROPE
        qrs_scr[...] = jnp.concatenate([jnp.where(lane == h, qr, jnp.zeros_like(qr)) for h in range(H)], axis=0)
        m_scr[...] = jnp.full_like(m_scr, NEG)
        l_scr[...] = jnp.zeros_like(l_scr)
        acc_scr[...] = jnp.zeros_like(acc_scr)

    def tile(masked):
        cb = c_ref[...].astype(BF16)
        st = (_dg(cb, qs_scr[...], NT) + _dg(kr_ref[...].astype(BF16), qrs_scr[...], NT)) * MLA_SCALE
        if masked:
            kpos = ki * lq + lax.broadcasted_iota(jnp.int32, st.shape, 0)
            qpos = qi * lq + lax.broadcasted_iota(jnp.int32, st.shape, 1) % lq
            st = jnp.where(jnp.logical_and(kpos <= qpos, kpos >= padf), st, NEG)
        m_old = m_scr[...]
        m_new = jnp.maximum(m_old, jnp.max(st, axis=0, keepdims=True))
        alpha = jnp.exp(m_old - m_new)
        pt = jnp.exp(st - m_new)
        l_scr[...] = alpha * l_scr[...] + jnp.sum(pt, axis=0, keepdims=True)
        acc_scr[...] = alpha * acc_scr[...] + _dg(cb, pt.astype(BF16), TN)
        m_scr[...] = m_new

    assert padf <= lq
    edge = jnp.logical_or(ki == qi, ki == 0)
    pl.when(jnp.logical_and(ki <= qi, edge))(functools.partial(tile, True))
    pl.when(jnp.logical_and(ki < qi, jnp.logical_not(edge)))(functools.partial(tile, False))

    @pl.when(ki == pl.num_programs(2) - 1)
    def _():
        ot = (acc_scr[...] * (1.0 / l_scr[...])).astype(BF16)
        y = jnp.zeros(o_ref.shape, F32)
        for h in range(H):
            y = y + _dg(ot[:, h * lq:(h + 1) * lq], wuv_ref[h], TN)
        o_ref[...] = y


def _mla_attn_prompt(qlat3, qrope3, c3, kr3, w_uv, geo):
    lq = _div_tile(geo.tp, SEQ_TILE)
    nq = geo.tp // lq
    wuv = _wuv_blockdiag(w_uv).reshape(MLA_HEADS, KV_LORA, MLA_WIDTH)
    rw = MLA_HEADS * QK_ROPE
    qmap = lambda b, qi, ki: (b, qi, 0)
    kmap = lambda b, qi, ki: (b, jnp.minimum(ki, qi), 0)
    return pl.pallas_call(
        functools.partial(_mla_attn_kernel, lq=lq, padf=geo.padf),
        grid=(NSLAB, nq, nq),
        in_specs=[pl.BlockSpec((None, lq, MLA_HEADS * KV_LORA), qmap), pl.BlockSpec((None, lq, rw), qmap),
                  pl.BlockSpec((None, lq, KV_LORA), kmap), pl.BlockSpec((None, lq, rw), kmap),
                  pl.BlockSpec(wuv.shape, lambda b, qi, ki: (0, 0, 0))],
        out_specs=pl.BlockSpec((None, lq, MLA_WIDTH), qmap),
        out_shape=jax.ShapeDtypeStruct((NSLAB, geo.slab, MLA_WIDTH), F32),
        scratch_shapes=[pltpu.VMEM((MLA_HEADS * lq, KV_LORA), BF16), pltpu.VMEM((MLA_HEADS * lq, rw), BF16),
                        pltpu.VMEM((1, MLA_HEADS * lq), F32), pltpu.VMEM((1, MLA_HEADS * lq), F32),
                        pltpu.VMEM((KV_LORA, MLA_HEADS * lq), F32)],
        compiler_params=_cp("parallel", "parallel", "arbitrary"),
        name="mla_attn",
    )(qlat3, qrope3, c3, kr3, wuv)


def _mla_decode_kernel(pt_ref, qlat_ref, qrope_ref, cnew_ref, krnew_ref, kv_hbm, kr_hbm, tile_ref, wuv_ref,
                       yprev_ref, o_ref, kv_buf, kr_buf, kb_scr, s_scr, sem, *, li, n_pages, ch):
    del yprev_ref
    b = pl.program_id(0)
    slot = b % 2
    H = MLA_HEADS

    def page_copies(page, sl, pg):
        return (pltpu.make_async_copy(kv_hbm.at[page, li], kv_buf.at[sl, pg], sem.at[0, sl]),
                pltpu.make_async_copy(kr_hbm.at[page, li], kr_buf.at[sl, pg], sem.at[1, sl]))

    def start_fetch(bb, sl):
        def body(pg, carry):
            for cp in page_copies(pt_ref[bb, pg], sl, pg):
                cp.start()
            return carry
        lax.fori_loop(0, n_pages, body, 0)

    def wait_fetch(sl):
        def body(pg, carry):
            for cp in page_copies(0, sl, pg):
                cp.wait()
            return carry
        lax.fori_loop(0, n_pages, body, 0)

    @pl.when(b == 0)
    def _():
        start_fetch(0, 0)

    @pl.when(b + 1 < pl.num_programs(0))
    def _():
        start_fetch(b + 1, 1 - slot)

    j = b % SUB
    rid = lax.broadcasted_iota(jnp.int32, (SUB, 1), 0)

    def pick(ref):
        return jnp.sum(jnp.where(rid == j, ref[...], 0.0), axis=0, keepdims=True)

    ql = pick(qlat_ref)
    qs = jnp.concatenate([ql[:, h * KV_LORA:(h + 1) * KV_LORA] for h in range(H)], axis=0).astype(BF16)
    qr = jnp.broadcast_to(pick(qrope_ref), (H, H * QK_ROPE))
    hl = lax.broadcasted_iota(jnp.int32, qr.shape, 1) // QK_ROPE
    hr = lax.broadcasted_iota(jnp.int32, qr.shape, 0)
    qrs = jnp.where(hl == hr, qr, 0.0).astype(BF16)
    qr32 = _dg(qrs, tile_ref[...], NT).astype(BF16)
    cn = _rb(pick(cnew_ref))
    kn = _rb(pick(krnew_ref))
    s_self = (jnp.sum(qs.astype(F32) * cn, axis=1, keepdims=True)
              + jnp.sum(qrs.astype(F32) * kn, axis=1, keepdims=True)) * MLA_SCALE

    wait_fetch(slot)
    cw = ch * PAGE_SIZE
    for ci in range(n_pages // ch):
        kc = kv_buf[slot, ci * ch:(ci + 1) * ch].reshape(cw, KV_LORA).astype(BF16)
        kb_scr[ci * cw:(ci + 1) * cw, :] = kc
        krc = jnp.concatenate([kr_buf[slot, pg] for pg in range(ci * ch, (ci + 1) * ch)], axis=1).astype(BF16)
        s_scr[:, ci * cw:(ci + 1) * cw] = (_dg(qs, kc, NT) + _dg(qr32, krc, NN)) * MLA_SCALE
    s = s_scr[...]
    m = jnp.maximum(jnp.max(s, axis=1, keepdims=True), s_self)
    e = jnp.exp(s - m)
    e_self = jnp.exp(s_self - m)
    rl = 1.0 / (jnp.sum(e, axis=1, keepdims=True) + e_self)
    p = (e * rl).astype(BF16)
    acc = _rb(e_self * rl) * cn
    for ci in range(n_pages // ch):
        acc = acc + _dg(p[:, ci * cw:(ci + 1) * cw], kb_scr[ci * cw:(ci + 1) * cw, :], NN)
    ol = jnp.concatenate([acc[h:h + 1] for h in range(H)], axis=1)
    y = _bdot(jnp.broadcast_to(ol, (SUB, ol.shape[1])), wuv_ref[...])
    o_ref[pl.ds(j, 1), :] = y[0:1]


def _mla_decode(qlat, qrope, c, kr8, cache_kv, cache_kr, page_table, w_uv, li, y_prev, geo):
    n = qlat.shape[0]
    n_pages = page_table.shape[1]
    ch = _div_tile(n_pages, 8, 1)
    rw = MLA_HEADS * QK_ROPE
    wuv = _wuv_blockdiag(w_uv)
    tile = jnp.tile(jnp.eye(QK_ROPE, dtype=F32), (1, MLA_HEADS)).astype(BF16)

    def rowmap(b, pt):
        return ((b // geo.sps) * geo.slab + geo.tp + b % geo.sps) // SUB, 0

    rowblk = lambda w: pl.BlockSpec((SUB, w), rowmap)
    grid_spec = pltpu.PrefetchScalarGridSpec(
        num_scalar_prefetch=1,
        grid=(geo.ns,),
        in_specs=[rowblk(MLA_HEADS * KV_LORA), rowblk(rw), rowblk(KV_LORA), rowblk(rw),
                  pl.BlockSpec(memory_space=pl.ANY), pl.BlockSpec(memory_space=pl.ANY),
                  pl.BlockSpec(tile.shape, lambda b, pt: (0, 0)),
                  pl.BlockSpec(wuv.shape, lambda b, pt: (0, 0)),
                  pl.BlockSpec(memory_space=pl.ANY)],
        out_specs=rowblk(MLA_WIDTH),
        scratch_shapes=[pltpu.VMEM((2, n_pages, PAGE_SIZE, KV_LORA), F32),
                        pltpu.VMEM((2, n_pages, QK_ROPE, PAGE_SIZE), F32),
                        pltpu.VMEM((n_pages * PAGE_SIZE, KV_LORA), BF16),
                        pltpu.VMEM((MLA_HEADS, n_pages * PAGE_SIZE), F32),
                        pltpu.SemaphoreType.DMA((2, 2))],
    )
    return pl.pallas_call(
        functools.partial(_mla_decode_kernel, li=li, n_pages=n_pages, ch=ch),
        grid_spec=grid_spec,
        out_shape=jax.ShapeDtypeStruct((n, MLA_WIDTH), F32),
        input_output_aliases={9: 0},
        compiler_params=_cp("arbitrary"),
        name="mla_decode",
    )(page_table, qlat, qrope, c, kr8, cache_kv, jnp.swapaxes(cache_kr, 2, 3), tile, wuv, y_prev)


def _forward(x_prompt, x_sample, state_ret, state_s5_re, state_s5_im, state_wkv, state_shift, cache_kv,
             cache_krope, page_table, meta_tokens, prm):
    nb, seq, _ = x_prompt.shape
    assert nb == NSLAB and x_sample.shape[1] == 1
    ns = x_sample.shape[0]
    geo = _Geom(seq, ns, page_table.shape[1])
    n, tm, tp, padf, sps = geo.n, geo.tm, geo.tp, geo.padf, geo.sps

    meta = jnp.broadcast_to(meta_tokens[None], (nb, N_META, D_MODEL))
    x = jnp.concatenate([jnp.zeros((nb, padf, D_MODEL), F32), meta, x_prompt, x_sample.reshape(nb, sps, D_MODEL)], 1)
    x = x.reshape(n, D_MODEL)
    pos_slab = jnp.concatenate([jnp.arange(tp, dtype=jnp.int32) - padf, jnp.full((sps,), geo.past, jnp.int32)])

    ret_p, ret_s, s5_p, s5_s, wkv_p, wkv_s, sh_p, sh_s, kv_l, kr_l = [], [], [], [], [], [], [], [], [], []
    for layer in range(DEPTH):
        li = layer // 2
        if layer % 2 == 0:
            w_in = prm['ev_w_in'][li].astype(BF16)
            p = _matmul(x, w_in, tm, _div_tile(w_in.shape[1], 1024, 128))
            p3 = geo.v3(p)
            gng, gnb = prm['ret_gn_g'][li], prm['ret_gn_b'][li]
            ret, s_p = _retention_prompt(p3, geo, gng, gnb)
            ret, s_s = _retention_sample(p3, geo, state_ret[li], gng, gnb, ret)
            s5o, h_p = _s5_prompt(p3, geo, prm, li)
            h0 = jnp.concatenate([state_s5_re[li].reshape(ns, S5_CH), state_s5_im[li].reshape(ns, S5_CH)], 1)
            s5o, h_s = _s5_sample(p3, geo, h0, prm, li, s5o)
            ret_p.append(s_p)
            ret_s.append(s_s)
            s5_p.append(h_p)
            s5_s.append(h_s)
            x = _proj_ln(ret.reshape(n, -1), s5o.reshape(n, -1), x, prm['ev_w_out'][li], prm['ln1_g'][layer],
                         prm['ln1_b'][layer], tm)
        else:
            w = prm['od_w_in'][li]
            o1 = RWKV_PROJ
            o2 = o1 + Q_LORA
            o3 = o2 + KV_LORA
            w_in = jnp.concatenate([w[:, :o1], w[:, o2:o3], w[:, o1:o2], jnp.zeros((D_MODEL, 512 - Q_LORA), F32),
                                    jnp.tile(w[:, o3:], (1, MLA_HEADS))], 1).astype(BF16)
            p = _matmul(x, w_in, tm, _div_tile(w_in.shape[1], 1408, 128))
            p3 = geo.v3(p)
            sh_p.append(p3[:, tp - 1, :RWKV_PROJ])
            sh_s.append(p3[:, tp:, :RWKV_PROJ].reshape(ns, RWKV_PROJ))
            r, lw, k, v, a, b, g = _rwkv_prep(p3, geo, state_shift[li], prm, li)
            seqs = (r, lw, k, v, a, b)
            y, g_p = _wkv_prompt(seqs, geo)
            y, g_s = _wkv_sample(seqs, geo, state_wkv[li], y)
            wkv_p.append(g_p)
            wkv_s.append(g_s)
            flat = lambda t: t.reshape(n, t.shape[-1])
            y_c = _rwkv_post(flat(y), flat(r), flat(k), flat(v), flat(g), prm, li, tm)
            qlat, qrope, c, kr8 = _mla_prep(p, geo, pos_slab, prm, li)
            kv_l.append(geo.v3(c))
            kr_l.append(geo.v3(kr8)[:, :, :QK_ROPE])
            y_d = _mla_attn_prompt(geo.v3(qlat), geo.v3(qrope), geo.v3(c), geo.v3(kr8), prm['mla_w_uv'][li], geo)
            y_d = _mla_decode(qlat, qrope, c, kr8, cache_kv, cache_krope, page_table, prm['mla_w_uv'][li], li,
                              flat(y_d), geo)
            x = _proj_ln(y_c, y_d, x, prm['od_w_out'][li], prm['ln1_g'][layer], prm['ln1_b'][layer], tm)
        x = _moe_ln(x, prm, layer, tm)

    x3 = geo.v3(x)
    y_p = x3[:, padf + N_META:tp]
    y_s = x3[:, tp:].reshape(ns, 1, D_MODEL)

    def s5_state(hs, part):
        return jnp.stack([h[:, part * S5_CH:(part + 1) * S5_CH].reshape(-1, S5_GROUPS, S5_STATE) for h in hs])

    kv_p = jnp.stack([c[:, padf:tp] for c in kv_l], 1)
    kv_s = jnp.stack([c[:, tp:].reshape(ns, 1, KV_LORA) for c in kv_l], 1)
    kr_p = jnp.stack([c[:, padf:tp] for c in kr_l], 1)
    kr_s = jnp.stack([c[:, tp:].reshape(ns, 1, QK_ROPE) for c in kr_l], 1)
    return (y_p, y_s, jnp.stack(ret_p), jnp.stack(ret_s), s5_state(s5_p, 0), s5_state(s5_s, 0),
            s5_state(s5_p, 1), s5_state(s5_s, 1), jnp.stack(wkv_p), jnp.stack(wkv_s),
            jnp.stack(sh_p), jnp.stack(sh_s), kv_p, kv_s, kr_p, kr_s)


def kernel(x_prompt, x_sample, state_ret, state_s5_re, state_s5_im, state_wkv, state_shift, cache_kv, cache_krope, page_table, meta_tokens, ev_w_in, ret_gn_g, ret_gn_b, s5_a_re, s5_a_im, s5_log_dt, s5_b_re, s5_b_im, s5_c_re, s5_c_im, s5_d, s5_w_glu, s5_b_glu, ev_w_out, od_w_in, rwkv_mu, rwkv_w0, rwkv_w2, rwkv_a0, rwkv_a2, rwkv_g2, rwkv_k_k, rwkv_k_a, rwkv_r_k, rwkv_gn_g, rwkv_gn_b, mla_q_norm, mla_w_qb, mla_kv_norm, mla_w_uk, mla_w_uv, od_w_out, ln1_g, ln1_b, ln2_g, ln2_b, moe_w_coarse, moe_b_coarse, moe_w_fine, moe_b_fine, moe_w_gate, moe_w_up, moe_w_down):
    prm = dict(ev_w_in=ev_w_in, ret_gn_g=ret_gn_g, ret_gn_b=ret_gn_b, s5_a_re=s5_a_re, s5_a_im=s5_a_im,
               s5_log_dt=s5_log_dt, s5_b_re=s5_b_re, s5_b_im=s5_b_im, s5_c_re=s5_c_re, s5_c_im=s5_c_im,
               s5_d=s5_d, s5_w_glu=s5_w_glu, s5_b_glu=s5_b_glu, ev_w_out=ev_w_out,
               od_w_in=od_w_in, rwkv_mu=rwkv_mu, rwkv_w0=rwkv_w0, rwkv_w2=rwkv_w2, rwkv_a0=rwkv_a0,
               rwkv_a2=rwkv_a2, rwkv_g2=rwkv_g2, rwkv_k_k=rwkv_k_k, rwkv_k_a=rwkv_k_a, rwkv_r_k=rwkv_r_k,
               rwkv_gn_g=rwkv_gn_g, rwkv_gn_b=rwkv_gn_b, mla_q_norm=mla_q_norm, mla_w_qb=mla_w_qb,
               mla_kv_norm=mla_kv_norm, mla_w_uk=mla_w_uk, mla_w_uv=mla_w_uv, od_w_out=od_w_out,
               ln1_g=ln1_g, ln1_b=ln1_b, ln2_g=ln2_g, ln2_b=ln2_b, moe_w_coarse=moe_w_coarse,
               moe_b_coarse=moe_b_coarse, moe_w_fine=moe_w_fine, moe_b_fine=moe_b_fine,
               moe_w_gate=moe_w_gate, moe_w_up=moe_w_up, moe_w_down=moe_w_down)
    return _forward(x_prompt, x_sample, state_ret, state_s5_re, state_s5_im, state_wkv, state_shift, cache_kv,
                    cache_krope, page_table, meta_tokens, prm)
```

```python
import functools
import math

import jax
import jax.numpy as jnp
from jax import lax
from jax.experimental import pallas as pl
from jax.experimental.pallas import tpu as pltpu

F32 = jnp.float32
BF16 = jnp.bfloat16

D_MODEL = 1024
N_META = 16
PAGE_SIZE = 128
RET_HEADS, RET_DK, RET_DV = 4, 64, 128
S5_WIDTH, S5_GROUP, S5_STATE = 512, 16, 64
S5_GROUPS = S5_WIDTH // S5_GROUP
S5_CH = S5_GROUPS * S5_STATE
RWKV_HEAD, RWKV_WIDTH = 64, 512
RWKV_HEADS = RWKV_WIDTH // RWKV_HEAD
LORA_W, LORA_A, LORA_G = 64, 64, 128
RWKV_PROJ = 3 * RWKV_WIDTH + LORA_W + LORA_A + LORA_G
RWKV_GN_EPS = 64e-5
MLA_HEADS, Q_LORA, KV_LORA, QK_NOPE, QK_ROPE, V_HEAD = 8, 384, 256, 64, 32, 64
MLA_WIDTH = MLA_HEADS * V_HEAD
MLA_SCALE = (QK_NOPE + QK_ROPE) ** -0.5
ROPE_THETA = 10000.0
N_EGROUPS, E_PER_GROUP, D_EXPERT = 4, 4, 256
N_EXPERTS = N_EGROUPS * E_PER_GROUP
DEPTH = 4
DN_ALPHA = (2 * DEPTH) ** 0.25
NSLAB = 8
SUB = 8
LANES = 128

VMEM_LIMIT = 56 * 1024 * 1024
WKV_L = 32
WKV_HG = 4
WKV_NB = 4
ROW_TILE = 1056
SEQ_TILE = 416
S5_TILE = 80
NEG = -1e30

NT = (((1,), (1,)), ((), ()))
TN = (((0,), (0,)), ((), ()))
NN = (((1,), (0,)), ((), ()))


def _cp(*sem):
    return pltpu.CompilerParams(dimension_semantics=sem, vmem_limit_bytes=VMEM_LIMIT)


def _div_tile(n, cap, mult=8):
    best = None
    for d in range(mult, min(n, cap) + 1, mult):
        if n % d == 0:
            best = d
    assert best is not None, (n, cap, mult)
    return best


def _dg(a, b, dims):
    return lax.dot_general(a, b, dims, preferred_element_type=F32)


def _bdot(a, b, dims=NN):
    return _dg(a.astype(BF16), b.astype(BF16), dims)


def _dot3(a, b, dims=NN):
    ah = a.astype(BF16)
    al = (a - ah.astype(F32)).astype(BF16)
    bh = b.astype(BF16)
    bl = (b - bh.astype(F32)).astype(BF16)
    return _dg(ah, bh, dims) + (_dg(ah, bl, dims) + _dg(al, bh, dims))


def _dotc(a, b, dims=NN):
    return _bdot(a, b, dims)


def _rb(x):
    return x.astype(BF16).astype(F32)


def _layer_norm(z, g, b, eps=1e-5):
    mu = jnp.mean(z, -1, keepdims=True)
    zc = z - mu
    var = jnp.mean(zc * zc, -1, keepdims=True)
    return zc * lax.rsqrt(var + eps) * g + b


def _rope(x, cos, sin_signed, half):
    w = x.shape[-1]
    lane = lax.broadcasted_iota(jnp.int32, x.shape, 1)
    first = (lane % (2 * half)) < half
    other = jnp.where(first, pltpu.roll(x, w - half, 1), pltpu.roll(x, half, 1))
    return x * cos + other * sin_signed


def _rope_tables(pos, dim, reps):
    inv = ROPE_THETA ** (-jnp.arange(0, dim, 2, dtype=F32) / dim)
    ang = pos.astype(F32)[:, None] * inv[None, :]
    cos = jnp.cos(ang)
    sin = jnp.sin(ang)
    cos = jnp.tile(jnp.concatenate([cos, cos], -1), (1, reps))
    sin = jnp.tile(jnp.concatenate([-sin, sin], -1), (1, reps))
    return cos, sin


def _drop_arg(kern, idx, *refs):
    return kern(*(refs[:idx] + refs[idx + 1:]))


class _Geom:
    def __init__(self, seq, ns, n_pages):
        self.tq = seq + N_META
        self.padf = (-self.tq) % WKV_L
        self.tp = self.padf + self.tq
        assert ns % (NSLAB * SUB) == 0
        self.ns = ns
        self.sps = ns // NSLAB
        self.slab = self.tp + self.sps
        self.n = NSLAB * self.slab
        self.past = n_pages * PAGE_SIZE
        self.tm = _div_tile(self.slab, ROW_TILE)
        self.tps = self.slab // self.tm

    def v3(self, x):
        return x.reshape(NSLAB, self.slab, x.shape[-1])

    def sample_blk(self, rows):
        per = self.sps // rows
        return lambda k: (k // per, self.tp // rows + k % per)


def _mm_kernel(x_ref, w_ref, o_ref):
    o_ref[...] = _dg(x_ref[...].astype(BF16), w_ref[...], NN)


def _matmul(x, w, tm, tn):
    n, k = x.shape
    m = w.shape[1]
    return pl.pallas_call(
        _mm_kernel,
        grid=(n // tm, m // tn),
        in_specs=[pl.BlockSpec((tm, k), lambda i, j: (i, 0)), pl.BlockSpec((k, tn), lambda i, j: (0, j))],
        out_specs=pl.BlockSpec((tm, tn), lambda i, j: (i, j)),
        out_shape=jax.ShapeDtypeStruct((n, m), F32),
        compiler_params=_cp("parallel", "arbitrary"),
        name="proj_in",
    )(x, w)


def _proj_ln_kernel(a1_ref, a2_ref, x_ref, w1_ref, w2_ref, g_ref, b_ref, o_ref):
    mix = _dg(a1_ref[...].astype(BF16), w1_ref[...], NN) + _dg(a2_ref[...].astype(BF16), w2_ref[...], NN)
    z = DN_ALPHA * x_ref[...] + mix
    o_ref[...] = _layer_norm(z, g_ref[...], b_ref[...])


def _proj_ln(a1, a2, x, w_out, g, b, tm):
    n = x.shape[0]
    k1, k2 = a1.shape[1], a2.shape[1]
    w1 = w_out[:k1].astype(BF16)
    w2 = w_out[k1:].astype(BF16)
    row = lambda i: (i, 0)
    fix = lambda i: (0, 0)
    return pl.pallas_call(
        _proj_ln_kernel,
        grid=(n // tm,),
        in_specs=[pl.BlockSpec((tm, k1), row), pl.BlockSpec((tm, k2), row), pl.BlockSpec((tm, D_MODEL), row),
                  pl.BlockSpec((k1, D_MODEL), fix), pl.BlockSpec((k2, D_MODEL), fix),
                  pl.BlockSpec((1, D_MODEL), fix), pl.BlockSpec((1, D_MODEL), fix)],
        out_specs=pl.BlockSpec((tm, D_MODEL), row),
        out_shape=jax.ShapeDtypeStruct((n, D_MODEL), F32),
        compiler_params=_cp("parallel"),
        name="proj_out_ln",
    )(a1, a2, x, w1, w2, g.reshape(1, -1), b.reshape(1, -1))


MOE_EPS = 2
ROUTE0 = N_EGROUPS


def _route(lt):
    lane = lax.broadcasted_iota(jnp.int32, lt.shape, 1)
    lanef = lane.astype(F32)
    coarse = lane < N_EGROUPS
    lc = jnp.where(coarse, lt, NEG)
    m = jnp.max(lc, axis=1, keepdims=True)
    gsel = jnp.min(jnp.where(lc == m, lanef, 1e9), axis=1, keepdims=True)
    gate_c = 1.0 / jnp.sum(jnp.where(coarse, jnp.exp(lc - m), 0.0), axis=1, keepdims=True)
    grp = ((lane - ROUTE0) // E_PER_GROUP).astype(F32)
    lf = jnp.where(grp == gsel, lt, NEG)
    t1 = jnp.max(lf, axis=1, keepdims=True)
    i1 = jnp.min(jnp.where(lf == t1, lanef, 1e9), axis=1, keepdims=True)
    lf2 = jnp.where(lanef == i1, NEG, lf)
    t2 = jnp.max(lf2, axis=1, keepdims=True)
    i2 = jnp.min(jnp.where(lf2 == t2, lanef, 1e9), axis=1, keepdims=True)
    e2 = jnp.exp(t2 - t1)
    w1 = gate_c / (1.0 + e2)
    return jnp.where(lanef == i1, w1, 0.0) + jnp.where(lanef == i2, w1 * e2, 0.0)


def _moe_kernel(x_ref, wr_ref, br_ref, wg_ref, wu_ref, wd_ref, g_ref, b_ref, o_ref, xb_scr, comb_scr, h_scr):
    e = pl.program_id(1)

    @pl.when(e == 0)
    def _():
        xb = x_ref[...].astype(BF16)
        xb_scr[...] = xb
        comb_scr[...] = _route(_dg(xb, wr_ref[...], NN) + br_ref[...])

    xb = xb_scr[...]
    comb = comb_scr[...]
    lane = lax.broadcasted_iota(jnp.int32, comb.shape, 1)
    hb = []
    for q in range(MOE_EPS):
        hg = _dg(xb, wg_ref[q], NN)
        hu = _dg(xb, wu_ref[q], NN)
        ce = jnp.sum(jnp.where(lane == e * MOE_EPS + q + ROUTE0, comb, 0.0), axis=1, keepdims=True)
        hb.append(((hg * jax.nn.sigmoid(hg)) * hu * ce).astype(BF16))
    for ee in range(N_EXPERTS // MOE_EPS):
        @pl.when(e == ee)
        def _():
            for q in range(MOE_EPS):
                x0 = (ee * MOE_EPS + q) * D_EXPERT
                h_scr[:, x0:x0 + D_EXPERT] = hb[q]

    @pl.when(e == pl.num_programs(1) - 1)
    def _():
        z = DN_ALPHA * x_ref[...] + _dg(h_scr[...], wd_ref[...], NN)
        o_ref[...] = _layer_norm(z, g_ref[...], b_ref[...])


def _moe_ln(x, prm, layer, tm):
    n = x.shape[0]
    wr = jnp.concatenate([prm['moe_w_coarse'][layer],
                          jnp.transpose(prm['moe_w_fine'][layer], (1, 0, 2)).reshape(D_MODEL, N_EXPERTS)], 1)
    wr = jnp.pad(wr, ((0, 0), (0, LANES - wr.shape[1]))).astype(BF16)
    br = jnp.pad(jnp.concatenate([prm['moe_b_coarse'][layer], prm['moe_b_fine'][layer].reshape(-1)]),
                 (0, LANES - N_EGROUPS - N_EXPERTS)).reshape(1, LANES)
    row = lambda i, e: (i, 0)
    fix = lambda i, e: (0, 0)
    exp3 = lambda i, e: (e, 0, 0)
    return pl.pallas_call(
        _moe_kernel,
        grid=(n // tm, N_EXPERTS // MOE_EPS),
        in_specs=[pl.BlockSpec((tm, D_MODEL), row), pl.BlockSpec((D_MODEL, LANES), fix), pl.BlockSpec((1, LANES), fix),
                  pl.BlockSpec((MOE_EPS, D_MODEL, D_EXPERT), exp3), pl.BlockSpec((MOE_EPS, D_MODEL, D_EXPERT), exp3),
                  pl.BlockSpec((N_EXPERTS * D_EXPERT, D_MODEL), fix, pipeline_mode=pl.Buffered(1)),
                  pl.BlockSpec((1, D_MODEL), fix), pl.BlockSpec((1, D_MODEL), fix)],
        out_specs=pl.BlockSpec((tm, D_MODEL), row),
        out_shape=jax.ShapeDtypeStruct((n, D_MODEL), F32),
        scratch_shapes=[pltpu.VMEM((tm, D_MODEL), BF16), pltpu.VMEM((tm, LANES), F32),
                        pltpu.VMEM((tm, N_EXPERTS * D_EXPERT), BF16)],
        compiler_params=_cp("parallel", "arbitrary"),
        name="moe_ln",
    )(x, wr, br, prm['moe_w_gate'][layer].astype(BF16), prm['moe_w_up'][layer].astype(BF16),
      prm['moe_w_down'][layer].astype(BF16).reshape(N_EXPERTS * D_EXPERT, D_MODEL),
      prm['ln2_g'][layer].reshape(1, -1), prm['ln2_b'][layer].reshape(1, -1))


def _ret_log_g():
    return [math.log1p(-2.0 ** (-5.0 - h)) for h in range(RET_HEADS)]


def _ret_kernel(q_ref, k_ref, v_ref, gate_ref, cos_ref, sin_ref, dmask_ref, cdec_ref, kdec_ref, sdec_ref,
                gng_ref, gnb_ref, o_ref, sout_ref, s_scr, *, L, padf):
    c = pl.program_id(1)
    HK, HV = RET_HEADS * RET_DK, RET_HEADS * RET_DV

    @pl.when(c == 0)
    def _():
        s_scr[...] = jnp.zeros_like(s_scr)

    cos, sin = cos_ref[...], sin_ref[...]
    q = _rope(q_ref[...], cos, sin, RET_DK // 2) * RET_DK ** -0.5
    k = _rope(k_ref[...], cos, sin, RET_DK // 2)
    real = (lax.broadcasted_iota(jnp.int32, (L, 1), 0) + c * L) >= padf
    k = jnp.where(real, k, 0.0)
    v = jnp.where(real, v_ref[...], 0.0)
    gate = gate_ref[...]
    qb, kb, vb = q.astype(BF16), k.astype(BF16), v.astype(BF16)
    sbd = s_scr[...]
    cross = _dg(qb, sbd.astype(BF16), NN) * cdec_ref[...]
    lane = lax.broadcasted_iota(jnp.int32, (L, HK), 1)
    outs = []
    for h in range(RET_HEADS):
        qh = jnp.where(lane // RET_DK == h, qb, jnp.zeros_like(qb))
        sc = _dg(qh, kb, NT) * dmask_ref[h]
        sl = slice(h * RET_DV, (h + 1) * RET_DV)
        o = _dg(sc.astype(BF16), vb[:, sl], NN) + cross[:, sl]
        mu = jnp.mean(o, -1, keepdims=True)
        oc = o - mu
        var = jnp.mean(oc * oc, -1, keepdims=True)
        outs.append(oc * lax.rsqrt(var + 1e-5))
    on = jnp.concatenate(outs, axis=1) * gng_ref[...] + gnb_ref[...]
    o_ref[...] = (gate * jax.nn.sigmoid(gate)) * on
    upd = _dg((k * kdec_ref[...]).astype(BF16), vb, TN)
    ri = lax.broadcasted_iota(jnp.int32, (HK, HV), 0) // RET_DK
    ci = lax.broadcasted_iota(jnp.int32, (HK, HV), 1) // RET_DV
    s_new = sdec_ref[...] * sbd + jnp.where(ri == ci, upd, 0.0)
    s_scr[...] = s_new

    @pl.when(c == pl.num_programs(1) - 1)
    def _():
        for h in range(RET_HEADS):
            sout_ref[h] = s_new[h * RET_DK:(h + 1) * RET_DK, h * RET_DV:(h + 1) * RET_DV]


def _retention_prompt(p3, geo, gn_g, gn_b):
    HK, HV = RET_HEADS * RET_DK, RET_HEADS * RET_DV
    L = _div_tile(geo.tp, SEQ_TILE)
    log_g = jnp.asarray(_ret_log_g(), F32)
    idx = jnp.arange(L, dtype=F32)
    diff = idx[:, None] - idx[None, :]
    dmask = jnp.where(diff >= 0, jnp.exp(log_g[:, None, None] * jnp.maximum(diff, 0.0)), 0.0)
    cdec = jnp.repeat(jnp.exp(log_g[None, :] * (idx + 1.0)[:, None]), RET_DV, axis=1)
    kdec = jnp.repeat(jnp.exp(log_g[None, :] * (L - 1.0 - idx)[:, None]), RET_DK, axis=1)
    sdec = jnp.broadcast_to(jnp.repeat(jnp.exp(log_g * L), RET_DK)[:, None], (HK, HV))
    cos, sin = _rope_tables(jnp.arange(geo.tp, dtype=jnp.int32) - geo.padf, RET_DK, RET_HEADS)
    fix2 = lambda b, c: (0, 0)
    tab = pl.BlockSpec((L, HK), lambda b, c: (c, 0))
    return pl.pallas_call(
        functools.partial(_ret_kernel, L=L, padf=geo.padf),
        grid=(NSLAB, geo.tp // L),
        in_specs=[pl.BlockSpec((None, L, HK), lambda b, c: (b, c, 0)), pl.BlockSpec((None, L, HK), lambda b, c: (b, c, 1)),
                  pl.BlockSpec((None, L, HV), lambda b, c: (b, c, 1)), pl.BlockSpec((None, L, HV), lambda b, c: (b, c, 2)),
                  tab, tab, pl.BlockSpec((RET_HEADS, L, L), lambda b, c: (0, 0, 0)),
                  pl.BlockSpec((L, HV), fix2), pl.BlockSpec((L, HK), fix2), pl.BlockSpec((HK, HV), fix2),
                  pl.BlockSpec((1, HV), fix2), pl.BlockSpec((1, HV), fix2)],
        out_specs=[pl.BlockSpec((None, L, HV), lambda b, c: (b, c, 0)),
                   pl.BlockSpec((None, RET_HEADS, RET_DK, RET_DV), lambda b, c: (b, 0, 0, 0))],
        out_shape=[jax.ShapeDtypeStruct((NSLAB, geo.slab, HV), F32),
                   jax.ShapeDtypeStruct((NSLAB, RET_HEADS, RET_DK, RET_DV), F32)],
        scratch_shapes=[pltpu.VMEM((HK, HV), F32)],
        compiler_params=_cp("parallel", "arbitrary"),
        name="retention",
    )(p3, p3, p3, p3, cos, sin, dmask, cdec, kdec, sdec, gn_g.reshape(1, -1), gn_b.reshape(1, -1))


def _ret_step_kernel(q_ref, k_ref, v_ref, gate_ref, cos_ref, sin_ref, gng_ref, gnb_ref, s_ref, o_ref, sout_ref):
    cos, sin = cos_ref[...], sin_ref[...]
    qb = _rb(_rope(q_ref[...], cos, sin, RET_DK // 2) * RET_DK ** -0.5)
    kb = _rb(_rope(k_ref[...], cos, sin, RET_DK // 2))
    vb = _rb(v_ref[...])
    gate = gate_ref[...]
    ri = lax.broadcasted_iota(jnp.int32, (RET_DK, RET_DK), 0)
    ci = lax.broadcasted_iota(jnp.int32, (RET_DK, RET_DK), 1)
    eye = jnp.where(ri == ci, 1.0, 0.0)

    def col(x):
        return jnp.sum(eye * jnp.broadcast_to(x, (RET_DK, RET_DK)), axis=1, keepdims=True)

    for j in range(SUB):
        outs = []
        for h, lg in enumerate(_ret_log_g()):
            g = math.exp(lg)
            qh = qb[j:j + 1, h * RET_DK:(h + 1) * RET_DK]
            kh = kb[j:j + 1, h * RET_DK:(h + 1) * RET_DK]
            vh = vb[j:j + 1, h * RET_DV:(h + 1) * RET_DV]
            s0 = s_ref[j, h]
            qk = _rb(jnp.sum(qh * kh, axis=1, keepdims=True))
            o = qk * vh + g * jnp.sum(col(qh) * _rb(s0), axis=0, keepdims=True)
            sout_ref[j, h] = g * s0 + col(kh) * vh
            mu = jnp.mean(o, -1, keepdims=True)
            oc = o - mu
            var = jnp.mean(oc * oc, -1, keepdims=True)
            outs.append(oc * lax.rsqrt(var + 1e-5))
        on = jnp.concatenate(outs, axis=1) * gng_ref[...] + gnb_ref[...]
        gj = gate[j:j + 1]
        o_ref[j:j + 1, :] = (gj * jax.nn.sigmoid(gj)) * on


def _retention_sample(p3, geo, s0, gn_g, gn_b, out_prev):
    HK, HV = RET_HEADS * RET_DK, RET_HEADS * RET_DV
    cos, sin = _rope_tables(jnp.full((1,), geo.past, jnp.int32), RET_DK, RET_HEADS)
    sb = geo.sample_blk(SUB)
    fix = lambda k: (0, 0)
    st = pl.BlockSpec((SUB, RET_HEADS, RET_DK, RET_DV), lambda k: (k, 0, 0, 0))
    return pl.pallas_call(
        functools.partial(_drop_arg, _ret_step_kernel, 9),
        grid=(geo.ns // SUB,),
        in_specs=[pl.BlockSpec((None, SUB, HK), lambda k: sb(k) + (0,)), pl.BlockSpec((None, SUB, HK), lambda k: sb(k) + (1,)),
                  pl.BlockSpec((None, SUB, HV), lambda k: sb(k) + (1,)), pl.BlockSpec((None, SUB, HV), lambda k: sb(k) + (2,)),
                  pl.BlockSpec((1, HK), fix), pl.BlockSpec((1, HK), fix),
                  pl.BlockSpec((1, HV), fix), pl.BlockSpec((1, HV), fix), st, pl.BlockSpec(memory_space=pl.ANY)],
        out_specs=[pl.BlockSpec((None, SUB, HV), lambda k: sb(k) + (0,)), st],
        out_shape=[jax.ShapeDtypeStruct(out_prev.shape, F32), jax.ShapeDtypeStruct(s0.shape, F32)],
        input_output_aliases={9: 0},
        compiler_params=_cp("parallel"),
        name="retention_step",
    )(p3, p3, p3, p3, cos, sin, gn_g.reshape(1, -1), gn_b.reshape(1, -1), s0, out_prev)


def _s5_params(prm, li):
    f32 = F32
    a_re, a_im = prm['s5_a_re'][li].astype(f32), prm['s5_a_im'][li].astype(f32)
    dt = jnp.exp(prm['s5_log_dt'][li].astype(f32))[:, None]
    mag = jnp.exp(dt * a_re)
    ab_re, ab_im = mag * jnp.cos(dt * a_im), mag * jnp.sin(dt * a_im)
    den = a_re * a_re + a_im * a_im
    n_re = ab_re - 1.0
    f_re = (n_re * a_re + ab_im * a_im) / den
    f_im = (ab_im * a_re - n_re * a_im) / den
    b_re, b_im = prm['s5_b_re'][li].astype(f32), prm['s5_b_im'][li].astype(f32)
    bb_re = f_re[..., None] * b_re - f_im[..., None] * b_im
    bb_im = f_re[..., None] * b_im + f_im[..., None] * b_re
    eye = jnp.eye(S5_GROUPS, dtype=f32)

    def blockdiag_in(w):
        return jnp.einsum('gpc,gh->gchp', w, eye).reshape(S5_WIDTH, S5_CH)

    def blockdiag_out(w):
        return jnp.einsum('gcp,gh->gphc', w, eye).reshape(S5_CH, S5_WIDTH)

    bbw = jnp.concatenate([blockdiag_in(bb_re), blockdiag_in(bb_im)], 1).astype(BF16)
    ccw = jnp.concatenate([blockdiag_out(prm['s5_c_re'][li]), -blockdiag_out(prm['s5_c_im'][li])], 0).astype(BF16)
    lam = jnp.concatenate([ab_re.reshape(1, -1), ab_im.reshape(1, -1)], 1)
    return [bbw, ccw, lam, prm['s5_d'][li].reshape(1, -1), prm['s5_w_glu'][li].astype(BF16),
            prm['s5_b_glu'][li].reshape(1, -1)]


def _s5_specs(fix):
    return [pl.BlockSpec((S5_WIDTH, 2 * S5_CH), fix), pl.BlockSpec((2 * S5_CH, S5_WIDTH), fix),
            pl.BlockSpec((1, 2 * S5_CH), fix), pl.BlockSpec((1, S5_WIDTH), fix),
            pl.BlockSpec((S5_WIDTH, S5_WIDTH), fix), pl.BlockSpec((1, S5_WIDTH), fix)]


def _s5_out(h, u, cc_ref, d_ref, wglu_ref, bglu_ref):
    y = _dg(h.astype(BF16), cc_ref[...], NN) + d_ref[...] * u
    z = 0.5 * y * (1.0 + jnp.tanh(math.sqrt(2.0 / math.pi) * (y + 0.044715 * (y * y * y))))
    return z * jax.nn.sigmoid(_dg(z.astype(BF16), wglu_ref[...], NN) + bglu_ref[...])


S5_NJ = 2 * S5_CH // LANES
S5_PAIRS = 4


def _s5_prompt_kernel(u_ref, bb_ref, cc_ref, lam_ref, d_ref, wglu_ref, bglu_ref, o_ref, hout_ref, sc, h_scr,
                      *, lt, pitch, padf):
    i = pl.program_id(0)

    @pl.when(i == 0)
    def _():
        h_scr[...] = jnp.zeros_like(h_scr)

    real = (lax.broadcasted_iota(jnp.int32, (lt, 1), 0) + i * lt) >= padf
    for b in range(NSLAB):
        bu = _dg(jnp.where(real, u_ref[b], 0.0).astype(BF16), bb_ref[...], NN)
        for j in range(S5_NJ):
            sc[j, b * pitch:b * pitch + lt, :] = bu[:, j * LANES:(j + 1) * LANES]
    half = S5_NJ // 2
    for j0 in range(0, half, S5_PAIRS):
        js = list(range(j0, j0 + S5_PAIRS))
        lr = [jnp.broadcast_to(lam_ref[:, j * LANES:(j + 1) * LANES], (NSLAB, LANES)) for j in js]
        li = [jnp.broadcast_to(lam_ref[:, S5_CH + j * LANES:S5_CH + (j + 1) * LANES], (NSLAB, LANES)) for j in js]

        def step(t, carry):
            hr, hi = carry
            rows = pl.ds(t, NSLAB, stride=pitch)
            nr, ni = [], []
            for q, j in enumerate(js):
                r = lr[q] * hr[q] - li[q] * hi[q] + sc[j, rows, :]
                m = lr[q] * hi[q] + li[q] * hr[q] + sc[half + j, rows, :]
                sc[j, rows, :] = r
                sc[half + j, rows, :] = m
                nr.append(r)
                ni.append(m)
            return tuple(nr), tuple(ni)

        init = (tuple(h_scr[:, j * LANES:(j + 1) * LANES] for j in js),
                tuple(h_scr[:, S5_CH + j * LANES:S5_CH + (j + 1) * LANES] for j in js))
        hr, hi = lax.fori_loop(0, lt, step, init)
        for q, j in enumerate(js):
            h_scr[:, j * LANES:(j + 1) * LANES] = hr[q]
            h_scr[:, S5_CH + j * LANES:S5_CH + (j + 1) * LANES] = hi[q]
    for b in range(NSLAB):
        h = jnp.concatenate([sc[j, b * pitch:b * pitch + lt, :] for j in range(S5_NJ)], axis=1)
        o_ref[b] = _s5_out(h, u_ref[b], cc_ref, d_ref, wglu_ref, bglu_ref)

    @pl.when(i == pl.num_programs(0) - 1)
    def _():
        hout_ref[...] = h_scr[...]


def _s5_prompt(p3, geo, prm, li):
    lt = _div_tile(geo.tp, S5_TILE, 16)
    pitch = lt + SUB
    ucol = p3.shape[-1] // S5_WIDTH - 1
    fix = lambda i: (0, 0)
    return pl.pallas_call(
        functools.partial(_s5_prompt_kernel, lt=lt, pitch=pitch, padf=geo.padf),
        grid=(geo.tp // lt,),
        in_specs=[pl.BlockSpec((NSLAB, lt, S5_WIDTH), lambda i: (0, i, ucol))] + _s5_specs(fix),
        out_specs=[pl.BlockSpec((NSLAB, lt, S5_WIDTH), lambda i: (0, i, 0)), pl.BlockSpec((NSLAB, 2 * S5_CH), fix)],
        out_shape=[jax.ShapeDtypeStruct((NSLAB, geo.slab, S5_WIDTH), F32), jax.ShapeDtypeStruct((NSLAB, 2 * S5_CH), F32)],
        scratch_shapes=[pltpu.VMEM((S5_NJ, NSLAB * pitch, LANES), F32), pltpu.VMEM((NSLAB, 2 * S5_CH), F32)],
        compiler_params=_cp("arbitrary"),
        name="s5",
    )(p3, *_s5_params(prm, li))


def _s5_step_kernel(u_ref, bb_ref, cc_ref, lam_ref, d_ref, wglu_ref, bglu_ref, h0_ref, o_ref, hout_ref):
    u = u_ref[...]
    bu = _dg(u.astype(BF16), bb_ref[...], NN)
    h0 = h0_ref[...]
    lr, li = lam_ref[:, :S5_CH], lam_ref[:, S5_CH:]
    hr, hi = h0[:, :S5_CH], h0[:, S5_CH:]
    h = jnp.concatenate([lr * hr - li * hi + bu[:, :S5_CH], lr * hi + li * hr + bu[:, S5_CH:]], axis=1)
    hout_ref[...] = h
    o_ref[...] = _s5_out(h, u, cc_ref, d_ref, wglu_ref, bglu_ref)


def _s5_sample(p3, geo, h0, prm, li, out_prev):
    rows = geo.sps
    sb = geo.sample_blk(rows)
    ucol = p3.shape[-1] // S5_WIDTH - 1
    fix = lambda k: (0, 0)
    st = pl.BlockSpec((rows, 2 * S5_CH), lambda k: (k, 0))
    return pl.pallas_call(
        functools.partial(_drop_arg, _s5_step_kernel, 8),
        grid=(NSLAB,),
        in_specs=[pl.BlockSpec((None, rows, S5_WIDTH), lambda k: sb(k) + (ucol,))] + _s5_specs(fix)
        + [st, pl.BlockSpec(memory_space=pl.ANY)],
        out_specs=[pl.BlockSpec((None, rows, S5_WIDTH), lambda k: sb(k) + (0,)), st],
        out_shape=[jax.ShapeDtypeStruct(out_prev.shape, F32), jax.ShapeDtypeStruct(h0.shape, F32)],
        input_output_aliases={8: 0},
        compiler_params=_cp("parallel"),
        name="s5_step",
    )(p3, *_s5_params(prm, li), h0, out_prev)


def _head_sum(x, ones_bd):
    return _dot3(x, ones_bd)


def _rwkv_prep_body(pr, prev, mu_ref, w0_ref, w2_ref, a0_ref, a2_ref, g2_ref, kk_ref, ka_ref, ones_ref,
                    r_ref, lw_ref, k_ref, v_ref, a_ref, b_ref, g_ref):
    W = RWKV_WIDTH
    xm = pr + (prev - pr) * mu_ref[...]
    r = xm[:, :W]
    k = xm[:, W:2 * W]
    v = xm[:, 2 * W:3 * W]
    xwa = xm[:, 3 * W:3 * W + LORA_W + LORA_A]
    xg = xm[:, 3 * W + LORA_W + LORA_A:]
    wl = w0_ref[...] + _bdot(jnp.tanh(xwa), w2_ref[...])
    sp = jnp.maximum(-wl, 0.0) + jnp.log(1.0 + jnp.exp(-jnp.abs(wl)))
    w = -sp - 0.5
    lw = -jnp.exp(w)
    a = jax.nn.sigmoid(a0_ref[...] + _bdot(xwa, a2_ref[...]))
    g = _bdot(jax.nn.sigmoid(xg), g2_ref[...])
    kk = k * kk_ref[...]
    nrm = jnp.sqrt(_head_sum(kk * kk, ones_ref[...]))
    kk = kk / jnp.maximum(nrm, 1e-12)
    k2 = k * (1.0 + (a - 1.0) * ka_ref[...])
    r_ref[...] = r
    lw_ref[...] = lw
    k_ref[...] = k2
    v_ref[...] = v
    a_ref[...] = -kk
    b_ref[...] = kk * a
    g_ref[...] = g


def _rwkv_prep_prompt_kernel(pr_ref, prev8_ref, *rest, tm, padf):
    pr = pr_ref[...]
    t = lax.broadcasted_iota(jnp.int32, (tm, 1), 0)
    last = jnp.broadcast_to(prev8_ref[SUB - 1:SUB, :], pr.shape)
    prev = jnp.where(t == 0, last, pltpu.roll(pr, 1, 0))
    first = (t + pl.program_id(1) * tm) == padf
    prev = jnp.where(first, 0.0, prev)
    _rwkv_prep_body(pr, prev, *rest)


def _rwkv_prep_sample_kernel(pr_ref, prev_ref, *rest):
    _rwkv_prep_body(pr_ref[...], prev_ref[...], *rest[:9], *rest[9 + 7:])


def _rwkv_prep(p3, geo, shift_s, prm, li):
    W = RWKV_WIDTH
    zeros64 = jnp.zeros((LORA_W, W), F32)
    w2p = jnp.concatenate([prm['rwkv_w2'][li], zeros64], 0).astype(BF16)
    a2p = jnp.concatenate([zeros64, prm['rwkv_a2'][li]], 0).astype(BF16)
    hid = jnp.arange(W) // RWKV_HEAD
    ones_bd = (hid[:, None] == hid[None, :]).astype(F32)
    consts = [prm['rwkv_mu'][li].reshape(1, -1), prm['rwkv_w0'][li].reshape(1, -1), w2p,
              prm['rwkv_a0'][li].reshape(1, -1), a2p, prm['rwkv_g2'][li].astype(BF16),
              prm['rwkv_k_k'][li].reshape(1, -1), prm['rwkv_k_a'][li].reshape(1, -1), ones_bd]
    outs_shape = [jax.ShapeDtypeStruct((NSLAB, geo.slab, W), F32)] * 7
    tm = _div_tile(geo.tp, SEQ_TILE)
    outs = pl.pallas_call(
        functools.partial(_rwkv_prep_prompt_kernel, tm=tm, padf=geo.padf),
        grid=(NSLAB, geo.tp // tm),
        in_specs=[pl.BlockSpec((None, tm, RWKV_PROJ), lambda b, c: (b, c, 0)),
                  pl.BlockSpec((None, SUB, RWKV_PROJ), lambda b, c: (b, jnp.maximum(c * (tm // SUB) - 1, 0), 0))]
        + [pl.BlockSpec(x.shape, lambda b, c: (0, 0)) for x in consts],
        out_specs=[pl.BlockSpec((None, tm, W), lambda b, c: (b, c, 0))] * 7,
        out_shape=outs_shape,
        compiler_params=_cp("parallel", "parallel"),
        name="rwkv_prep_prompt",
    )(p3, p3, *consts)
    rows = geo.sps
    sb = geo.sample_blk(rows)
    nin = 2 + len(consts)
    outs = pl.pallas_call(
        _rwkv_prep_sample_kernel,
        grid=(NSLAB,),
        in_specs=[pl.BlockSpec((None, rows, RWKV_PROJ), lambda k: sb(k) + (0,)),
                  pl.BlockSpec((rows, RWKV_PROJ), lambda k: (k, 0))]
        + [pl.BlockSpec(x.shape, lambda k: (0, 0)) for x in consts] + [pl.BlockSpec(memory_space=pl.ANY)] * 7,
        out_specs=[pl.BlockSpec((None, rows, W), lambda k: sb(k) + (0,))] * 7,
        out_shape=outs_shape,
        input_output_aliases={nin + j: j for j in range(7)},
        compiler_params=_cp("parallel"),
        name="rwkv_prep_sample",
    )(p3, shift_s, *consts, *outs)
    return outs


def _wkv_chunk_kernel(r_ref, lw_ref, k_ref, v_ref, a_ref, b_ref, y_ref, gout_ref, g_scr, *, padf):
    c = pl.program_id(1)
    L, HG = WKV_L, WKV_HG
    NG = RWKV_HEADS // HG
    WD = HG * RWKV_HEAD
    M = HG * L

    @pl.when(c == 0)
    def _():
        g_scr[...] = jnp.zeros_like(g_scr)

    real = (lax.broadcasted_iota(jnp.int32, (L, 1), 0) + c * L) >= padf

    def ld(ref):
        return jnp.where(real, jnp.concatenate([ref[i] for i in range(WKV_NB)], axis=1), 0.0)

    r, lw, k, v, a, b = ld(r_ref), ld(lw_ref), ld(k_ref), ld(v_ref), ld(a_ref), ld(b_ref)
    ti = lax.broadcasted_iota(jnp.int32, (L, L), 0)
    tj = lax.broadcasted_iota(jnp.int32, (L, L), 1)
    tri = jnp.where(ti >= tj, 1.0, 0.0)
    cum = _dot3(tri, lw)
    dec = jnp.exp(cum)
    inv = jnp.exp(-cum)
    at = a * jnp.exp(cum - lw)
    bt = b * inv
    kt = k * inv
    rt = r * dec
    rh = lax.broadcasted_iota(jnp.int32, (M, WD), 0) // L
    lh = lax.broadcasted_iota(jnp.int32, (M, WD), 1) // RWKV_HEAD
    bd = rh == lh
    mi = lax.broadcasted_iota(jnp.int32, (M, M), 0)
    mj = lax.broadcasted_iota(jnp.int32, (M, M), 1)
    same = (mi // L) == (mj // L)
    m_strict = jnp.logical_and(same, (mj % L) < (mi % L))
    m_incl = jnp.logical_and(same, (mj % L) <= (mi % L))
    eye = jnp.where(mi == mj, 1.0, 0.0)

    def stack(x):
        return jnp.where(bd, jnp.concatenate([x] * HG, axis=0), 0.0)

    chains = range(WKV_NB * NG)
    sls = [slice(i * WD, (i + 1) * WD) for i in chains]
    a_s = [stack(at[:, sl]).astype(BF16) for sl in sls]
    r_s = [stack(rt[:, sl]).astype(BF16) for sl in sls]
    v_s = [stack(v[:, sl]).astype(BF16) for sl in sls]
    bk_s = [jnp.concatenate([stack(bt[:, sl]), stack(kt[:, sl])], axis=0).astype(BF16) for sl in sls]
    g = [g_scr[i] for i in chains]
    gb = [x.astype(BF16) for x in g]
    prod = [_dg(jnp.concatenate([a_s[i], r_s[i]], axis=0), bk_s[i], NT) for i in chains]
    ab = [jnp.where(m_strict, p[:M, :M], 0.0) for p in prod]
    ak = [jnp.where(m_strict, p[:M, M:], 0.0).astype(BF16) for p in prod]
    rb = [jnp.where(m_incl, p[M:, :M], 0.0).astype(BF16) for p in prod]
    rk = [jnp.where(m_incl, p[M:, M:], 0.0).astype(BF16) for p in prod]
    tinv = [eye + x for x in ab]
    pw = ab
    for _ in range(int(math.log2(L)) - 1):
        pw = [_dotc(x, x) for x in pw]
        tinv = [t + _dotc(t, x) for t, x in zip(tinv, pw)]
    z = [_dg(a_s[i], gb[i], NT) + _dg(ak[i], v_s[i], NN) for i in chains]
    u = [_dotc(tinv[i], z[i]).astype(BF16) for i in chains]
    ybd = [_dg(r_s[i], gb[i], NT) + _dg(rb[i], u[i], NN) + _dg(rk[i], v_s[i], NN) for i in chains]
    upd = [_dg(jnp.concatenate([u[i], v_s[i]], axis=0), bk_s[i], TN) for i in chains]
    g_last = []
    for i in chains:
        y = ybd[i][0:L]
        for h in range(1, HG):
            y = y + ybd[i][h * L:(h + 1) * L]
        y_ref[i // NG, :, (i % NG) * WD:(i % NG + 1) * WD] = y
        g_new = (g[i] + upd[i]) * dec[L - 1:L, sls[i]]
        g_scr[i] = g_new
        g_last.append(g_new)

    @pl.when(c == pl.num_programs(1) - 1)
    def _():
        for grp, g_new in enumerate(g_last):
            gout_ref[grp // NG, grp % NG] = g_new


def _wkv_prompt(seq, geo):
    L = WKV_L
    NG = RWKV_HEADS // WKV_HG
    WD = WKV_HG * RWKV_HEAD
    spec = pl.BlockSpec((WKV_NB, L, RWKV_WIDTH), lambda b, c: (b, c, 0))
    y, g = pl.pallas_call(
        functools.partial(_wkv_chunk_kernel, padf=geo.padf),
        grid=(NSLAB // WKV_NB, geo.tp // L),
        in_specs=[spec] * 6,
        out_specs=[spec, pl.BlockSpec((WKV_NB, NG, WD, WD), lambda b, c: (b, 0, 0, 0))],
        out_shape=[jax.ShapeDtypeStruct((NSLAB, geo.slab, RWKV_WIDTH), F32),
                   jax.ShapeDtypeStruct((NSLAB, NG, WD, WD), F32)],
        scratch_shapes=[pltpu.VMEM((WKV_NB * NG, WD, WD), F32)],
        compiler_params=_cp("parallel", "arbitrary"),
        name="wkv_chunk",
    )(*seq)
    g = g.reshape(NSLAB, NG, WKV_HG, RWKV_HEAD, WKV_HG, RWKV_HEAD)
    idx = jnp.arange(WKV_HG)
    s = g[:, :, idx, :, idx, :]
    s = jnp.transpose(s, (1, 2, 0, 3, 4)).reshape(NSLAB, RWKV_HEADS, RWKV_HEAD, RWKV_HEAD)
    return y, s


def _wkv_step_kernel(r_ref, lw_ref, k_ref, v_ref, a_ref, b_ref, s_ref, y_ref, sout_ref):
    H, D = RWKV_HEADS, RWKV_HEAD
    ri = lax.broadcasted_iota(jnp.int32, (H * D, D), 0) % D
    ci = lax.broadcasted_iota(jnp.int32, (H * D, D), 1)
    eye = jnp.where(ri == ci, 1.0, 0.0)

    def expand(x):
        return jnp.concatenate([jnp.broadcast_to(x[:, h * D:(h + 1) * D], (D, D)) for h in range(H)], axis=0)

    for j in range(SUB):
        row = lambda ref: ref[j:j + 1, :]
        s = s_ref[j].reshape(H * D, D)
        w_e = jnp.exp(expand(row(lw_ref)))
        v_col = jnp.sum(expand(row(v_ref)) * eye, axis=1, keepdims=True)
        sa = jnp.sum(s * expand(row(a_ref)), axis=1, keepdims=True)
        s_new = s * w_e + sa * expand(row(b_ref)) + v_col * expand(row(k_ref))
        y_col = jnp.sum(s_new * expand(row(r_ref)), axis=1, keepdims=True)
        ye = y_col * eye
        y_row = jnp.concatenate([jnp.sum(ye[h * D:(h + 1) * D], axis=0, keepdims=True) for h in range(H)], axis=1)
        y_ref[j:j + 1, :] = y_row
        sout_ref[j] = s_new.reshape(H, D, D)


def _wkv_sample(seq, geo, s0, y_prev):
    sb = geo.sample_blk(SUB)
    spec = pl.BlockSpec((None, SUB, RWKV_WIDTH), lambda k: sb(k) + (0,))
    sspec = pl.BlockSpec((SUB, RWKV_HEADS, RWKV_HEAD, RWKV_HEAD), lambda k: (k, 0, 0, 0))
    return pl.pallas_call(
        functools.partial(_drop_arg, _wkv_step_kernel, 7),
        grid=(geo.ns // SUB,),
        in_specs=[spec] * 6 + [sspec, pl.BlockSpec(memory_space=pl.ANY)],
        out_specs=[spec, sspec],
        out_shape=[jax.ShapeDtypeStruct(y_prev.shape, F32), jax.ShapeDtypeStruct(s0.shape, F32)],
        input_output_aliases={7: 0},
        compiler_params=_cp("parallel"),
        name="wkv_step",
    )(*seq, s0, y_prev)


def _rwkv_post_kernel(y_ref, r_ref, k_ref, v_ref, g_ref, gng_ref, gnb_ref, rk_ref, ones_ref, o_ref):
    y = y_ref[...]
    ones = ones_ref[...]
    mu = _head_sum(y, ones) * (1.0 / RWKV_HEAD)
    yc = y - mu
    var = _head_sum(yc * yc, ones) * (1.0 / RWKV_HEAD)
    yn = yc * lax.rsqrt(var + RWKV_GN_EPS) * gng_ref[...] + gnb_ref[...]
    v = v_ref[...]
    bonus = _head_sum(r_ref[...] * k_ref[...] * rk_ref[...], ones) * v
    o_ref[...] = (yn + bonus) * g_ref[...]


def _rwkv_post(y, r, k, v, g, prm, li, tm):
    n = y.shape[0]
    W = RWKV_WIDTH
    hid = jnp.arange(W) // RWKV_HEAD
    ones_bd = (hid[:, None] == hid[None, :]).astype(F32)
    row = pl.BlockSpec((tm, W), lambda i: (i, 0))
    vec = pl.BlockSpec((1, W), lambda i: (0, 0))
    return pl.pallas_call(
        _rwkv_post_kernel,
        grid=(n // tm,),
        in_specs=[row] * 5 + [vec] * 3 + [pl.BlockSpec((W, W), lambda i: (0, 0))],
        out_specs=row,
        out_shape=jax.ShapeDtypeStruct((n, W), F32),
        compiler_params=_cp("parallel"),
        name="rwkv_post",
    )(y, r, k, v, g, prm['rwkv_gn_g'][li].reshape(1, -1), prm['rwkv_gn_b'][li].reshape(1, -1),
      prm['rwkv_r_k'][li].reshape(1, -1), ones_bd)


def _mla_prep_kernel(ckv_ref, qa_ref, kr_ref, cos_ref, sin_ref, qn_ref, kvn_ref, wqb_ref, wuk_ref,
                     qlat_ref, qrope_ref, c_ref, kr8_ref):
    qa = qa_ref[:, :Q_LORA]
    qa = qa * lax.rsqrt(jnp.mean(qa * qa, -1, keepdims=True) + 1e-6) * qn_ref[...]
    q = _bdot(qa, wqb_ref[...])
    nope_w = MLA_HEADS * QK_NOPE
    cos, sin = cos_ref[...], sin_ref[...]
    qrope_ref[...] = _rope(q[:, nope_w:], cos, sin, QK_ROPE // 2)
    qlat_ref[...] = _bdot(q[:, :nope_w], wuk_ref[...])
    ckv = ckv_ref[...]
    c_ref[...] = ckv * lax.rsqrt(jnp.mean(ckv * ckv, -1, keepdims=True) + 1e-6) * kvn_ref[...]
    kr8_ref[...] = _rope(kr_ref[...], cos, sin, QK_ROPE // 2)


def _mla_prep(p, geo, pos_slab, prm, li):
    n = p.shape[0]
    tm, tps = geo.tm, geo.tps
    rw = MLA_HEADS * QK_ROPE
    cos, sin = _rope_tables(pos_slab, QK_ROPE, MLA_HEADS)
    wqb = prm['mla_w_qb'][li].reshape(Q_LORA, MLA_HEADS, QK_NOPE + QK_ROPE)
    wqb = jnp.concatenate([wqb[:, :, :QK_NOPE].reshape(Q_LORA, -1), wqb[:, :, QK_NOPE:].reshape(Q_LORA, -1)], 1)
    eye = jnp.eye(MLA_HEADS, dtype=F32)
    wuk = jnp.einsum('chn,hg->hngc', prm['mla_w_uk'][li], eye).reshape(MLA_HEADS * QK_NOPE, MLA_HEADS * KV_LORA)
    fix = lambda i: (0, 0)
    c0 = RWKV_PROJ // KV_LORA
    q0 = (RWKV_PROJ + KV_LORA) // 512
    k0 = (RWKV_PROJ + KV_LORA + 512) // rw
    tab = pl.BlockSpec((tm, rw), lambda i: (i % tps, 0))
    return pl.pallas_call(
        _mla_prep_kernel,
        grid=(n // tm,),
        in_specs=[pl.BlockSpec((tm, KV_LORA), lambda i: (i, c0)), pl.BlockSpec((tm, 512), lambda i: (i, q0)),
                  pl.BlockSpec((tm, rw), lambda i: (i, k0)), tab, tab,
                  pl.BlockSpec((1, Q_LORA), fix), pl.BlockSpec((1, KV_LORA), fix),
                  pl.BlockSpec(wqb.shape, fix), pl.BlockSpec(wuk.shape, fix)],
        out_specs=[pl.BlockSpec((tm, MLA_HEADS * KV_LORA), lambda i: (i, 0)), pl.BlockSpec((tm, rw), lambda i: (i, 0)),
                   pl.BlockSpec((tm, KV_LORA), lambda i: (i, 0)), pl.BlockSpec((tm, rw), lambda i: (i, 0))],
        out_shape=[jax.ShapeDtypeStruct((n, MLA_HEADS * KV_LORA), F32), jax.ShapeDtypeStruct((n, rw), F32),
                   jax.ShapeDtypeStruct((n, KV_LORA), F32), jax.ShapeDtypeStruct((n, rw), F32)],
        compiler_params=_cp("parallel"),
        name="mla_prep",
    )(p, p, p, cos, sin, prm['mla_q_norm'][li].reshape(1, -1), prm['mla_kv_norm'][li].reshape(1, -1),
      wqb.astype(BF16), wuk.astype(BF16))


def _wuv_blockdiag(w_uv):
    eye = jnp.eye(MLA_HEADS, dtype=F32)
    return jnp.einsum('chv,hg->hcgv', w_uv, eye).reshape(MLA_HEADS * KV_LORA, MLA_WIDTH).astype(BF16)


def _mla_attn_kernel(qlat_ref, qrope_ref, c_ref, kr_ref, wuv_ref, o_ref, qs_scr, qrs_scr, m_scr, l_scr, acc_scr,
                     *, lq, padf):
    qi, ki = pl.program_id(1), pl.program_id(2)
    H = MLA_HEADS

    @pl.when(ki == 0)
    def _():
        ql = qlat_ref[...]
        qs_scr[...] = jnp.concatenate([ql[:, h * KV_LORA:(h + 1) * KV_LORA] for h in range(H)], axis=0).astype(BF16)
        qr = qrope_ref[...].astype(BF16)
        lane = lax.broadcasted_iota(jnp.int32, qr.shape, 1) // QK_ROPE
        qrs_scr[...] = jnp.concatenate([jnp.where(lane == h, qr, jnp.zeros_like(qr)) for h in range(H)], axis=0)
        m_scr[...] = jnp.full_like(m_scr, NEG)
        l_scr[...] = jnp.zeros_like(l_scr)
        acc_scr[...] = jnp.zeros_like(acc_scr)

    def tile(masked):
        cb = c_ref[...].astype(BF16)
        st = (_dg(cb, qs_scr[...], NT) + _dg(kr_ref[...].astype(BF16), qrs_scr[...], NT)) * MLA_SCALE
        if masked:
            kpos = ki * lq + lax.broadcasted_iota(jnp.int32, st.shape, 0)
            qpos = qi * lq + lax.broadcasted_iota(jnp.int32, st.shape, 1) % lq
            st = jnp.where(jnp.logical_and(kpos <= qpos, kpos >= padf), st, NEG)
        m_old = m_scr[...]
        m_new = jnp.maximum(m_old, jnp.max(st, axis=0, keepdims=True))
        alpha = jnp.exp(m_old - m_new)
        pt = jnp.exp(st - m_new)
        l_scr[...] = alpha * l_scr[...] + jnp.sum(pt, axis=0, keepdims=True)
        acc_scr[...] = alpha * acc_scr[...] + _dg(cb, pt.astype(BF16), TN)
        m_scr[...] = m_new

    assert padf <= lq
    edge = jnp.logical_or(ki == qi, ki == 0)
    pl.when(jnp.logical_and(ki <= qi, edge))(functools.partial(tile, True))
    pl.when(jnp.logical_and(ki < qi, jnp.logical_not(edge)))(functools.partial(tile, False))

    @pl.when(ki == pl.num_programs(2) - 1)
    def _():
        ot = (acc_scr[...] * (1.0 / l_scr[...])).astype(BF16)
        y = jnp.zeros(o_ref.shape, F32)
        for h in range(H):
            y = y + _dg(ot[:, h * lq:(h + 1) * lq], wuv_ref[h], TN)
        o_ref[...] = y


def _mla_attn_prompt(qlat3, qrope3, c3, kr3, w_uv, geo):
    lq = _div_tile(geo.tp, SEQ_TILE)
    nq = geo.tp // lq
    wuv = _wuv_blockdiag(w_uv).reshape(MLA_HEADS, KV_LORA, MLA_WIDTH)
    rw = MLA_HEADS * QK_ROPE
    qmap = lambda b, qi, ki: (b, qi, 0)
    kmap = lambda b, qi, ki: (b, jnp.minimum(ki, qi), 0)
    return pl.pallas_call(
        functools.partial(_mla_attn_kernel, lq=lq, padf=geo.padf),
        grid=(NSLAB, nq, nq),
        in_specs=[pl.BlockSpec((None, lq, MLA_HEADS * KV_LORA), qmap), pl.BlockSpec((None, lq, rw), qmap),
                  pl.BlockSpec((None, lq, KV_LORA), kmap), pl.BlockSpec((None, lq, rw), kmap),
                  pl.BlockSpec(wuv.shape, lambda b, qi, ki: (0, 0, 0))],
        out_specs=pl.BlockSpec((None, lq, MLA_WIDTH), qmap),
        out_shape=jax.ShapeDtypeStruct((NSLAB, geo.slab, MLA_WIDTH), F32),
        scratch_shapes=[pltpu.VMEM((MLA_HEADS * lq, KV_LORA), BF16), pltpu.VMEM((MLA_HEADS * lq, rw), BF16),
                        pltpu.VMEM((1, MLA_HEADS * lq), F32), pltpu.VMEM((1, MLA_HEADS * lq), F32),
                        pltpu.VMEM((KV_LORA, MLA_HEADS * lq), F32)],
        compiler_params=_cp("parallel", "parallel", "arbitrary"),
        name="mla_attn",
    )(qlat3, qrope3, c3, kr3, wuv)


def _mla_decode_kernel(pt_ref, qlat_ref, qrope_ref, cnew_ref, krnew_ref, kv_hbm, kr_hbm, tile_ref, wuv_ref,
                       yprev_ref, o_ref, kv_buf, kr_buf, kb_scr, s_scr, sem, *, li, n_pages, ch):
    del yprev_ref
    b = pl.program_id(0)
    slot = b % 2
    H = MLA_HEADS

    def page_copies(page, sl, pg):
        return (pltpu.make_async_copy(kv_hbm.at[page, li], kv_buf.at[sl, pg], sem.at[0, sl]),
                pltpu.make_async_copy(kr_hbm.at[page, li], kr_buf.at[sl, pg], sem.at[1, sl]))

    def start_fetch(bb, sl):
        def body(pg, carry):
            for cp in page_copies(pt_ref[bb, pg], sl, pg):
                cp.start()
            return carry
        lax.fori_loop(0, n_pages, body, 0)

    def wait_fetch(sl):
        def body(pg, carry):
            for cp in page_copies(0, sl, pg):
                cp.wait()
            return carry
        lax.fori_loop(0, n_pages, body, 0)

    @pl.when(b == 0)
    def _():
        start_fetch(0, 0)

    nxt = (b + 1) % pl.num_programs(0)

    j = b % SUB
    rid = lax.broadcasted_iota(jnp.int32, (SUB, 1), 0)

    def pick(ref):
        return jnp.sum(jnp.where(rid == j, ref[...], 0.0), axis=0, keepdims=True)

    ql = pick(qlat_ref)
    qs = jnp.concatenate([ql[:, h * KV_LORA:(h + 1) * KV_LORA] for h in range(H)], axis=0).astype(BF16)
    qr = jnp.broadcast_to(pick(qrope_ref), (H, H * QK_ROPE))
    hl = lax.broadcasted_iota(jnp.int32, qr.shape, 1) // QK_ROPE
    hr = lax.broadcasted_iota(jnp.int32, qr.shape, 0)
    qrs = jnp.where(hl == hr, qr, 0.0).astype(BF16)
    qr32 = _dg(qrs, tile_ref[...], NT).astype(BF16)
    cn = _rb(pick(cnew_ref))
    kn = _rb(pick(krnew_ref))
    s_self = (jnp.sum(qs.astype(F32) * cn, axis=1, keepdims=True)
              + jnp.sum(qrs.astype(F32) * kn, axis=1, keepdims=True)) * MLA_SCALE

    wait_fetch(slot)
    cw = ch * PAGE_SIZE
    for ci in range(n_pages // ch):
        for pg in range(ci * ch, (ci + 1) * ch):
            for cp in page_copies(pt_ref[nxt, pg], 1 - slot, pg):
                cp.start()
        kc = kv_buf[slot, ci * ch:(ci + 1) * ch].reshape(cw, KV_LORA).astype(BF16)
        kb_scr[ci * cw:(ci + 1) * cw, :] = kc
        krc = jnp.concatenate([kr_buf[slot, pg] for pg in range(ci * ch, (ci + 1) * ch)], axis=1).astype(BF16)
        s_scr[:, ci * cw:(ci + 1) * cw] = (_dg(qs, kc, NT) + _dg(qr32, krc, NN)) * MLA_SCALE
    s = s_scr[...]
    m = jnp.maximum(jnp.max(s, axis=1, keepdims=True), s_self)
    e = jnp.exp(s - m)
    e_self = jnp.exp(s_self - m)
    rl = 1.0 / (jnp.sum(e, axis=1, keepdims=True) + e_self)
    p = (e * rl).astype(BF16)
    acc = _rb(e_self * rl) * cn
    for ci in range(n_pages // ch):
        acc = acc + _dg(p[:, ci * cw:(ci + 1) * cw], kb_scr[ci * cw:(ci + 1) * cw, :], NN)
    ol = jnp.concatenate([acc[h:h + 1] for h in range(H)], axis=1)
    y = _bdot(jnp.broadcast_to(ol, (SUB, ol.shape[1])), wuv_ref[...])
    o_ref[pl.ds(j, 1), :] = y[0:1]

    @pl.when(b == pl.num_programs(0) - 1)
    def _():
        wait_fetch(1 - slot)


def _mla_decode(qlat, qrope, c, kr8, cache_kv, cache_kr, page_table, w_uv, li, y_prev, geo):
    n = qlat.shape[0]
    n_pages = page_table.shape[1]
    ch = _div_tile(n_pages, 8, 1)
    rw = MLA_HEADS * QK_ROPE
    wuv = _wuv_blockdiag(w_uv)
    tile = jnp.tile(jnp.eye(QK_ROPE, dtype=F32), (1, MLA_HEADS)).astype(BF16)

    def rowmap(b, pt):
        return ((b // geo.sps) * geo.slab + geo.tp + b % geo.sps) // SUB, 0

    rowblk = lambda w: pl.BlockSpec((SUB, w), rowmap)
    grid_spec = pltpu.PrefetchScalarGridSpec(
        num_scalar_prefetch=1,
        grid=(geo.ns,),
        in_specs=[rowblk(MLA_HEADS * KV_LORA), rowblk(rw), rowblk(KV_LORA), rowblk(rw),
                  pl.BlockSpec(memory_space=pl.ANY), pl.BlockSpec(memory_space=pl.ANY),
                  pl.BlockSpec(tile.shape, lambda b, pt: (0, 0)),
                  pl.BlockSpec(wuv.shape, lambda b, pt: (0, 0)),
                  pl.BlockSpec(memory_space=pl.ANY)],
        out_specs=rowblk(MLA_WIDTH),
        scratch_shapes=[pltpu.VMEM((2, n_pages, PAGE_SIZE, KV_LORA), F32),
                        pltpu.VMEM((2, n_pages, QK_ROPE, PAGE_SIZE), F32),
                        pltpu.VMEM((n_pages * PAGE_SIZE, KV_LORA), BF16),
                        pltpu.VMEM((MLA_HEADS, n_pages * PAGE_SIZE), F32),
                        pltpu.SemaphoreType.DMA((2, 2))],
    )
    return pl.pallas_call(
        functools.partial(_mla_decode_kernel, li=li, n_pages=n_pages, ch=ch),
        grid_spec=grid_spec,
        out_shape=jax.ShapeDtypeStruct((n, MLA_WIDTH), F32),
        input_output_aliases={9: 0},
        compiler_params=_cp("arbitrary"),
        name="mla_decode",
    )(page_table, qlat, qrope, c, kr8, cache_kv, jnp.swapaxes(cache_kr, 2, 3), tile, wuv, y_prev)


def _forward(x_prompt, x_sample, state_ret, state_s5_re, state_s5_im, state_wkv, state_shift, cache_kv,
             cache_krope, page_table, meta_tokens, prm):
    nb, seq, _ = x_prompt.shape
    assert nb == NSLAB and x_sample.shape[1] == 1
    ns = x_sample.shape[0]
    geo = _Geom(seq, ns, page_table.shape[1])
    n, tm, tp, padf, sps = geo.n, geo.tm, geo.tp, geo.padf, geo.sps

    meta = jnp.broadcast_to(meta_tokens[None], (nb, N_META, D_MODEL))
    x = jnp.concatenate([jnp.zeros((nb, padf, D_MODEL), F32), meta, x_prompt, x_sample.reshape(nb, sps, D_MODEL)], 1)
    x = x.reshape(n, D_MODEL)
    pos_slab = jnp.concatenate([jnp.arange(tp, dtype=jnp.int32) - padf, jnp.full((sps,), geo.past, jnp.int32)])

    ret_p, ret_s, s5_p, s5_s, wkv_p, wkv_s, sh_p, sh_s, kv_l, kr_l = [], [], [], [], [], [], [], [], [], []
    for layer in range(DEPTH):
        li = layer // 2
        if layer % 2 == 0:
            w_in = prm['ev_w_in'][li].astype(BF16)
            p = _matmul(x, w_in, tm, _div_tile(w_in.shape[1], 1024, 128))
            p3 = geo.v3(p)
            gng, gnb = prm['ret_gn_g'][li], prm['ret_gn_b'][li]
            ret, s_p = _retention_prompt(p3, geo, gng, gnb)
            ret, s_s = _retention_sample(p3, geo, state_ret[li], gng, gnb, ret)
            s5o, h_p = _s5_prompt(p3, geo, prm, li)
            h0 = jnp.concatenate([state_s5_re[li].reshape(ns, S5_CH), state_s5_im[li].reshape(ns, S5_CH)], 1)
            s5o, h_s = _s5_sample(p3, geo, h0, prm, li, s5o)
            ret_p.append(s_p)
            ret_s.append(s_s)
            s5_p.append(h_p)
            s5_s.append(h_s)
            x = _proj_ln(ret.reshape(n, -1), s5o.reshape(n, -1), x, prm['ev_w_out'][li], prm['ln1_g'][layer],
                         prm['ln1_b'][layer], tm)
        else:
            w = prm['od_w_in'][li]
            o1 = RWKV_PROJ
            o2 = o1 + Q_LORA
            o3 = o2 + KV_LORA
            w_in = jnp.concatenate([w[:, :o1], w[:, o2:o3], w[:, o1:o2], jnp.zeros((D_MODEL, 512 - Q_LORA), F32),
                                    jnp.tile(w[:, o3:], (1, MLA_HEADS))], 1).astype(BF16)
            p = _matmul(x, w_in, tm, _div_tile(w_in.shape[1], 1408, 128))
            p3 = geo.v3(p)
            sh_p.append(p3[:, tp - 1, :RWKV_PROJ])
            sh_s.append(p3[:, tp:, :RWKV_PROJ].reshape(ns, RWKV_PROJ))
            r, lw, k, v, a, b, g = _rwkv_prep(p3, geo, state_shift[li], prm, li)
            seqs = (r, lw, k, v, a, b)
            y, g_p = _wkv_prompt(seqs, geo)
            y, g_s = _wkv_sample(seqs, geo, state_wkv[li], y)
            wkv_p.append(g_p)
            wkv_s.append(g_s)
            flat = lambda t: t.reshape(n, t.shape[-1])
            y_c = _rwkv_post(flat(y), flat(r), flat(k), flat(v), flat(g), prm, li, tm)
            qlat, qrope, c, kr8 = _mla_prep(p, geo, pos_slab, prm, li)
            kv_l.append(geo.v3(c))
            kr_l.append(geo.v3(kr8)[:, :, :QK_ROPE])
            y_d = _mla_attn_prompt(geo.v3(qlat), geo.v3(qrope), geo.v3(c), geo.v3(kr8), prm['mla_w_uv'][li], geo)
            y_d = _mla_decode(qlat, qrope, c, kr8, cache_kv, cache_krope, page_table, prm['mla_w_uv'][li], li,
                              flat(y_d), geo)
            x = _proj_ln(y_c, y_d, x, prm['od_w_out'][li], prm['ln1_g'][layer], prm['ln1_b'][layer], tm)
        x = _moe_ln(x, prm, layer, tm)

    x3 = geo.v3(x)
    y_p = x3[:, padf + N_META:tp]
    y_s = x3[:, tp:].reshape(ns, 1, D_MODEL)

    def s5_state(hs, part):
        return jnp.stack([h[:, part * S5_CH:(part + 1) * S5_CH].reshape(-1, S5_GROUPS, S5_STATE) for h in hs])

    kv_p = jnp.stack([c[:, padf:tp] for c in kv_l], 1)
    kv_s = jnp.stack([c[:, tp:].reshape(ns, 1, KV_LORA) for c in kv_l], 1)
    kr_p = jnp.stack([c[:, padf:tp] for c in kr_l], 1)
    kr_s = jnp.stack([c[:, tp:].reshape(ns, 1, QK_ROPE) for c in kr_l], 1)
    return (y_p, y_s, jnp.stack(ret_p), jnp.stack(ret_s), s5_state(s5_p, 0), s5_state(s5_s, 0),
            s5_state(s5_p, 1), s5_state(s5_s, 1), jnp.stack(wkv_p), jnp.stack(wkv_s),
            jnp.stack(sh_p), jnp.stack(sh_s), kv_p, kv_s, kr_p, kr_s)


def kernel(x_prompt, x_sample, state_ret, state_s5_re, state_s5_im, state_wkv, state_shift, cache_kv, cache_krope, page_table, meta_tokens, ev_w_in, ret_gn_g, ret_gn_b, s5_a_re, s5_a_im, s5_log_dt, s5_b_re, s5_b_im, s5_c_re, s5_c_im, s5_d, s5_w_glu, s5_b_glu, ev_w_out, od_w_in, rwkv_mu, rwkv_w0, rwkv_w2, rwkv_a0, rwkv_a2, rwkv_g2, rwkv_k_k, rwkv_k_a, rwkv_r_k, rwkv_gn_g, rwkv_gn_b, mla_q_norm, mla_w_qb, mla_kv_norm, mla_w_uk, mla_w_uv, od_w_out, ln1_g, ln1_b, ln2_g, ln2_b, moe_w_coarse, moe_b_coarse, moe_w_fine, moe_b_fine, moe_w_gate, moe_w_up, moe_w_down):
    prm = dict(ev_w_in=ev_w_in, ret_gn_g=ret_gn_g, ret_gn_b=ret_gn_b, s5_a_re=s5_a_re, s5_a_im=s5_a_im,
               s5_log_dt=s5_log_dt, s5_b_re=s5_b_re, s5_b_im=s5_b_im, s5_c_re=s5_c_re, s5_c_im=s5_c_im,
               s5_d=s5_d, s5_w_glu=s5_w_glu, s5_b_glu=s5_b_glu, ev_w_out=ev_w_out,
               od_w_in=od_w_in, rwkv_mu=rwkv_mu, rwkv_w0=rwkv_w0, rwkv_w2=rwkv_w2, rwkv_a0=rwkv_a0,
               rwkv_a2=rwkv_a2, rwkv_g2=rwkv_g2, rwkv_k_k=rwkv_k_k, rwkv_k_a=rwkv_k_a, rwkv_r_k=rwkv_r_k,
               rwkv_gn_g=rwkv_gn_g, rwkv_gn_b=rwkv_gn_b, mla_q_norm=mla_q_norm, mla_w_qb=mla_w_qb,
               mla_kv_norm=mla_kv_norm, mla_w_uk=mla_w_uk, mla_w_uv=mla_w_uv, od_w_out=od_w_out,
               ln1_g=ln1_g, ln1_b=ln1_b, ln2_g=ln2_g, ln2_b=ln2_b, moe_w_coarse=moe_w_coarse,
               moe_b_coarse=moe_b_coarse, moe_w_fine=moe_w_fine, moe_b_fine=moe_b_fine,
               moe_w_gate=moe_w_gate, moe_w_up=moe_w_up, moe_w_down=moe_w_down)
    return _forward(x_prompt, x_sample, state_ret, state_s5_re, state_s5_im, state_wkv, state_shift, cache_kv,
                    cache_krope, page_table, meta_tokens, prm)
```

```python
import functools
import math

import jax
import jax.numpy as jnp
from jax import lax
from jax.experimental import pallas as pl
from jax.experimental.pallas import tpu as pltpu

F32 = jnp.float32
BF16 = jnp.bfloat16

D_MODEL = 1024
N_META = 16
PAGE_SIZE = 128
RET_HEADS, RET_DK, RET_DV = 4, 64, 128
S5_WIDTH, S5_GROUP, S5_STATE = 512, 16, 64
S5_GROUPS = S5_WIDTH // S5_GROUP
S5_CH = S5_GROUPS * S5_STATE
S5_GPB = 128 // S5_GROUP
S5_NBLK = S5_GROUPS // S5_GPB
S5_BCH = S5_GPB * S5_STATE
RWKV_HEAD, RWKV_WIDTH = 64, 512
RWKV_HEADS = RWKV_WIDTH // RWKV_HEAD
LORA_W, LORA_A, LORA_G = 64, 64, 128
RWKV_PROJ = 3 * RWKV_WIDTH + LORA_W + LORA_A + LORA_G
RWKV_GN_EPS = 64e-5
MLA_HEADS, Q_LORA, KV_LORA, QK_NOPE, QK_ROPE, V_HEAD = 8, 384, 256, 64, 32, 64
MLA_WIDTH = MLA_HEADS * V_HEAD
MLA_SCALE = (QK_NOPE + QK_ROPE) ** -0.5
ROPE_THETA = 10000.0
N_EGROUPS, E_PER_GROUP, D_EXPERT = 4, 4, 256
N_EXPERTS = N_EGROUPS * E_PER_GROUP
DEPTH = 4
DN_ALPHA = (2 * DEPTH) ** 0.25
NSLAB = 8
SUB = 8
LANES = 128

VMEM_LIMIT = 56 * 1024 * 1024
WKV_L = 32
WKV_HG = 4
WKV_NB = 4
ROW_TILE = 1056
SEQ_TILE = 416
S5_TILE = 80
NEG = -1e30

NT = (((1,), (1,)), ((), ()))
TN = (((0,), (0,)), ((), ()))
NN = (((1,), (0,)), ((), ()))


def _cp(*sem):
    return pltpu.CompilerParams(dimension_semantics=sem, vmem_limit_bytes=VMEM_LIMIT)


def _div_tile(n, cap, mult=8):
    best = None
    for d in range(mult, min(n, cap) + 1, mult):
        if n % d == 0:
            best = d
    assert best is not None, (n, cap, mult)
    return best


def _dg(a, b, dims):
    return lax.dot_general(a, b, dims, preferred_element_type=F32)


def _bdot(a, b, dims=NN):
    return _dg(a.astype(BF16), b.astype(BF16), dims)


def _dot3(a, b, dims=NN):
    ah = a.astype(BF16)
    al = (a - ah.astype(F32)).astype(BF16)
    bh = b.astype(BF16)
    bl = (b - bh.astype(F32)).astype(BF16)
    return _dg(ah, bh, dims) + (_dg(ah, bl, dims) + _dg(al, bh, dims))


def _dotc(a, b, dims=NN):
    return _bdot(a, b, dims)


def _rb(x):
    return x.astype(BF16).astype(F32)


def _layer_norm(z, g, b, eps=1e-5):
    mu = jnp.mean(z, -1, keepdims=True)
    zc = z - mu
    var = jnp.mean(zc * zc, -1, keepdims=True)
    return zc * lax.rsqrt(var + eps) * g + b


def _rope(x, cos, sin_signed, half):
    w = x.shape[-1]
    lane = lax.broadcasted_iota(jnp.int32, x.shape, 1)
    first = (lane % (2 * half)) < half
    other = jnp.where(first, pltpu.roll(x, w - half, 1), pltpu.roll(x, half, 1))
    return x * cos + other * sin_signed


def _rope_tables(pos, dim, reps):
    inv = ROPE_THETA ** (-jnp.arange(0, dim, 2, dtype=F32) / dim)
    ang = pos.astype(F32)[:, None] * inv[None, :]
    cos = jnp.cos(ang)
    sin = jnp.sin(ang)
    cos = jnp.tile(jnp.concatenate([cos, cos], -1), (1, reps))
    sin = jnp.tile(jnp.concatenate([-sin, sin], -1), (1, reps))
    return cos, sin


def _drop_arg(kern, idx, *refs):
    return kern(*(refs[:idx] + refs[idx + 1:]))


class _Geom:
    def __init__(self, seq, ns, n_pages):
        self.tq = seq + N_META
        self.padf = (-self.tq) % WKV_L
        self.tp = self.padf + self.tq
        assert ns % (NSLAB * SUB) == 0
        self.ns = ns
        self.sps = ns // NSLAB
        self.slab = self.tp + self.sps
        self.n = NSLAB * self.slab
        self.past = n_pages * PAGE_SIZE
        self.tm = _div_tile(self.slab, ROW_TILE)
        self.tps = self.slab // self.tm

    def v3(self, x):
        return x.reshape(NSLAB, self.slab, x.shape[-1])

    def sample_blk(self, rows):
        per = self.sps // rows
        return lambda k: (k // per, self.tp // rows + k % per)


def _mm_kernel(x_ref, w_ref, o_ref):
    o_ref[...] = _dg(x_ref[...].astype(BF16), w_ref[...], NN)


def _matmul(x, w, tm, tn):
    n, k = x.shape
    m = w.shape[1]
    return pl.pallas_call(
        _mm_kernel,
        grid=(n // tm, m // tn),
        in_specs=[pl.BlockSpec((tm, k), lambda i, j: (i, 0)), pl.BlockSpec((k, tn), lambda i, j: (0, j))],
        out_specs=pl.BlockSpec((tm, tn), lambda i, j: (i, j)),
        out_shape=jax.ShapeDtypeStruct((n, m), F32),
        compiler_params=_cp("parallel", "arbitrary"),
        name="proj_in",
    )(x, w)


def _proj_ln_kernel(a1_ref, a2_ref, x_ref, w1_ref, w2_ref, g_ref, b_ref, o_ref):
    mix = _dg(a1_ref[...].astype(BF16), w1_ref[...], NN) + _dg(a2_ref[...].astype(BF16), w2_ref[...], NN)
    z = DN_ALPHA * x_ref[...] + mix
    o_ref[...] = _layer_norm(z, g_ref[...], b_ref[...])


def _proj_ln(a1, a2, x, w_out, g, b, tm):
    n = x.shape[0]
    k1, k2 = a1.shape[1], a2.shape[1]
    w1 = w_out[:k1].astype(BF16)
    w2 = w_out[k1:].astype(BF16)
    row = lambda i: (i, 0)
    fix = lambda i: (0, 0)
    return pl.pallas_call(
        _proj_ln_kernel,
        grid=(n // tm,),
        in_specs=[pl.BlockSpec((tm, k1), row), pl.BlockSpec((tm, k2), row), pl.BlockSpec((tm, D_MODEL), row),
                  pl.BlockSpec((k1, D_MODEL), fix), pl.BlockSpec((k2, D_MODEL), fix),
                  pl.BlockSpec((1, D_MODEL), fix), pl.BlockSpec((1, D_MODEL), fix)],
        out_specs=pl.BlockSpec((tm, D_MODEL), row),
        out_shape=jax.ShapeDtypeStruct((n, D_MODEL), F32),
        compiler_params=_cp("parallel"),
        name="proj_out_ln",
    )(a1, a2, x, w1, w2, g.reshape(1, -1), b.reshape(1, -1))


MOE_EPS = 2
ROUTE0 = N_EGROUPS


def _route(lt):
    lane = lax.broadcasted_iota(jnp.int32, lt.shape, 1)
    lanef = lane.astype(F32)
    coarse = lane < N_EGROUPS
    lc = jnp.where(coarse, lt, NEG)
    m = jnp.max(lc, axis=1, keepdims=True)
    gsel = jnp.min(jnp.where(lc == m, lanef, 1e9), axis=1, keepdims=True)
    gate_c = 1.0 / jnp.sum(jnp.where(coarse, jnp.exp(lc - m), 0.0), axis=1, keepdims=True)
    grp = ((lane - ROUTE0) // E_PER_GROUP).astype(F32)
    lf = jnp.where(grp == gsel, lt, NEG)
    t1 = jnp.max(lf, axis=1, keepdims=True)
    i1 = jnp.min(jnp.where(lf == t1, lanef, 1e9), axis=1, keepdims=True)
    lf2 = jnp.where(lanef == i1, NEG, lf)
    t2 = jnp.max(lf2, axis=1, keepdims=True)
    i2 = jnp.min(jnp.where(lf2 == t2, lanef, 1e9), axis=1, keepdims=True)
    e2 = jnp.exp(t2 - t1)
    w1 = gate_c / (1.0 + e2)
    return jnp.where(lanef == i1, w1, 0.0) + jnp.where(lanef == i2, w1 * e2, 0.0)


def _moe_kernel(x_ref, wr_ref, br_ref, wgu_ref, wd_ref, g_ref, b_ref, o_ref, xb_scr, comb_scr, h_scr):
    e = pl.program_id(1)

    @pl.when(e == 0)
    def _():
        xb = x_ref[...].astype(BF16)
        xb_scr[...] = xb
        comb_scr[...] = _route(_dg(xb, wr_ref[...], NN) + br_ref[...])

    xb = xb_scr[...]
    comb = comb_scr[...]
    lane = lax.broadcasted_iota(jnp.int32, comb.shape, 1)
    hb = []
    for q in range(MOE_EPS):
        gu = _dg(xb, wgu_ref[q], NN)
        hg, hu = gu[:, :D_EXPERT], gu[:, D_EXPERT:]
        ce = jnp.sum(jnp.where(lane == e * MOE_EPS + q + ROUTE0, comb, 0.0), axis=1, keepdims=True)
        hb.append(((hg * jax.nn.sigmoid(hg)) * hu * ce).astype(BF16))
    for ee in range(N_EXPERTS // MOE_EPS):
        @pl.when(e == ee)
        def _():
            for q in range(MOE_EPS):
                x0 = (ee * MOE_EPS + q) * D_EXPERT
                h_scr[:, x0:x0 + D_EXPERT] = hb[q]

    @pl.when(e == pl.num_programs(1) - 1)
    def _():
        z = DN_ALPHA * x_ref[...] + _dg(h_scr[...], wd_ref[...], NN)
        o_ref[...] = _layer_norm(z, g_ref[...], b_ref[...])


def _moe_ln(x, prm, layer, tm):
    n = x.shape[0]
    wr = jnp.concatenate([prm['moe_w_coarse'][layer],
                          jnp.transpose(prm['moe_w_fine'][layer], (1, 0, 2)).reshape(D_MODEL, N_EXPERTS)], 1)
    wr = jnp.pad(wr, ((0, 0), (0, LANES - wr.shape[1]))).astype(BF16)
    br = jnp.pad(jnp.concatenate([prm['moe_b_coarse'][layer], prm['moe_b_fine'][layer].reshape(-1)]),
                 (0, LANES - N_EGROUPS - N_EXPERTS)).reshape(1, LANES)
    row = lambda i, e: (i, 0)
    fix = lambda i, e: (0, 0)
    exp3 = lambda i, e: (e, 0, 0)
    return pl.pallas_call(
        _moe_kernel,
        grid=(n // tm, N_EXPERTS // MOE_EPS),
        in_specs=[pl.BlockSpec((tm, D_MODEL), row), pl.BlockSpec((D_MODEL, LANES), fix), pl.BlockSpec((1, LANES), fix),
                  pl.BlockSpec((MOE_EPS, D_MODEL, 2 * D_EXPERT), exp3),
                  pl.BlockSpec((N_EXPERTS * D_EXPERT, D_MODEL), fix, pipeline_mode=pl.Buffered(1)),
                  pl.BlockSpec((1, D_MODEL), fix), pl.BlockSpec((1, D_MODEL), fix)],
        out_specs=pl.BlockSpec((tm, D_MODEL), row),
        out_shape=jax.ShapeDtypeStruct((n, D_MODEL), F32),
        scratch_shapes=[pltpu.VMEM((tm, D_MODEL), BF16), pltpu.VMEM((tm, LANES), F32),
                        pltpu.VMEM((tm, N_EXPERTS * D_EXPERT), BF16)],
        compiler_params=_cp("parallel", "arbitrary"),
        name="moe_ln",
    )(x, wr, br, jnp.concatenate([prm['moe_w_gate'][layer], prm['moe_w_up'][layer]], -1).astype(BF16),
      prm['moe_w_down'][layer].astype(BF16).reshape(N_EXPERTS * D_EXPERT, D_MODEL),
      prm['ln2_g'][layer].reshape(1, -1), prm['ln2_b'][layer].reshape(1, -1))


def _ret_log_g():
    return [math.log1p(-2.0 ** (-5.0 - h)) for h in range(RET_HEADS)]


def _ret_kernel(q_ref, k_ref, v_ref, gate_ref, cos_ref, sin_ref, dmask_ref, cdec_ref, kdec_ref, sdec_ref,
                gng_ref, gnb_ref, o_ref, sout_ref, s_scr, *, L, padf):
    c = pl.program_id(1)
    HK, HV = RET_HEADS * RET_DK, RET_HEADS * RET_DV

    @pl.when(c == 0)
    def _():
        s_scr[...] = jnp.zeros_like(s_scr)

    cos, sin = cos_ref[...], sin_ref[...]
    q = _rope(q_ref[...], cos, sin, RET_DK // 2) * RET_DK ** -0.5
    k = _rope(k_ref[...], cos, sin, RET_DK // 2)
    real = (lax.broadcasted_iota(jnp.int32, (L, 1), 0) + c * L) >= padf
    k = jnp.where(real, k, 0.0)
    v = jnp.where(real, v_ref[...], 0.0)
    gate = gate_ref[...]
    qb, kb, vb = q.astype(BF16), k.astype(BF16), v.astype(BF16)
    sbd = s_scr[...]
    cross = _dg(qb, sbd.astype(BF16), NN) * cdec_ref[...]
    lane = lax.broadcasted_iota(jnp.int32, (L, HK), 1)
    outs = []
    for h in range(RET_HEADS):
        qh = jnp.where(lane // RET_DK == h, qb, jnp.zeros_like(qb))
        sc = _dg(qh, kb, NT) * dmask_ref[h]
        sl = slice(h * RET_DV, (h + 1) * RET_DV)
        o = _dg(sc.astype(BF16), vb[:, sl], NN) + cross[:, sl]
        mu = jnp.mean(o, -1, keepdims=True)
        oc = o - mu
        var = jnp.mean(oc * oc, -1, keepdims=True)
        outs.append(oc * lax.rsqrt(var + 1e-5))
    on = jnp.concatenate(outs, axis=1) * gng_ref[...] + gnb_ref[...]
    o_ref[...] = (gate * jax.nn.sigmoid(gate)) * on
    upd = _dg((k * kdec_ref[...]).astype(BF16), vb, TN)
    ri = lax.broadcasted_iota(jnp.int32, (HK, HV), 0) // RET_DK
    ci = lax.broadcasted_iota(jnp.int32, (HK, HV), 1) // RET_DV
    s_new = sdec_ref[...] * sbd + jnp.where(ri == ci, upd, 0.0)
    s_scr[...] = s_new

    @pl.when(c == pl.num_programs(1) - 1)
    def _():
        for h in range(RET_HEADS):
            sout_ref[h] = s_new[h * RET_DK:(h + 1) * RET_DK, h * RET_DV:(h + 1) * RET_DV]


def _retention_prompt(p3, geo, gn_g, gn_b):
    HK, HV = RET_HEADS * RET_DK, RET_HEADS * RET_DV
    L = _div_tile(geo.tp, SEQ_TILE)
    log_g = jnp.asarray(_ret_log_g(), F32)
    idx = jnp.arange(L, dtype=F32)
    diff = idx[:, None] - idx[None, :]
    dmask = jnp.where(diff >= 0, jnp.exp(log_g[:, None, None] * jnp.maximum(diff, 0.0)), 0.0)
    cdec = jnp.repeat(jnp.exp(log_g[None, :] * (idx + 1.0)[:, None]), RET_DV, axis=1)
    kdec = jnp.repeat(jnp.exp(log_g[None, :] * (L - 1.0 - idx)[:, None]), RET_DK, axis=1)
    sdec = jnp.broadcast_to(jnp.repeat(jnp.exp(log_g * L), RET_DK)[:, None], (HK, HV))
    cos, sin = _rope_tables(jnp.arange(geo.tp, dtype=jnp.int32) - geo.padf, RET_DK, RET_HEADS)
    fix2 = lambda b, c: (0, 0)
    tab = pl.BlockSpec((L, HK), lambda b, c: (c, 0))
    return pl.pallas_call(
        functools.partial(_ret_kernel, L=L, padf=geo.padf),
        grid=(NSLAB, geo.tp // L),
        in_specs=[pl.BlockSpec((None, L, HK), lambda b, c: (b, c, 0)), pl.BlockSpec((None, L, HK), lambda b, c: (b, c, 1)),
                  pl.BlockSpec((None, L, HV), lambda b, c: (b, c, 1)), pl.BlockSpec((None, L, HV), lambda b, c: (b, c, 2)),
                  tab, tab, pl.BlockSpec((RET_HEADS, L, L), lambda b, c: (0, 0, 0)),
                  pl.BlockSpec((L, HV), fix2), pl.BlockSpec((L, HK), fix2), pl.BlockSpec((HK, HV), fix2),
                  pl.BlockSpec((1, HV), fix2), pl.BlockSpec((1, HV), fix2)],
        out_specs=[pl.BlockSpec((None, L, HV), lambda b, c: (b, c, 0)),
                   pl.BlockSpec((None, RET_HEADS, RET_DK, RET_DV), lambda b, c: (b, 0, 0, 0))],
        out_shape=[jax.ShapeDtypeStruct((NSLAB, geo.slab, HV), F32),
                   jax.ShapeDtypeStruct((NSLAB, RET_HEADS, RET_DK, RET_DV), F32)],
        scratch_shapes=[pltpu.VMEM((HK, HV), F32)],
        compiler_params=_cp("parallel", "arbitrary"),
        name="retention",
    )(p3, p3, p3, p3, cos, sin, dmask, cdec, kdec, sdec, gn_g.reshape(1, -1), gn_b.reshape(1, -1))


def _ret_step_kernel(q_ref, k_ref, v_ref, gate_ref, cos_ref, sin_ref, gng_ref, gnb_ref, s_ref, o_ref, sout_ref):
    cos, sin = cos_ref[...], sin_ref[...]
    qb = _rb(_rope(q_ref[...], cos, sin, RET_DK // 2) * RET_DK ** -0.5)
    kb = _rb(_rope(k_ref[...], cos, sin, RET_DK // 2))
    vb = _rb(v_ref[...])
    gate = gate_ref[...]
    ri = lax.broadcasted_iota(jnp.int32, (RET_DK, RET_DK), 0)
    ci = lax.broadcasted_iota(jnp.int32, (RET_DK, RET_DK), 1)
    eye = jnp.where(ri == ci, 1.0, 0.0)

    def col(x):
        return jnp.sum(eye * jnp.broadcast_to(x, (RET_DK, RET_DK)), axis=1, keepdims=True)

    for j in range(SUB):
        outs = []
        for h, lg in enumerate(_ret_log_g()):
            g = math.exp(lg)
            qh = qb[j:j + 1, h * RET_DK:(h + 1) * RET_DK]
            kh = kb[j:j + 1, h * RET_DK:(h + 1) * RET_DK]
            vh = vb[j:j + 1, h * RET_DV:(h + 1) * RET_DV]
            s0 = s_ref[j, h]
            qk = _rb(jnp.sum(qh * kh, axis=1, keepdims=True))
            o = qk * vh + g * jnp.sum(col(qh) * _rb(s0), axis=0, keepdims=True)
            sout_ref[j, h] = g * s0 + col(kh) * vh
            mu = jnp.mean(o, -1, keepdims=True)
            oc = o - mu
            var = jnp.mean(oc * oc, -1, keepdims=True)
            outs.append(oc * lax.rsqrt(var + 1e-5))
        on = jnp.concatenate(outs, axis=1) * gng_ref[...] + gnb_ref[...]
        gj = gate[j:j + 1]
        o_ref[j:j + 1, :] = (gj * jax.nn.sigmoid(gj)) * on


def _retention_sample(p3, geo, s0, gn_g, gn_b, out_prev):
    HK, HV = RET_HEADS * RET_DK, RET_HEADS * RET_DV
    cos, sin = _rope_tables(jnp.full((1,), geo.past, jnp.int32), RET_DK, RET_HEADS)
    sb = geo.sample_blk(SUB)
    fix = lambda k: (0, 0)
    st = pl.BlockSpec((SUB, RET_HEADS, RET_DK, RET_DV), lambda k: (k, 0, 0, 0))
    return pl.pallas_call(
        functools.partial(_drop_arg, _ret_step_kernel, 9),
        grid=(geo.ns // SUB,),
        in_specs=[pl.BlockSpec((None, SUB, HK), lambda k: sb(k) + (0,)), pl.BlockSpec((None, SUB, HK), lambda k: sb(k) + (1,)),
                  pl.BlockSpec((None, SUB, HV), lambda k: sb(k) + (1,)), pl.BlockSpec((None, SUB, HV), lambda k: sb(k) + (2,)),
                  pl.BlockSpec((1, HK), fix), pl.BlockSpec((1, HK), fix),
                  pl.BlockSpec((1, HV), fix), pl.BlockSpec((1, HV), fix), st, pl.BlockSpec(memory_space=pl.ANY)],
        out_specs=[pl.BlockSpec((None, SUB, HV), lambda k: sb(k) + (0,)), st],
        out_shape=[jax.ShapeDtypeStruct(out_prev.shape, F32), jax.ShapeDtypeStruct(s0.shape, F32)],
        input_output_aliases={9: 0},
        compiler_params=_cp("parallel"),
        name="retention_step",
    )(p3, p3, p3, p3, cos, sin, gn_g.reshape(1, -1), gn_b.reshape(1, -1), s0, out_prev)


def _s5_params(prm, li):
    f32 = F32
    a_re, a_im = prm['s5_a_re'][li].astype(f32), prm['s5_a_im'][li].astype(f32)
    dt = jnp.exp(prm['s5_log_dt'][li].astype(f32))[:, None]
    mag = jnp.exp(dt * a_re)
    ab_re, ab_im = mag * jnp.cos(dt * a_im), mag * jnp.sin(dt * a_im)
    den = a_re * a_re + a_im * a_im
    n_re = ab_re - 1.0
    f_re = (n_re * a_re + ab_im * a_im) / den
    f_im = (ab_im * a_re - n_re * a_im) / den
    b_re, b_im = prm['s5_b_re'][li].astype(f32), prm['s5_b_im'][li].astype(f32)
    bb_re = f_re[..., None] * b_re - f_im[..., None] * b_im
    bb_im = f_re[..., None] * b_im + f_im[..., None] * b_re
    eye = jnp.eye(S5_GPB, dtype=f32)

    def block_in(w):
        w = w.reshape(S5_NBLK, S5_GPB, S5_STATE, S5_GROUP)
        return jnp.einsum('bgpc,gh->bgchp', w, eye).reshape(S5_NBLK, LANES, S5_BCH)

    def block_out(w):
        w = w.reshape(S5_NBLK, S5_GPB, S5_GROUP, S5_STATE)
        return jnp.einsum('bgcp,gh->bgphc', w, eye).reshape(S5_NBLK, S5_BCH, LANES)

    bbw = jnp.concatenate([block_in(bb_re), block_in(bb_im)], 2).astype(BF16)
    ccw = jnp.concatenate([block_out(prm['s5_c_re'][li]), -block_out(prm['s5_c_im'][li])], 1).astype(BF16)
    lam = jnp.concatenate([ab_re.reshape(1, -1), ab_im.reshape(1, -1)], 1)
    return [bbw, ccw, lam, prm['s5_d'][li].reshape(1, -1), prm['s5_w_glu'][li].astype(BF16),
            prm['s5_b_glu'][li].reshape(1, -1)]


def _s5_specs(fix):
    fix3 = lambda *a: (0, 0, 0)
    return [pl.BlockSpec((S5_NBLK, LANES, 2 * S5_BCH), fix3), pl.BlockSpec((S5_NBLK, 2 * S5_BCH, LANES), fix3),
            pl.BlockSpec((1, 2 * S5_CH), fix), pl.BlockSpec((1, S5_WIDTH), fix),
            pl.BlockSpec((S5_WIDTH, S5_WIDTH), fix), pl.BlockSpec((1, S5_WIDTH), fix)]


def _s5_gate(y, u, d_ref, wglu_ref, bglu_ref):
    y = y + d_ref[...] * u
    z = 0.5 * y * (1.0 + jnp.tanh(math.sqrt(2.0 / math.pi) * (y + 0.044715 * (y * y * y))))
    return z * jax.nn.sigmoid(_dg(z.astype(BF16), wglu_ref[...], NN) + bglu_ref[...])


S5_NJ = 2 * S5_CH // LANES
S5_PAIRS = 8


def _s5_prompt_kernel(u_ref, bb_ref, cc_ref, lam_ref, d_ref, wglu_ref, bglu_ref, o_ref, hout_ref, sc, h_scr,
                      *, lt, pitch, padf):
    i = pl.program_id(0)

    @pl.when(i == 0)
    def _():
        h_scr[...] = jnp.zeros_like(h_scr)

    real = (lax.broadcasted_iota(jnp.int32, (lt, 1), 0) + i * lt) >= padf
    half = S5_NJ // 2
    spb = S5_BCH // LANES
    ub = [jnp.where(real, u_ref[b], 0.0).astype(BF16) for b in range(NSLAB)]
    for k in range(S5_NBLK):
        for b in range(NSLAB):
            bu = _dg(ub[b][:, k * LANES:(k + 1) * LANES], bb_ref[k], NN)
            for q in range(spb):
                sc[k * spb + q, b * pitch:b * pitch + lt, :] = bu[:, q * LANES:(q + 1) * LANES]
                sc[half + k * spb + q, b * pitch:b * pitch + lt, :] = bu[:, S5_BCH + q * LANES:S5_BCH + (q + 1) * LANES]
    for j0 in range(0, half, S5_PAIRS):
        js = list(range(j0, j0 + S5_PAIRS))
        lr = [jnp.broadcast_to(lam_ref[:, j * LANES:(j + 1) * LANES], (NSLAB, LANES)) for j in js]
        li = [jnp.broadcast_to(lam_ref[:, S5_CH + j * LANES:S5_CH + (j + 1) * LANES], (NSLAB, LANES)) for j in js]

        def step(t, carry):
            hr, hi = carry
            rows = pl.ds(t, NSLAB, stride=pitch)
            nr, ni = [], []
            for q, j in enumerate(js):
                r = lr[q] * hr[q] - li[q] * hi[q] + sc[j, rows, :]
                m = lr[q] * hi[q] + li[q] * hr[q] + sc[half + j, rows, :]
                sc[j, rows, :] = r
                sc[half + j, rows, :] = m
                nr.append(r)
                ni.append(m)
            return tuple(nr), tuple(ni)

        init = (tuple(h_scr[:, j * LANES:(j + 1) * LANES] for j in js),
                tuple(h_scr[:, S5_CH + j * LANES:S5_CH + (j + 1) * LANES] for j in js))
        hr, hi = lax.fori_loop(0, lt, step, init, unroll=2)
        for q, j in enumerate(js):
            h_scr[:, j * LANES:(j + 1) * LANES] = hr[q]
            h_scr[:, S5_CH + j * LANES:S5_CH + (j + 1) * LANES] = hi[q]
    ys = [[] for _ in range(NSLAB)]
    for k in range(S5_NBLK):
        for b in range(NSLAB):
            rows = slice(b * pitch, b * pitch + lt)
            hk = jnp.concatenate([sc[k * spb + q, rows, :] for q in range(spb)]
                                 + [sc[half + k * spb + q, rows, :] for q in range(spb)], axis=1)
            ys[b].append(_dg(hk.astype(BF16), cc_ref[k], NN))
    for b in range(NSLAB):
        o_ref[b] = _s5_gate(jnp.concatenate(ys[b], axis=1), u_ref[b], d_ref, wglu_ref, bglu_ref)

    @pl.when(i == pl.num_programs(0) - 1)
    def _():
        hout_ref[...] = h_scr[...]


def _s5_prompt(p3, geo, prm, li):
    lt = _div_tile(geo.tp, S5_TILE, 16)
    pitch = lt + SUB
    ucol = p3.shape[-1] // S5_WIDTH - 1
    fix = lambda i: (0, 0)
    return pl.pallas_call(
        functools.partial(_s5_prompt_kernel, lt=lt, pitch=pitch, padf=geo.padf),
        grid=(geo.tp // lt,),
        in_specs=[pl.BlockSpec((NSLAB, lt, S5_WIDTH), lambda i: (0, i, ucol))] + _s5_specs(fix),
        out_specs=[pl.BlockSpec((NSLAB, lt, S5_WIDTH), lambda i: (0, i, 0)), pl.BlockSpec((NSLAB, 2 * S5_CH), fix)],
        out_shape=[jax.ShapeDtypeStruct((NSLAB, geo.slab, S5_WIDTH), F32), jax.ShapeDtypeStruct((NSLAB, 2 * S5_CH), F32)],
        scratch_shapes=[pltpu.VMEM((S5_NJ, NSLAB * pitch, LANES), F32), pltpu.VMEM((NSLAB, 2 * S5_CH), F32)],
        compiler_params=_cp("arbitrary"),
        name="s5",
    )(p3, *_s5_params(prm, li))


def _s5_step_kernel(u_ref, bb_ref, cc_ref, lam_ref, d_ref, wglu_ref, bglu_ref, h0_ref, o_ref, hout_ref):
    u = u_ref[...]
    ub = u.astype(BF16)
    bu = [_dg(ub[:, k * LANES:(k + 1) * LANES], bb_ref[k], NN) for k in range(S5_NBLK)]
    bu_re = jnp.concatenate([x[:, :S5_BCH] for x in bu], axis=1)
    bu_im = jnp.concatenate([x[:, S5_BCH:] for x in bu], axis=1)
    h0 = h0_ref[...]
    lr, li = lam_ref[:, :S5_CH], lam_ref[:, S5_CH:]
    hr, hi = h0[:, :S5_CH], h0[:, S5_CH:]
    nr = lr * hr - li * hi + bu_re
    ni = lr * hi + li * hr + bu_im
    hout_ref[...] = jnp.concatenate([nr, ni], axis=1)
    ys = []
    for k in range(S5_NBLK):
        sl = slice(k * S5_BCH, (k + 1) * S5_BCH)
        ys.append(_dg(jnp.concatenate([nr[:, sl], ni[:, sl]], axis=1).astype(BF16), cc_ref[k], NN))
    o_ref[...] = _s5_gate(jnp.concatenate(ys, axis=1), u, d_ref, wglu_ref, bglu_ref)


def _s5_sample(p3, geo, h0, prm, li, out_prev):
    rows = geo.sps
    sb = geo.sample_blk(rows)
    ucol = p3.shape[-1] // S5_WIDTH - 1
    fix = lambda k: (0, 0)
    st = pl.BlockSpec((rows, 2 * S5_CH), lambda k: (k, 0))
    return pl.pallas_call(
        functools.partial(_drop_arg, _s5_step_kernel, 8),
        grid=(NSLAB,),
        in_specs=[pl.BlockSpec((None, rows, S5_WIDTH), lambda k: sb(k) + (ucol,))] + _s5_specs(fix)
        + [st, pl.BlockSpec(memory_space=pl.ANY)],
        out_specs=[pl.BlockSpec((None, rows, S5_WIDTH), lambda k: sb(k) + (0,)), st],
        out_shape=[jax.ShapeDtypeStruct(out_prev.shape, F32), jax.ShapeDtypeStruct(h0.shape, F32)],
        input_output_aliases={8: 0},
        compiler_params=_cp("parallel"),
        name="s5_step",
    )(p3, *_s5_params(prm, li), h0, out_prev)


def _head_sum(x, ones_bd):
    ones = ones_bd.astype(BF16)
    hi = x.astype(BF16)
    lo = (x - hi.astype(F32)).astype(BF16)
    return _dg(hi, ones, NN) + _dg(lo, ones, NN)


def _rwkv_prep_body(pr, prev, mu_ref, w0_ref, w2_ref, a0_ref, a2_ref, g2_ref, kk_ref, ka_ref, ones_ref,
                    r_ref, lw_ref, k_ref, v_ref, a_ref, b_ref, g_ref):
    W = RWKV_WIDTH
    xm = pr + (prev - pr) * mu_ref[...]
    r = xm[:, :W]
    k = xm[:, W:2 * W]
    v = xm[:, 2 * W:3 * W]
    xwa = xm[:, 3 * W:3 * W + LORA_W + LORA_A]
    xg = xm[:, 3 * W + LORA_W + LORA_A:]
    wl = w0_ref[...] + _bdot(jnp.tanh(xwa), w2_ref[...])
    sp = jnp.maximum(-wl, 0.0) + jnp.log(1.0 + jnp.exp(-jnp.abs(wl)))
    w = -sp - 0.5
    lw = -jnp.exp(w)
    a = jax.nn.sigmoid(a0_ref[...] + _bdot(xwa, a2_ref[...]))
    g = _bdot(jax.nn.sigmoid(xg), g2_ref[...])
    kk = k * kk_ref[...]
    nrm = jnp.sqrt(_head_sum(kk * kk, ones_ref[...]))
    kk = kk / jnp.maximum(nrm, 1e-12)
    k2 = k * (1.0 + (a - 1.0) * ka_ref[...])
    r_ref[...] = r
    lw_ref[...] = lw
    k_ref[...] = k2
    v_ref[...] = v
    a_ref[...] = -kk
    b_ref[...] = kk * a
    g_ref[...] = g


def _rwkv_prep_prompt_kernel(pr_ref, prev8_ref, *rest, tm, padf):
    pr = pr_ref[...]
    t = lax.broadcasted_iota(jnp.int32, (tm, 1), 0)
    last = jnp.broadcast_to(prev8_ref[SUB - 1:SUB, :], pr.shape)
    prev = jnp.where(t == 0, last, pltpu.roll(pr, 1, 0))
    first = (t + pl.program_id(1) * tm) == padf
    prev = jnp.where(first, 0.0, prev)
    _rwkv_prep_body(pr, prev, *rest)


def _rwkv_prep_sample_kernel(pr_ref, prev_ref, *rest):
    _rwkv_prep_body(pr_ref[...], prev_ref[...], *rest[:9], *rest[9 + 7:])


def _rwkv_prep(p3, geo, shift_s, prm, li):
    W = RWKV_WIDTH
    zeros64 = jnp.zeros((LORA_W, W), F32)
    w2p = jnp.concatenate([prm['rwkv_w2'][li], zeros64], 0).astype(BF16)
    a2p = jnp.concatenate([zeros64, prm['rwkv_a2'][li]], 0).astype(BF16)
    hid = jnp.arange(W) // RWKV_HEAD
    ones_bd = (hid[:, None] == hid[None, :]).astype(F32)
    consts = [prm['rwkv_mu'][li].reshape(1, -1), prm['rwkv_w0'][li].reshape(1, -1), w2p,
              prm['rwkv_a0'][li].reshape(1, -1), a2p, prm['rwkv_g2'][li].astype(BF16),
              prm['rwkv_k_k'][li].reshape(1, -1), prm['rwkv_k_a'][li].reshape(1, -1), ones_bd]
    outs_shape = [jax.ShapeDtypeStruct((NSLAB, geo.slab, W), F32)] * 7
    tm = _div_tile(geo.tp, SEQ_TILE)
    outs = pl.pallas_call(
        functools.partial(_rwkv_prep_prompt_kernel, tm=tm, padf=geo.padf),
        grid=(NSLAB, geo.tp // tm),
        in_specs=[pl.BlockSpec((None, tm, RWKV_PROJ), lambda b, c: (b, c, 0)),
                  pl.BlockSpec((None, SUB, RWKV_PROJ), lambda b, c: (b, jnp.maximum(c * (tm // SUB) - 1, 0), 0))]
        + [pl.BlockSpec(x.shape, lambda b, c: (0, 0)) for x in consts],
        out_specs=[pl.BlockSpec((None, tm, W), lambda b, c: (b, c, 0))] * 7,
        out_shape=outs_shape,
        compiler_params=_cp("parallel", "parallel"),
        name="rwkv_prep_prompt",
    )(p3, p3, *consts)
    rows = geo.sps
    sb = geo.sample_blk(rows)
    nin = 2 + len(consts)
    outs = pl.pallas_call(
        _rwkv_prep_sample_kernel,
        grid=(NSLAB,),
        in_specs=[pl.BlockSpec((None, rows, RWKV_PROJ), lambda k: sb(k) + (0,)),
                  pl.BlockSpec((rows, RWKV_PROJ), lambda k: (k, 0))]
        + [pl.BlockSpec(x.shape, lambda k: (0, 0)) for x in consts] + [pl.BlockSpec(memory_space=pl.ANY)] * 7,
        out_specs=[pl.BlockSpec((None, rows, W), lambda k: sb(k) + (0,))] * 7,
        out_shape=outs_shape,
        input_output_aliases={nin + j: j for j in range(7)},
        compiler_params=_cp("parallel"),
        name="rwkv_prep_sample",
    )(p3, shift_s, *consts, *outs)
    return outs


def _wkv_chunk_kernel(r_ref, lw_ref, k_ref, v_ref, a_ref, b_ref, y_ref, gout_ref, g_scr, *, padf):
    c = pl.program_id(1)
    L, HG = WKV_L, WKV_HG
    NG = RWKV_HEADS // HG
    WD = HG * RWKV_HEAD
    M = HG * L

    @pl.when(c == 0)
    def _():
        g_scr[...] = jnp.zeros_like(g_scr)

    real = (lax.broadcasted_iota(jnp.int32, (L, 1), 0) + c * L) >= padf

    def ld(ref):
        return jnp.where(real, jnp.concatenate([ref[i] for i in range(WKV_NB)], axis=1), 0.0)

    r, lw, k, v, a, b = ld(r_ref), ld(lw_ref), ld(k_ref), ld(v_ref), ld(a_ref), ld(b_ref)
    ti = lax.broadcasted_iota(jnp.int32, (L, L), 0)
    tj = lax.broadcasted_iota(jnp.int32, (L, L), 1)
    tri = jnp.where(ti >= tj, 1.0, 0.0)
    cum = _dot3(tri, lw)
    dec = jnp.exp(cum)
    inv = jnp.exp(-cum)
    at = a * jnp.exp(cum - lw)
    bt = b * inv
    kt = k * inv
    rt = r * dec
    rh = lax.broadcasted_iota(jnp.int32, (M, WD), 0) // L
    lh = lax.broadcasted_iota(jnp.int32, (M, WD), 1) // RWKV_HEAD
    bd = rh == lh
    mi = lax.broadcasted_iota(jnp.int32, (M, M), 0)
    mj = lax.broadcasted_iota(jnp.int32, (M, M), 1)
    same = (mi // L) == (mj // L)
    m_strict = jnp.logical_and(same, (mj % L) < (mi % L))
    m_incl = jnp.logical_and(same, (mj % L) <= (mi % L))
    eye = jnp.where(mi == mj, 1.0, 0.0)

    def stack(x):
        return jnp.where(bd, jnp.concatenate([x] * HG, axis=0), 0.0)

    chains = range(WKV_NB * NG)
    sls = [slice(i * WD, (i + 1) * WD) for i in chains]
    a_s = [stack(at[:, sl]).astype(BF16) for sl in sls]
    r_s = [stack(rt[:, sl]).astype(BF16) for sl in sls]
    v_s = [stack(v[:, sl]).astype(BF16) for sl in sls]
    bk_s = [jnp.concatenate([stack(bt[:, sl]), stack(kt[:, sl])], axis=0).astype(BF16) for sl in sls]
    g = [g_scr[i] for i in chains]
    gb = [x.astype(BF16) for x in g]
    prod = [_dg(jnp.concatenate([a_s[i], r_s[i]], axis=0), bk_s[i], NT) for i in chains]
    ab = [jnp.where(m_strict, p[:M, :M], 0.0) for p in prod]
    ak = [jnp.where(m_strict, p[:M, M:], 0.0).astype(BF16) for p in prod]
    rb = [jnp.where(m_incl, p[M:, :M], 0.0).astype(BF16) for p in prod]
    rk = [jnp.where(m_incl, p[M:, M:], 0.0).astype(BF16) for p in prod]
    tinv = [eye + x for x in ab]
    pw = ab
    for _ in range(int(math.log2(L)) - 1):
        pw = [_dotc(x, x) for x in pw]
        tinv = [t + _dotc(t, x) for t, x in zip(tinv, pw)]
    z = [_dg(a_s[i], gb[i], NT) + _dg(ak[i], v_s[i], NN) for i in chains]
    u = [_dotc(tinv[i], z[i]).astype(BF16) for i in chains]
    ybd = [_dg(r_s[i], gb[i], NT) + _dg(rb[i], u[i], NN) + _dg(rk[i], v_s[i], NN) for i in chains]
    upd = [_dg(jnp.concatenate([u[i], v_s[i]], axis=0), bk_s[i], TN) for i in chains]
    g_last = []
    for i in chains:
        y = ybd[i][0:L]
        for h in range(1, HG):
            y = y + ybd[i][h * L:(h + 1) * L]
        y_ref[i // NG, :, (i % NG) * WD:(i % NG + 1) * WD] = y
        g_new = (g[i] + upd[i]) * dec[L - 1:L, sls[i]]
        g_scr[i] = g_new
        g_last.append(g_new)

    @pl.when(c == pl.num_programs(1) - 1)
    def _():
        for grp, g_new in enumerate(g_last):
            gout_ref[grp // NG, grp % NG] = g_new


def _wkv_prompt(seq, geo):
    L = WKV_L
    NG = RWKV_HEADS // WKV_HG
    WD = WKV_HG * RWKV_HEAD
    spec = pl.BlockSpec((WKV_NB, L, RWKV_WIDTH), lambda b, c: (b, c, 0))
    y, g = pl.pallas_call(
        functools.partial(_wkv_chunk_kernel, padf=geo.padf),
        grid=(NSLAB // WKV_NB, geo.tp // L),
        in_specs=[spec] * 6,
        out_specs=[spec, pl.BlockSpec((WKV_NB, NG, WD, WD), lambda b, c: (b, 0, 0, 0))],
        out_shape=[jax.ShapeDtypeStruct((NSLAB, geo.slab, RWKV_WIDTH), F32),
                   jax.ShapeDtypeStruct((NSLAB, NG, WD, WD), F32)],
        scratch_shapes=[pltpu.VMEM((WKV_NB * NG, WD, WD), F32)],
        compiler_params=_cp("parallel", "arbitrary"),
        name="wkv_chunk",
    )(*seq)
    g = g.reshape(NSLAB, NG, WKV_HG, RWKV_HEAD, WKV_HG, RWKV_HEAD)
    idx = jnp.arange(WKV_HG)
    s = g[:, :, idx, :, idx, :]
    s = jnp.transpose(s, (1, 2, 0, 3, 4)).reshape(NSLAB, RWKV_HEADS, RWKV_HEAD, RWKV_HEAD)
    return y, s


def _wkv_step_kernel(r_ref, lw_ref, k_ref, v_ref, a_ref, b_ref, s_ref, y_ref, sout_ref):
    H, D = RWKV_HEADS, RWKV_HEAD
    ri = lax.broadcasted_iota(jnp.int32, (H * D, D), 0) % D
    ci = lax.broadcasted_iota(jnp.int32, (H * D, D), 1)
    eye = jnp.where(ri == ci, 1.0, 0.0)

    def expand(x):
        return jnp.concatenate([jnp.broadcast_to(x[:, h * D:(h + 1) * D], (D, D)) for h in range(H)], axis=0)

    for j in range(SUB):
        row = lambda ref: ref[j:j + 1, :]
        s = s_ref[j].reshape(H * D, D)
        w_e = jnp.exp(expand(row(lw_ref)))
        v_col = jnp.sum(expand(row(v_ref)) * eye, axis=1, keepdims=True)
        sa = jnp.sum(s * expand(row(a_ref)), axis=1, keepdims=True)
        s_new = s * w_e + sa * expand(row(b_ref)) + v_col * expand(row(k_ref))
        y_col = jnp.sum(s_new * expand(row(r_ref)), axis=1, keepdims=True)
        ye = y_col * eye
        y_row = jnp.concatenate([jnp.sum(ye[h * D:(h + 1) * D], axis=0, keepdims=True) for h in range(H)], axis=1)
        y_ref[j:j + 1, :] = y_row
        sout_ref[j] = s_new.reshape(H, D, D)


def _wkv_sample(seq, geo, s0, y_prev):
    sb = geo.sample_blk(SUB)
    spec = pl.BlockSpec((None, SUB, RWKV_WIDTH), lambda k: sb(k) + (0,))
    sspec = pl.BlockSpec((SUB, RWKV_HEADS, RWKV_HEAD, RWKV_HEAD), lambda k: (k, 0, 0, 0))
    return pl.pallas_call(
        functools.partial(_drop_arg, _wkv_step_kernel, 7),
        grid=(geo.ns // SUB,),
        in_specs=[spec] * 6 + [sspec, pl.BlockSpec(memory_space=pl.ANY)],
        out_specs=[spec, sspec],
        out_shape=[jax.ShapeDtypeStruct(y_prev.shape, F32), jax.ShapeDtypeStruct(s0.shape, F32)],
        input_output_aliases={7: 0},
        compiler_params=_cp("parallel"),
        name="wkv_step",
    )(*seq, s0, y_prev)


def _rwkv_post_kernel(y_ref, r_ref, k_ref, v_ref, g_ref, gng_ref, gnb_ref, rk_ref, ones_ref, o_ref):
    y = y_ref[...]
    ones = ones_ref[...]
    mu = _head_sum(y, ones) * (1.0 / RWKV_HEAD)
    yc = y - mu
    var = _head_sum(yc * yc, ones) * (1.0 / RWKV_HEAD)
    yn = yc * lax.rsqrt(var + RWKV_GN_EPS) * gng_ref[...] + gnb_ref[...]
    v = v_ref[...]
    bonus = _head_sum(r_ref[...] * k_ref[...] * rk_ref[...], ones) * v
    o_ref[...] = (yn + bonus) * g_ref[...]


def _rwkv_post(y, r, k, v, g, prm, li, tm):
    n = y.shape[0]
    W = RWKV_WIDTH
    hid = jnp.arange(W) // RWKV_HEAD
    ones_bd = (hid[:, None] == hid[None, :]).astype(F32)
    row = pl.BlockSpec((tm, W), lambda i: (i, 0))
    vec = pl.BlockSpec((1, W), lambda i: (0, 0))
    return pl.pallas_call(
        _rwkv_post_kernel,
        grid=(n // tm,),
        in_specs=[row] * 5 + [vec] * 3 + [pl.BlockSpec((W, W), lambda i: (0, 0))],
        out_specs=row,
        out_shape=jax.ShapeDtypeStruct((n, W), F32),
        compiler_params=_cp("parallel"),
        name="rwkv_post",
    )(y, r, k, v, g, prm['rwkv_gn_g'][li].reshape(1, -1), prm['rwkv_gn_b'][li].reshape(1, -1),
      prm['rwkv_r_k'][li].reshape(1, -1), ones_bd)


def _mla_prep_kernel(ckv_ref, qa_ref, kr_ref, cos_ref, sin_ref, qn_ref, kvn_ref, wqb_ref, wuk_ref,
                     qlat_ref, qrope_ref, c_ref, kr8_ref):
    qa = qa_ref[:, :Q_LORA]
    qa = qa * lax.rsqrt(jnp.mean(qa * qa, -1, keepdims=True) + 1e-6) * qn_ref[...]
    q = _bdot(qa, wqb_ref[...])
    nope_w = MLA_HEADS * QK_NOPE
    cos, sin = cos_ref[...], sin_ref[...]
    qrope_ref[...] = _rope(q[:, nope_w:], cos, sin, QK_ROPE // 2)
    qlat_ref[...] = _bdot(q[:, :nope_w], wuk_ref[...])
    ckv = ckv_ref[...]
    c_ref[...] = ckv * lax.rsqrt(jnp.mean(ckv * ckv, -1, keepdims=True) + 1e-6) * kvn_ref[...]
    kr8_ref[...] = _rope(kr_ref[...], cos, sin, QK_ROPE // 2)


def _mla_prep(p, geo, pos_slab, prm, li):
    n = p.shape[0]
    tm, tps = geo.tm, geo.tps
    rw = MLA_HEADS * QK_ROPE
    cos, sin = _rope_tables(pos_slab, QK_ROPE, MLA_HEADS)
    wqb = prm['mla_w_qb'][li].reshape(Q_LORA, MLA_HEADS, QK_NOPE + QK_ROPE)
    wqb = jnp.concatenate([wqb[:, :, :QK_NOPE].reshape(Q_LORA, -1), wqb[:, :, QK_NOPE:].reshape(Q_LORA, -1)], 1)
    eye = jnp.eye(MLA_HEADS, dtype=F32)
    wuk = jnp.einsum('chn,hg->hngc', prm['mla_w_uk'][li], eye).reshape(MLA_HEADS * QK_NOPE, MLA_HEADS * KV_LORA)
    fix = lambda i: (0, 0)
    c0 = RWKV_PROJ // KV_LORA
    q0 = (RWKV_PROJ + KV_LORA) // 512
    k0 = (RWKV_PROJ + KV_LORA + 512) // rw
    tab = pl.BlockSpec((tm, rw), lambda i: (i % tps, 0))
    return pl.pallas_call(
        _mla_prep_kernel,
        grid=(n // tm,),
        in_specs=[pl.BlockSpec((tm, KV_LORA), lambda i: (i, c0)), pl.BlockSpec((tm, 512), lambda i: (i, q0)),
                  pl.BlockSpec((tm, rw), lambda i: (i, k0)), tab, tab,
                  pl.BlockSpec((1, Q_LORA), fix), pl.BlockSpec((1, KV_LORA), fix),
                  pl.BlockSpec(wqb.shape, fix), pl.BlockSpec(wuk.shape, fix)],
        out_specs=[pl.BlockSpec((tm, MLA_HEADS * KV_LORA), lambda i: (i, 0)), pl.BlockSpec((tm, rw), lambda i: (i, 0)),
                   pl.BlockSpec((tm, KV_LORA), lambda i: (i, 0)), pl.BlockSpec((tm, rw), lambda i: (i, 0))],
        out_shape=[jax.ShapeDtypeStruct((n, MLA_HEADS * KV_LORA), F32), jax.ShapeDtypeStruct((n, rw), F32),
                   jax.ShapeDtypeStruct((n, KV_LORA), F32), jax.ShapeDtypeStruct((n, rw), F32)],
        compiler_params=_cp("parallel"),
        name="mla_prep",
    )(p, p, p, cos, sin, prm['mla_q_norm'][li].reshape(1, -1), prm['mla_kv_norm'][li].reshape(1, -1),
      wqb.astype(BF16), wuk.astype(BF16))


def _wuv_blockdiag(w_uv):
    eye = jnp.eye(MLA_HEADS, dtype=F32)
    return jnp.einsum('chv,hg->hcgv', w_uv, eye).reshape(MLA_HEADS * KV_LORA, MLA_WIDTH).astype(BF16)


def _mla_attn_kernel(qlat_ref, qrope_ref, c_ref, kr_ref, wuv_ref, o_ref, qs_scr, qrs_scr, m_scr, l_scr, acc_scr,
                     *, lq, padf):
    qi, ki = pl.program_id(1), pl.program_id(2)
    H = MLA_HEADS

    @pl.when(ki == 0)
    def _():
        ql = qlat_ref[...]
        qs_scr[...] = jnp.concatenate([ql[:, h * KV_LORA:(h + 1) * KV_LORA] for h in range(H)], axis=0).astype(BF16)
        qr = qrope_ref[...].astype(BF16)
        lane = lax.broadcasted_iota(jnp.int32, qr.shape, 1) // QK_ROPE
        qrs_scr[...] = jnp.concatenate([jnp.where(lane == h, qr, jnp.zeros_like(qr)) for h in range(H)], axis=0)
        m_scr[...] = jnp.full_like(m_scr, NEG)
        l_scr[...] = jnp.zeros_like(l_scr)
        acc_scr[...] = jnp.zeros_like(acc_scr)

    def tile(masked):
        cb = c_ref[...].astype(BF16)
        st = (_dg(cb, qs_scr[...], NT) + _dg(kr_ref[...].astype(BF16), qrs_scr[...], NT)) * MLA_SCALE
        if masked:
            kpos = ki * lq + lax.broadcasted_iota(jnp.int32, st.shape, 0)
            qpos = qi * lq + lax.broadcasted_iota(jnp.int32, st.shape, 1) % lq
            st = jnp.where(jnp.logical_and(kpos <= qpos, kpos >= padf), st, NEG)
        m_old = m_scr[...]
        m_new = jnp.maximum(m_old, jnp.max(st, axis=0, keepdims=True))
        alpha = jnp.exp(m_old - m_new)
        pt = jnp.exp(st - m_new)
        l_scr[...] = alpha * l_scr[...] + jnp.sum(pt, axis=0, keepdims=True)
        acc_scr[...] = alpha * acc_scr[...] + _dg(cb, pt.astype(BF16), TN)
        m_scr[...] = m_new

    assert padf <= lq
    edge = jnp.logical_or(ki == qi, ki == 0)
    pl.when(jnp.logical_and(ki <= qi, edge))(functools.partial(tile, True))
    pl.when(jnp.logical_and(ki < qi, jnp.logical_not(edge)))(functools.partial(tile, False))

    @pl.when(ki == pl.num_programs(2) - 1)
    def _():
        ot = (acc_scr[...] * (1.0 / l_scr[...])).astype(BF16)
        y = jnp.zeros(o_ref.shape, F32)
        for h in range(H):
            y = y + _dg(ot[:, h * lq:(h + 1) * lq], wuv_ref[h], TN)
        o_ref[...] = y


def _mla_attn_prompt(qlat3, qrope3, c3, kr3, w_uv, geo):
    lq = _div_tile(geo.tp, SEQ_TILE)
    nq = geo.tp // lq
    wuv = _wuv_blockdiag(w_uv).reshape(MLA_HEADS, KV_LORA, MLA_WIDTH)
    rw = MLA_HEADS * QK_ROPE
    qmap = lambda b, qi, ki: (b, qi, 0)
    kmap = lambda b, qi, ki: (b, jnp.minimum(ki, qi), 0)
    return pl.pallas_call(
        functools.partial(_mla_attn_kernel, lq=lq, padf=geo.padf),
        grid=(NSLAB, nq, nq),
        in_specs=[pl.BlockSpec((None, lq, MLA_HEADS * KV_LORA), qmap), pl.BlockSpec((None, lq, rw), qmap),
                  pl.BlockSpec((None, lq, KV_LORA), kmap), pl.BlockSpec((None, lq, rw), kmap),
                  pl.BlockSpec(wuv.shape, lambda b, qi, ki: (0, 0, 0))],
        out_specs=pl.BlockSpec((None, lq, MLA_WIDTH), qmap),
        out_shape=jax.ShapeDtypeStruct((NSLAB, geo.slab, MLA_WIDTH), F32),
        scratch_shapes=[pltpu.VMEM((MLA_HEADS * lq, KV_LORA), BF16), pltpu.VMEM((MLA_HEADS * lq, rw), BF16),
                        pltpu.VMEM((1, MLA_HEADS * lq), F32), pltpu.VMEM((1, MLA_HEADS * lq), F32),
                        pltpu.VMEM((KV_LORA, MLA_HEADS * lq), F32)],
        compiler_params=_cp("parallel", "parallel", "arbitrary"),
        name="mla_attn",
    )(qlat3, qrope3, c3, kr3, wuv)


def _mla_decode_kernel(pt_ref, qlat_ref, qrope_ref, cnew_ref, krnew_ref, kv_hbm, kr_hbm, tile_ref, wuv_ref,
                       yprev_ref, o_ref, kv_buf, kr_buf, kb_scr, s_scr, sem, *, li, n_pages, ch):
    del yprev_ref
    b = pl.program_id(0)
    slot = b % 2
    H = MLA_HEADS

    def page_copies(page, sl, pg):
        return (pltpu.make_async_copy(kv_hbm.at[page, li], kv_buf.at[sl, pg], sem.at[0, sl]),
                pltpu.make_async_copy(kr_hbm.at[page, li], kr_buf.at[sl, pg], sem.at[1, sl]))

    def start_fetch(bb, sl):
        def body(pg, carry):
            for cp in page_copies(pt_ref[bb, pg], sl, pg):
                cp.start()
            return carry
        lax.fori_loop(0, n_pages, body, 0)

    def wait_fetch(sl):
        def body(pg, carry):
            for cp in page_copies(0, sl, pg):
                cp.wait()
            return carry
        lax.fori_loop(0, n_pages, body, 0)

    @pl.when(b == 0)
    def _():
        start_fetch(0, 0)

    nxt = (b + 1) % pl.num_programs(0)

    j = b % SUB
    rid = lax.broadcasted_iota(jnp.int32, (SUB, 1), 0)

    def pick(ref):
        return jnp.sum(jnp.where(rid == j, ref[...], 0.0), axis=0, keepdims=True)

    ql = pick(qlat_ref)
    qs = jnp.concatenate([ql[:, h * KV_LORA:(h + 1) * KV_LORA] for h in range(H)], axis=0).astype(BF16)
    qr = jnp.broadcast_to(pick(qrope_ref), (H, H * QK_ROPE))
    hl = lax.broadcasted_iota(jnp.int32, qr.shape, 1) // QK_ROPE
    hr = lax.broadcasted_iota(jnp.int32, qr.shape, 0)
    qrs = jnp.where(hl == hr, qr, 0.0).astype(BF16)
    qr32 = _dg(qrs, tile_ref[...], NT).astype(BF16)
    cn = _rb(pick(cnew_ref))
    kn = _rb(pick(krnew_ref))
    s_self = (jnp.sum(qs.astype(F32) * cn, axis=1, keepdims=True)
              + jnp.sum(qrs.astype(F32) * kn, axis=1, keepdims=True)) * MLA_SCALE

    wait_fetch(slot)
    cw = ch * PAGE_SIZE
    for ci in range(n_pages // ch):
        for pg in range(ci * ch, (ci + 1) * ch):
            for cp in page_copies(pt_ref[nxt, pg], 1 - slot, pg):
                cp.start()
        kc = kv_buf[slot, ci * ch:(ci + 1) * ch].reshape(cw, KV_LORA).astype(BF16)
        kb_scr[ci * cw:(ci + 1) * cw, :] = kc
        krc = jnp.concatenate([kr_buf[slot, pg] for pg in range(ci * ch, (ci + 1) * ch)], axis=1).astype(BF16)
        s_scr[:, ci * cw:(ci + 1) * cw] = (_dg(qs, kc, NT) + _dg(qr32, krc, NN)) * MLA_SCALE
    s = s_scr[...]
    m = jnp.maximum(jnp.max(s, axis=1, keepdims=True), s_self)
    e = jnp.exp(s - m)
    e_self = jnp.exp(s_self - m)
    rl = 1.0 / (jnp.sum(e, axis=1, keepdims=True) + e_self)
    p = (e * rl).astype(BF16)
    acc = _rb(e_self * rl) * cn
    for ci in range(n_pages // ch):
        acc = acc + _dg(p[:, ci * cw:(ci + 1) * cw], kb_scr[ci * cw:(ci + 1) * cw, :], NN)
    ol = jnp.concatenate([acc[h:h + 1] for h in range(H)], axis=1)
    y = _bdot(jnp.broadcast_to(ol, (SUB, ol.shape[1])), wuv_ref[...])
    o_ref[pl.ds(j, 1), :] = y[0:1]

    @pl.when(b == pl.num_programs(0) - 1)
    def _():
        wait_fetch(1 - slot)


def _mla_decode(qlat, qrope, c, kr8, cache_kv, cache_kr, page_table, w_uv, li, y_prev, geo):
    n = qlat.shape[0]
    n_pages = page_table.shape[1]
    ch = _div_tile(n_pages, 8, 1)
    rw = MLA_HEADS * QK_ROPE
    wuv = _wuv_blockdiag(w_uv)
    tile = jnp.tile(jnp.eye(QK_ROPE, dtype=F32), (1, MLA_HEADS)).astype(BF16)

    def rowmap(b, pt):
        return ((b // geo.sps) * geo.slab + geo.tp + b % geo.sps) // SUB, 0

    rowblk = lambda w: pl.BlockSpec((SUB, w), rowmap)
    grid_spec = pltpu.PrefetchScalarGridSpec(
        num_scalar_prefetch=1,
        grid=(geo.ns,),
        in_specs=[rowblk(MLA_HEADS * KV_LORA), rowblk(rw), rowblk(KV_LORA), rowblk(rw),
                  pl.BlockSpec(memory_space=pl.ANY), pl.BlockSpec(memory_space=pl.ANY),
                  pl.BlockSpec(tile.shape, lambda b, pt: (0, 0)),
                  pl.BlockSpec(wuv.shape, lambda b, pt: (0, 0)),
                  pl.BlockSpec(memory_space=pl.ANY)],
        out_specs=rowblk(MLA_WIDTH),
        scratch_shapes=[pltpu.VMEM((2, n_pages, PAGE_SIZE, KV_LORA), F32),
                        pltpu.VMEM((2, n_pages, QK_ROPE, PAGE_SIZE), F32),
                        pltpu.VMEM((n_pages * PAGE_SIZE, KV_LORA), BF16),
                        pltpu.VMEM((MLA_HEADS, n_pages * PAGE_SIZE), F32),
                        pltpu.SemaphoreType.DMA((2, 2))],
    )
    return pl.pallas_call(
        functools.partial(_mla_decode_kernel, li=li, n_pages=n_pages, ch=ch),
        grid_spec=grid_spec,
        out_shape=jax.ShapeDtypeStruct((n, MLA_WIDTH), F32),
        input_output_aliases={9: 0},
        compiler_params=_cp("arbitrary"),
        name="mla_decode",
    )(page_table, qlat, qrope, c, kr8, cache_kv, jnp.swapaxes(cache_kr, 2, 3), tile, wuv, y_prev)


def _forward(x_prompt, x_sample, state_ret, state_s5_re, state_s5_im, state_wkv, state_shift, cache_kv,
             cache_krope, page_table, meta_tokens, prm):
    nb, seq, _ = x_prompt.shape
    assert nb == NSLAB and x_sample.shape[1] == 1
    ns = x_sample.shape[0]
    geo = _Geom(seq, ns, page_table.shape[1])
    n, tm, tp, padf, sps = geo.n, geo.tm, geo.tp, geo.padf, geo.sps

    meta = jnp.broadcast_to(meta_tokens[None], (nb, N_META, D_MODEL))
    x = jnp.concatenate([jnp.zeros((nb, padf, D_MODEL), F32), meta, x_prompt, x_sample.reshape(nb, sps, D_MODEL)], 1)
    x = x.reshape(n, D_MODEL)
    pos_slab = jnp.concatenate([jnp.arange(tp, dtype=jnp.int32) - padf, jnp.full((sps,), geo.past, jnp.int32)])

    ret_p, ret_s, s5_p, s5_s, wkv_p, wkv_s, sh_p, sh_s, kv_l, kr_l = [], [], [], [], [], [], [], [], [], []
    for layer in range(DEPTH):
        li = layer // 2
        if layer % 2 == 0:
            w_in = prm['ev_w_in'][li].astype(BF16)
            p = _matmul(x, w_in, tm, _div_tile(w_in.shape[1], 1024, 128))
            p3 = geo.v3(p)
            gng, gnb = prm['ret_gn_g'][li], prm['ret_gn_b'][li]
            ret, s_p = _retention_prompt(p3, geo, gng, gnb)
            ret, s_s = _retention_sample(p3, geo, state_ret[li], gng, gnb, ret)
            s5o, h_p = _s5_prompt(p3, geo, prm, li)
            h0 = jnp.concatenate([state_s5_re[li].reshape(ns, S5_CH), state_s5_im[li].reshape(ns, S5_CH)], 1)
            s5o, h_s = _s5_sample(p3, geo, h0, prm, li, s5o)
            ret_p.append(s_p)
            ret_s.append(s_s)
            s5_p.append(h_p)
            s5_s.append(h_s)
            x = _proj_ln(ret.reshape(n, -1), s5o.reshape(n, -1), x, prm['ev_w_out'][li], prm['ln1_g'][layer],
                         prm['ln1_b'][layer], tm)
        else:
            w = prm['od_w_in'][li]
            o1 = RWKV_PROJ
            o2 = o1 + Q_LORA
            o3 = o2 + KV_LORA
            w_in = jnp.concatenate([w[:, :o1], w[:, o2:o3], w[:, o1:o2], jnp.zeros((D_MODEL, 512 - Q_LORA), F32),
                                    jnp.tile(w[:, o3:], (1, MLA_HEADS))], 1).astype(BF16)
            p = _matmul(x, w_in, tm, _div_tile(w_in.shape[1], 1408, 128))
            p3 = geo.v3(p)
            sh_p.append(p3[:, tp - 1, :RWKV_PROJ])
            sh_s.append(p3[:, tp:, :RWKV_PROJ].reshape(ns, RWKV_PROJ))
            r, lw, k, v, a, b, g = _rwkv_prep(p3, geo, state_shift[li], prm, li)
            seqs = (r, lw, k, v, a, b)
            y, g_p = _wkv_prompt(seqs, geo)
            y, g_s = _wkv_sample(seqs, geo, state_wkv[li], y)
            wkv_p.append(g_p)
            wkv_s.append(g_s)
            flat = lambda t: t.reshape(n, t.shape[-1])
            y_c = _rwkv_post(flat(y), flat(r), flat(k), flat(v), flat(g), prm, li, tm)
            qlat, qrope, c, kr8 = _mla_prep(p, geo, pos_slab, prm, li)
            kv_l.append(geo.v3(c))
            kr_l.append(geo.v3(kr8)[:, :, :QK_ROPE])
            y_d = _mla_attn_prompt(geo.v3(qlat), geo.v3(qrope), geo.v3(c), geo.v3(kr8), prm['mla_w_uv'][li], geo)
            y_d = _mla_decode(qlat, qrope, c, kr8, cache_kv, cache_krope, page_table, prm['mla_w_uv'][li], li,
                              flat(y_d), geo)
            x = _proj_ln(y_c, y_d, x, prm['od_w_out'][li], prm['ln1_g'][layer], prm['ln1_b'][layer], tm)
        x = _moe_ln(x, prm, layer, tm)

    x3 = geo.v3(x)
    y_p = x3[:, padf + N_META:tp]
    y_s = x3[:, tp:].reshape(ns, 1, D_MODEL)

    def s5_state(hs, part):
        return jnp.stack([h[:, part * S5_CH:(part + 1) * S5_CH].reshape(-1, S5_GROUPS, S5_STATE) for h in hs])

    kv_p = jnp.stack([c[:, padf:tp] for c in kv_l], 1)
    kv_s = jnp.stack([c[:, tp:].reshape(ns, 1, KV_LORA) for c in kv_l], 1)
    kr_p = jnp.stack([c[:, padf:tp] for c in kr_l], 1)
    kr_s = jnp.stack([c[:, tp:].reshape(ns, 1, QK_ROPE) for c in kr_l], 1)
    return (y_p, y_s, jnp.stack(ret_p), jnp.stack(ret_s), s5_state(s5_p, 0), s5_state(s5_s, 0),
            s5_state(s5_p, 1), s5_state(s5_s, 1), jnp.stack(wkv_p), jnp.stack(wkv_s),
            jnp.stack(sh_p), jnp.stack(sh_s), kv_p, kv_s, kr_p, kr_s)


def kernel(x_prompt, x_sample, state_ret, state_s5_re, state_s5_im, state_wkv, state_shift, cache_kv, cache_krope, page_table, meta_tokens, ev_w_in, ret_gn_g, ret_gn_b, s5_a_re, s5_a_im, s5_log_dt, s5_b_re, s5_b_im, s5_c_re, s5_c_im, s5_d, s5_w_glu, s5_b_glu, ev_w_out, od_w_in, rwkv_mu, rwkv_w0, rwkv_w2, rwkv_a0, rwkv_a2, rwkv_g2, rwkv_k_k, rwkv_k_a, rwkv_r_k, rwkv_gn_g, rwkv_gn_b, mla_q_norm, mla_w_qb, mla_kv_norm, mla_w_uk, mla_w_uv, od_w_out, ln1_g, ln1_b, ln2_g, ln2_b, moe_w_coarse, moe_b_coarse, moe_w_fine, moe_b_fine, moe_w_gate, moe_w_up, moe_w_down):
    prm = dict(ev_w_in=ev_w_in, ret_gn_g=ret_gn_g, ret_gn_b=ret_gn_b, s5_a_re=s5_a_re, s5_a_im=s5_a_im,
               s5_log_dt=s5_log_dt, s5_b_re=s5_b_re, s5_b_im=s5_b_im, s5_c_re=s5_c_re, s5_c_im=s5_c_im,
               s5_d=s5_d, s5_w_glu=s5_w_glu, s5_b_glu=s5_b_glu, ev_w_out=ev_w_out,
               od_w_in=od_w_in, rwkv_mu=rwkv_mu, rwkv_w0=rwkv_w0, rwkv_w2=rwkv_w2, rwkv_a0=rwkv_a0,
               rwkv_a2=rwkv_a2, rwkv_g2=rwkv_g2, rwkv_k_k=rwkv_k_k, rwkv_k_a=rwkv_k_a, rwkv_r_k=rwkv_r_k,
               rwkv_gn_g=rwkv_gn_g, rwkv_gn_b=rwkv_gn_b, mla_q_norm=mla_q_norm, mla_w_qb=mla_w_qb,
               mla_kv_norm=mla_kv_norm, mla_w_uk=mla_w_uk, mla_w_uv=mla_w_uv, od_w_out=od_w_out,
               ln1_g=ln1_g, ln1_b=ln1_b, ln2_g=ln2_g, ln2_b=ln2_b, moe_w_coarse=moe_w_coarse,
               moe_b_coarse=moe_b_coarse, moe_w_fine=moe_w_fine, moe_b_fine=moe_b_fine,
               moe_w_gate=moe_w_gate, moe_w_up=moe_w_up, moe_w_down=moe_w_down)
    return _forward(x_prompt, x_sample, state_ret, state_s5_re, state_s5_im, state_wkv, state_shift, cache_kv,
                    cache_krope, page_table, meta_tokens, prm)
```

```python
import functools
import math

import jax
import jax.numpy as jnp
from jax import lax
from jax.experimental import pallas as pl
from jax.experimental.pallas import tpu as pltpu

F32 = jnp.float32
BF16 = jnp.bfloat16

D_MODEL = 1024
N_META = 16
PAGE_SIZE = 128
RET_HEADS, RET_DK, RET_DV = 4, 64, 128
S5_WIDTH, S5_GROUP, S5_STATE = 512, 16, 64
S5_GROUPS = S5_WIDTH // S5_GROUP
S5_CH = S5_GROUPS * S5_STATE
S5_GPB = 128 // S5_GROUP
S5_NBLK = S5_GROUPS // S5_GPB
S5_BCH = S5_GPB * S5_STATE
RWKV_HEAD, RWKV_WIDTH = 64, 512
RWKV_HEADS = RWKV_WIDTH // RWKV_HEAD
LORA_W, LORA_A, LORA_G = 64, 64, 128
RWKV_PROJ = 3 * RWKV_WIDTH + LORA_W + LORA_A + LORA_G
RWKV_GN_EPS = 64e-5
MLA_HEADS, Q_LORA, KV_LORA, QK_NOPE, QK_ROPE, V_HEAD = 8, 384, 256, 64, 32, 64
MLA_WIDTH = MLA_HEADS * V_HEAD
MLA_SCALE = (QK_NOPE + QK_ROPE) ** -0.5
ROPE_THETA = 10000.0
N_EGROUPS, E_PER_GROUP, D_EXPERT = 4, 4, 256
N_EXPERTS = N_EGROUPS * E_PER_GROUP
DEPTH = 4
DN_ALPHA = (2 * DEPTH) ** 0.25
NSLAB = 8
SUB = 8
LANES = 128

VMEM_LIMIT = 56 * 1024 * 1024
WKV_L = 32
WKV_HG = 4
WKV_NB = 8
ROW_TILE = 1056
SEQ_TILE = 416
S5_TILE = 80
NEG = -1e30

NT = (((1,), (1,)), ((), ()))
TN = (((0,), (0,)), ((), ()))
NN = (((1,), (0,)), ((), ()))


def _cp(*sem):
    return pltpu.CompilerParams(dimension_semantics=sem, vmem_limit_bytes=VMEM_LIMIT)


def _div_tile(n, cap, mult=8):
    best = None
    for d in range(mult, min(n, cap) + 1, mult):
        if n % d == 0:
            best = d
    assert best is not None, (n, cap, mult)
    return best


def _dg(a, b, dims):
    return lax.dot_general(a, b, dims, preferred_element_type=F32)


def _bdot(a, b, dims=NN):
    return _dg(a.astype(BF16), b.astype(BF16), dims)


def _dot3(a, b, dims=NN):
    ah = a.astype(BF16)
    al = (a - ah.astype(F32)).astype(BF16)
    bh = b.astype(BF16)
    bl = (b - bh.astype(F32)).astype(BF16)
    return _dg(ah, bh, dims) + (_dg(ah, bl, dims) + _dg(al, bh, dims))


def _dotc(a, b, dims=NN):
    return _bdot(a, b, dims)


def _rb(x):
    return x.astype(BF16).astype(F32)


def _layer_norm(z, g, b, eps=1e-5):
    mu = jnp.mean(z, -1, keepdims=True)
    zc = z - mu
    var = jnp.mean(zc * zc, -1, keepdims=True)
    return zc * lax.rsqrt(var + eps) * g + b


def _rope(x, cos, sin_signed, half):
    w = x.shape[-1]
    lane = lax.broadcasted_iota(jnp.int32, x.shape, 1)
    first = (lane % (2 * half)) < half
    other = jnp.where(first, pltpu.roll(x, w - half, 1), pltpu.roll(x, half, 1))
    return x * cos + other * sin_signed


def _rope_tables(pos, dim, reps):
    inv = ROPE_THETA ** (-jnp.arange(0, dim, 2, dtype=F32) / dim)
    ang = pos.astype(F32)[:, None] * inv[None, :]
    cos = jnp.cos(ang)
    sin = jnp.sin(ang)
    cos = jnp.tile(jnp.concatenate([cos, cos], -1), (1, reps))
    sin = jnp.tile(jnp.concatenate([-sin, sin], -1), (1, reps))
    return cos, sin


def _drop_arg(kern, idx, *refs):
    return kern(*(refs[:idx] + refs[idx + 1:]))


class _Geom:
    def __init__(self, seq, ns, n_pages):
        self.tq = seq + N_META
        self.padf = (-self.tq) % WKV_L
        self.tp = self.padf + self.tq
        assert ns % (NSLAB * SUB) == 0
        self.ns = ns
        self.sps = ns // NSLAB
        self.slab = self.tp + self.sps
        self.n = NSLAB * self.slab
        self.past = n_pages * PAGE_SIZE
        self.tm = _div_tile(self.slab, ROW_TILE)
        self.tps = self.slab // self.tm

    def v3(self, x):
        return x.reshape(NSLAB, self.slab, x.shape[-1])

    def sample_blk(self, rows):
        per = self.sps // rows
        return lambda k: (k // per, self.tp // rows + k % per)


def _mm_kernel(x_ref, w_ref, o_ref):
    o_ref[...] = _dg(x_ref[...].astype(BF16), w_ref[...], NN)


def _matmul(x, w, tm, tn):
    n, k = x.shape
    m = w.shape[1]
    return pl.pallas_call(
        _mm_kernel,
        grid=(n // tm, m // tn),
        in_specs=[pl.BlockSpec((tm, k), lambda i, j: (i, 0)), pl.BlockSpec((k, tn), lambda i, j: (0, j))],
        out_specs=pl.BlockSpec((tm, tn), lambda i, j: (i, j)),
        out_shape=jax.ShapeDtypeStruct((n, m), F32),
        compiler_params=_cp("parallel", "arbitrary"),
        name="proj_in",
    )(x, w)


def _proj_ln_kernel(a1_ref, a2_ref, x_ref, w1_ref, w2_ref, g_ref, b_ref, o_ref):
    mix = _dg(a1_ref[...].astype(BF16), w1_ref[...], NN) + _dg(a2_ref[...].astype(BF16), w2_ref[...], NN)
    z = DN_ALPHA * x_ref[...] + mix
    o_ref[...] = _layer_norm(z, g_ref[...], b_ref[...])


def _proj_ln(a1, a2, x, w_out, g, b, tm):
    n = x.shape[0]
    k1, k2 = a1.shape[1], a2.shape[1]
    w1 = w_out[:k1].astype(BF16)
    w2 = w_out[k1:].astype(BF16)
    row = lambda i: (i, 0)
    fix = lambda i: (0, 0)
    return pl.pallas_call(
        _proj_ln_kernel,
        grid=(n // tm,),
        in_specs=[pl.BlockSpec((tm, k1), row), pl.BlockSpec((tm, k2), row), pl.BlockSpec((tm, D_MODEL), row),
                  pl.BlockSpec((k1, D_MODEL), fix), pl.BlockSpec((k2, D_MODEL), fix),
                  pl.BlockSpec((1, D_MODEL), fix), pl.BlockSpec((1, D_MODEL), fix)],
        out_specs=pl.BlockSpec((tm, D_MODEL), row),
        out_shape=jax.ShapeDtypeStruct((n, D_MODEL), F32),
        compiler_params=_cp("parallel"),
        name="proj_out_ln",
    )(a1, a2, x, w1, w2, g.reshape(1, -1), b.reshape(1, -1))


MOE_EPS = 2
ROUTE0 = N_EGROUPS


def _route(lt):
    lane = lax.broadcasted_iota(jnp.int32, lt.shape, 1)
    lanef = lane.astype(F32)
    coarse = lane < N_EGROUPS
    lc = jnp.where(coarse, lt, NEG)
    m = jnp.max(lc, axis=1, keepdims=True)
    gsel = jnp.min(jnp.where(lc == m, lanef, 1e9), axis=1, keepdims=True)
    gate_c = 1.0 / jnp.sum(jnp.where(coarse, jnp.exp(lc - m), 0.0), axis=1, keepdims=True)
    grp = ((lane - ROUTE0) // E_PER_GROUP).astype(F32)
    lf = jnp.where(grp == gsel, lt, NEG)
    t1 = jnp.max(lf, axis=1, keepdims=True)
    i1 = jnp.min(jnp.where(lf == t1, lanef, 1e9), axis=1, keepdims=True)
    lf2 = jnp.where(lanef == i1, NEG, lf)
    t2 = jnp.max(lf2, axis=1, keepdims=True)
    i2 = jnp.min(jnp.where(lf2 == t2, lanef, 1e9), axis=1, keepdims=True)
    e2 = jnp.exp(t2 - t1)
    w1 = gate_c / (1.0 + e2)
    return jnp.where(lanef == i1, w1, 0.0) + jnp.where(lanef == i2, w1 * e2, 0.0)


def _moe_kernel(x_ref, wr_ref, br_ref, wgu_ref, wd_ref, g_ref, b_ref, o_ref, xb_scr, comb_scr, h_scr):
    e = pl.program_id(1)

    @pl.when(e == 0)
    def _():
        xb = x_ref[...].astype(BF16)
        xb_scr[...] = xb
        comb_scr[...] = _route(_dg(xb, wr_ref[...], NN) + br_ref[...])

    xb = xb_scr[...]
    comb = comb_scr[...]
    lane = lax.broadcasted_iota(jnp.int32, comb.shape, 1)
    gus = [_dg(xb, wgu_ref[q], NN) for q in range(MOE_EPS)]
    hb = []
    for q, gu in enumerate(gus):
        hg, hu = gu[:, :D_EXPERT], gu[:, D_EXPERT:]
        ce = jnp.sum(jnp.where(lane == e * MOE_EPS + q + ROUTE0, comb, 0.0), axis=1, keepdims=True)
        hb.append(((hg * jax.nn.sigmoid(hg)) * hu * ce).astype(BF16))
    for ee in range(N_EXPERTS // MOE_EPS):
        @pl.when(e == ee)
        def _():
            for q in range(MOE_EPS):
                x0 = (ee * MOE_EPS + q) * D_EXPERT
                h_scr[:, x0:x0 + D_EXPERT] = hb[q]

    @pl.when(e == pl.num_programs(1) - 1)
    def _():
        z = DN_ALPHA * x_ref[...] + _dg(h_scr[...], wd_ref[...], NN)
        o_ref[...] = _layer_norm(z, g_ref[...], b_ref[...])


def _moe_ln(x, prm, layer, tm):
    n = x.shape[0]
    wr = jnp.concatenate([prm['moe_w_coarse'][layer],
                          jnp.transpose(prm['moe_w_fine'][layer], (1, 0, 2)).reshape(D_MODEL, N_EXPERTS)], 1)
    wr = jnp.pad(wr, ((0, 0), (0, LANES - wr.shape[1]))).astype(BF16)
    br = jnp.pad(jnp.concatenate([prm['moe_b_coarse'][layer], prm['moe_b_fine'][layer].reshape(-1)]),
                 (0, LANES - N_EGROUPS - N_EXPERTS)).reshape(1, LANES)
    row = lambda i, e: (i, 0)
    fix = lambda i, e: (0, 0)
    exp3 = lambda i, e: (e, 0, 0)
    return pl.pallas_call(
        _moe_kernel,
        grid=(n // tm, N_EXPERTS // MOE_EPS),
        in_specs=[pl.BlockSpec((tm, D_MODEL), row), pl.BlockSpec((D_MODEL, LANES), fix), pl.BlockSpec((1, LANES), fix),
                  pl.BlockSpec((MOE_EPS, D_MODEL, 2 * D_EXPERT), exp3),
                  pl.BlockSpec((N_EXPERTS * D_EXPERT, D_MODEL), fix, pipeline_mode=pl.Buffered(1)),
                  pl.BlockSpec((1, D_MODEL), fix), pl.BlockSpec((1, D_MODEL), fix)],
        out_specs=pl.BlockSpec((tm, D_MODEL), row),
        out_shape=jax.ShapeDtypeStruct((n, D_MODEL), F32),
        scratch_shapes=[pltpu.VMEM((tm, D_MODEL), BF16), pltpu.VMEM((tm, LANES), F32),
                        pltpu.VMEM((tm, N_EXPERTS * D_EXPERT), BF16)],
        compiler_params=_cp("parallel", "arbitrary"),
        name="moe_ln",
    )(x, wr, br, jnp.concatenate([prm['moe_w_gate'][layer], prm['moe_w_up'][layer]], -1).astype(BF16),
      prm['moe_w_down'][layer].astype(BF16).reshape(N_EXPERTS * D_EXPERT, D_MODEL),
      prm['ln2_g'][layer].reshape(1, -1), prm['ln2_b'][layer].reshape(1, -1))


def _ret_log_g():
    return [math.log1p(-2.0 ** (-5.0 - h)) for h in range(RET_HEADS)]


def _ret_kernel(q_ref, k_ref, v_ref, gate_ref, cos_ref, sin_ref, dmask_ref, cdec_ref, kdec_ref, sdec_ref,
                gng_ref, gnb_ref, o_ref, sout_ref, s_scr, *, L, padf):
    c = pl.program_id(1)
    HK, HV = RET_HEADS * RET_DK, RET_HEADS * RET_DV

    @pl.when(c == 0)
    def _():
        s_scr[...] = jnp.zeros_like(s_scr)

    cos, sin = cos_ref[...], sin_ref[...]
    q = _rope(q_ref[...], cos, sin, RET_DK // 2) * RET_DK ** -0.5
    k = _rope(k_ref[...], cos, sin, RET_DK // 2)
    real = (lax.broadcasted_iota(jnp.int32, (L, 1), 0) + c * L) >= padf
    k = jnp.where(real, k, 0.0)
    v = jnp.where(real, v_ref[...], 0.0)
    gate = gate_ref[...]
    qb, kb, vb = q.astype(BF16), k.astype(BF16), v.astype(BF16)
    sbd = s_scr[...]
    cross = _dg(qb, sbd.astype(BF16), NN) * cdec_ref[...]
    lane = lax.broadcasted_iota(jnp.int32, (L, HK), 1)
    outs = []
    scs = [(_dg(jnp.where(lane // RET_DK == h, qb, jnp.zeros_like(qb)), kb, NT) * dmask_ref[h]).astype(BF16)
           for h in range(RET_HEADS)]
    for h in range(RET_HEADS):
        sl = slice(h * RET_DV, (h + 1) * RET_DV)
        o = _dg(scs[h], vb[:, sl], NN) + cross[:, sl]
        mu = jnp.mean(o, -1, keepdims=True)
        oc = o - mu
        var = jnp.mean(oc * oc, -1, keepdims=True)
        outs.append(oc * lax.rsqrt(var + 1e-5))
    on = jnp.concatenate(outs, axis=1) * gng_ref[...] + gnb_ref[...]
    o_ref[...] = (gate * jax.nn.sigmoid(gate)) * on
    upd = _dg((k * kdec_ref[...]).astype(BF16), vb, TN)
    ri = lax.broadcasted_iota(jnp.int32, (HK, HV), 0) // RET_DK
    ci = lax.broadcasted_iota(jnp.int32, (HK, HV), 1) // RET_DV
    s_new = sdec_ref[...] * sbd + jnp.where(ri == ci, upd, 0.0)
    s_scr[...] = s_new

    @pl.when(c == pl.num_programs(1) - 1)
    def _():
        for h in range(RET_HEADS):
            sout_ref[h] = s_new[h * RET_DK:(h + 1) * RET_DK, h * RET_DV:(h + 1) * RET_DV]


def _retention_prompt(p3, geo, gn_g, gn_b):
    HK, HV = RET_HEADS * RET_DK, RET_HEADS * RET_DV
    L = _div_tile(geo.tp, SEQ_TILE)
    log_g = jnp.asarray(_ret_log_g(), F32)
    idx = jnp.arange(L, dtype=F32)
    diff = idx[:, None] - idx[None, :]
    dmask = jnp.where(diff >= 0, jnp.exp(log_g[:, None, None] * jnp.maximum(diff, 0.0)), 0.0)
    cdec = jnp.repeat(jnp.exp(log_g[None, :] * (idx + 1.0)[:, None]), RET_DV, axis=1)
    kdec = jnp.repeat(jnp.exp(log_g[None, :] * (L - 1.0 - idx)[:, None]), RET_DK, axis=1)
    sdec = jnp.broadcast_to(jnp.repeat(jnp.exp(log_g * L), RET_DK)[:, None], (HK, HV))
    cos, sin = _rope_tables(jnp.arange(geo.tp, dtype=jnp.int32) - geo.padf, RET_DK, RET_HEADS)
    fix2 = lambda b, c: (0, 0)
    tab = pl.BlockSpec((L, HK), lambda b, c: (c, 0))
    return pl.pallas_call(
        functools.partial(_ret_kernel, L=L, padf=geo.padf),
        grid=(NSLAB, geo.tp // L),
        in_specs=[pl.BlockSpec((None, L, HK), lambda b, c: (b, c, 0)), pl.BlockSpec((None, L, HK), lambda b, c: (b, c, 1)),
                  pl.BlockSpec((None, L, HV), lambda b, c: (b, c, 1)), pl.BlockSpec((None, L, HV), lambda b, c: (b, c, 2)),
                  tab, tab, pl.BlockSpec((RET_HEADS, L, L), lambda b, c: (0, 0, 0)),
                  pl.BlockSpec((L, HV), fix2), pl.BlockSpec((L, HK), fix2), pl.BlockSpec((HK, HV), fix2),
                  pl.BlockSpec((1, HV), fix2), pl.BlockSpec((1, HV), fix2)],
        out_specs=[pl.BlockSpec((None, L, HV), lambda b, c: (b, c, 0)),
                   pl.BlockSpec((None, RET_HEADS, RET_DK, RET_DV), lambda b, c: (b, 0, 0, 0))],
        out_shape=[jax.ShapeDtypeStruct((NSLAB, geo.slab, HV), F32),
                   jax.ShapeDtypeStruct((NSLAB, RET_HEADS, RET_DK, RET_DV), F32)],
        scratch_shapes=[pltpu.VMEM((HK, HV), F32)],
        compiler_params=_cp("parallel", "arbitrary"),
        name="retention",
    )(p3, p3, p3, p3, cos, sin, dmask, cdec, kdec, sdec, gn_g.reshape(1, -1), gn_b.reshape(1, -1))


def _ret_step_kernel(q_ref, k_ref, v_ref, gate_ref, cos_ref, sin_ref, gng_ref, gnb_ref, s_ref, o_ref, sout_ref):
    cos, sin = cos_ref[...], sin_ref[...]
    qb = _rb(_rope(q_ref[...], cos, sin, RET_DK // 2) * RET_DK ** -0.5)
    kb = _rb(_rope(k_ref[...], cos, sin, RET_DK // 2))
    vb = _rb(v_ref[...])
    gate = gate_ref[...]
    ri = lax.broadcasted_iota(jnp.int32, (RET_DK, RET_DK), 0)
    ci = lax.broadcasted_iota(jnp.int32, (RET_DK, RET_DK), 1)
    eye = jnp.where(ri == ci, 1.0, 0.0)

    def col(x):
        return jnp.sum(eye * jnp.broadcast_to(x, (RET_DK, RET_DK)), axis=1, keepdims=True)

    for j in range(SUB):
        outs = []
        for h, lg in enumerate(_ret_log_g()):
            g = math.exp(lg)
            qh = qb[j:j + 1, h * RET_DK:(h + 1) * RET_DK]
            kh = kb[j:j + 1, h * RET_DK:(h + 1) * RET_DK]
            vh = vb[j:j + 1, h * RET_DV:(h + 1) * RET_DV]
            s0 = s_ref[j, h]
            qk = _rb(jnp.sum(qh * kh, axis=1, keepdims=True))
            o = qk * vh + g * jnp.sum(col(qh) * _rb(s0), axis=0, keepdims=True)
            sout_ref[j, h] = g * s0 + col(kh) * vh
            mu = jnp.mean(o, -1, keepdims=True)
            oc = o - mu
            var = jnp.mean(oc * oc, -1, keepdims=True)
            outs.append(oc * lax.rsqrt(var + 1e-5))
        on = jnp.concatenate(outs, axis=1) * gng_ref[...] + gnb_ref[...]
        gj = gate[j:j + 1]
        o_ref[j:j + 1, :] = (gj * jax.nn.sigmoid(gj)) * on


def _retention_sample(p3, geo, s0, gn_g, gn_b, out_prev):
    HK, HV = RET_HEADS * RET_DK, RET_HEADS * RET_DV
    cos, sin = _rope_tables(jnp.full((1,), geo.past, jnp.int32), RET_DK, RET_HEADS)
    sb = geo.sample_blk(SUB)
    fix = lambda k: (0, 0)
    st = pl.BlockSpec((SUB, RET_HEADS, RET_DK, RET_DV), lambda k: (k, 0, 0, 0))
    return pl.pallas_call(
        functools.partial(_drop_arg, _ret_step_kernel, 9),
        grid=(geo.ns // SUB,),
        in_specs=[pl.BlockSpec((None, SUB, HK), lambda k: sb(k) + (0,)), pl.BlockSpec((None, SUB, HK), lambda k: sb(k) + (1,)),
                  pl.BlockSpec((None, SUB, HV), lambda k: sb(k) + (1,)), pl.BlockSpec((None, SUB, HV), lambda k: sb(k) + (2,)),
                  pl.BlockSpec((1, HK), fix), pl.BlockSpec((1, HK), fix),
                  pl.BlockSpec((1, HV), fix), pl.BlockSpec((1, HV), fix), st, pl.BlockSpec(memory_space=pl.ANY)],
        out_specs=[pl.BlockSpec((None, SUB, HV), lambda k: sb(k) + (0,)), st],
        out_shape=[jax.ShapeDtypeStruct(out_prev.shape, F32), jax.ShapeDtypeStruct(s0.shape, F32)],
        input_output_aliases={9: 0},
        compiler_params=_cp("parallel"),
        name="retention_step",
    )(p3, p3, p3, p3, cos, sin, gn_g.reshape(1, -1), gn_b.reshape(1, -1), s0, out_prev)


def _s5_params(prm, li):
    f32 = F32
    a_re, a_im = prm['s5_a_re'][li].astype(f32), prm['s5_a_im'][li].astype(f32)
    dt = jnp.exp(prm['s5_log_dt'][li].astype(f32))[:, None]
    mag = jnp.exp(dt * a_re)
    ab_re, ab_im = mag * jnp.cos(dt * a_im), mag * jnp.sin(dt * a_im)
    den = a_re * a_re + a_im * a_im
    n_re = ab_re - 1.0
    f_re = (n_re * a_re + ab_im * a_im) / den
    f_im = (ab_im * a_re - n_re * a_im) / den
    b_re, b_im = prm['s5_b_re'][li].astype(f32), prm['s5_b_im'][li].astype(f32)
    bb_re = f_re[..., None] * b_re - f_im[..., None] * b_im
    bb_im = f_re[..., None] * b_im + f_im[..., None] * b_re
    eye = jnp.eye(S5_GPB, dtype=f32)

    def block_in(w):
        w = w.reshape(S5_NBLK, S5_GPB, S5_STATE, S5_GROUP)
        return jnp.einsum('bgpc,gh->bgchp', w, eye).reshape(S5_NBLK, LANES, S5_BCH)

    def block_out(w):
        w = w.reshape(S5_NBLK, S5_GPB, S5_GROUP, S5_STATE)
        return jnp.einsum('bgcp,gh->bgphc', w, eye).reshape(S5_NBLK, S5_BCH, LANES)

    bbw = jnp.concatenate([block_in(bb_re), block_in(bb_im)], 2).astype(BF16)
    ccw = jnp.concatenate([block_out(prm['s5_c_re'][li]), -block_out(prm['s5_c_im'][li])], 1).astype(BF16)
    lam = jnp.concatenate([ab_re.reshape(1, -1), ab_im.reshape(1, -1)], 1)
    return [bbw, ccw, lam, prm['s5_d'][li].reshape(1, -1), prm['s5_w_glu'][li].astype(BF16),
            prm['s5_b_glu'][li].reshape(1, -1)]


def _s5_specs(fix):
    fix3 = lambda *a: (0, 0, 0)
    return [pl.BlockSpec((S5_NBLK, LANES, 2 * S5_BCH), fix3), pl.BlockSpec((S5_NBLK, 2 * S5_BCH, LANES), fix3),
            pl.BlockSpec((1, 2 * S5_CH), fix), pl.BlockSpec((1, S5_WIDTH), fix),
            pl.BlockSpec((S5_WIDTH, S5_WIDTH), fix), pl.BlockSpec((1, S5_WIDTH), fix)]


def _s5_gate(y, u, d_ref, wglu_ref, bglu_ref):
    y = y + d_ref[...] * u
    z = 0.5 * y * (1.0 + jnp.tanh(math.sqrt(2.0 / math.pi) * (y + 0.044715 * (y * y * y))))
    return z * jax.nn.sigmoid(_dg(z.astype(BF16), wglu_ref[...], NN) + bglu_ref[...])


S5_NJ = 2 * S5_CH // LANES
S5_PAIRS = 8


def _s5_prompt_kernel(u_ref, bb_ref, cc_ref, lam_ref, d_ref, wglu_ref, bglu_ref, o_ref, hout_ref, sc, h_scr,
                      *, lt, pitch, padf):
    i = pl.program_id(0)

    @pl.when(i == 0)
    def _():
        h_scr[...] = jnp.zeros_like(h_scr)

    real = (lax.broadcasted_iota(jnp.int32, (lt, 1), 0) + i * lt) >= padf
    half = S5_NJ // 2
    spb = S5_BCH // LANES
    ub = [jnp.where(real, u_ref[b], 0.0).astype(BF16) for b in range(NSLAB)]
    for k in range(S5_NBLK):
        for b in range(NSLAB):
            bu = _dg(ub[b][:, k * LANES:(k + 1) * LANES], bb_ref[k], NN)
            for q in range(spb):
                sc[k * spb + q, b * pitch:b * pitch + lt, :] = bu[:, q * LANES:(q + 1) * LANES]
                sc[half + k * spb + q, b * pitch:b * pitch + lt, :] = bu[:, S5_BCH + q * LANES:S5_BCH + (q + 1) * LANES]
    for j0 in range(0, half, S5_PAIRS):
        js = list(range(j0, j0 + S5_PAIRS))
        lr = [jnp.broadcast_to(lam_ref[:, j * LANES:(j + 1) * LANES], (NSLAB, LANES)) for j in js]
        li = [jnp.broadcast_to(lam_ref[:, S5_CH + j * LANES:S5_CH + (j + 1) * LANES], (NSLAB, LANES)) for j in js]

        def step(t, carry):
            hr, hi = carry
            rows = pl.ds(t, NSLAB, stride=pitch)
            nr, ni = [], []
            for q, j in enumerate(js):
                r = lr[q] * hr[q] - li[q] * hi[q] + sc[j, rows, :]
                m = lr[q] * hi[q] + li[q] * hr[q] + sc[half + j, rows, :]
                sc[j, rows, :] = r
                sc[half + j, rows, :] = m
                nr.append(r)
                ni.append(m)
            return tuple(nr), tuple(ni)

        init = (tuple(h_scr[:, j * LANES:(j + 1) * LANES] for j in js),
                tuple(h_scr[:, S5_CH + j * LANES:S5_CH + (j + 1) * LANES] for j in js))
        hr, hi = lax.fori_loop(0, lt, step, init, unroll=2)
        for q, j in enumerate(js):
            h_scr[:, j * LANES:(j + 1) * LANES] = hr[q]
            h_scr[:, S5_CH + j * LANES:S5_CH + (j + 1) * LANES] = hi[q]
    ys = [[] for _ in range(NSLAB)]
    for k in range(S5_NBLK):
        for b in range(NSLAB):
            rows = slice(b * pitch, b * pitch + lt)
            hk = jnp.concatenate([sc[k * spb + q, rows, :] for q in range(spb)]
                                 + [sc[half + k * spb + q, rows, :] for q in range(spb)], axis=1)
            ys[b].append(_dg(hk.astype(BF16), cc_ref[k], NN))
    for b in range(NSLAB):
        o_ref[b] = _s5_gate(jnp.concatenate(ys[b], axis=1), u_ref[b], d_ref, wglu_ref, bglu_ref)

    @pl.when(i == pl.num_programs(0) - 1)
    def _():
        hout_ref[...] = h_scr[...]


def _s5_prompt(p3, geo, prm, li):
    lt = _div_tile(geo.tp, S5_TILE, 16)
    pitch = lt + SUB
    ucol = p3.shape[-1] // S5_WIDTH - 1
    fix = lambda i: (0, 0)
    return pl.pallas_call(
        functools.partial(_s5_prompt_kernel, lt=lt, pitch=pitch, padf=geo.padf),
        grid=(geo.tp // lt,),
        in_specs=[pl.BlockSpec((NSLAB, lt, S5_WIDTH), lambda i: (0, i, ucol))] + _s5_specs(fix),
        out_specs=[pl.BlockSpec((NSLAB, lt, S5_WIDTH), lambda i: (0, i, 0)), pl.BlockSpec((NSLAB, 2 * S5_CH), fix)],
        out_shape=[jax.ShapeDtypeStruct((NSLAB, geo.slab, S5_WIDTH), F32), jax.ShapeDtypeStruct((NSLAB, 2 * S5_CH), F32)],
        scratch_shapes=[pltpu.VMEM((S5_NJ, NSLAB * pitch, LANES), F32), pltpu.VMEM((NSLAB, 2 * S5_CH), F32)],
        compiler_params=_cp("arbitrary"),
        name="s5",
    )(p3, *_s5_params(prm, li))


def _s5_step_kernel(u_ref, bb_ref, cc_ref, lam_ref, d_ref, wglu_ref, bglu_ref, h0_ref, o_ref, hout_ref):
    u = u_ref[...]
    ub = u.astype(BF16)
    bu = [_dg(ub[:, k * LANES:(k + 1) * LANES], bb_ref[k], NN) for k in range(S5_NBLK)]
    bu_re = jnp.concatenate([x[:, :S5_BCH] for x in bu], axis=1)
    bu_im = jnp.concatenate([x[:, S5_BCH:] for x in bu], axis=1)
    h0 = h0_ref[...]
    lr, li = lam_ref[:, :S5_CH], lam_ref[:, S5_CH:]
    hr, hi = h0[:, :S5_CH], h0[:, S5_CH:]
    nr = lr * hr - li * hi + bu_re
    ni = lr * hi + li * hr + bu_im
    hout_ref[...] = jnp.concatenate([nr, ni], axis=1)
    ys = []
    for k in range(S5_NBLK):
        sl = slice(k * S5_BCH, (k + 1) * S5_BCH)
        ys.append(_dg(jnp.concatenate([nr[:, sl], ni[:, sl]], axis=1).astype(BF16), cc_ref[k], NN))
    o_ref[...] = _s5_gate(jnp.concatenate(ys, axis=1), u, d_ref, wglu_ref, bglu_ref)


def _s5_sample(p3, geo, h0, prm, li, out_prev):
    rows = geo.sps
    sb = geo.sample_blk(rows)
    ucol = p3.shape[-1] // S5_WIDTH - 1
    fix = lambda k: (0, 0)
    st = pl.BlockSpec((rows, 2 * S5_CH), lambda k: (k, 0))
    return pl.pallas_call(
        functools.partial(_drop_arg, _s5_step_kernel, 8),
        grid=(NSLAB,),
        in_specs=[pl.BlockSpec((None, rows, S5_WIDTH), lambda k: sb(k) + (ucol,))] + _s5_specs(fix)
        + [st, pl.BlockSpec(memory_space=pl.ANY)],
        out_specs=[pl.BlockSpec((None, rows, S5_WIDTH), lambda k: sb(k) + (0,)), st],
        out_shape=[jax.ShapeDtypeStruct(out_prev.shape, F32), jax.ShapeDtypeStruct(h0.shape, F32)],
        input_output_aliases={8: 0},
        compiler_params=_cp("parallel"),
        name="s5_step",
    )(p3, *_s5_params(prm, li), h0, out_prev)


def _head_sum(x, ones_bd):
    ones = ones_bd.astype(BF16)
    hi = x.astype(BF16)
    lo = (x - hi.astype(F32)).astype(BF16)
    return _dg(hi, ones, NN) + _dg(lo, ones, NN)


def _rwkv_prep_body(pr, prev, mu_ref, w0_ref, w2_ref, a0_ref, a2_ref, g2_ref, kk_ref, ka_ref, ones_ref,
                    r_ref, lw_ref, k_ref, v_ref, a_ref, b_ref, g_ref):
    W = RWKV_WIDTH
    xm = pr + (prev - pr) * mu_ref[...]
    r = xm[:, :W]
    k = xm[:, W:2 * W]
    v = xm[:, 2 * W:3 * W]
    xwa = xm[:, 3 * W:3 * W + LORA_W + LORA_A]
    xg = xm[:, 3 * W + LORA_W + LORA_A:]
    wl = w0_ref[...] + _bdot(jnp.tanh(xwa), w2_ref[...])
    sp = jnp.maximum(-wl, 0.0) + jnp.log(1.0 + jnp.exp(-jnp.abs(wl)))
    w = -sp - 0.5
    lw = -jnp.exp(w)
    a = jax.nn.sigmoid(a0_ref[...] + _bdot(xwa, a2_ref[...]))
    g = _bdot(jax.nn.sigmoid(xg), g2_ref[...])
    kk = k * kk_ref[...]
    nrm = jnp.sqrt(_head_sum(kk * kk, ones_ref[...]))
    kk = kk / jnp.maximum(nrm, 1e-12)
    k2 = k * (1.0 + (a - 1.0) * ka_ref[...])
    r_ref[...] = r
    lw_ref[...] = lw
    k_ref[...] = k2
    v_ref[...] = v
    a_ref[...] = -kk
    b_ref[...] = kk * a
    g_ref[...] = g


def _rwkv_prep_prompt_kernel(pr_ref, prev8_ref, *rest, tm, padf):
    pr = pr_ref[...]
    t = lax.broadcasted_iota(jnp.int32, (tm, 1), 0)
    last = jnp.broadcast_to(prev8_ref[SUB - 1:SUB, :], pr.shape)
    prev = jnp.where(t == 0, last, pltpu.roll(pr, 1, 0))
    first = (t + pl.program_id(1) * tm) == padf
    prev = jnp.where(first, 0.0, prev)
    _rwkv_prep_body(pr, prev, *rest)


def _rwkv_prep_sample_kernel(pr_ref, prev_ref, *rest):
    _rwkv_prep_body(pr_ref[...], prev_ref[...], *rest[:9], *rest[9 + 7:])


def _rwkv_prep(p3, geo, shift_s, prm, li):
    W = RWKV_WIDTH
    zeros64 = jnp.zeros((LORA_W, W), F32)
    w2p = jnp.concatenate([prm['rwkv_w2'][li], zeros64], 0).astype(BF16)
    a2p = jnp.concatenate([zeros64, prm['rwkv_a2'][li]], 0).astype(BF16)
    hid = jnp.arange(W) // RWKV_HEAD
    ones_bd = (hid[:, None] == hid[None, :]).astype(F32)
    consts = [prm['rwkv_mu'][li].reshape(1, -1), prm['rwkv_w0'][li].reshape(1, -1), w2p,
              prm['rwkv_a0'][li].reshape(1, -1), a2p, prm['rwkv_g2'][li].astype(BF16),
              prm['rwkv_k_k'][li].reshape(1, -1), prm['rwkv_k_a'][li].reshape(1, -1), ones_bd]
    outs_shape = [jax.ShapeDtypeStruct((NSLAB, geo.slab, W), F32)] * 7
    tm = _div_tile(geo.tp, SEQ_TILE)
    outs = pl.pallas_call(
        functools.partial(_rwkv_prep_prompt_kernel, tm=tm, padf=geo.padf),
        grid=(NSLAB, geo.tp // tm),
        in_specs=[pl.BlockSpec((None, tm, RWKV_PROJ), lambda b, c: (b, c, 0)),
                  pl.BlockSpec((None, SUB, RWKV_PROJ), lambda b, c: (b, jnp.maximum(c * (tm // SUB) - 1, 0), 0))]
        + [pl.BlockSpec(x.shape, lambda b, c: (0, 0)) for x in consts],
        out_specs=[pl.BlockSpec((None, tm, W), lambda b, c: (b, c, 0))] * 7,
        out_shape=outs_shape,
        compiler_params=_cp("parallel", "parallel"),
        name="rwkv_prep_prompt",
    )(p3, p3, *consts)
    rows = geo.sps
    sb = geo.sample_blk(rows)
    nin = 2 + len(consts)
    outs = pl.pallas_call(
        _rwkv_prep_sample_kernel,
        grid=(NSLAB,),
        in_specs=[pl.BlockSpec((None, rows, RWKV_PROJ), lambda k: sb(k) + (0,)),
                  pl.BlockSpec((rows, RWKV_PROJ), lambda k: (k, 0))]
        + [pl.BlockSpec(x.shape, lambda k: (0, 0)) for x in consts] + [pl.BlockSpec(memory_space=pl.ANY)] * 7,
        out_specs=[pl.BlockSpec((None, rows, W), lambda k: sb(k) + (0,))] * 7,
        out_shape=outs_shape,
        input_output_aliases={nin + j: j for j in range(7)},
        compiler_params=_cp("parallel"),
        name="rwkv_prep_sample",
    )(p3, shift_s, *consts, *outs)
    return outs


def _wkv_chunk_kernel(r_ref, lw_ref, k_ref, v_ref, a_ref, b_ref, y_ref, gout_ref, g_scr, *, padf):
    c = pl.program_id(1)
    L, HG = WKV_L, WKV_HG
    NG = RWKV_HEADS // HG
    WD = HG * RWKV_HEAD
    M = HG * L

    @pl.when(c == 0)
    def _():
        g_scr[...] = jnp.zeros_like(g_scr)

    real = (lax.broadcasted_iota(jnp.int32, (L, 1), 0) + c * L) >= padf

    def ld(ref):
        return jnp.where(real, jnp.concatenate([ref[i] for i in range(WKV_NB)], axis=1), 0.0)

    r, lw, k, v, a, b = ld(r_ref), ld(lw_ref), ld(k_ref), ld(v_ref), ld(a_ref), ld(b_ref)
    ti = lax.broadcasted_iota(jnp.int32, (L, L), 0)
    tj = lax.broadcasted_iota(jnp.int32, (L, L), 1)
    tri = jnp.where(ti >= tj, 1.0, 0.0)
    cum = _dot3(tri, lw)
    dec = jnp.exp(cum)
    inv = jnp.exp(-cum)
    at = a * jnp.exp(cum - lw)
    bt = b * inv
    kt = k * inv
    rt = r * dec
    rh = lax.broadcasted_iota(jnp.int32, (M, WD), 0) // L
    lh = lax.broadcasted_iota(jnp.int32, (M, WD), 1) // RWKV_HEAD
    bd = rh == lh
    mi = lax.broadcasted_iota(jnp.int32, (M, M), 0)
    mj = lax.broadcasted_iota(jnp.int32, (M, M), 1)
    same = (mi // L) == (mj // L)
    m_strict = jnp.logical_and(same, (mj % L) < (mi % L))
    m_incl = jnp.logical_and(same, (mj % L) <= (mi % L))
    eye = jnp.where(mi == mj, 1.0, 0.0)

    def stack(x):
        return jnp.where(bd, jnp.concatenate([x] * HG, axis=0), 0.0)

    chains = range(WKV_NB * NG)
    sls = [slice(i * WD, (i + 1) * WD) for i in chains]
    a_s = [stack(at[:, sl]).astype(BF16) for sl in sls]
    r_s = [stack(rt[:, sl]).astype(BF16) for sl in sls]
    v_s = [stack(v[:, sl]).astype(BF16) for sl in sls]
    bk_s = [jnp.concatenate([stack(bt[:, sl]), stack(kt[:, sl])], axis=0).astype(BF16) for sl in sls]
    g = [g_scr[i] for i in chains]
    gb = [x.astype(BF16) for x in g]
    prod = [_dg(jnp.concatenate([a_s[i], r_s[i]], axis=0), bk_s[i], NT) for i in chains]
    ab = [jnp.where(m_strict, p[:M, :M], 0.0) for p in prod]
    ak = [jnp.where(m_strict, p[:M, M:], 0.0).astype(BF16) for p in prod]
    rb = [jnp.where(m_incl, p[M:, :M], 0.0).astype(BF16) for p in prod]
    rk = [jnp.where(m_incl, p[M:, M:], 0.0).astype(BF16) for p in prod]
    tinv = [eye + x for x in ab]
    pw = ab
    for _ in range(int(math.log2(L)) - 1):
        pw = [_dotc(x, x) for x in pw]
        tinv = [t + _dotc(t, x) for t, x in zip(tinv, pw)]
    z = [_dg(a_s[i], gb[i], NT) + _dg(ak[i], v_s[i], NN) for i in chains]
    u = [_dotc(tinv[i], z[i]).astype(BF16) for i in chains]
    ybd = [_dg(r_s[i], gb[i], NT) + _dg(rb[i], u[i], NN) + _dg(rk[i], v_s[i], NN) for i in chains]
    upd = [_dg(jnp.concatenate([u[i], v_s[i]], axis=0), bk_s[i], TN) for i in chains]
    g_last = []
    for i in chains:
        y = ybd[i][0:L]
        for h in range(1, HG):
            y = y + ybd[i][h * L:(h + 1) * L]
        y_ref[i // NG, :, (i % NG) * WD:(i % NG + 1) * WD] = y
        g_new = (g[i] + upd[i]) * dec[L - 1:L, sls[i]]
        g_scr[i] = g_new
        g_last.append(g_new)

    @pl.when(c == pl.num_programs(1) - 1)
    def _():
        for grp, g_new in enumerate(g_last):
            gout_ref[grp // NG, grp % NG] = g_new


def _wkv_prompt(seq, geo):
    L = WKV_L
    NG = RWKV_HEADS // WKV_HG
    WD = WKV_HG * RWKV_HEAD
    spec = pl.BlockSpec((WKV_NB, L, RWKV_WIDTH), lambda b, c: (b, c, 0))
    y, g = pl.pallas_call(
        functools.partial(_wkv_chunk_kernel, padf=geo.padf),
        grid=(NSLAB // WKV_NB, geo.tp // L),
        in_specs=[spec] * 6,
        out_specs=[spec, pl.BlockSpec((WKV_NB, NG, WD, WD), lambda b, c: (b, 0, 0, 0))],
        out_shape=[jax.ShapeDtypeStruct((NSLAB, geo.slab, RWKV_WIDTH), F32),
                   jax.ShapeDtypeStruct((NSLAB, NG, WD, WD), F32)],
        scratch_shapes=[pltpu.VMEM((WKV_NB * NG, WD, WD), F32)],
        compiler_params=_cp("parallel", "arbitrary"),
        name="wkv_chunk",
    )(*seq)
    g = g.reshape(NSLAB, NG, WKV_HG, RWKV_HEAD, WKV_HG, RWKV_HEAD)
    idx = jnp.arange(WKV_HG)
    s = g[:, :, idx, :, idx, :]
    s = jnp.transpose(s, (1, 2, 0, 3, 4)).reshape(NSLAB, RWKV_HEADS, RWKV_HEAD, RWKV_HEAD)
    return y, s


def _wkv_step_kernel(r_ref, lw_ref, k_ref, v_ref, a_ref, b_ref, s_ref, y_ref, sout_ref):
    H, D = RWKV_HEADS, RWKV_HEAD
    ri = lax.broadcasted_iota(jnp.int32, (H * D, D), 0) % D
    ci = lax.broadcasted_iota(jnp.int32, (H * D, D), 1)
    eye = jnp.where(ri == ci, 1.0, 0.0)

    def expand(x):
        return jnp.concatenate([jnp.broadcast_to(x[:, h * D:(h + 1) * D], (D, D)) for h in range(H)], axis=0)

    for j in range(SUB):
        row = lambda ref: ref[j:j + 1, :]
        s = s_ref[j].reshape(H * D, D)
        w_e = jnp.exp(expand(row(lw_ref)))
        v_col = jnp.sum(expand(row(v_ref)) * eye, axis=1, keepdims=True)
        sa = jnp.sum(s * expand(row(a_ref)), axis=1, keepdims=True)
        s_new = s * w_e + sa * expand(row(b_ref)) + v_col * expand(row(k_ref))
        y_col = jnp.sum(s_new * expand(row(r_ref)), axis=1, keepdims=True)
        ye = y_col * eye
        y_row = jnp.concatenate([jnp.sum(ye[h * D:(h + 1) * D], axis=0, keepdims=True) for h in range(H)], axis=1)
        y_ref[j:j + 1, :] = y_row
        sout_ref[j] = s_new.reshape(H, D, D)


def _wkv_sample(seq, geo, s0, y_prev):
    sb = geo.sample_blk(SUB)
    spec = pl.BlockSpec((None, SUB, RWKV_WIDTH), lambda k: sb(k) + (0,))
    sspec = pl.BlockSpec((SUB, RWKV_HEADS, RWKV_HEAD, RWKV_HEAD), lambda k: (k, 0, 0, 0))
    return pl.pallas_call(
        functools.partial(_drop_arg, _wkv_step_kernel, 7),
        grid=(geo.ns // SUB,),
        in_specs=[spec] * 6 + [sspec, pl.BlockSpec(memory_space=pl.ANY)],
        out_specs=[spec, sspec],
        out_shape=[jax.ShapeDtypeStruct(y_prev.shape, F32), jax.ShapeDtypeStruct(s0.shape, F32)],
        input_output_aliases={7: 0},
        compiler_params=_cp("parallel"),
        name="wkv_step",
    )(*seq, s0, y_prev)


def _rwkv_post_kernel(y_ref, r_ref, k_ref, v_ref, g_ref, gng_ref, gnb_ref, rk_ref, ones_ref, o_ref):
    y = y_ref[...]
    ones = ones_ref[...]
    mu = _head_sum(y, ones) * (1.0 / RWKV_HEAD)
    yc = y - mu
    var = _head_sum(yc * yc, ones) * (1.0 / RWKV_HEAD)
    yn = yc * lax.rsqrt(var + RWKV_GN_EPS) * gng_ref[...] + gnb_ref[...]
    v = v_ref[...]
    bonus = _head_sum(r_ref[...] * k_ref[...] * rk_ref[...], ones) * v
    o_ref[...] = (yn + bonus) * g_ref[...]


def _rwkv_post(y, r, k, v, g, prm, li, tm):
    n = y.shape[0]
    W = RWKV_WIDTH
    hid = jnp.arange(W) // RWKV_HEAD
    ones_bd = (hid[:, None] == hid[None, :]).astype(F32)
    row = pl.BlockSpec((tm, W), lambda i: (i, 0))
    vec = pl.BlockSpec((1, W), lambda i: (0, 0))
    return pl.pallas_call(
        _rwkv_post_kernel,
        grid=(n // tm,),
        in_specs=[row] * 5 + [vec] * 3 + [pl.BlockSpec((W, W), lambda i: (0, 0))],
        out_specs=row,
        out_shape=jax.ShapeDtypeStruct((n, W), F32),
        compiler_params=_cp("parallel"),
        name="rwkv_post",
    )(y, r, k, v, g, prm['rwkv_gn_g'][li].reshape(1, -1), prm['rwkv_gn_b'][li].reshape(1, -1),
      prm['rwkv_r_k'][li].reshape(1, -1), ones_bd)


def _mla_prep_kernel(ckv_ref, qa_ref, kr_ref, cos_ref, sin_ref, qn_ref, kvn_ref, wqb_ref, wuk_ref,
                     qlat_ref, qrope_ref, c_ref, kr8_ref):
    qa = qa_ref[:, :Q_LORA]
    qa = qa * lax.rsqrt(jnp.mean(qa * qa, -1, keepdims=True) + 1e-6) * qn_ref[...]
    q = _bdot(qa, wqb_ref[...])
    nope_w = MLA_HEADS * QK_NOPE
    cos, sin = cos_ref[...], sin_ref[...]
    qrope_ref[...] = _rope(q[:, nope_w:], cos, sin, QK_ROPE // 2)
    hp = LANES // QK_NOPE
    for k in range(MLA_HEADS // hp):
        qlat_ref[:, k * hp * KV_LORA:(k + 1) * hp * KV_LORA] = _bdot(q[:, k * LANES:(k + 1) * LANES], wuk_ref[k])
    ckv = ckv_ref[...]
    c_ref[...] = ckv * lax.rsqrt(jnp.mean(ckv * ckv, -1, keepdims=True) + 1e-6) * kvn_ref[...]
    kr8_ref[...] = _rope(kr_ref[...], cos, sin, QK_ROPE // 2)


def _mla_prep(p, geo, pos_slab, prm, li):
    n = p.shape[0]
    tm, tps = geo.tm, geo.tps
    rw = MLA_HEADS * QK_ROPE
    cos, sin = _rope_tables(pos_slab, QK_ROPE, MLA_HEADS)
    wqb = prm['mla_w_qb'][li].reshape(Q_LORA, MLA_HEADS, QK_NOPE + QK_ROPE)
    wqb = jnp.concatenate([wqb[:, :, :QK_NOPE].reshape(Q_LORA, -1), wqb[:, :, QK_NOPE:].reshape(Q_LORA, -1)], 1)
    eye = jnp.eye(MLA_HEADS, dtype=F32)
    wuk = jnp.einsum('chn,hg->hngc', prm['mla_w_uk'][li], eye).reshape(MLA_HEADS * QK_NOPE, MLA_HEADS * KV_LORA)
    hp = LANES // QK_NOPE
    wuk = jnp.stack([wuk[k * LANES:(k + 1) * LANES, k * hp * KV_LORA:(k + 1) * hp * KV_LORA]
                     for k in range(MLA_HEADS // hp)])
    fix = lambda i: (0, 0)
    c0 = RWKV_PROJ // KV_LORA
    q0 = (RWKV_PROJ + KV_LORA) // 512
    k0 = (RWKV_PROJ + KV_LORA + 512) // rw
    tab = pl.BlockSpec((tm, rw), lambda i: (i % tps, 0))
    return pl.pallas_call(
        _mla_prep_kernel,
        grid=(n // tm,),
        in_specs=[pl.BlockSpec((tm, KV_LORA), lambda i: (i, c0)), pl.BlockSpec((tm, 512), lambda i: (i, q0)),
                  pl.BlockSpec((tm, rw), lambda i: (i, k0)), tab, tab,
                  pl.BlockSpec((1, Q_LORA), fix), pl.BlockSpec((1, KV_LORA), fix),
                  pl.BlockSpec(wqb.shape, fix), pl.BlockSpec(wuk.shape, lambda i: (0, 0, 0))],
        out_specs=[pl.BlockSpec((tm, MLA_HEADS * KV_LORA), lambda i: (i, 0)), pl.BlockSpec((tm, rw), lambda i: (i, 0)),
                   pl.BlockSpec((tm, KV_LORA), lambda i: (i, 0)), pl.BlockSpec((tm, rw), lambda i: (i, 0))],
        out_shape=[jax.ShapeDtypeStruct((n, MLA_HEADS * KV_LORA), F32), jax.ShapeDtypeStruct((n, rw), F32),
                   jax.ShapeDtypeStruct((n, KV_LORA), F32), jax.ShapeDtypeStruct((n, rw), F32)],
        compiler_params=_cp("parallel"),
        name="mla_prep",
    )(p, p, p, cos, sin, prm['mla_q_norm'][li].reshape(1, -1), prm['mla_kv_norm'][li].reshape(1, -1),
      wqb.astype(BF16), wuk.astype(BF16))


def _wuv_blockdiag(w_uv):
    eye = jnp.eye(MLA_HEADS, dtype=F32)
    return jnp.einsum('chv,hg->hcgv', w_uv, eye).reshape(MLA_HEADS * KV_LORA, MLA_WIDTH).astype(BF16)


def _mla_attn_kernel(qlat_ref, qrope_ref, c_ref, kr_ref, wuv_ref, o_ref, qs_scr, qrs_scr, m_scr, l_scr, acc_scr,
                     *, lq, padf):
    qi, ki = pl.program_id(1), pl.program_id(2)
    H = MLA_HEADS

    @pl.when(ki == 0)
    def _():
        ql = qlat_ref[...]
        qs_scr[...] = jnp.concatenate([ql[:, h * KV_LORA:(h + 1) * KV_LORA] for h in range(H)], axis=0).astype(BF16)
        qr = qrope_ref[...].astype(BF16)
        lane = lax.broadcasted_iota(jnp.int32, qr.shape, 1) // QK_ROPE
        qrs_scr[...] = jnp.concatenate([jnp.where(lane == h, qr, jnp.zeros_like(qr)) for h in range(H)], axis=0)
        m_scr[...] = jnp.full_like(m_scr, NEG)
        l_scr[...] = jnp.zeros_like(l_scr)
        acc_scr[...] = jnp.zeros_like(acc_scr)

    def tile(kind):
        c2 = MLA_SCALE * math.log2(math.e)
        cb = c_ref[...].astype(BF16)
        st = _dg(cb, qs_scr[...], NT) + _dg(kr_ref[...].astype(BF16), qrs_scr[...], NT)
        if kind == "diagonal":
            kpos = ki * lq + lax.broadcasted_iota(jnp.int32, st.shape, 0)
            qpos = qi * lq + lax.broadcasted_iota(jnp.int32, st.shape, 1) % lq
            st = jnp.where(jnp.logical_and(kpos <= qpos, kpos >= padf), st, NEG)
        elif kind == "first":
            st = jnp.where(lax.broadcasted_iota(jnp.int32, (lq, 1), 0) >= padf, st, NEG)
        m_old = m_scr[...]
        m_new = jnp.maximum(m_old, jnp.max(st, axis=0, keepdims=True))
        alpha = jnp.exp2((m_old - m_new) * c2)
        pt = jnp.exp2((st - m_new) * c2)
        l_scr[...] = alpha * l_scr[...] + jnp.sum(pt, axis=0, keepdims=True)
        acc_scr[...] = alpha * acc_scr[...] + _dg(cb, pt.astype(BF16), TN)
        m_scr[...] = m_new

    assert padf <= lq
    pl.when(ki == qi)(functools.partial(tile, "diagonal"))
    pl.when(jnp.logical_and(ki < qi, ki == 0))(functools.partial(tile, "first"))
    pl.when(jnp.logical_and(ki < qi, ki > 0))(functools.partial(tile, "plain"))

    @pl.when(ki == pl.num_programs(2) - 1)
    def _():
        ot = (acc_scr[...] * (1.0 / l_scr[...])).astype(BF16)
        y = jnp.zeros(o_ref.shape, F32)
        for h in range(H):
            y = y + _dg(ot[:, h * lq:(h + 1) * lq], wuv_ref[h], TN)
        o_ref[...] = y


def _mla_attn_prompt(qlat3, qrope3, c3, kr3, w_uv, geo):
    lq = _div_tile(geo.tp, SEQ_TILE)
    nq = geo.tp // lq
    wuv = _wuv_blockdiag(w_uv).reshape(MLA_HEADS, KV_LORA, MLA_WIDTH)
    rw = MLA_HEADS * QK_ROPE
    qmap = lambda b, qi, ki: (b, qi, 0)
    kmap = lambda b, qi, ki: (b, jnp.minimum(ki, qi), 0)
    return pl.pallas_call(
        functools.partial(_mla_attn_kernel, lq=lq, padf=geo.padf),
        grid=(NSLAB, nq, nq),
        in_specs=[pl.BlockSpec((None, lq, MLA_HEADS * KV_LORA), qmap), pl.BlockSpec((None, lq, rw), qmap),
                  pl.BlockSpec((None, lq, KV_LORA), kmap), pl.BlockSpec((None, lq, rw), kmap),
                  pl.BlockSpec(wuv.shape, lambda b, qi, ki: (0, 0, 0))],
        out_specs=pl.BlockSpec((None, lq, MLA_WIDTH), qmap),
        out_shape=jax.ShapeDtypeStruct((NSLAB, geo.slab, MLA_WIDTH), F32),
        scratch_shapes=[pltpu.VMEM((MLA_HEADS * lq, KV_LORA), BF16), pltpu.VMEM((MLA_HEADS * lq, rw), BF16),
                        pltpu.VMEM((1, MLA_HEADS * lq), F32), pltpu.VMEM((1, MLA_HEADS * lq), F32),
                        pltpu.VMEM((KV_LORA, MLA_HEADS * lq), F32)],
        compiler_params=_cp("parallel", "parallel", "arbitrary"),
        name="mla_attn",
    )(qlat3, qrope3, c3, kr3, wuv)


def _mla_decode_kernel(pt_ref, qlat_ref, qrope_ref, cnew_ref, krnew_ref, kv_hbm, kr_hbm, tile_ref, wuv_ref,
                       yprev_ref, o_ref, kv_buf, kr_buf, kb_scr, s_scr, sem, *, li, n_pages, ch):
    del yprev_ref
    b = pl.program_id(0)
    slot = b % 2
    H = MLA_HEADS

    def page_copies(page, sl, pg):
        return (pltpu.make_async_copy(kv_hbm.at[page, li], kv_buf.at[sl, pg], sem.at[0, sl]),
                pltpu.make_async_copy(kr_hbm.at[page, li], kr_buf.at[sl, pg], sem.at[1, sl]))

    def start_fetch(bb, sl):
        def body(pg, carry):
            for cp in page_copies(pt_ref[bb, pg], sl, pg):
                cp.start()
            return carry
        lax.fori_loop(0, n_pages, body, 0)

    def wait_fetch(sl):
        def body(pg, carry):
            for cp in page_copies(0, sl, pg):
                cp.wait()
            return carry
        lax.fori_loop(0, n_pages, body, 0)

    @pl.when(b == 0)
    def _():
        start_fetch(0, 0)

    nxt = (b + 1) % pl.num_programs(0)

    j = b % SUB
    rid = lax.broadcasted_iota(jnp.int32, (SUB, 1), 0)

    def pick(ref):
        return jnp.sum(jnp.where(rid == j, ref[...], 0.0), axis=0, keepdims=True)

    ql = pick(qlat_ref)
    qs = jnp.concatenate([ql[:, h * KV_LORA:(h + 1) * KV_LORA] for h in range(H)], axis=0).astype(BF16)
    qr = jnp.broadcast_to(pick(qrope_ref), (H, H * QK_ROPE))
    hl = lax.broadcasted_iota(jnp.int32, qr.shape, 1) // QK_ROPE
    hr = lax.broadcasted_iota(jnp.int32, qr.shape, 0)
    qrs = jnp.where(hl == hr, qr, 0.0).astype(BF16)
    qr32 = _dg(qrs, tile_ref[...], NT).astype(BF16)
    cn = _rb(pick(cnew_ref))
    kn = _rb(pick(krnew_ref))
    s_self = (jnp.sum(qs.astype(F32) * cn, axis=1, keepdims=True)
              + jnp.sum(qrs.astype(F32) * kn, axis=1, keepdims=True)) * MLA_SCALE

    wait_fetch(slot)
    cw = ch * PAGE_SIZE
    for ci in range(n_pages // ch):
        for pg in range(ci * ch, (ci + 1) * ch):
            for cp in page_copies(pt_ref[nxt, pg], 1 - slot, pg):
                cp.start()
        kc = kv_buf[slot, ci * ch:(ci + 1) * ch].reshape(cw, KV_LORA).astype(BF16)
        kb_scr[ci * cw:(ci + 1) * cw, :] = kc
        krc = jnp.concatenate([kr_buf[slot, pg] for pg in range(ci * ch, (ci + 1) * ch)], axis=1).astype(BF16)
        s_scr[:, ci * cw:(ci + 1) * cw] = (_dg(qs, kc, NT) + _dg(qr32, krc, NN)) * MLA_SCALE
    s = s_scr[...]
    m = jnp.maximum(jnp.max(s, axis=1, keepdims=True), s_self)
    e = jnp.exp(s - m)
    e_self = jnp.exp(s_self - m)
    rl = 1.0 / (jnp.sum(e, axis=1, keepdims=True) + e_self)
    p = (e * rl).astype(BF16)
    acc = _rb(e_self * rl) * cn
    for ci in range(n_pages // ch):
        acc = acc + _dg(p[:, ci * cw:(ci + 1) * cw], kb_scr[ci * cw:(ci + 1) * cw, :], NN)
    ol = jnp.concatenate([acc[h:h + 1] for h in range(H)], axis=1)
    y = _bdot(jnp.broadcast_to(ol, (SUB, ol.shape[1])), wuv_ref[...])
    o_ref[pl.ds(j, 1), :] = y[0:1]

    @pl.when(b == pl.num_programs(0) - 1)
    def _():
        wait_fetch(1 - slot)


def _mla_decode(qlat, qrope, c, kr8, cache_kv, cache_kr, page_table, w_uv, li, y_prev, geo):
    n = qlat.shape[0]
    n_pages = page_table.shape[1]
    ch = _div_tile(n_pages, 8, 1)
    rw = MLA_HEADS * QK_ROPE
    wuv = _wuv_blockdiag(w_uv)
    tile = jnp.tile(jnp.eye(QK_ROPE, dtype=F32), (1, MLA_HEADS)).astype(BF16)

    def rowmap(b, pt):
        return ((b // geo.sps) * geo.slab + geo.tp + b % geo.sps) // SUB, 0

    rowblk = lambda w: pl.BlockSpec((SUB, w), rowmap)
    grid_spec = pltpu.PrefetchScalarGridSpec(
        num_scalar_prefetch=1,
        grid=(geo.ns,),
        in_specs=[rowblk(MLA_HEADS * KV_LORA), rowblk(rw), rowblk(KV_LORA), rowblk(rw),
                  pl.BlockSpec(memory_space=pl.ANY), pl.BlockSpec(memory_space=pl.ANY),
                  pl.BlockSpec(tile.shape, lambda b, pt: (0, 0)),
                  pl.BlockSpec(wuv.shape, lambda b, pt: (0, 0)),
                  pl.BlockSpec(memory_space=pl.ANY)],
        out_specs=rowblk(MLA_WIDTH),
        scratch_shapes=[pltpu.VMEM((2, n_pages, PAGE_SIZE, KV_LORA), F32),
                        pltpu.VMEM((2, n_pages, QK_ROPE, PAGE_SIZE), F32),
                        pltpu.VMEM((n_pages * PAGE_SIZE, KV_LORA), BF16),
                        pltpu.VMEM((MLA_HEADS, n_pages * PAGE_SIZE), F32),
                        pltpu.SemaphoreType.DMA((2, 2))],
    )
    return pl.pallas_call(
        functools.partial(_mla_decode_kernel, li=li, n_pages=n_pages, ch=ch),
        grid_spec=grid_spec,
        out_shape=jax.ShapeDtypeStruct((n, MLA_WIDTH), F32),
        input_output_aliases={9: 0},
        compiler_params=_cp("arbitrary"),
        name="mla_decode",
    )(page_table, qlat, qrope, c, kr8, cache_kv, jnp.swapaxes(cache_kr, 2, 3), tile, wuv, y_prev)


def _forward(x_prompt, x_sample, state_ret, state_s5_re, state_s5_im, state_wkv, state_shift, cache_kv,
             cache_krope, page_table, meta_tokens, prm):
    nb, seq, _ = x_prompt.shape
    assert nb == NSLAB and x_sample.shape[1] == 1
    ns = x_sample.shape[0]
    geo = _Geom(seq, ns, page_table.shape[1])
    n, tm, tp, padf, sps = geo.n, geo.tm, geo.tp, geo.padf, geo.sps

    meta = jnp.broadcast_to(meta_tokens[None], (nb, N_META, D_MODEL))
    x = jnp.concatenate([jnp.zeros((nb, padf, D_MODEL), F32), meta, x_prompt, x_sample.reshape(nb, sps, D_MODEL)], 1)
    x = x.reshape(n, D_MODEL)
    pos_slab = jnp.concatenate([jnp.arange(tp, dtype=jnp.int32) - padf, jnp.full((sps,), geo.past, jnp.int32)])

    ret_p, ret_s, s5_p, s5_s, wkv_p, wkv_s, sh_p, sh_s, kv_l, kr_l = [], [], [], [], [], [], [], [], [], []
    for layer in range(DEPTH):
        li = layer // 2
        if layer % 2 == 0:
            w_in = prm['ev_w_in'][li].astype(BF16)
            p = _matmul(x, w_in, tm, _div_tile(w_in.shape[1], 1024, 128))
            p3 = geo.v3(p)
            gng, gnb = prm['ret_gn_g'][li], prm['ret_gn_b'][li]
            ret, s_p = _retention_prompt(p3, geo, gng, gnb)
            ret, s_s = _retention_sample(p3, geo, state_ret[li], gng, gnb, ret)
            s5o, h_p = _s5_prompt(p3, geo, prm, li)
            h0 = jnp.concatenate([state_s5_re[li].reshape(ns, S5_CH), state_s5_im[li].reshape(ns, S5_CH)], 1)
            s5o, h_s = _s5_sample(p3, geo, h0, prm, li, s5o)
            ret_p.append(s_p)
            ret_s.append(s_s)
            s5_p.append(h_p)
            s5_s.append(h_s)
            x = _proj_ln(ret.reshape(n, -1), s5o.reshape(n, -1), x, prm['ev_w_out'][li], prm['ln1_g'][layer],
                         prm['ln1_b'][layer], tm)
        else:
            w = prm['od_w_in'][li]
            o1 = RWKV_PROJ
            o2 = o1 + Q_LORA
            o3 = o2 + KV_LORA
            w_in = jnp.concatenate([w[:, :o1], w[:, o2:o3], w[:, o1:o2], jnp.zeros((D_MODEL, 512 - Q_LORA), F32),
                                    jnp.tile(w[:, o3:], (1, MLA_HEADS))], 1).astype(BF16)
            p = _matmul(x, w_in, tm, _div_tile(w_in.shape[1], 1408, 128))
            p3 = geo.v3(p)
            sh_p.append(p3[:, tp - 1, :RWKV_PROJ])
            sh_s.append(p3[:, tp:, :RWKV_PROJ].reshape(ns, RWKV_PROJ))
            r, lw, k, v, a, b, g = _rwkv_prep(p3, geo, state_shift[li], prm, li)
            seqs = (r, lw, k, v, a, b)
            y, g_p = _wkv_prompt(seqs, geo)
            y, g_s = _wkv_sample(seqs, geo, state_wkv[li], y)
            wkv_p.append(g_p)
            wkv_s.append(g_s)
            flat = lambda t: t.reshape(n, t.shape[-1])
            y_c = _rwkv_post(flat(y), flat(r), flat(k), flat(v), flat(g), prm, li, tm)
            qlat, qrope, c, kr8 = _mla_prep(p, geo, pos_slab, prm, li)
            kv_l.append(geo.v3(c))
            kr_l.append(geo.v3(kr8)[:, :, :QK_ROPE])
            y_d = _mla_attn_prompt(geo.v3(qlat), geo.v3(qrope), geo.v3(c), geo.v3(kr8), prm['mla_w_uv'][li], geo)
            y_d = _mla_decode(qlat, qrope, c, kr8, cache_kv, cache_krope, page_table, prm['mla_w_uv'][li], li,
                              flat(y_d), geo)
            x = _proj_ln(y_c, y_d, x, prm['od_w_out'][li], prm['ln1_g'][layer], prm['ln1_b'][layer], tm)
        x = _moe_ln(x, prm, layer, tm)

    x3 = geo.v3(x)
    y_p = x3[:, padf + N_META:tp]
    y_s = x3[:, tp:].reshape(ns, 1, D_MODEL)

    def s5_state(hs, part):
        return jnp.stack([h[:, part * S5_CH:(part + 1) * S5_CH].reshape(-1, S5_GROUPS, S5_STATE) for h in hs])

    kv_p = jnp.stack([c[:, padf:tp] for c in kv_l], 1)
    kv_s = jnp.stack([c[:, tp:].reshape(ns, 1, KV_LORA) for c in kv_l], 1)
    kr_p = jnp.stack([c[:, padf:tp] for c in kr_l], 1)
    kr_s = jnp.stack([c[:, tp:].reshape(ns, 1, QK_ROPE) for c in kr_l], 1)
    return (y_p, y_s, jnp.stack(ret_p), jnp.stack(ret_s), s5_state(s5_p, 0), s5_state(s5_s, 0),
            s5_state(s5_p, 1), s5_state(s5_s, 1), jnp.stack(wkv_p), jnp.stack(wkv_s),
            jnp.stack(sh_p), jnp.stack(sh_s), kv_p, kv_s, kr_p, kr_s)


def kernel(x_prompt, x_sample, state_ret, state_s5_re, state_s5_im, state_wkv, state_shift, cache_kv, cache_krope, page_table, meta_tokens, ev_w_in, ret_gn_g, ret_gn_b, s5_a_re, s5_a_im, s5_log_dt, s5_b_re, s5_b_im, s5_c_re, s5_c_im, s5_d, s5_w_glu, s5_b_glu, ev_w_out, od_w_in, rwkv_mu, rwkv_w0, rwkv_w2, rwkv_a0, rwkv_a2, rwkv_g2, rwkv_k_k, rwkv_k_a, rwkv_r_k, rwkv_gn_g, rwkv_gn_b, mla_q_norm, mla_w_qb, mla_kv_norm, mla_w_uk, mla_w_uv, od_w_out, ln1_g, ln1_b, ln2_g, ln2_b, moe_w_coarse, moe_b_coarse, moe_w_fine, moe_b_fine, moe_w_gate, moe_w_up, moe_w_down):
    prm = dict(ev_w_in=ev_w_in, ret_gn_g=ret_gn_g, ret_gn_b=ret_gn_b, s5_a_re=s5_a_re, s5_a_im=s5_a_im,
               s5_log_dt=s5_log_dt, s5_b_re=s5_b_re, s5_b_im=s5_b_im, s5_c_re=s5_c_re, s5_c_im=s5_c_im,
               s5_d=s5_d, s5_w_glu=s5_w_glu, s5_b_glu=s5_b_glu, ev_w_out=ev_w_out,
               od_w_in=od_w_in, rwkv_mu=rwkv_mu, rwkv_w0=rwkv_w0, rwkv_w2=rwkv_w2, rwkv_a0=rwkv_a0,
               rwkv_a2=rwkv_a2, rwkv_g2=rwkv_g2, rwkv_k_k=rwkv_k_k, rwkv_k_a=rwkv_k_a, rwkv_r_k=rwkv_r_k,
               rwkv_gn_g=rwkv_gn_g, rwkv_gn_b=rwkv_gn_b, mla_q_norm=mla_q_norm, mla_w_qb=mla_w_qb,
               mla_kv_norm=mla_kv_norm, mla_w_uk=mla_w_uk, mla_w_uv=mla_w_uv, od_w_out=od_w_out,
               ln1_g=ln1_g, ln1_b=ln1_b, ln2_g=ln2_g, ln2_b=ln2_b, moe_w_coarse=moe_w_coarse,
               moe_b_coarse=moe_b_coarse, moe_w_fine=moe_w_fine, moe_b_fine=moe_b_fine,
               moe_w_gate=moe_w_gate, moe_w_up=moe_w_up, moe_w_down=moe_w_down)
    return _forward(x_prompt, x_sample, state_ret, state_s5_re, state_s5_im, state_wkv, state_shift, cache_kv,
                    cache_krope, page_table, meta_tokens, prm)
```

```python
import functools
import math

import jax
import jax.numpy as jnp
from jax import lax
from jax.experimental import pallas as pl
from jax.experimental.pallas import tpu as pltpu

F32 = jnp.float32
BF16 = jnp.bfloat16

D_MODEL = 1024
N_META = 16
PAGE_SIZE = 128
RET_HEADS, RET_DK, RET_DV = 4, 64, 128
S5_WIDTH, S5_GROUP, S5_STATE = 512, 16, 64
S5_GROUPS = S5_WIDTH // S5_GROUP
S5_CH = S5_GROUPS * S5_STATE
S5_GPB = 128 // S5_GROUP
S5_NBLK = S5_GROUPS // S5_GPB
S5_BCH = S5_GPB * S5_STATE
RWKV_HEAD, RWKV_WIDTH = 64, 512
RWKV_HEADS = RWKV_WIDTH // RWKV_HEAD
LORA_W, LORA_A, LORA_G = 64, 64, 128
RWKV_PROJ = 3 * RWKV_WIDTH + LORA_W + LORA_A + LORA_G
RWKV_GN_EPS = 64e-5
MLA_HEADS, Q_LORA, KV_LORA, QK_NOPE, QK_ROPE, V_HEAD = 8, 384, 256, 64, 32, 64
MLA_WIDTH = MLA_HEADS * V_HEAD
MLA_SCALE = (QK_NOPE + QK_ROPE) ** -0.5
ROPE_THETA = 10000.0
N_EGROUPS, E_PER_GROUP, D_EXPERT = 4, 4, 256
N_EXPERTS = N_EGROUPS * E_PER_GROUP
DEPTH = 4
DN_ALPHA = (2 * DEPTH) ** 0.25
NSLAB = 8
SUB = 8
LANES = 128

VMEM_LIMIT = 56 * 1024 * 1024
WKV_L = 32
WKV_HG = 4
WKV_NB = 8
ROW_TILE = 1056
SEQ_TILE = 416
S5_TILE = 80
NEG = -1e30

NT = (((1,), (1,)), ((), ()))
TN = (((0,), (0,)), ((), ()))
NN = (((1,), (0,)), ((), ()))


def _cp(*sem):
    return pltpu.CompilerParams(dimension_semantics=sem, vmem_limit_bytes=VMEM_LIMIT)


def _div_tile(n, cap, mult=8):
    best = None
    for d in range(mult, min(n, cap) + 1, mult):
        if n % d == 0:
            best = d
    assert best is not None, (n, cap, mult)
    return best


def _dg(a, b, dims):
    return lax.dot_general(a, b, dims, preferred_element_type=F32)


def _bdot(a, b, dims=NN):
    return _dg(a.astype(BF16), b.astype(BF16), dims)


def _dot3(a, b, dims=NN):
    ah = a.astype(BF16)
    al = (a - ah.astype(F32)).astype(BF16)
    bh = b.astype(BF16)
    bl = (b - bh.astype(F32)).astype(BF16)
    return _dg(ah, bh, dims) + (_dg(ah, bl, dims) + _dg(al, bh, dims))


def _dotc(a, b, dims=NN):
    return _bdot(a, b, dims)


def _rb(x):
    return x.astype(BF16).astype(F32)


def _layer_norm(z, g, b, eps=1e-5):
    mu = jnp.mean(z, -1, keepdims=True)
    zc = z - mu
    var = jnp.mean(zc * zc, -1, keepdims=True)
    return zc * lax.rsqrt(var + eps) * g + b


def _rope(x, cos, sin_signed, half):
    w = x.shape[-1]
    lane = lax.broadcasted_iota(jnp.int32, x.shape, 1)
    first = (lane % (2 * half)) < half
    other = jnp.where(first, pltpu.roll(x, w - half, 1), pltpu.roll(x, half, 1))
    return x * cos + other * sin_signed


def _rope_tables(pos, dim, reps):
    inv = ROPE_THETA ** (-jnp.arange(0, dim, 2, dtype=F32) / dim)
    ang = pos.astype(F32)[:, None] * inv[None, :]
    cos = jnp.cos(ang)
    sin = jnp.sin(ang)
    cos = jnp.tile(jnp.concatenate([cos, cos], -1), (1, reps))
    sin = jnp.tile(jnp.concatenate([-sin, sin], -1), (1, reps))
    return cos, sin


def _drop_arg(kern, idx, *refs):
    return kern(*(refs[:idx] + refs[idx + 1:]))


class _Geom:
    def __init__(self, seq, ns, n_pages):
        self.tq = seq + N_META
        self.padf = (-self.tq) % WKV_L
        self.tp = self.padf + self.tq
        assert ns % (NSLAB * SUB) == 0
        self.ns = ns
        self.sps = ns // NSLAB
        self.slab = self.tp + self.sps
        self.n = NSLAB * self.slab
        self.past = n_pages * PAGE_SIZE
        self.tm = _div_tile(self.slab, ROW_TILE)
        self.tps = self.slab // self.tm

    def v3(self, x):
        return x.reshape(NSLAB, self.slab, x.shape[-1])

    def sample_blk(self, rows):
        per = self.sps // rows
        return lambda k: (k // per, self.tp // rows + k % per)


def _mm_kernel(x_ref, w_ref, o_ref):
    o_ref[...] = _dg(x_ref[...].astype(BF16), w_ref[...], NN)


def _matmul(x, w, tm, tn):
    n, k = x.shape
    m = w.shape[1]
    return pl.pallas_call(
        _mm_kernel,
        grid=(n // tm, m // tn),
        in_specs=[pl.BlockSpec((tm, k), lambda i, j: (i, 0)), pl.BlockSpec((k, tn), lambda i, j: (0, j))],
        out_specs=pl.BlockSpec((tm, tn), lambda i, j: (i, j)),
        out_shape=jax.ShapeDtypeStruct((n, m), F32),
        compiler_params=_cp("parallel", "arbitrary"),
        name="proj_in",
    )(x, w)


def _proj_ln_kernel(a1_ref, a2_ref, x_ref, w1_ref, w2_ref, g_ref, b_ref, o_ref):
    mix = _dg(a1_ref[...].astype(BF16), w1_ref[...], NN) + _dg(a2_ref[...].astype(BF16), w2_ref[...], NN)
    z = DN_ALPHA * x_ref[...] + mix
    o_ref[...] = _layer_norm(z, g_ref[...], b_ref[...])


def _proj_ln(a1, a2, x, w_out, g, b, tm):
    n = x.shape[0]
    k1, k2 = a1.shape[1], a2.shape[1]
    w1 = w_out[:k1].astype(BF16)
    w2 = w_out[k1:].astype(BF16)
    row = lambda i: (i, 0)
    fix = lambda i: (0, 0)
    return pl.pallas_call(
        _proj_ln_kernel,
        grid=(n // tm,),
        in_specs=[pl.BlockSpec((tm, k1), row), pl.BlockSpec((tm, k2), row), pl.BlockSpec((tm, D_MODEL), row),
                  pl.BlockSpec((k1, D_MODEL), fix), pl.BlockSpec((k2, D_MODEL), fix),
                  pl.BlockSpec((1, D_MODEL), fix), pl.BlockSpec((1, D_MODEL), fix)],
        out_specs=pl.BlockSpec((tm, D_MODEL), row),
        out_shape=jax.ShapeDtypeStruct((n, D_MODEL), F32),
        compiler_params=_cp("parallel"),
        name="proj_out_ln",
    )(a1, a2, x, w1, w2, g.reshape(1, -1), b.reshape(1, -1))


MOE_EPS = 2
ROUTE0 = N_EGROUPS


def _route(lt):
    lane = lax.broadcasted_iota(jnp.int32, lt.shape, 1)
    lanef = lane.astype(F32)
    coarse = lane < N_EGROUPS
    lc = jnp.where(coarse, lt, NEG)
    m = jnp.max(lc, axis=1, keepdims=True)
    gsel = jnp.min(jnp.where(lc == m, lanef, 1e9), axis=1, keepdims=True)
    gate_c = 1.0 / jnp.sum(jnp.where(coarse, jnp.exp(lc - m), 0.0), axis=1, keepdims=True)
    grp = ((lane - ROUTE0) // E_PER_GROUP).astype(F32)
    lf = jnp.where(grp == gsel, lt, NEG)
    t1 = jnp.max(lf, axis=1, keepdims=True)
    i1 = jnp.min(jnp.where(lf == t1, lanef, 1e9), axis=1, keepdims=True)
    lf2 = jnp.where(lanef == i1, NEG, lf)
    t2 = jnp.max(lf2, axis=1, keepdims=True)
    i2 = jnp.min(jnp.where(lf2 == t2, lanef, 1e9), axis=1, keepdims=True)
    e2 = jnp.exp(t2 - t1)
    w1 = gate_c / (1.0 + e2)
    return jnp.where(lanef == i1, w1, 0.0) + jnp.where(lanef == i2, w1 * e2, 0.0)


def _moe_kernel(x_ref, wr_ref, br_ref, wgu_ref, wd_ref, g_ref, b_ref, o_ref, xb_scr, comb_scr, h_scr):
    e = pl.program_id(1)

    @pl.when(e == 0)
    def _():
        xb = x_ref[...].astype(BF16)
        xb_scr[...] = xb
        comb_scr[...] = _route(_dg(xb, wr_ref[...], NN) + br_ref[...])

    xb = xb_scr[...]
    comb = comb_scr[...]
    lane = lax.broadcasted_iota(jnp.int32, comb.shape, 1)
    gus = [_dg(xb, wgu_ref[q], NN) for q in range(MOE_EPS)]
    hb = []
    for q, gu in enumerate(gus):
        hg, hu = gu[:, :D_EXPERT], gu[:, D_EXPERT:]
        ce = jnp.sum(jnp.where(lane == e * MOE_EPS + q + ROUTE0, comb, 0.0), axis=1, keepdims=True)
        hb.append(((hg * jax.nn.sigmoid(hg)) * hu * ce).astype(BF16))
    for ee in range(N_EXPERTS // MOE_EPS):
        @pl.when(e == ee)
        def _():
            for q in range(MOE_EPS):
                x0 = (ee * MOE_EPS + q) * D_EXPERT
                h_scr[:, x0:x0 + D_EXPERT] = hb[q]

    @pl.when(e == pl.num_programs(1) - 1)
    def _():
        z = DN_ALPHA * x_ref[...] + _dg(h_scr[...], wd_ref[...], NN)
        o_ref[...] = _layer_norm(z, g_ref[...], b_ref[...])


def _moe_ln(x, prm, layer, tm):
    n = x.shape[0]
    wr = jnp.concatenate([prm['moe_w_coarse'][layer],
                          jnp.transpose(prm['moe_w_fine'][layer], (1, 0, 2)).reshape(D_MODEL, N_EXPERTS)], 1)
    wr = jnp.pad(wr, ((0, 0), (0, LANES - wr.shape[1]))).astype(BF16)
    br = jnp.pad(jnp.concatenate([prm['moe_b_coarse'][layer], prm['moe_b_fine'][layer].reshape(-1)]),
                 (0, LANES - N_EGROUPS - N_EXPERTS)).reshape(1, LANES)
    row = lambda i, e: (i, 0)
    fix = lambda i, e: (0, 0)
    exp3 = lambda i, e: (e, 0, 0)
    return pl.pallas_call(
        _moe_kernel,
        grid=(n // tm, N_EXPERTS // MOE_EPS),
        in_specs=[pl.BlockSpec((tm, D_MODEL), row), pl.BlockSpec((D_MODEL, LANES), fix), pl.BlockSpec((1, LANES), fix),
                  pl.BlockSpec((MOE_EPS, D_MODEL, 2 * D_EXPERT), exp3),
                  pl.BlockSpec((N_EXPERTS * D_EXPERT, D_MODEL), fix, pipeline_mode=pl.Buffered(1)),
                  pl.BlockSpec((1, D_MODEL), fix), pl.BlockSpec((1, D_MODEL), fix)],
        out_specs=pl.BlockSpec((tm, D_MODEL), row),
        out_shape=jax.ShapeDtypeStruct((n, D_MODEL), F32),
        scratch_shapes=[pltpu.VMEM((tm, D_MODEL), BF16), pltpu.VMEM((tm, LANES), F32),
                        pltpu.VMEM((tm, N_EXPERTS * D_EXPERT), BF16)],
        compiler_params=_cp("parallel", "arbitrary"),
        name="moe_ln",
    )(x, wr, br, jnp.concatenate([prm['moe_w_gate'][layer], prm['moe_w_up'][layer]], -1).astype(BF16),
      prm['moe_w_down'][layer].astype(BF16).reshape(N_EXPERTS * D_EXPERT, D_MODEL),
      prm['ln2_g'][layer].reshape(1, -1), prm['ln2_b'][layer].reshape(1, -1))


def _ret_log_g():
    return [math.log1p(-2.0 ** (-5.0 - h)) for h in range(RET_HEADS)]


def _ret_kernel(q_ref, k_ref, v_ref, gate_ref, cos_ref, sin_ref, dmask_ref, cdec_ref, kdec_ref, sdec_ref,
                gng_ref, gnb_ref, o_ref, sout_ref, s_scr, *, L, padf):
    c = pl.program_id(1)
    HK, HV = RET_HEADS * RET_DK, RET_HEADS * RET_DV

    @pl.when(c == 0)
    def _():
        s_scr[...] = jnp.zeros_like(s_scr)

    cos, sin = cos_ref[...], sin_ref[...]
    q = _rope(q_ref[...], cos, sin, RET_DK // 2) * RET_DK ** -0.5
    k = _rope(k_ref[...], cos, sin, RET_DK // 2)
    real = (lax.broadcasted_iota(jnp.int32, (L, 1), 0) + c * L) >= padf
    k = jnp.where(real, k, 0.0)
    v = jnp.where(real, v_ref[...], 0.0)
    gate = gate_ref[...]
    qb, kb, vb = q.astype(BF16), k.astype(BF16), v.astype(BF16)
    sbd = s_scr[...]
    cross = _dg(qb, sbd.astype(BF16), NN) * cdec_ref[...]
    lane = lax.broadcasted_iota(jnp.int32, (L, HK), 1)
    outs = []
    scs = [(_dg(jnp.where(lane // RET_DK == h, qb, jnp.zeros_like(qb)), kb, NT) * dmask_ref[h]).astype(BF16)
           for h in range(RET_HEADS)]
    for h in range(RET_HEADS):
        sl = slice(h * RET_DV, (h + 1) * RET_DV)
        o = _dg(scs[h], vb[:, sl], NN) + cross[:, sl]
        mu = jnp.mean(o, -1, keepdims=True)
        oc = o - mu
        var = jnp.mean(oc * oc, -1, keepdims=True)
        outs.append(oc * lax.rsqrt(var + 1e-5))
    on = jnp.concatenate(outs, axis=1) * gng_ref[...] + gnb_ref[...]
    o_ref[...] = (gate * jax.nn.sigmoid(gate)) * on
    upd = _dg((k * kdec_ref[...]).astype(BF16), vb, TN)
    ri = lax.broadcasted_iota(jnp.int32, (HK, HV), 0) // RET_DK
    ci = lax.broadcasted_iota(jnp.int32, (HK, HV), 1) // RET_DV
    s_new = sdec_ref[...] * sbd + jnp.where(ri == ci, upd, 0.0)
    s_scr[...] = s_new

    @pl.when(c == pl.num_programs(1) - 1)
    def _():
        for h in range(RET_HEADS):
            sout_ref[h] = s_new[h * RET_DK:(h + 1) * RET_DK, h * RET_DV:(h + 1) * RET_DV]


def _retention_prompt(p3, geo, gn_g, gn_b):
    HK, HV = RET_HEADS * RET_DK, RET_HEADS * RET_DV
    L = _div_tile(geo.tp, SEQ_TILE)
    log_g = jnp.asarray(_ret_log_g(), F32)
    idx = jnp.arange(L, dtype=F32)
    diff = idx[:, None] - idx[None, :]
    dmask = jnp.where(diff >= 0, jnp.exp(log_g[:, None, None] * jnp.maximum(diff, 0.0)), 0.0)
    cdec = jnp.repeat(jnp.exp(log_g[None, :] * (idx + 1.0)[:, None]), RET_DV, axis=1)
    kdec = jnp.repeat(jnp.exp(log_g[None, :] * (L - 1.0 - idx)[:, None]), RET_DK, axis=1)
    sdec = jnp.broadcast_to(jnp.repeat(jnp.exp(log_g * L), RET_DK)[:, None], (HK, HV))
    cos, sin = _rope_tables(jnp.arange(geo.tp, dtype=jnp.int32) - geo.padf, RET_DK, RET_HEADS)
    fix2 = lambda b, c: (0, 0)
    tab = pl.BlockSpec((L, HK), lambda b, c: (c, 0))
    return pl.pallas_call(
        functools.partial(_ret_kernel, L=L, padf=geo.padf),
        grid=(NSLAB, geo.tp // L),
        in_specs=[pl.BlockSpec((None, L, HK), lambda b, c: (b, c, 0)), pl.BlockSpec((None, L, HK), lambda b, c: (b, c, 1)),
                  pl.BlockSpec((None, L, HV), lambda b, c: (b, c, 1)), pl.BlockSpec((None, L, HV), lambda b, c: (b, c, 2)),
                  tab, tab, pl.BlockSpec((RET_HEADS, L, L), lambda b, c: (0, 0, 0)),
                  pl.BlockSpec((L, HV), fix2), pl.BlockSpec((L, HK), fix2), pl.BlockSpec((HK, HV), fix2),
                  pl.BlockSpec((1, HV), fix2), pl.BlockSpec((1, HV), fix2)],
        out_specs=[pl.BlockSpec((None, L, HV), lambda b, c: (b, c, 0)),
                   pl.BlockSpec((None, RET_HEADS, RET_DK, RET_DV), lambda b, c: (b, 0, 0, 0))],
        out_shape=[jax.ShapeDtypeStruct((NSLAB, geo.slab, HV), F32),
                   jax.ShapeDtypeStruct((NSLAB, RET_HEADS, RET_DK, RET_DV), F32)],
        scratch_shapes=[pltpu.VMEM((HK, HV), F32)],
        compiler_params=_cp("parallel", "arbitrary"),
        name="retention",
    )(p3, p3, p3, p3, cos, sin, dmask, cdec, kdec, sdec, gn_g.reshape(1, -1), gn_b.reshape(1, -1))


def _ret_step_kernel(q_ref, k_ref, v_ref, gate_ref, cos_ref, sin_ref, gng_ref, gnb_ref, s_ref, o_ref, sout_ref):
    cos, sin = cos_ref[...], sin_ref[...]
    qb = _rb(_rope(q_ref[...], cos, sin, RET_DK // 2) * RET_DK ** -0.5)
    kb = _rb(_rope(k_ref[...], cos, sin, RET_DK // 2))
    vb = _rb(v_ref[...])
    gate = gate_ref[...]
    ri = lax.broadcasted_iota(jnp.int32, (RET_DK, RET_DK), 0)
    ci = lax.broadcasted_iota(jnp.int32, (RET_DK, RET_DK), 1)
    eye = jnp.where(ri == ci, 1.0, 0.0)

    def col(x):
        return jnp.sum(eye * jnp.broadcast_to(x, (RET_DK, RET_DK)), axis=1, keepdims=True)

    for j in range(SUB):
        outs = []
        for h, lg in enumerate(_ret_log_g()):
            g = math.exp(lg)
            qh = qb[j:j + 1, h * RET_DK:(h + 1) * RET_DK]
            kh = kb[j:j + 1, h * RET_DK:(h + 1) * RET_DK]
            vh = vb[j:j + 1, h * RET_DV:(h + 1) * RET_DV]
            s0 = s_ref[j, h]
            qk = _rb(jnp.sum(qh * kh, axis=1, keepdims=True))
            o = qk * vh + g * jnp.sum(col(qh) * _rb(s0), axis=0, keepdims=True)
            sout_ref[j, h] = g * s0 + col(kh) * vh
            mu = jnp.mean(o, -1, keepdims=True)
            oc = o - mu
            var = jnp.mean(oc * oc, -1, keepdims=True)
            outs.append(oc * lax.rsqrt(var + 1e-5))
        on = jnp.concatenate(outs, axis=1) * gng_ref[...] + gnb_ref[...]
        gj = gate[j:j + 1]
        o_ref[j:j + 1, :] = (gj * jax.nn.sigmoid(gj)) * on


def _retention_sample(p3, geo, s0, gn_g, gn_b, out_prev):
    HK, HV = RET_HEADS * RET_DK, RET_HEADS * RET_DV
    cos, sin = _rope_tables(jnp.full((1,), geo.past, jnp.int32), RET_DK, RET_HEADS)
    sb = geo.sample_blk(SUB)
    fix = lambda k: (0, 0)
    st = pl.BlockSpec((SUB, RET_HEADS, RET_DK, RET_DV), lambda k: (k, 0, 0, 0))
    return pl.pallas_call(
        functools.partial(_drop_arg, _ret_step_kernel, 9),
        grid=(geo.ns // SUB,),
        in_specs=[pl.BlockSpec((None, SUB, HK), lambda k: sb(k) + (0,)), pl.BlockSpec((None, SUB, HK), lambda k: sb(k) + (1,)),
                  pl.BlockSpec((None, SUB, HV), lambda k: sb(k) + (1,)), pl.BlockSpec((None, SUB, HV), lambda k: sb(k) + (2,)),
                  pl.BlockSpec((1, HK), fix), pl.BlockSpec((1, HK), fix),
                  pl.BlockSpec((1, HV), fix), pl.BlockSpec((1, HV), fix), st, pl.BlockSpec(memory_space=pl.ANY)],
        out_specs=[pl.BlockSpec((None, SUB, HV), lambda k: sb(k) + (0,)), st],
        out_shape=[jax.ShapeDtypeStruct(out_prev.shape, F32), jax.ShapeDtypeStruct(s0.shape, F32)],
        input_output_aliases={9: 0},
        compiler_params=_cp("parallel"),
        name="retention_step",
    )(p3, p3, p3, p3, cos, sin, gn_g.reshape(1, -1), gn_b.reshape(1, -1), s0, out_prev)


def _s5_params(prm, li):
    f32 = F32
    a_re, a_im = prm['s5_a_re'][li].astype(f32), prm['s5_a_im'][li].astype(f32)
    dt = jnp.exp(prm['s5_log_dt'][li].astype(f32))[:, None]
    mag = jnp.exp(dt * a_re)
    ab_re, ab_im = mag * jnp.cos(dt * a_im), mag * jnp.sin(dt * a_im)
    den = a_re * a_re + a_im * a_im
    n_re = ab_re - 1.0
    f_re = (n_re * a_re + ab_im * a_im) / den
    f_im = (ab_im * a_re - n_re * a_im) / den
    b_re, b_im = prm['s5_b_re'][li].astype(f32), prm['s5_b_im'][li].astype(f32)
    bb_re = f_re[..., None] * b_re - f_im[..., None] * b_im
    bb_im = f_re[..., None] * b_im + f_im[..., None] * b_re
    eye = jnp.eye(S5_GPB, dtype=f32)

    def block_in(w):
        w = w.reshape(S5_NBLK, S5_GPB, S5_STATE, S5_GROUP)
        return jnp.einsum('bgpc,gh->bgchp', w, eye).reshape(S5_NBLK, LANES, S5_BCH)

    def block_out(w):
        w = w.reshape(S5_NBLK, S5_GPB, S5_GROUP, S5_STATE)
        return jnp.einsum('bgcp,gh->bgphc', w, eye).reshape(S5_NBLK, S5_BCH, LANES)

    bbw = jnp.concatenate([block_in(bb_re), block_in(bb_im)], 2).astype(BF16)
    ccw = jnp.concatenate([block_out(prm['s5_c_re'][li]), -block_out(prm['s5_c_im'][li])], 1).astype(BF16)
    lam = jnp.concatenate([ab_re.reshape(1, -1), ab_im.reshape(1, -1)], 1)
    return [bbw, ccw, lam, prm['s5_d'][li].reshape(1, -1), prm['s5_w_glu'][li].astype(BF16),
            prm['s5_b_glu'][li].reshape(1, -1)]


def _s5_specs(fix):
    fix3 = lambda *a: (0, 0, 0)
    return [pl.BlockSpec((S5_NBLK, LANES, 2 * S5_BCH), fix3), pl.BlockSpec((S5_NBLK, 2 * S5_BCH, LANES), fix3),
            pl.BlockSpec((1, 2 * S5_CH), fix), pl.BlockSpec((1, S5_WIDTH), fix),
            pl.BlockSpec((S5_WIDTH, S5_WIDTH), fix), pl.BlockSpec((1, S5_WIDTH), fix)]


def _s5_gate(y, u, d_ref, wglu_ref, bglu_ref):
    y = y + d_ref[...] * u
    z = 0.5 * y * (1.0 + jnp.tanh(math.sqrt(2.0 / math.pi) * (y + 0.044715 * (y * y * y))))
    return z * jax.nn.sigmoid(_dg(z.astype(BF16), wglu_ref[...], NN) + bglu_ref[...])


S5_NJ = 2 * S5_CH // LANES
S5_PAIRS = 8


def _s5_prompt_kernel(u_ref, bb_ref, cc_ref, lam_ref, d_ref, wglu_ref, bglu_ref, o_ref, hout_ref, sc, h_scr,
                      *, lt, pitch, padf):
    i = pl.program_id(0)

    @pl.when(i == 0)
    def _():
        h_scr[...] = jnp.zeros_like(h_scr)

    real = (lax.broadcasted_iota(jnp.int32, (lt, 1), 0) + i * lt) >= padf
    half = S5_NJ // 2
    spb = S5_BCH // LANES
    ub = [jnp.where(real, u_ref[b], 0.0).astype(BF16) for b in range(NSLAB)]
    for k in range(S5_NBLK):
        for b in range(NSLAB):
            bu = _dg(ub[b][:, k * LANES:(k + 1) * LANES], bb_ref[k], NN)
            for q in range(spb):
                sc[k * spb + q, b * pitch:b * pitch + lt, :] = bu[:, q * LANES:(q + 1) * LANES]
                sc[half + k * spb + q, b * pitch:b * pitch + lt, :] = bu[:, S5_BCH + q * LANES:S5_BCH + (q + 1) * LANES]
    for j0 in range(0, half, S5_PAIRS):
        js = list(range(j0, j0 + S5_PAIRS))
        lr = [jnp.broadcast_to(lam_ref[:, j * LANES:(j + 1) * LANES], (NSLAB, LANES)) for j in js]
        li = [jnp.broadcast_to(lam_ref[:, S5_CH + j * LANES:S5_CH + (j + 1) * LANES], (NSLAB, LANES)) for j in js]

        def step(t, carry):
            hr, hi = carry
            rows = pl.ds(t, NSLAB, stride=pitch)
            nr, ni = [], []
            for q, j in enumerate(js):
                r = lr[q] * hr[q] - li[q] * hi[q] + sc[j, rows, :]
                m = lr[q] * hi[q] + li[q] * hr[q] + sc[half + j, rows, :]
                sc[j, rows, :] = r
                sc[half + j, rows, :] = m
                nr.append(r)
                ni.append(m)
            return tuple(nr), tuple(ni)

        init = (tuple(h_scr[:, j * LANES:(j + 1) * LANES] for j in js),
                tuple(h_scr[:, S5_CH + j * LANES:S5_CH + (j + 1) * LANES] for j in js))
        hr, hi = lax.fori_loop(0, lt, step, init, unroll=2)
        for q, j in enumerate(js):
            h_scr[:, j * LANES:(j + 1) * LANES] = hr[q]
            h_scr[:, S5_CH + j * LANES:S5_CH + (j + 1) * LANES] = hi[q]
    ys = [[] for _ in range(NSLAB)]
    for k in range(S5_NBLK):
        for b in range(NSLAB):
            rows = slice(b * pitch, b * pitch + lt)
            hk = jnp.concatenate([sc[k * spb + q, rows, :] for q in range(spb)]
                                 + [sc[half + k * spb + q, rows, :] for q in range(spb)], axis=1)
            ys[b].append(_dg(hk.astype(BF16), cc_ref[k], NN))
    for b in range(NSLAB):
        o_ref[b] = _s5_gate(jnp.concatenate(ys[b], axis=1), u_ref[b], d_ref, wglu_ref, bglu_ref)

    @pl.when(i == pl.num_programs(0) - 1)
    def _():
        hout_ref[...] = h_scr[...]


def _s5_prompt(p3, geo, prm, li):
    lt = _div_tile(geo.tp, S5_TILE, 16)
    pitch = lt + SUB
    ucol = p3.shape[-1] // S5_WIDTH - 1
    fix = lambda i: (0, 0)
    return pl.pallas_call(
        functools.partial(_s5_prompt_kernel, lt=lt, pitch=pitch, padf=geo.padf),
        grid=(geo.tp // lt,),
        in_specs=[pl.BlockSpec((NSLAB, lt, S5_WIDTH), lambda i: (0, i, ucol))] + _s5_specs(fix),
        out_specs=[pl.BlockSpec((NSLAB, lt, S5_WIDTH), lambda i: (0, i, 0)), pl.BlockSpec((NSLAB, 2 * S5_CH), fix)],
        out_shape=[jax.ShapeDtypeStruct((NSLAB, geo.slab, S5_WIDTH), F32), jax.ShapeDtypeStruct((NSLAB, 2 * S5_CH), F32)],
        scratch_shapes=[pltpu.VMEM((S5_NJ, NSLAB * pitch, LANES), F32), pltpu.VMEM((NSLAB, 2 * S5_CH), F32)],
        compiler_params=_cp("arbitrary"),
        name="s5",
    )(p3, *_s5_params(prm, li))


def _s5_step_kernel(u_ref, bb_ref, cc_ref, lam_ref, d_ref, wglu_ref, bglu_ref, h0_ref, o_ref, hout_ref):
    u = u_ref[...]
    ub = u.astype(BF16)
    bu = [_dg(ub[:, k * LANES:(k + 1) * LANES], bb_ref[k], NN) for k in range(S5_NBLK)]
    bu_re = jnp.concatenate([x[:, :S5_BCH] for x in bu], axis=1)
    bu_im = jnp.concatenate([x[:, S5_BCH:] for x in bu], axis=1)
    h0 = h0_ref[...]
    lr, li = lam_ref[:, :S5_CH], lam_ref[:, S5_CH:]
    hr, hi = h0[:, :S5_CH], h0[:, S5_CH:]
    nr = lr * hr - li * hi + bu_re
    ni = lr * hi + li * hr + bu_im
    hout_ref[...] = jnp.concatenate([nr, ni], axis=1)
    ys = []
    for k in range(S5_NBLK):
        sl = slice(k * S5_BCH, (k + 1) * S5_BCH)
        ys.append(_dg(jnp.concatenate([nr[:, sl], ni[:, sl]], axis=1).astype(BF16), cc_ref[k], NN))
    o_ref[...] = _s5_gate(jnp.concatenate(ys, axis=1), u, d_ref, wglu_ref, bglu_ref)


def _s5_sample(p3, geo, h0, prm, li, out_prev):
    rows = geo.sps
    sb = geo.sample_blk(rows)
    ucol = p3.shape[-1] // S5_WIDTH - 1
    fix = lambda k: (0, 0)
    st = pl.BlockSpec((rows, 2 * S5_CH), lambda k: (k, 0))
    return pl.pallas_call(
        functools.partial(_drop_arg, _s5_step_kernel, 8),
        grid=(NSLAB,),
        in_specs=[pl.BlockSpec((None, rows, S5_WIDTH), lambda k: sb(k) + (ucol,))] + _s5_specs(fix)
        + [st, pl.BlockSpec(memory_space=pl.ANY)],
        out_specs=[pl.BlockSpec((None, rows, S5_WIDTH), lambda k: sb(k) + (0,)), st],
        out_shape=[jax.ShapeDtypeStruct(out_prev.shape, F32), jax.ShapeDtypeStruct(h0.shape, F32)],
        input_output_aliases={8: 0},
        compiler_params=_cp("parallel"),
        name="s5_step",
    )(p3, *_s5_params(prm, li), h0, out_prev)


def _head_sum(x, ones_bd):
    ones = ones_bd.astype(BF16)
    hi = x.astype(BF16)
    lo = (x - hi.astype(F32)).astype(BF16)
    return _dg(hi, ones, NN) + _dg(lo, ones, NN)


def _rwkv_prep_body(pr, prev, mu_ref, w0_ref, w2_ref, a0_ref, a2_ref, g2_ref, kk_ref, ka_ref, ones_ref,
                    r_ref, lw_ref, k_ref, v_ref, a_ref, b_ref, g_ref):
    W = RWKV_WIDTH
    xm = pr + (prev - pr) * mu_ref[...]
    r = xm[:, :W]
    k = xm[:, W:2 * W]
    v = xm[:, 2 * W:3 * W]
    xwa = xm[:, 3 * W:3 * W + LORA_W + LORA_A]
    xg = xm[:, 3 * W + LORA_W + LORA_A:]
    wl = w0_ref[...] + _bdot(jnp.tanh(xwa), w2_ref[...])
    sp = jnp.maximum(-wl, 0.0) + jnp.log(1.0 + jnp.exp(-jnp.abs(wl)))
    w = -sp - 0.5
    lw = -jnp.exp(w)
    a = jax.nn.sigmoid(a0_ref[...] + _bdot(xwa, a2_ref[...]))
    g = _bdot(jax.nn.sigmoid(xg), g2_ref[...])
    kk = k * kk_ref[...]
    nrm = jnp.sqrt(_head_sum(kk * kk, ones_ref[...]))
    kk = kk / jnp.maximum(nrm, 1e-12)
    k2 = k * (1.0 + (a - 1.0) * ka_ref[...])
    r_ref[...] = r
    lw_ref[...] = lw
    k_ref[...] = k2
    v_ref[...] = v
    a_ref[...] = -kk
    b_ref[...] = kk * a
    g_ref[...] = g


def _rwkv_prep_prompt_kernel(pr_ref, prev8_ref, *rest, tm, padf):
    pr = pr_ref[...]
    t = lax.broadcasted_iota(jnp.int32, (tm, 1), 0)
    last = jnp.broadcast_to(prev8_ref[SUB - 1:SUB, :], pr.shape)
    prev = jnp.where(t == 0, last, pltpu.roll(pr, 1, 0))
    first = (t + pl.program_id(1) * tm) == padf
    prev = jnp.where(first, 0.0, prev)
    _rwkv_prep_body(pr, prev, *rest)


def _rwkv_prep_sample_kernel(pr_ref, prev_ref, *rest):
    _rwkv_prep_body(pr_ref[...], prev_ref[...], *rest[:9], *rest[9 + 7:])


def _rwkv_prep(p3, geo, shift_s, prm, li):
    W = RWKV_WIDTH
    zeros64 = jnp.zeros((LORA_W, W), F32)
    w2p = jnp.concatenate([prm['rwkv_w2'][li], zeros64], 0).astype(BF16)
    a2p = jnp.concatenate([zeros64, prm['rwkv_a2'][li]], 0).astype(BF16)
    hid = jnp.arange(W) // RWKV_HEAD
    ones_bd = (hid[:, None] == hid[None, :]).astype(F32)
    consts = [prm['rwkv_mu'][li].reshape(1, -1), prm['rwkv_w0'][li].reshape(1, -1), w2p,
              prm['rwkv_a0'][li].reshape(1, -1), a2p, prm['rwkv_g2'][li].astype(BF16),
              prm['rwkv_k_k'][li].reshape(1, -1), prm['rwkv_k_a'][li].reshape(1, -1), ones_bd]
    outs_shape = [jax.ShapeDtypeStruct((NSLAB, geo.slab, W), F32)] * 7
    tm = _div_tile(geo.tp, SEQ_TILE)
    outs = pl.pallas_call(
        functools.partial(_rwkv_prep_prompt_kernel, tm=tm, padf=geo.padf),
        grid=(NSLAB, geo.tp // tm),
        in_specs=[pl.BlockSpec((None, tm, RWKV_PROJ), lambda b, c: (b, c, 0)),
                  pl.BlockSpec((None, SUB, RWKV_PROJ), lambda b, c: (b, jnp.maximum(c * (tm // SUB) - 1, 0), 0))]
        + [pl.BlockSpec(x.shape, lambda b, c: (0, 0)) for x in consts],
        out_specs=[pl.BlockSpec((None, tm, W), lambda b, c: (b, c, 0))] * 7,
        out_shape=outs_shape,
        compiler_params=_cp("parallel", "parallel"),
        name="rwkv_prep_prompt",
    )(p3, p3, *consts)
    rows = geo.sps
    sb = geo.sample_blk(rows)
    nin = 2 + len(consts)
    outs = pl.pallas_call(
        _rwkv_prep_sample_kernel,
        grid=(NSLAB,),
        in_specs=[pl.BlockSpec((None, rows, RWKV_PROJ), lambda k: sb(k) + (0,)),
                  pl.BlockSpec((rows, RWKV_PROJ), lambda k: (k, 0))]
        + [pl.BlockSpec(x.shape, lambda k: (0, 0)) for x in consts] + [pl.BlockSpec(memory_space=pl.ANY)] * 7,
        out_specs=[pl.BlockSpec((None, rows, W), lambda k: sb(k) + (0,))] * 7,
        out_shape=outs_shape,
        input_output_aliases={nin + j: j for j in range(7)},
        compiler_params=_cp("parallel"),
        name="rwkv_prep_sample",
    )(p3, shift_s, *consts, *outs)
    return outs


def _wkv_chunk_kernel(r_ref, lw_ref, k_ref, v_ref, a_ref, b_ref, y_ref, gout_ref, g_scr, *, padf):
    c = pl.program_id(1)
    L, HG = WKV_L, WKV_HG
    NG = RWKV_HEADS // HG
    WD = HG * RWKV_HEAD
    M = HG * L

    @pl.when(c == 0)
    def _():
        g_scr[...] = jnp.zeros_like(g_scr)

    real = (lax.broadcasted_iota(jnp.int32, (L, 1), 0) + c * L) >= padf

    def ld(ref):
        return jnp.where(real, jnp.concatenate([ref[i] for i in range(WKV_NB)], axis=1), 0.0)

    r, lw, k, v, a, b = ld(r_ref), ld(lw_ref), ld(k_ref), ld(v_ref), ld(a_ref), ld(b_ref)
    ti = lax.broadcasted_iota(jnp.int32, (L, L), 0)
    tj = lax.broadcasted_iota(jnp.int32, (L, L), 1)
    tri = jnp.where(ti >= tj, 1.0, 0.0)
    cum = _dot3(tri, lw)
    dec = jnp.exp(cum)
    inv = jnp.exp(-cum)
    at = a * jnp.exp(cum - lw)
    bt = b * inv
    kt = k * inv
    rt = r * dec
    rh = lax.broadcasted_iota(jnp.int32, (M, WD), 0) // L
    lh = lax.broadcasted_iota(jnp.int32, (M, WD), 1) // RWKV_HEAD
    bd = rh == lh
    mi = lax.broadcasted_iota(jnp.int32, (M, M), 0)
    mj = lax.broadcasted_iota(jnp.int32, (M, M), 1)
    same = (mi // L) == (mj // L)
    m_strict = jnp.logical_and(same, (mj % L) < (mi % L))
    m_incl = jnp.logical_and(same, (mj % L) <= (mi % L))
    eye = jnp.where(mi == mj, 1.0, 0.0)

    def stack(x):
        return jnp.where(bd, jnp.concatenate([x] * HG, axis=0), 0.0)

    chains = range(WKV_NB * NG)
    sls = [slice(i * WD, (i + 1) * WD) for i in chains]
    a_s = [stack(at[:, sl]).astype(BF16) for sl in sls]
    r_s = [stack(rt[:, sl]).astype(BF16) for sl in sls]
    v_s = [stack(v[:, sl]).astype(BF16) for sl in sls]
    bk_s = [jnp.concatenate([stack(bt[:, sl]), stack(kt[:, sl])], axis=0).astype(BF16) for sl in sls]
    g = [g_scr[i] for i in chains]
    gb = [x.astype(BF16) for x in g]
    prod = [_dg(jnp.concatenate([a_s[i], r_s[i]], axis=0), bk_s[i], NT) for i in chains]
    ab = [jnp.where(m_strict, p[:M, :M], 0.0) for p in prod]
    ak = [jnp.where(m_strict, p[:M, M:], 0.0).astype(BF16) for p in prod]
    rb = [jnp.where(m_incl, p[M:, :M], 0.0).astype(BF16) for p in prod]
    rk = [jnp.where(m_incl, p[M:, M:], 0.0).astype(BF16) for p in prod]
    tinv = [eye + x for x in ab]
    pw = ab
    for _ in range(int(math.log2(L)) - 1):
        pw = [_dotc(x, x) for x in pw]
        tinv = [t + _dotc(t, x) for t, x in zip(tinv, pw)]
    z = [_dg(a_s[i], gb[i], NT) + _dg(ak[i], v_s[i], NN) for i in chains]
    u = [_dotc(tinv[i], z[i]).astype(BF16) for i in chains]
    ybd = [_dg(r_s[i], gb[i], NT) + _dg(rb[i], u[i], NN) + _dg(rk[i], v_s[i], NN) for i in chains]
    upd = [_dg(jnp.concatenate([u[i], v_s[i]], axis=0), bk_s[i], TN) for i in chains]
    g_last = []
    for i in chains:
        y = ybd[i][0:L]
        for h in range(1, HG):
            y = y + ybd[i][h * L:(h + 1) * L]
        y_ref[i // NG, :, (i % NG) * WD:(i % NG + 1) * WD] = y
        g_new = (g[i] + upd[i]) * dec[L - 1:L, sls[i]]
        g_scr[i] = g_new
        g_last.append(g_new)

    @pl.when(c == pl.num_programs(1) - 1)
    def _():
        for grp, g_new in enumerate(g_last):
            gout_ref[grp // NG, grp % NG] = g_new


def _wkv_prompt(seq, geo):
    L = WKV_L
    NG = RWKV_HEADS // WKV_HG
    WD = WKV_HG * RWKV_HEAD
    spec = pl.BlockSpec((WKV_NB, L, RWKV_WIDTH), lambda b, c: (b, c, 0))
    y, g = pl.pallas_call(
        functools.partial(_wkv_chunk_kernel, padf=geo.padf),
        grid=(NSLAB // WKV_NB, geo.tp // L),
        in_specs=[spec] * 6,
        out_specs=[spec, pl.BlockSpec((WKV_NB, NG, WD, WD), lambda b, c: (b, 0, 0, 0))],
        out_shape=[jax.ShapeDtypeStruct((NSLAB, geo.slab, RWKV_WIDTH), F32),
                   jax.ShapeDtypeStruct((NSLAB, NG, WD, WD), F32)],
        scratch_shapes=[pltpu.VMEM((WKV_NB * NG, WD, WD), F32)],
        compiler_params=_cp("parallel", "arbitrary"),
        name="wkv_chunk",
    )(*seq)
    g = g.reshape(NSLAB, NG, WKV_HG, RWKV_HEAD, WKV_HG, RWKV_HEAD)
    idx = jnp.arange(WKV_HG)
    s = g[:, :, idx, :, idx, :]
    s = jnp.transpose(s, (1, 2, 0, 3, 4)).reshape(NSLAB, RWKV_HEADS, RWKV_HEAD, RWKV_HEAD)
    return y, s


def _wkv_step_kernel(r_ref, lw_ref, k_ref, v_ref, a_ref, b_ref, s_ref, y_ref, sout_ref):
    H, D = RWKV_HEADS, RWKV_HEAD
    ri = lax.broadcasted_iota(jnp.int32, (H * D, D), 0) % D
    ci = lax.broadcasted_iota(jnp.int32, (H * D, D), 1)
    eye = jnp.where(ri == ci, 1.0, 0.0)

    def expand(x):
        return jnp.concatenate([jnp.broadcast_to(x[:, h * D:(h + 1) * D], (D, D)) for h in range(H)], axis=0)

    for j in range(SUB):
        row = lambda ref: ref[j:j + 1, :]
        s = s_ref[j].reshape(H * D, D)
        w_e = jnp.exp(expand(row(lw_ref)))
        v_col = jnp.sum(expand(row(v_ref)) * eye, axis=1, keepdims=True)
        sa = jnp.sum(s * expand(row(a_ref)), axis=1, keepdims=True)
        s_new = s * w_e + sa * expand(row(b_ref)) + v_col * expand(row(k_ref))
        y_col = jnp.sum(s_new * expand(row(r_ref)), axis=1, keepdims=True)
        ye = y_col * eye
        y_row = jnp.concatenate([jnp.sum(ye[h * D:(h + 1) * D], axis=0, keepdims=True) for h in range(H)], axis=1)
        y_ref[j:j + 1, :] = y_row
        sout_ref[j] = s_new.reshape(H, D, D)


def _wkv_sample(seq, geo, s0, y_prev):
    sb = geo.sample_blk(SUB)
    spec = pl.BlockSpec((None, SUB, RWKV_WIDTH), lambda k: sb(k) + (0,))
    sspec = pl.BlockSpec((SUB, RWKV_HEADS, RWKV_HEAD, RWKV_HEAD), lambda k: (k, 0, 0, 0))
    return pl.pallas_call(
        functools.partial(_drop_arg, _wkv_step_kernel, 7),
        grid=(geo.ns // SUB,),
        in_specs=[spec] * 6 + [sspec, pl.BlockSpec(memory_space=pl.ANY)],
        out_specs=[spec, sspec],
        out_shape=[jax.ShapeDtypeStruct(y_prev.shape, F32), jax.ShapeDtypeStruct(s0.shape, F32)],
        input_output_aliases={7: 0},
        compiler_params=_cp("parallel"),
        name="wkv_step",
    )(*seq, s0, y_prev)


def _rwkv_post_kernel(y_ref, r_ref, k_ref, v_ref, g_ref, gng_ref, gnb_ref, rk_ref, ones_ref, o_ref):
    y = y_ref[...]
    ones = ones_ref[...]
    mu = _head_sum(y, ones) * (1.0 / RWKV_HEAD)
    yc = y - mu
    var = _head_sum(yc * yc, ones) * (1.0 / RWKV_HEAD)
    yn = yc * lax.rsqrt(var + RWKV_GN_EPS) * gng_ref[...] + gnb_ref[...]
    v = v_ref[...]
    bonus = _head_sum(r_ref[...] * k_ref[...] * rk_ref[...], ones) * v
    o_ref[...] = (yn + bonus) * g_ref[...]


def _rwkv_post(y, r, k, v, g, prm, li, tm):
    n = y.shape[0]
    W = RWKV_WIDTH
    hid = jnp.arange(W) // RWKV_HEAD
    ones_bd = (hid[:, None] == hid[None, :]).astype(F32)
    row = pl.BlockSpec((tm, W), lambda i: (i, 0))
    vec = pl.BlockSpec((1, W), lambda i: (0, 0))
    return pl.pallas_call(
        _rwkv_post_kernel,
        grid=(n // tm,),
        in_specs=[row] * 5 + [vec] * 3 + [pl.BlockSpec((W, W), lambda i: (0, 0))],
        out_specs=row,
        out_shape=jax.ShapeDtypeStruct((n, W), F32),
        compiler_params=_cp("parallel"),
        name="rwkv_post",
    )(y, r, k, v, g, prm['rwkv_gn_g'][li].reshape(1, -1), prm['rwkv_gn_b'][li].reshape(1, -1),
      prm['rwkv_r_k'][li].reshape(1, -1), ones_bd)


def _mla_prep_kernel(ckv_ref, qa_ref, kr_ref, cos_ref, sin_ref, qn_ref, kvn_ref, wqb_ref, wuk_ref,
                     qlat_ref, qrope_ref, c_ref, kr8_ref):
    qa = qa_ref[:, :Q_LORA]
    qa = qa * lax.rsqrt(jnp.mean(qa * qa, -1, keepdims=True) + 1e-6) * qn_ref[...]
    q = _bdot(qa, wqb_ref[...])
    nope_w = MLA_HEADS * QK_NOPE
    cos, sin = cos_ref[...], sin_ref[...]
    qrope_ref[...] = _rope(q[:, nope_w:], cos, sin, QK_ROPE // 2)
    hp = LANES // QK_NOPE
    for k in range(MLA_HEADS // hp):
        qlat_ref[:, k * hp * KV_LORA:(k + 1) * hp * KV_LORA] = _bdot(q[:, k * LANES:(k + 1) * LANES], wuk_ref[k])
    ckv = ckv_ref[...]
    c_ref[...] = ckv * lax.rsqrt(jnp.mean(ckv * ckv, -1, keepdims=True) + 1e-6) * kvn_ref[...]
    kr8_ref[...] = _rope(kr_ref[...], cos, sin, QK_ROPE // 2)


def _mla_prep(p, geo, pos_slab, prm, li):
    n = p.shape[0]
    tm, tps = geo.tm, geo.tps
    rw = MLA_HEADS * QK_ROPE
    cos, sin = _rope_tables(pos_slab, QK_ROPE, MLA_HEADS)
    wqb = prm['mla_w_qb'][li].reshape(Q_LORA, MLA_HEADS, QK_NOPE + QK_ROPE)
    wqb = jnp.concatenate([wqb[:, :, :QK_NOPE].reshape(Q_LORA, -1), wqb[:, :, QK_NOPE:].reshape(Q_LORA, -1)], 1)
    eye = jnp.eye(MLA_HEADS, dtype=F32)
    wuk = jnp.einsum('chn,hg->hngc', prm['mla_w_uk'][li], eye).reshape(MLA_HEADS * QK_NOPE, MLA_HEADS * KV_LORA)
    hp = LANES // QK_NOPE
    wuk = jnp.stack([wuk[k * LANES:(k + 1) * LANES, k * hp * KV_LORA:(k + 1) * hp * KV_LORA]
                     for k in range(MLA_HEADS // hp)])
    fix = lambda i: (0, 0)
    c0 = RWKV_PROJ // KV_LORA
    q0 = (RWKV_PROJ + KV_LORA) // 512
    k0 = (RWKV_PROJ + KV_LORA + 512) // rw
    tab = pl.BlockSpec((tm, rw), lambda i: (i % tps, 0))
    return pl.pallas_call(
        _mla_prep_kernel,
        grid=(n // tm,),
        in_specs=[pl.BlockSpec((tm, KV_LORA), lambda i: (i, c0)), pl.BlockSpec((tm, 512), lambda i: (i, q0)),
                  pl.BlockSpec((tm, rw), lambda i: (i, k0)), tab, tab,
                  pl.BlockSpec((1, Q_LORA), fix), pl.BlockSpec((1, KV_LORA), fix),
                  pl.BlockSpec(wqb.shape, fix), pl.BlockSpec(wuk.shape, lambda i: (0, 0, 0))],
        out_specs=[pl.BlockSpec((tm, MLA_HEADS * KV_LORA), lambda i: (i, 0)), pl.BlockSpec((tm, rw), lambda i: (i, 0)),
                   pl.BlockSpec((tm, KV_LORA), lambda i: (i, 0)), pl.BlockSpec((tm, rw), lambda i: (i, 0))],
        out_shape=[jax.ShapeDtypeStruct((n, MLA_HEADS * KV_LORA), F32), jax.ShapeDtypeStruct((n, rw), F32),
                   jax.ShapeDtypeStruct((n, KV_LORA), F32), jax.ShapeDtypeStruct((n, rw), F32)],
        compiler_params=_cp("parallel"),
        name="mla_prep",
    )(p, p, p, cos, sin, prm['mla_q_norm'][li].reshape(1, -1), prm['mla_kv_norm'][li].reshape(1, -1),
      wqb.astype(BF16), wuk.astype(BF16))


def _wuv_blockdiag(w_uv):
    eye = jnp.eye(MLA_HEADS, dtype=F32)
    return jnp.einsum('chv,hg->hcgv', w_uv, eye).reshape(MLA_HEADS * KV_LORA, MLA_WIDTH).astype(BF16)


def _mla_attn_kernel(qi_ref, ki_ref, qlat_ref, qrope_ref, c_ref, kr_ref, wuv_ref, o_ref, qs_scr, qrs_scr, m_scr,
                     l_scr, acc_scr, *, lq, padf):
    qi, ki = qi_ref[pl.program_id(1)], ki_ref[pl.program_id(1)]
    H = MLA_HEADS

    @pl.when(ki == 0)
    def _():
        ql = qlat_ref[...]
        qs_scr[...] = jnp.concatenate([ql[:, h * KV_LORA:(h + 1) * KV_LORA] for h in range(H)], axis=0).astype(BF16)
        qr = qrope_ref[...].astype(BF16)
        lane = lax.broadcasted_iota(jnp.int32, qr.shape, 1) // QK_ROPE
        qrs_scr[...] = jnp.concatenate([jnp.where(lane == h, qr, jnp.zeros_like(qr)) for h in range(H)], axis=0)
        m_scr[...] = jnp.full_like(m_scr, NEG)
        l_scr[...] = jnp.zeros_like(l_scr)
        acc_scr[...] = jnp.zeros_like(acc_scr)

    def tile(kind):
        c2 = MLA_SCALE * math.log2(math.e)
        cb = c_ref[...].astype(BF16)
        st = _dg(cb, qs_scr[...], NT) + _dg(kr_ref[...].astype(BF16), qrs_scr[...], NT)
        if kind == "diagonal":
            kpos = ki * lq + lax.broadcasted_iota(jnp.int32, st.shape, 0)
            qpos = qi * lq + lax.broadcasted_iota(jnp.int32, st.shape, 1) % lq
            st = jnp.where(jnp.logical_and(kpos <= qpos, kpos >= padf), st, NEG)
        elif kind == "first":
            st = jnp.where(lax.broadcasted_iota(jnp.int32, (lq, 1), 0) >= padf, st, NEG)
        m_old = m_scr[...]
        m_new = jnp.maximum(m_old, jnp.max(st, axis=0, keepdims=True))
        alpha = jnp.exp2((m_old - m_new) * c2)
        pt = jnp.exp2((st - m_new) * c2)
        l_scr[...] = alpha * l_scr[...] + jnp.sum(pt, axis=0, keepdims=True)
        acc_scr[...] = alpha * acc_scr[...] + _dg(cb, pt.astype(BF16), TN)
        m_scr[...] = m_new

    assert padf <= lq
    pl.when(ki == qi)(functools.partial(tile, "diagonal"))
    pl.when(jnp.logical_and(ki < qi, ki == 0))(functools.partial(tile, "first"))
    pl.when(jnp.logical_and(ki < qi, ki > 0))(functools.partial(tile, "plain"))

    @pl.when(ki == qi)
    def _():
        ot = (acc_scr[...] * (1.0 / l_scr[...])).astype(BF16)
        y = jnp.zeros(o_ref.shape, F32)
        for h in range(H):
            y = y + _dg(ot[:, h * lq:(h + 1) * lq], wuv_ref[h], TN)
        o_ref[...] = y


def _mla_attn_prompt(qlat3, qrope3, c3, kr3, w_uv, geo):
    lq = _div_tile(geo.tp, SEQ_TILE)
    nq = geo.tp // lq
    wuv = _wuv_blockdiag(w_uv).reshape(MLA_HEADS, KV_LORA, MLA_WIDTH)
    rw = MLA_HEADS * QK_ROPE
    pairs = [(qi, ki) for qi in range(nq) for ki in range(qi + 1)]
    qis = jnp.asarray([p[0] for p in pairs], jnp.int32)
    kis = jnp.asarray([p[1] for p in pairs], jnp.int32)
    qmap = lambda b, p, qr, kr: (b, qr[p], 0)
    kmap = lambda b, p, qr, kr: (b, kr[p], 0)
    grid_spec = pltpu.PrefetchScalarGridSpec(
        num_scalar_prefetch=2,
        grid=(NSLAB, len(pairs)),
        in_specs=[pl.BlockSpec((None, lq, MLA_HEADS * KV_LORA), qmap), pl.BlockSpec((None, lq, rw), qmap),
                  pl.BlockSpec((None, lq, KV_LORA), kmap), pl.BlockSpec((None, lq, rw), kmap),
                  pl.BlockSpec(wuv.shape, lambda b, p, qr, kr: (0, 0, 0))],
        out_specs=pl.BlockSpec((None, lq, MLA_WIDTH), qmap),
        scratch_shapes=[pltpu.VMEM((MLA_HEADS * lq, KV_LORA), BF16), pltpu.VMEM((MLA_HEADS * lq, rw), BF16),
                        pltpu.VMEM((1, MLA_HEADS * lq), F32), pltpu.VMEM((1, MLA_HEADS * lq), F32),
                        pltpu.VMEM((KV_LORA, MLA_HEADS * lq), F32)],
    )
    return pl.pallas_call(
        functools.partial(_mla_attn_kernel, lq=lq, padf=geo.padf),
        grid_spec=grid_spec,
        out_shape=jax.ShapeDtypeStruct((NSLAB, geo.slab, MLA_WIDTH), F32),
        compiler_params=_cp("parallel", "arbitrary"),
        name="mla_attn",
    )(qis, kis, qlat3, qrope3, c3, kr3, wuv)


def _mla_decode_kernel(pt_ref, qlat_ref, qrope_ref, cnew_ref, krnew_ref, kv_hbm, kr_hbm, tile_ref, wuv_ref,
                       yprev_ref, o_ref, kv_buf, kr_buf, kb_scr, s_scr, sem, *, li, n_pages, ch):
    del yprev_ref
    b = pl.program_id(0)
    slot = b % 2
    H = MLA_HEADS

    def page_copies(page, sl, pg):
        return (pltpu.make_async_copy(kv_hbm.at[page, li], kv_buf.at[sl, pg], sem.at[0, sl]),
                pltpu.make_async_copy(kr_hbm.at[page, li], kr_buf.at[sl, pg], sem.at[1, sl]))

    def start_fetch(bb, sl):
        def body(pg, carry):
            for cp in page_copies(pt_ref[bb, pg], sl, pg):
                cp.start()
            return carry
        lax.fori_loop(0, n_pages, body, 0)

    def wait_fetch(sl):
        for pg in range(n_pages):
            for cp in page_copies(0, sl, pg):
                cp.wait()

    @pl.when(b == 0)
    def _():
        start_fetch(0, 0)

    nxt = (b + 1) % pl.num_programs(0)

    j = b % SUB
    rid = lax.broadcasted_iota(jnp.int32, (SUB, 1), 0)

    def pick(ref):
        return jnp.sum(jnp.where(rid == j, ref[...], 0.0), axis=0, keepdims=True)

    ql = pick(qlat_ref)
    qs = jnp.concatenate([ql[:, h * KV_LORA:(h + 1) * KV_LORA] for h in range(H)], axis=0).astype(BF16)
    qr = jnp.broadcast_to(pick(qrope_ref), (H, H * QK_ROPE))
    hl = lax.broadcasted_iota(jnp.int32, qr.shape, 1) // QK_ROPE
    hr = lax.broadcasted_iota(jnp.int32, qr.shape, 0)
    qrs = jnp.where(hl == hr, qr, 0.0).astype(BF16)
    qr32 = _dg(qrs, tile_ref[...], NT).astype(BF16)
    cn = _rb(pick(cnew_ref))
    kn = _rb(pick(krnew_ref))
    s_self = (jnp.sum(qs.astype(F32) * cn, axis=1, keepdims=True)
              + jnp.sum(qrs.astype(F32) * kn, axis=1, keepdims=True)) * MLA_SCALE

    wait_fetch(slot)
    cw = ch * PAGE_SIZE
    for ci in range(n_pages // ch):
        for pg in range(ci * ch, (ci + 1) * ch):
            for cp in page_copies(pt_ref[nxt, pg], 1 - slot, pg):
                cp.start()
        kc = kv_buf[slot, ci * ch:(ci + 1) * ch].reshape(cw, KV_LORA).astype(BF16)
        kb_scr[ci * cw:(ci + 1) * cw, :] = kc
        krc = jnp.concatenate([kr_buf[slot, pg] for pg in range(ci * ch, (ci + 1) * ch)], axis=1).astype(BF16)
        s_scr[:, ci * cw:(ci + 1) * cw] = (_dg(qs, kc, NT) + _dg(qr32, krc, NN)) * MLA_SCALE
    s = s_scr[...]
    m = jnp.maximum(jnp.max(s, axis=1, keepdims=True), s_self)
    e = jnp.exp(s - m)
    e_self = jnp.exp(s_self - m)
    rl = 1.0 / (jnp.sum(e, axis=1, keepdims=True) + e_self)
    p = (e * rl).astype(BF16)
    acc = _rb(e_self * rl) * cn
    for ci in range(n_pages // ch):
        acc = acc + _dg(p[:, ci * cw:(ci + 1) * cw], kb_scr[ci * cw:(ci + 1) * cw, :], NN)
    ol = jnp.concatenate([acc[h:h + 1] for h in range(H)], axis=1)
    y = _bdot(jnp.broadcast_to(ol, (SUB, ol.shape[1])), wuv_ref[...])
    o_ref[pl.ds(j, 1), :] = y[0:1]

    @pl.when(b == pl.num_programs(0) - 1)
    def _():
        wait_fetch(1 - slot)


def _mla_decode(qlat, qrope, c, kr8, cache_kv, cache_kr, page_table, w_uv, li, y_prev, geo):
    n = qlat.shape[0]
    n_pages = page_table.shape[1]
    ch = _div_tile(n_pages, 8, 1)
    rw = MLA_HEADS * QK_ROPE
    wuv = _wuv_blockdiag(w_uv)
    tile = jnp.tile(jnp.eye(QK_ROPE, dtype=F32), (1, MLA_HEADS)).astype(BF16)

    def rowmap(b, pt):
        return ((b // geo.sps) * geo.slab + geo.tp + b % geo.sps) // SUB, 0

    rowblk = lambda w: pl.BlockSpec((SUB, w), rowmap)
    grid_spec = pltpu.PrefetchScalarGridSpec(
        num_scalar_prefetch=1,
        grid=(geo.ns,),
        in_specs=[rowblk(MLA_HEADS * KV_LORA), rowblk(rw), rowblk(KV_LORA), rowblk(rw),
                  pl.BlockSpec(memory_space=pl.ANY), pl.BlockSpec(memory_space=pl.ANY),
                  pl.BlockSpec(tile.shape, lambda b, pt: (0, 0)),
                  pl.BlockSpec(wuv.shape, lambda b, pt: (0, 0)),
                  pl.BlockSpec(memory_space=pl.ANY)],
        out_specs=rowblk(MLA_WIDTH),
        scratch_shapes=[pltpu.VMEM((2, n_pages, PAGE_SIZE, KV_LORA), F32),
                        pltpu.VMEM((2, n_pages, QK_ROPE, PAGE_SIZE), F32),
                        pltpu.VMEM((n_pages * PAGE_SIZE, KV_LORA), BF16),
                        pltpu.VMEM((MLA_HEADS, n_pages * PAGE_SIZE), F32),
                        pltpu.SemaphoreType.DMA((2, 2))],
    )
    return pl.pallas_call(
        functools.partial(_mla_decode_kernel, li=li, n_pages=n_pages, ch=ch),
        grid_spec=grid_spec,
        out_shape=jax.ShapeDtypeStruct((n, MLA_WIDTH), F32),
        input_output_aliases={9: 0},
        compiler_params=_cp("arbitrary"),
        name="mla_decode",
    )(page_table, qlat, qrope, c, kr8, cache_kv, jnp.swapaxes(cache_kr, 2, 3), tile, wuv, y_prev)


def _forward(x_prompt, x_sample, state_ret, state_s5_re, state_s5_im, state_wkv, state_shift, cache_kv,
             cache_krope, page_table, meta_tokens, prm):
    nb, seq, _ = x_prompt.shape
    assert nb == NSLAB and x_sample.shape[1] == 1
    ns = x_sample.shape[0]
    geo = _Geom(seq, ns, page_table.shape[1])
    n, tm, tp, padf, sps = geo.n, geo.tm, geo.tp, geo.padf, geo.sps

    meta = jnp.broadcast_to(meta_tokens[None], (nb, N_META, D_MODEL))
    x = jnp.concatenate([jnp.zeros((nb, padf, D_MODEL), F32), meta, x_prompt, x_sample.reshape(nb, sps, D_MODEL)], 1)
    x = x.reshape(n, D_MODEL)
    pos_slab = jnp.concatenate([jnp.arange(tp, dtype=jnp.int32) - padf, jnp.full((sps,), geo.past, jnp.int32)])

    ret_p, ret_s, s5_p, s5_s, wkv_p, wkv_s, sh_p, sh_s, kv_l, kr_l = [], [], [], [], [], [], [], [], [], []
    for layer in range(DEPTH):
        li = layer // 2
        if layer % 2 == 0:
            w_in = prm['ev_w_in'][li].astype(BF16)
            p = _matmul(x, w_in, tm, _div_tile(w_in.shape[1], 1024, 128))
            p3 = geo.v3(p)
            gng, gnb = prm['ret_gn_g'][li], prm['ret_gn_b'][li]
            ret, s_p = _retention_prompt(p3, geo, gng, gnb)
            ret, s_s = _retention_sample(p3, geo, state_ret[li], gng, gnb, ret)
            s5o, h_p = _s5_prompt(p3, geo, prm, li)
            h0 = jnp.concatenate([state_s5_re[li].reshape(ns, S5_CH), state_s5_im[li].reshape(ns, S5_CH)], 1)
            s5o, h_s = _s5_sample(p3, geo, h0, prm, li, s5o)
            ret_p.append(s_p)
            ret_s.append(s_s)
            s5_p.append(h_p)
            s5_s.append(h_s)
            x = _proj_ln(ret.reshape(n, -1), s5o.reshape(n, -1), x, prm['ev_w_out'][li], prm['ln1_g'][layer],
                         prm['ln1_b'][layer], tm)
        else:
            w = prm['od_w_in'][li]
            o1 = RWKV_PROJ
            o2 = o1 + Q_LORA
            o3 = o2 + KV_LORA
            w_in = jnp.concatenate([w[:, :o1], w[:, o2:o3], w[:, o1:o2], jnp.zeros((D_MODEL, 512 - Q_LORA), F32),
                                    jnp.tile(w[:, o3:], (1, MLA_HEADS))], 1).astype(BF16)
            p = _matmul(x, w_in, tm, _div_tile(w_in.shape[1], 1408, 128))
            p3 = geo.v3(p)
            sh_p.append(p3[:, tp - 1, :RWKV_PROJ])
            sh_s.append(p3[:, tp:, :RWKV_PROJ].reshape(ns, RWKV_PROJ))
            r, lw, k, v, a, b, g = _rwkv_prep(p3, geo, state_shift[li], prm, li)
            seqs = (r, lw, k, v, a, b)
            y, g_p = _wkv_prompt(seqs, geo)
            y, g_s = _wkv_sample(seqs, geo, state_wkv[li], y)
            wkv_p.append(g_p)
            wkv_s.append(g_s)
            flat = lambda t: t.reshape(n, t.shape[-1])
            y_c = _rwkv_post(flat(y), flat(r), flat(k), flat(v), flat(g), prm, li, tm)
            qlat, qrope, c, kr8 = _mla_prep(p, geo, pos_slab, prm, li)
            kv_l.append(geo.v3(c))
            kr_l.append(geo.v3(kr8)[:, :, :QK_ROPE])
            y_d = _mla_attn_prompt(geo.v3(qlat), geo.v3(qrope), geo.v3(c), geo.v3(kr8), prm['mla_w_uv'][li], geo)
            y_d = _mla_decode(qlat, qrope, c, kr8, cache_kv, cache_krope, page_table, prm['mla_w_uv'][li], li,
                              flat(y_d), geo)
            x = _proj_ln(y_c, y_d, x, prm['od_w_out'][li], prm['ln1_g'][layer], prm['ln1_b'][layer], tm)
        x = _moe_ln(x, prm, layer, tm)

    x3 = geo.v3(x)
    y_p = x3[:, padf + N_META:tp]
    y_s = x3[:, tp:].reshape(ns, 1, D_MODEL)

    def s5_state(hs, part):
        return jnp.stack([h[:, part * S5_CH:(part + 1) * S5_CH].reshape(-1, S5_GROUPS, S5_STATE) for h in hs])

    kv_p = jnp.stack([c[:, padf:tp] for c in kv_l], 1)
    kv_s = jnp.stack([c[:, tp:].reshape(ns, 1, KV_LORA) for c in kv_l], 1)
    kr_p = jnp.stack([c[:, padf:tp] for c in kr_l], 1)
    kr_s = jnp.stack([c[:, tp:].reshape(ns, 1, QK_ROPE) for c in kr_l], 1)
    return (y_p, y_s, jnp.stack(ret_p), jnp.stack(ret_s), s5_state(s5_p, 0), s5_state(s5_s, 0),
            s5_state(s5_p, 1), s5_state(s5_s, 1), jnp.stack(wkv_p), jnp.stack(wkv_s),
            jnp.stack(sh_p), jnp.stack(sh_s), kv_p, kv_s, kr_p, kr_s)


def kernel(x_prompt, x_sample, state_ret, state_s5_re, state_s5_im, state_wkv, state_shift, cache_kv, cache_krope, page_table, meta_tokens, ev_w_in, ret_gn_g, ret_gn_b, s5_a_re, s5_a_im, s5_log_dt, s5_b_re, s5_b_im, s5_c_re, s5_c_im, s5_d, s5_w_glu, s5_b_glu, ev_w_out, od_w_in, rwkv_mu, rwkv_w0, rwkv_w2, rwkv_a0, rwkv_a2, rwkv_g2, rwkv_k_k, rwkv_k_a, rwkv_r_k, rwkv_gn_g, rwkv_gn_b, mla_q_norm, mla_w_qb, mla_kv_norm, mla_w_uk, mla_w_uv, od_w_out, ln1_g, ln1_b, ln2_g, ln2_b, moe_w_coarse, moe_b_coarse, moe_w_fine, moe_b_fine, moe_w_gate, moe_w_up, moe_w_down):
    prm = dict(ev_w_in=ev_w_in, ret_gn_g=ret_gn_g, ret_gn_b=ret_gn_b, s5_a_re=s5_a_re, s5_a_im=s5_a_im,
               s5_log_dt=s5_log_dt, s5_b_re=s5_b_re, s5_b_im=s5_b_im, s5_c_re=s5_c_re, s5_c_im=s5_c_im,
               s5_d=s5_d, s5_w_glu=s5_w_glu, s5_b_glu=s5_b_glu, ev_w_out=ev_w_out,
               od_w_in=od_w_in, rwkv_mu=rwkv_mu, rwkv_w0=rwkv_w0, rwkv_w2=rwkv_w2, rwkv_a0=rwkv_a0,
               rwkv_a2=rwkv_a2, rwkv_g2=rwkv_g2, rwkv_k_k=rwkv_k_k, rwkv_k_a=rwkv_k_a, rwkv_r_k=rwkv_r_k,
               rwkv_gn_g=rwkv_gn_g, rwkv_gn_b=rwkv_gn_b, mla_q_norm=mla_q_norm, mla_w_qb=mla_w_qb,
               mla_kv_norm=mla_kv_norm, mla_w_uk=mla_w_uk, mla_w_uv=mla_w_uv, od_w_out=od_w_out,
               ln1_g=ln1_g, ln1_b=ln1_b, ln2_g=ln2_g, ln2_b=ln2_b, moe_w_coarse=moe_w_coarse,
               moe_b_coarse=moe_b_coarse, moe_w_fine=moe_w_fine, moe_b_fine=moe_b_fine,
               moe_w_gate=moe_w_gate, moe_w_up=moe_w_up, moe_w_down=moe_w_down)
    return _forward(x_prompt, x_sample, state_ret, state_s5_re, state_s5_im, state_wkv, state_shift, cache_kv,
                    cache_krope, page_table, meta_tokens, prm)
```

```python
import functools
import math

import jax
import jax.numpy as jnp
from jax import lax
from jax.experimental import pallas as pl
from jax.experimental.pallas import tpu as pltpu

F32 = jnp.float32
BF16 = jnp.bfloat16

D_MODEL = 1024
N_META = 16
PAGE_SIZE = 128
RET_HEADS, RET_DK, RET_DV = 4, 64, 128
S5_WIDTH, S5_GROUP, S5_STATE = 512, 16, 64
S5_GROUPS = S5_WIDTH // S5_GROUP
S5_CH = S5_GROUPS * S5_STATE
S5_GPB = 128 // S5_GROUP
S5_NBLK = S5_GROUPS // S5_GPB
S5_BCH = S5_GPB * S5_STATE
RWKV_HEAD, RWKV_WIDTH = 64, 512
RWKV_HEADS = RWKV_WIDTH // RWKV_HEAD
LORA_W, LORA_A, LORA_G = 64, 64, 128
RWKV_PROJ = 3 * RWKV_WIDTH + LORA_W + LORA_A + LORA_G
RWKV_GN_EPS = 64e-5
MLA_HEADS, Q_LORA, KV_LORA, QK_NOPE, QK_ROPE, V_HEAD = 8, 384, 256, 64, 32, 64
MLA_WIDTH = MLA_HEADS * V_HEAD
MLA_SCALE = (QK_NOPE + QK_ROPE) ** -0.5
ROPE_THETA = 10000.0
N_EGROUPS, E_PER_GROUP, D_EXPERT = 4, 4, 256
N_EXPERTS = N_EGROUPS * E_PER_GROUP
DEPTH = 4
DN_ALPHA = (2 * DEPTH) ** 0.25
NSLAB = 8
SUB = 8
LANES = 128

VMEM_LIMIT = 56 * 1024 * 1024
WKV_L = 32
WKV_HG = 4
WKV_NB = 8
ROW_TILE = 1056
SEQ_TILE = 416
S5_TILE = 80
NEG = -1e30

NT = (((1,), (1,)), ((), ()))
TN = (((0,), (0,)), ((), ()))
NN = (((1,), (0,)), ((), ()))


def _cp(*sem):
    return pltpu.CompilerParams(dimension_semantics=sem, vmem_limit_bytes=VMEM_LIMIT)


def _div_tile(n, cap, mult=8):
    best = None
    for d in range(mult, min(n, cap) + 1, mult):
        if n % d == 0:
            best = d
    assert best is not None, (n, cap, mult)
    return best


def _dg(a, b, dims):
    return lax.dot_general(a, b, dims, preferred_element_type=F32)


def _bdot(a, b, dims=NN):
    return _dg(a.astype(BF16), b.astype(BF16), dims)


def _dot3(a, b, dims=NN):
    ah = a.astype(BF16)
    al = (a - ah.astype(F32)).astype(BF16)
    bh = b.astype(BF16)
    bl = (b - bh.astype(F32)).astype(BF16)
    return _dg(ah, bh, dims) + (_dg(ah, bl, dims) + _dg(al, bh, dims))


def _dotc(a, b, dims=NN):
    return _bdot(a, b, dims)


def _rb(x):
    return x.astype(BF16).astype(F32)


def _layer_norm(z, g, b, eps=1e-5):
    mu = jnp.mean(z, -1, keepdims=True)
    zc = z - mu
    var = jnp.mean(zc * zc, -1, keepdims=True)
    return zc * lax.rsqrt(var + eps) * g + b


def _rope(x, cos, sin_signed, half):
    w = x.shape[-1]
    lane = lax.broadcasted_iota(jnp.int32, x.shape, 1)
    first = (lane % (2 * half)) < half
    other = jnp.where(first, pltpu.roll(x, w - half, 1), pltpu.roll(x, half, 1))
    return x * cos + other * sin_signed


def _rope_tables(pos, dim, reps):
    inv = ROPE_THETA ** (-jnp.arange(0, dim, 2, dtype=F32) / dim)
    ang = pos.astype(F32)[:, None] * inv[None, :]
    cos = jnp.cos(ang)
    sin = jnp.sin(ang)
    cos = jnp.tile(jnp.concatenate([cos, cos], -1), (1, reps))
    sin = jnp.tile(jnp.concatenate([-sin, sin], -1), (1, reps))
    return cos, sin


def _drop_arg(kern, idx, *refs):
    return kern(*(refs[:idx] + refs[idx + 1:]))


class _Geom:
    def __init__(self, seq, ns, n_pages):
        self.tq = seq + N_META
        self.padf = (-self.tq) % WKV_L
        self.tp = self.padf + self.tq
        assert ns % (NSLAB * SUB) == 0
        self.ns = ns
        self.sps = ns // NSLAB
        self.slab = self.tp + self.sps
        self.n = NSLAB * self.slab
        self.past = n_pages * PAGE_SIZE
        self.tm = _div_tile(self.slab, ROW_TILE)
        self.tps = self.slab // self.tm

    def v3(self, x):
        return x.reshape(NSLAB, self.slab, x.shape[-1])

    def sample_blk(self, rows):
        per = self.sps // rows
        return lambda k: (k // per, self.tp // rows + k % per)


def _mm_kernel(x_ref, w_ref, o_ref):
    o_ref[...] = _dg(x_ref[...].astype(BF16), w_ref[...], NN)


def _matmul(x, w, tm, tn):
    n, k = x.shape
    m = w.shape[1]
    return pl.pallas_call(
        _mm_kernel,
        grid=(n // tm, m // tn),
        in_specs=[pl.BlockSpec((tm, k), lambda i, j: (i, 0)), pl.BlockSpec((k, tn), lambda i, j: (0, j))],
        out_specs=pl.BlockSpec((tm, tn), lambda i, j: (i, j)),
        out_shape=jax.ShapeDtypeStruct((n, m), F32),
        compiler_params=_cp("parallel", "arbitrary"),
        name="proj_in",
    )(x, w)


def _proj_ln_kernel(a1_ref, a2_ref, x_ref, w1_ref, w2_ref, g_ref, b_ref, o_ref):
    mix = _dg(a1_ref[...].astype(BF16), w1_ref[...], NN) + _dg(a2_ref[...].astype(BF16), w2_ref[...], NN)
    z = DN_ALPHA * x_ref[...] + mix
    o_ref[...] = _layer_norm(z, g_ref[...], b_ref[...])


def _proj_ln(a1, a2, x, w_out, g, b, tm):
    n = x.shape[0]
    k1, k2 = a1.shape[1], a2.shape[1]
    w1 = w_out[:k1].astype(BF16)
    w2 = w_out[k1:].astype(BF16)
    row = lambda i: (i, 0)
    fix = lambda i: (0, 0)
    return pl.pallas_call(
        _proj_ln_kernel,
        grid=(n // tm,),
        in_specs=[pl.BlockSpec((tm, k1), row), pl.BlockSpec((tm, k2), row), pl.BlockSpec((tm, D_MODEL), row),
                  pl.BlockSpec((k1, D_MODEL), fix), pl.BlockSpec((k2, D_MODEL), fix),
                  pl.BlockSpec((1, D_MODEL), fix), pl.BlockSpec((1, D_MODEL), fix)],
        out_specs=pl.BlockSpec((tm, D_MODEL), row),
        out_shape=jax.ShapeDtypeStruct((n, D_MODEL), F32),
        compiler_params=_cp("parallel"),
        name="proj_out_ln",
    )(a1, a2, x, w1, w2, g.reshape(1, -1), b.reshape(1, -1))


MOE_EPS = 2
ROUTE0 = N_EGROUPS


def _route(lt):
    lane = lax.broadcasted_iota(jnp.int32, lt.shape, 1)
    lanef = lane.astype(F32)
    coarse = lane < N_EGROUPS
    lc = jnp.where(coarse, lt, NEG)
    m = jnp.max(lc, axis=1, keepdims=True)
    gsel = jnp.min(jnp.where(lc == m, lanef, 1e9), axis=1, keepdims=True)
    gate_c = 1.0 / jnp.sum(jnp.where(coarse, jnp.exp(lc - m), 0.0), axis=1, keepdims=True)
    grp = ((lane - ROUTE0) // E_PER_GROUP).astype(F32)
    lf = jnp.where(grp == gsel, lt, NEG)
    t1 = jnp.max(lf, axis=1, keepdims=True)
    i1 = jnp.min(jnp.where(lf == t1, lanef, 1e9), axis=1, keepdims=True)
    lf2 = jnp.where(lanef == i1, NEG, lf)
    t2 = jnp.max(lf2, axis=1, keepdims=True)
    i2 = jnp.min(jnp.where(lf2 == t2, lanef, 1e9), axis=1, keepdims=True)
    e2 = jnp.exp(t2 - t1)
    w1 = gate_c / (1.0 + e2)
    return jnp.where(lanef == i1, w1, 0.0) + jnp.where(lanef == i2, w1 * e2, 0.0)


def _moe_kernel(x_ref, wr_ref, br_ref, wgu_ref, wd_ref, g_ref, b_ref, o_ref, xb_scr, comb_scr, h_scr):
    e = pl.program_id(1)

    @pl.when(e == 0)
    def _():
        xb = x_ref[...].astype(BF16)
        xb_scr[...] = xb
        comb_scr[...] = _route(_dg(xb, wr_ref[...], NN) + br_ref[...])

    xb = xb_scr[...]
    comb = comb_scr[...]
    lane = lax.broadcasted_iota(jnp.int32, comb.shape, 1)
    gus = [_dg(xb, wgu_ref[q], NN) for q in range(MOE_EPS)]
    hb = []
    for q, gu in enumerate(gus):
        hg, hu = gu[:, :D_EXPERT], gu[:, D_EXPERT:]
        ce = jnp.sum(jnp.where(lane == e * MOE_EPS + q + ROUTE0, comb, 0.0), axis=1, keepdims=True)
        hb.append(((hg * jax.nn.sigmoid(hg)) * hu * ce).astype(BF16))
    for ee in range(N_EXPERTS // MOE_EPS):
        @pl.when(e == ee)
        def _():
            for q in range(MOE_EPS):
                x0 = (ee * MOE_EPS + q) * D_EXPERT
                h_scr[:, x0:x0 + D_EXPERT] = hb[q]

    @pl.when(e == pl.num_programs(1) - 1)
    def _():
        z = DN_ALPHA * x_ref[...] + _dg(h_scr[...], wd_ref[...], NN)
        o_ref[...] = _layer_norm(z, g_ref[...], b_ref[...])


def _moe_ln(x, prm, layer, tm):
    n = x.shape[0]
    wr = jnp.concatenate([prm['moe_w_coarse'][layer],
                          jnp.transpose(prm['moe_w_fine'][layer], (1, 0, 2)).reshape(D_MODEL, N_EXPERTS)], 1)
    wr = jnp.pad(wr, ((0, 0), (0, LANES - wr.shape[1]))).astype(BF16)
    br = jnp.pad(jnp.concatenate([prm['moe_b_coarse'][layer], prm['moe_b_fine'][layer].reshape(-1)]),
                 (0, LANES - N_EGROUPS - N_EXPERTS)).reshape(1, LANES)
    row = lambda i, e: (i, 0)
    fix = lambda i, e: (0, 0)
    exp3 = lambda i, e: (e, 0, 0)
    return pl.pallas_call(
        _moe_kernel,
        grid=(n // tm, N_EXPERTS // MOE_EPS),
        in_specs=[pl.BlockSpec((tm, D_MODEL), row), pl.BlockSpec((D_MODEL, LANES), fix), pl.BlockSpec((1, LANES), fix),
                  pl.BlockSpec((MOE_EPS, D_MODEL, 2 * D_EXPERT), exp3),
                  pl.BlockSpec((N_EXPERTS * D_EXPERT, D_MODEL), fix, pipeline_mode=pl.Buffered(1)),
                  pl.BlockSpec((1, D_MODEL), fix), pl.BlockSpec((1, D_MODEL), fix)],
        out_specs=pl.BlockSpec((tm, D_MODEL), row),
        out_shape=jax.ShapeDtypeStruct((n, D_MODEL), F32),
        scratch_shapes=[pltpu.VMEM((tm, D_MODEL), BF16), pltpu.VMEM((tm, LANES), F32),
                        pltpu.VMEM((tm, N_EXPERTS * D_EXPERT), BF16)],
        compiler_params=_cp("parallel", "arbitrary"),
        name="moe_ln",
    )(x, wr, br, jnp.concatenate([prm['moe_w_gate'][layer], prm['moe_w_up'][layer]], -1).astype(BF16),
      prm['moe_w_down'][layer].astype(BF16).reshape(N_EXPERTS * D_EXPERT, D_MODEL),
      prm['ln2_g'][layer].reshape(1, -1), prm['ln2_b'][layer].reshape(1, -1))


def _ret_log_g():
    return [math.log1p(-2.0 ** (-5.0 - h)) for h in range(RET_HEADS)]


def _ret_kernel(q_ref, k_ref, v_ref, gate_ref, cos_ref, sin_ref, dmask_ref, cdec_ref, kdec_ref, sdec_ref,
                gng_ref, gnb_ref, o_ref, sout_ref, s_scr, *, L, padf):
    c = pl.program_id(1)
    HK, HV = RET_HEADS * RET_DK, RET_HEADS * RET_DV

    @pl.when(c == 0)
    def _():
        s_scr[...] = jnp.zeros_like(s_scr)

    cos, sin = cos_ref[...], sin_ref[...]
    q = _rope(q_ref[...], cos, sin, RET_DK // 2) * RET_DK ** -0.5
    k = _rope(k_ref[...], cos, sin, RET_DK // 2)
    real = (lax.broadcasted_iota(jnp.int32, (L, 1), 0) + c * L) >= padf
    k = jnp.where(real, k, 0.0)
    v = jnp.where(real, v_ref[...], 0.0)
    gate = gate_ref[...]
    qb, kb, vb = q.astype(BF16), k.astype(BF16), v.astype(BF16)
    sbd = s_scr[...]
    cross = _dg(qb, sbd.astype(BF16), NN) * cdec_ref[...]
    lane = lax.broadcasted_iota(jnp.int32, (L, HK), 1)
    outs = []
    scs = [(_dg(jnp.where(lane // RET_DK == h, qb, jnp.zeros_like(qb)), kb, NT) * dmask_ref[h]).astype(BF16)
           for h in range(RET_HEADS)]
    for h in range(RET_HEADS):
        sl = slice(h * RET_DV, (h + 1) * RET_DV)
        o = _dg(scs[h], vb[:, sl], NN) + cross[:, sl]
        mu = jnp.mean(o, -1, keepdims=True)
        oc = o - mu
        var = jnp.mean(oc * oc, -1, keepdims=True)
        outs.append(oc * lax.rsqrt(var + 1e-5))
    on = jnp.concatenate(outs, axis=1) * gng_ref[...] + gnb_ref[...]
    o_ref[...] = (gate * jax.nn.sigmoid(gate)) * on
    upd = _dg((k * kdec_ref[...]).astype(BF16), vb, TN)
    ri = lax.broadcasted_iota(jnp.int32, (HK, HV), 0) // RET_DK
    ci = lax.broadcasted_iota(jnp.int32, (HK, HV), 1) // RET_DV
    s_new = sdec_ref[...] * sbd + jnp.where(ri == ci, upd, 0.0)
    s_scr[...] = s_new

    @pl.when(c == pl.num_programs(1) - 1)
    def _():
        for h in range(RET_HEADS):
            sout_ref[h] = s_new[h * RET_DK:(h + 1) * RET_DK, h * RET_DV:(h + 1) * RET_DV]


def _retention_prompt(p3, geo, gn_g, gn_b):
    HK, HV = RET_HEADS * RET_DK, RET_HEADS * RET_DV
    L = _div_tile(geo.tp, SEQ_TILE)
    log_g = jnp.asarray(_ret_log_g(), F32)
    idx = jnp.arange(L, dtype=F32)
    diff = idx[:, None] - idx[None, :]
    dmask = jnp.where(diff >= 0, jnp.exp(log_g[:, None, None] * jnp.maximum(diff, 0.0)), 0.0)
    cdec = jnp.repeat(jnp.exp(log_g[None, :] * (idx + 1.0)[:, None]), RET_DV, axis=1)
    kdec = jnp.repeat(jnp.exp(log_g[None, :] * (L - 1.0 - idx)[:, None]), RET_DK, axis=1)
    sdec = jnp.broadcast_to(jnp.repeat(jnp.exp(log_g * L), RET_DK)[:, None], (HK, HV))
    cos, sin = _rope_tables(jnp.arange(geo.tp, dtype=jnp.int32) - geo.padf, RET_DK, RET_HEADS)
    fix2 = lambda b, c: (0, 0)
    tab = pl.BlockSpec((L, HK), lambda b, c: (c, 0))
    return pl.pallas_call(
        functools.partial(_ret_kernel, L=L, padf=geo.padf),
        grid=(NSLAB, geo.tp // L),
        in_specs=[pl.BlockSpec((None, L, HK), lambda b, c: (b, c, 0)), pl.BlockSpec((None, L, HK), lambda b, c: (b, c, 1)),
                  pl.BlockSpec((None, L, HV), lambda b, c: (b, c, 1)), pl.BlockSpec((None, L, HV), lambda b, c: (b, c, 2)),
                  tab, tab, pl.BlockSpec((RET_HEADS, L, L), lambda b, c: (0, 0, 0)),
                  pl.BlockSpec((L, HV), fix2), pl.BlockSpec((L, HK), fix2), pl.BlockSpec((HK, HV), fix2),
                  pl.BlockSpec((1, HV), fix2), pl.BlockSpec((1, HV), fix2)],
        out_specs=[pl.BlockSpec((None, L, HV), lambda b, c: (b, c, 0)),
                   pl.BlockSpec((None, RET_HEADS, RET_DK, RET_DV), lambda b, c: (b, 0, 0, 0))],
        out_shape=[jax.ShapeDtypeStruct((NSLAB, geo.slab, HV), F32),
                   jax.ShapeDtypeStruct((NSLAB, RET_HEADS, RET_DK, RET_DV), F32)],
        scratch_shapes=[pltpu.VMEM((HK, HV), F32)],
        compiler_params=_cp("parallel", "arbitrary"),
        name="retention",
    )(p3, p3, p3, p3, cos, sin, dmask, cdec, kdec, sdec, gn_g.reshape(1, -1), gn_b.reshape(1, -1))


def _ret_step_kernel(q_ref, k_ref, v_ref, gate_ref, cos_ref, sin_ref, gng_ref, gnb_ref, s_ref, o_ref, sout_ref):
    cos, sin = cos_ref[...], sin_ref[...]
    qb = _rb(_rope(q_ref[...], cos, sin, RET_DK // 2) * RET_DK ** -0.5)
    kb = _rb(_rope(k_ref[...], cos, sin, RET_DK // 2))
    vb = _rb(v_ref[...])
    gate = gate_ref[...]
    ri = lax.broadcasted_iota(jnp.int32, (RET_DK, RET_DK), 0)
    ci = lax.broadcasted_iota(jnp.int32, (RET_DK, RET_DK), 1)
    eye = jnp.where(ri == ci, 1.0, 0.0)

    def col(x):
        return jnp.sum(eye * jnp.broadcast_to(x, (RET_DK, RET_DK)), axis=1, keepdims=True)

    decay = [math.exp(lg) for lg in _ret_log_g()]
    pairs = [(j, h) for j in range(SUB) for h in range(RET_HEADS)]
    qh = {p: qb[p[0]:p[0] + 1, p[1] * RET_DK:(p[1] + 1) * RET_DK] for p in pairs}
    kh = {p: kb[p[0]:p[0] + 1, p[1] * RET_DK:(p[1] + 1) * RET_DK] for p in pairs}
    vh = {p: vb[p[0]:p[0] + 1, p[1] * RET_DV:(p[1] + 1) * RET_DV] for p in pairs}
    qcol = {p: col(qh[p]) for p in pairs}
    kcol = {p: col(kh[p]) for p in pairs}
    qk = {p: _rb(jnp.sum(qh[p] * kh[p], axis=1, keepdims=True)) for p in pairs}
    o = {}
    for p in pairs:
        s0 = s_ref[p[0], p[1]]
        o[p] = qk[p] * vh[p] + decay[p[1]] * jnp.sum(qcol[p] * _rb(s0), axis=0, keepdims=True)
        sout_ref[p[0], p[1]] = decay[p[1]] * s0 + kcol[p] * vh[p]
    mu = {p: jnp.mean(o[p], -1, keepdims=True) for p in pairs}
    oc = {p: o[p] - mu[p] for p in pairs}
    var = {p: jnp.mean(oc[p] * oc[p], -1, keepdims=True) for p in pairs}
    for j in range(SUB):
        on = jnp.concatenate([oc[(j, h)] * lax.rsqrt(var[(j, h)] + 1e-5) for h in range(RET_HEADS)], axis=1)
        on = on * gng_ref[...] + gnb_ref[...]
        gj = gate[j:j + 1]
        o_ref[j:j + 1, :] = (gj * jax.nn.sigmoid(gj)) * on


def _retention_sample(p3, geo, s0, gn_g, gn_b, out_prev):
    HK, HV = RET_HEADS * RET_DK, RET_HEADS * RET_DV
    cos, sin = _rope_tables(jnp.full((1,), geo.past, jnp.int32), RET_DK, RET_HEADS)
    sb = geo.sample_blk(SUB)
    fix = lambda k: (0, 0)
    st = pl.BlockSpec((SUB, RET_HEADS, RET_DK, RET_DV), lambda k: (k, 0, 0, 0))
    return pl.pallas_call(
        functools.partial(_drop_arg, _ret_step_kernel, 9),
        grid=(geo.ns // SUB,),
        in_specs=[pl.BlockSpec((None, SUB, HK), lambda k: sb(k) + (0,)), pl.BlockSpec((None, SUB, HK), lambda k: sb(k) + (1,)),
                  pl.BlockSpec((None, SUB, HV), lambda k: sb(k) + (1,)), pl.BlockSpec((None, SUB, HV), lambda k: sb(k) + (2,)),
                  pl.BlockSpec((1, HK), fix), pl.BlockSpec((1, HK), fix),
                  pl.BlockSpec((1, HV), fix), pl.BlockSpec((1, HV), fix), st, pl.BlockSpec(memory_space=pl.ANY)],
        out_specs=[pl.BlockSpec((None, SUB, HV), lambda k: sb(k) + (0,)), st],
        out_shape=[jax.ShapeDtypeStruct(out_prev.shape, F32), jax.ShapeDtypeStruct(s0.shape, F32)],
        input_output_aliases={9: 0},
        compiler_params=_cp("parallel"),
        name="retention_step",
    )(p3, p3, p3, p3, cos, sin, gn_g.reshape(1, -1), gn_b.reshape(1, -1), s0, out_prev)


def _s5_params(prm, li):
    f32 = F32
    a_re, a_im = prm['s5_a_re'][li].astype(f32), prm['s5_a_im'][li].astype(f32)
    dt = jnp.exp(prm['s5_log_dt'][li].astype(f32))[:, None]
    mag = jnp.exp(dt * a_re)
    ab_re, ab_im = mag * jnp.cos(dt * a_im), mag * jnp.sin(dt * a_im)
    den = a_re * a_re + a_im * a_im
    n_re = ab_re - 1.0
    f_re = (n_re * a_re + ab_im * a_im) / den
    f_im = (ab_im * a_re - n_re * a_im) / den
    b_re, b_im = prm['s5_b_re'][li].astype(f32), prm['s5_b_im'][li].astype(f32)
    bb_re = f_re[..., None] * b_re - f_im[..., None] * b_im
    bb_im = f_re[..., None] * b_im + f_im[..., None] * b_re
    eye = jnp.eye(S5_GPB, dtype=f32)

    def block_in(w):
        w = w.reshape(S5_NBLK, S5_GPB, S5_STATE, S5_GROUP)
        return jnp.einsum('bgpc,gh->bgchp', w, eye).reshape(S5_NBLK, LANES, S5_BCH)

    def block_out(w):
        w = w.reshape(S5_NBLK, S5_GPB, S5_GROUP, S5_STATE)
        return jnp.einsum('bgcp,gh->bgphc', w, eye).reshape(S5_NBLK, S5_BCH, LANES)

    bbw = jnp.concatenate([block_in(bb_re), block_in(bb_im)], 2).astype(BF16)
    ccw = jnp.concatenate([block_out(prm['s5_c_re'][li]), -block_out(prm['s5_c_im'][li])], 1).astype(BF16)
    lam = jnp.concatenate([ab_re.reshape(1, -1), ab_im.reshape(1, -1)], 1)
    return [bbw, ccw, lam, prm['s5_d'][li].reshape(1, -1), prm['s5_w_glu'][li].astype(BF16),
            prm['s5_b_glu'][li].reshape(1, -1)]


def _s5_specs(fix):
    fix3 = lambda *a: (0, 0, 0)
    return [pl.BlockSpec((S5_NBLK, LANES, 2 * S5_BCH), fix3), pl.BlockSpec((S5_NBLK, 2 * S5_BCH, LANES), fix3),
            pl.BlockSpec((1, 2 * S5_CH), fix), pl.BlockSpec((1, S5_WIDTH), fix),
            pl.BlockSpec((S5_WIDTH, S5_WIDTH), fix), pl.BlockSpec((1, S5_WIDTH), fix)]


def _s5_gate(y, u, d_ref, wglu_ref, bglu_ref):
    y = y + d_ref[...] * u
    z = 0.5 * y * (1.0 + jnp.tanh(math.sqrt(2.0 / math.pi) * (y + 0.044715 * (y * y * y))))
    return z * jax.nn.sigmoid(_dg(z.astype(BF16), wglu_ref[...], NN) + bglu_ref[...])


S5_NJ = 2 * S5_CH // LANES
S5_PAIRS = 8


def _s5_prompt_kernel(u_ref, bb_ref, cc_ref, lam_ref, d_ref, wglu_ref, bglu_ref, o_ref, hout_ref, sc, h_scr,
                      *, lt, pitch, padf):
    i = pl.program_id(0)

    @pl.when(i == 0)
    def _():
        h_scr[...] = jnp.zeros_like(h_scr)

    real = (lax.broadcasted_iota(jnp.int32, (lt, 1), 0) + i * lt) >= padf
    half = S5_NJ // 2
    spb = S5_BCH // LANES
    ub = [jnp.where(real, u_ref[b], 0.0).astype(BF16) for b in range(NSLAB)]
    for k in range(S5_NBLK):
        for b in range(NSLAB):
            bu = _dg(ub[b][:, k * LANES:(k + 1) * LANES], bb_ref[k], NN)
            for q in range(spb):
                sc[k * spb + q, b * pitch:b * pitch + lt, :] = bu[:, q * LANES:(q + 1) * LANES]
                sc[half + k * spb + q, b * pitch:b * pitch + lt, :] = bu[:, S5_BCH + q * LANES:S5_BCH + (q + 1) * LANES]
    for j0 in range(0, half, S5_PAIRS):
        js = list(range(j0, j0 + S5_PAIRS))
        lr = [jnp.broadcast_to(lam_ref[:, j * LANES:(j + 1) * LANES], (NSLAB, LANES)) for j in js]
        li = [jnp.broadcast_to(lam_ref[:, S5_CH + j * LANES:S5_CH + (j + 1) * LANES], (NSLAB, LANES)) for j in js]

        def step(t, carry):
            hr, hi = carry
            rows = pl.ds(t, NSLAB, stride=pitch)
            nr, ni = [], []
            for q, j in enumerate(js):
                r = lr[q] * hr[q] - li[q] * hi[q] + sc[j, rows, :]
                m = lr[q] * hi[q] + li[q] * hr[q] + sc[half + j, rows, :]
                sc[j, rows, :] = r
                sc[half + j, rows, :] = m
                nr.append(r)
                ni.append(m)
            return tuple(nr), tuple(ni)

        init = (tuple(h_scr[:, j * LANES:(j + 1) * LANES] for j in js),
                tuple(h_scr[:, S5_CH + j * LANES:S5_CH + (j + 1) * LANES] for j in js))
        hr, hi = lax.fori_loop(0, lt, step, init, unroll=2)
        for q, j in enumerate(js):
            h_scr[:, j * LANES:(j + 1) * LANES] = hr[q]
            h_scr[:, S5_CH + j * LANES:S5_CH + (j + 1) * LANES] = hi[q]
    ys = [[] for _ in range(NSLAB)]
    for k in range(S5_NBLK):
        for b in range(NSLAB):
            rows = slice(b * pitch, b * pitch + lt)
            hk = jnp.concatenate([sc[k * spb + q, rows, :] for q in range(spb)]
                                 + [sc[half + k * spb + q, rows, :] for q in range(spb)], axis=1)
            ys[b].append(_dg(hk.astype(BF16), cc_ref[k], NN))
    for b in range(NSLAB):
        o_ref[b] = _s5_gate(jnp.concatenate(ys[b], axis=1), u_ref[b], d_ref, wglu_ref, bglu_ref)

    @pl.when(i == pl.num_programs(0) - 1)
    def _():
        hout_ref[...] = h_scr[...]


def _s5_prompt(p3, geo, prm, li):
    lt = _div_tile(geo.tp, S5_TILE, 16)
    pitch = lt + SUB
    ucol = p3.shape[-1] // S5_WIDTH - 1
    fix = lambda i: (0, 0)
    return pl.pallas_call(
        functools.partial(_s5_prompt_kernel, lt=lt, pitch=pitch, padf=geo.padf),
        grid=(geo.tp // lt,),
        in_specs=[pl.BlockSpec((NSLAB, lt, S5_WIDTH), lambda i: (0, i, ucol))] + _s5_specs(fix),
        out_specs=[pl.BlockSpec((NSLAB, lt, S5_WIDTH), lambda i: (0, i, 0)), pl.BlockSpec((NSLAB, 2 * S5_CH), fix)],
        out_shape=[jax.ShapeDtypeStruct((NSLAB, geo.slab, S5_WIDTH), F32), jax.ShapeDtypeStruct((NSLAB, 2 * S5_CH), F32)],
        scratch_shapes=[pltpu.VMEM((S5_NJ, NSLAB * pitch, LANES), F32), pltpu.VMEM((NSLAB, 2 * S5_CH), F32)],
        compiler_params=_cp("arbitrary"),
        name="s5",
    )(p3, *_s5_params(prm, li))


def _s5_step_kernel(u_ref, bb_ref, cc_ref, lam_ref, d_ref, wglu_ref, bglu_ref, h0_ref, o_ref, hout_ref):
    u = u_ref[...]
    ub = u.astype(BF16)
    bu = [_dg(ub[:, k * LANES:(k + 1) * LANES], bb_ref[k], NN) for k in range(S5_NBLK)]
    bu_re = jnp.concatenate([x[:, :S5_BCH] for x in bu], axis=1)
    bu_im = jnp.concatenate([x[:, S5_BCH:] for x in bu], axis=1)
    h0 = h0_ref[...]
    lr, li = lam_ref[:, :S5_CH], lam_ref[:, S5_CH:]
    hr, hi = h0[:, :S5_CH], h0[:, S5_CH:]
    nr = lr * hr - li * hi + bu_re
    ni = lr * hi + li * hr + bu_im
    hout_ref[...] = jnp.concatenate([nr, ni], axis=1)
    ys = []
    for k in range(S5_NBLK):
        sl = slice(k * S5_BCH, (k + 1) * S5_BCH)
        ys.append(_dg(jnp.concatenate([nr[:, sl], ni[:, sl]], axis=1).astype(BF16), cc_ref[k], NN))
    o_ref[...] = _s5_gate(jnp.concatenate(ys, axis=1), u, d_ref, wglu_ref, bglu_ref)


def _s5_sample(p3, geo, h0, prm, li, out_prev):
    rows = geo.sps
    sb = geo.sample_blk(rows)
    ucol = p3.shape[-1] // S5_WIDTH - 1
    fix = lambda k: (0, 0)
    st = pl.BlockSpec((rows, 2 * S5_CH), lambda k: (k, 0))
    return pl.pallas_call(
        functools.partial(_drop_arg, _s5_step_kernel, 8),
        grid=(NSLAB,),
        in_specs=[pl.BlockSpec((None, rows, S5_WIDTH), lambda k: sb(k) + (ucol,))] + _s5_specs(fix)
        + [st, pl.BlockSpec(memory_space=pl.ANY)],
        out_specs=[pl.BlockSpec((None, rows, S5_WIDTH), lambda k: sb(k) + (0,)), st],
        out_shape=[jax.ShapeDtypeStruct(out_prev.shape, F32), jax.ShapeDtypeStruct(h0.shape, F32)],
        input_output_aliases={8: 0},
        compiler_params=_cp("parallel"),
        name="s5_step",
    )(p3, *_s5_params(prm, li), h0, out_prev)


def _head_sum(x, ones_bd):
    ones = ones_bd.astype(BF16)
    hi = x.astype(BF16)
    lo = (x - hi.astype(F32)).astype(BF16)
    return _dg(hi, ones, NN) + _dg(lo, ones, NN)


def _rwkv_prep_body(pr, prev, mu_ref, w0_ref, w2_ref, a0_ref, a2_ref, g2_ref, kk_ref, ka_ref, ones_ref,
                    r_ref, lw_ref, k_ref, v_ref, a_ref, b_ref, g_ref):
    W = RWKV_WIDTH
    xm = pr + (prev - pr) * mu_ref[...]
    r = xm[:, :W]
    k = xm[:, W:2 * W]
    v = xm[:, 2 * W:3 * W]
    xwa = xm[:, 3 * W:3 * W + LORA_W + LORA_A]
    xg = xm[:, 3 * W + LORA_W + LORA_A:]
    wl = w0_ref[...] + _bdot(jnp.tanh(xwa), w2_ref[...])
    sp = jnp.maximum(-wl, 0.0) + jnp.log(1.0 + jnp.exp(-jnp.abs(wl)))
    w = -sp - 0.5
    lw = -jnp.exp(w)
    a = jax.nn.sigmoid(a0_ref[...] + _bdot(xwa, a2_ref[...]))
    g = _bdot(jax.nn.sigmoid(xg), g2_ref[...])
    kk = k * kk_ref[...]
    nrm = jnp.sqrt(_head_sum(kk * kk, ones_ref[...]))
    kk = kk / jnp.maximum(nrm, 1e-12)
    k2 = k * (1.0 + (a - 1.0) * ka_ref[...])
    r_ref[...] = r
    lw_ref[...] = lw
    k_ref[...] = k2
    v_ref[...] = v
    a_ref[...] = -kk
    b_ref[...] = kk * a
    g_ref[...] = g


def _rwkv_prep_prompt_kernel(pr_ref, prev8_ref, *rest, tm, padf):
    pr = pr_ref[...]
    t = lax.broadcasted_iota(jnp.int32, (tm, 1), 0)
    last = jnp.broadcast_to(prev8_ref[SUB - 1:SUB, :], pr.shape)
    prev = jnp.where(t == 0, last, pltpu.roll(pr, 1, 0))
    first = (t + pl.program_id(1) * tm) == padf
    prev = jnp.where(first, 0.0, prev)
    _rwkv_prep_body(pr, prev, *rest)


def _rwkv_prep_sample_kernel(pr_ref, prev_ref, *rest):
    _rwkv_prep_body(pr_ref[...], prev_ref[...], *rest[:9], *rest[9 + 7:])


def _rwkv_prep(p3, geo, shift_s, prm, li):
    W = RWKV_WIDTH
    zeros64 = jnp.zeros((LORA_W, W), F32)
    w2p = jnp.concatenate([prm['rwkv_w2'][li], zeros64], 0).astype(BF16)
    a2p = jnp.concatenate([zeros64, prm['rwkv_a2'][li]], 0).astype(BF16)
    hid = jnp.arange(W) // RWKV_HEAD
    ones_bd = (hid[:, None] == hid[None, :]).astype(F32)
    consts = [prm['rwkv_mu'][li].reshape(1, -1), prm['rwkv_w0'][li].reshape(1, -1), w2p,
              prm['rwkv_a0'][li].reshape(1, -1), a2p, prm['rwkv_g2'][li].astype(BF16),
              prm['rwkv_k_k'][li].reshape(1, -1), prm['rwkv_k_a'][li].reshape(1, -1), ones_bd]
    outs_shape = [jax.ShapeDtypeStruct((NSLAB, geo.slab, W), F32)] * 7
    tm = _div_tile(geo.tp, SEQ_TILE)
    outs = pl.pallas_call(
        functools.partial(_rwkv_prep_prompt_kernel, tm=tm, padf=geo.padf),
        grid=(NSLAB, geo.tp // tm),
        in_specs=[pl.BlockSpec((None, tm, RWKV_PROJ), lambda b, c: (b, c, 0)),
                  pl.BlockSpec((None, SUB, RWKV_PROJ), lambda b, c: (b, jnp.maximum(c * (tm // SUB) - 1, 0), 0))]
        + [pl.BlockSpec(x.shape, lambda b, c: (0, 0)) for x in consts],
        out_specs=[pl.BlockSpec((None, tm, W), lambda b, c: (b, c, 0))] * 7,
        out_shape=outs_shape,
        compiler_params=_cp("parallel", "parallel"),
        name="rwkv_prep_prompt",
    )(p3, p3, *consts)
    rows = geo.sps
    sb = geo.sample_blk(rows)
    nin = 2 + len(consts)
    outs = pl.pallas_call(
        _rwkv_prep_sample_kernel,
        grid=(NSLAB,),
        in_specs=[pl.BlockSpec((None, rows, RWKV_PROJ), lambda k: sb(k) + (0,)),
                  pl.BlockSpec((rows, RWKV_PROJ), lambda k: (k, 0))]
        + [pl.BlockSpec(x.shape, lambda k: (0, 0)) for x in consts] + [pl.BlockSpec(memory_space=pl.ANY)] * 7,
        out_specs=[pl.BlockSpec((None, rows, W), lambda k: sb(k) + (0,))] * 7,
        out_shape=outs_shape,
        input_output_aliases={nin + j: j for j in range(7)},
        compiler_params=_cp("parallel"),
        name="rwkv_prep_sample",
    )(p3, shift_s, *consts, *outs)
    return outs


def _wkv_chunk_kernel(r_ref, lw_ref, k_ref, v_ref, a_ref, b_ref, y_ref, gout_ref, g_scr, *, padf):
    c = pl.program_id(1)
    L, HG = WKV_L, WKV_HG
    NG = RWKV_HEADS // HG
    WD = HG * RWKV_HEAD
    M = HG * L

    @pl.when(c == 0)
    def _():
        g_scr[...] = jnp.zeros_like(g_scr)

    real = (lax.broadcasted_iota(jnp.int32, (L, 1), 0) + c * L) >= padf

    def ld(ref):
        return jnp.where(real, jnp.concatenate([ref[i] for i in range(WKV_NB)], axis=1), 0.0)

    r, lw, k, v, a, b = ld(r_ref), ld(lw_ref), ld(k_ref), ld(v_ref), ld(a_ref), ld(b_ref)
    ti = lax.broadcasted_iota(jnp.int32, (L, L), 0)
    tj = lax.broadcasted_iota(jnp.int32, (L, L), 1)
    tri = jnp.where(ti >= tj, 1.0, 0.0)
    cum = _dot3(tri, lw)
    dec = jnp.exp(cum)
    inv = jnp.exp(-cum)
    at = a * jnp.exp(cum - lw)
    bt = b * inv
    kt = k * inv
    rt = r * dec
    rh = lax.broadcasted_iota(jnp.int32, (M, WD), 0) // L
    lh = lax.broadcasted_iota(jnp.int32, (M, WD), 1) // RWKV_HEAD
    bd = rh == lh
    mi = lax.broadcasted_iota(jnp.int32, (M, M), 0)
    mj = lax.broadcasted_iota(jnp.int32, (M, M), 1)
    same = (mi // L) == (mj // L)
    m_strict = jnp.logical_and(same, (mj % L) < (mi % L))
    m_incl = jnp.logical_and(same, (mj % L) <= (mi % L))
    eye = jnp.where(mi == mj, 1.0, 0.0)

    def stack(x):
        return jnp.where(bd, jnp.concatenate([x] * HG, axis=0), 0.0)

    chains = range(WKV_NB * NG)
    sls = [slice(i * WD, (i + 1) * WD) for i in chains]
    a_s = [stack(at[:, sl]).astype(BF16) for sl in sls]
    r_s = [stack(rt[:, sl]).astype(BF16) for sl in sls]
    v_s = [stack(v[:, sl]).astype(BF16) for sl in sls]
    bk_s = [jnp.concatenate([stack(bt[:, sl]), stack(kt[:, sl])], axis=0).astype(BF16) for sl in sls]
    g = [g_scr[i] for i in chains]
    gb = [x.astype(BF16) for x in g]
    prod = [_dg(jnp.concatenate([a_s[i], r_s[i]], axis=0), bk_s[i], NT) for i in chains]
    ab = [jnp.where(m_strict, p[:M, :M], 0.0) for p in prod]
    ak = [jnp.where(m_strict, p[:M, M:], 0.0).astype(BF16) for p in prod]
    rb = [jnp.where(m_incl, p[M:, :M], 0.0).astype(BF16) for p in prod]
    rk = [jnp.where(m_incl, p[M:, M:], 0.0).astype(BF16) for p in prod]
    tinv = [eye + x for x in ab]
    pw = ab
    for _ in range(int(math.log2(L)) - 1):
        pw = [_dotc(x, x) for x in pw]
        tinv = [t + _dotc(t, x) for t, x in zip(tinv, pw)]
    z = [_dg(a_s[i], gb[i], NT) + _dg(ak[i], v_s[i], NN) for i in chains]
    u = [_dotc(tinv[i], z[i]).astype(BF16) for i in chains]
    ybd = [_dg(r_s[i], gb[i], NT) + _dg(rb[i], u[i], NN) + _dg(rk[i], v_s[i], NN) for i in chains]
    upd = [_dg(jnp.concatenate([u[i], v_s[i]], axis=0), bk_s[i], TN) for i in chains]
    g_last = []
    for i in chains:
        y = ybd[i][0:L]
        for h in range(1, HG):
            y = y + ybd[i][h * L:(h + 1) * L]
        y_ref[i // NG, :, (i % NG) * WD:(i % NG + 1) * WD] = y
        g_new = (g[i] + upd[i]) * dec[L - 1:L, sls[i]]
        g_scr[i] = g_new
        g_last.append(g_new)

    @pl.when(c == pl.num_programs(1) - 1)
    def _():
        for grp, g_new in enumerate(g_last):
            gout_ref[grp // NG, grp % NG] = g_new


def _wkv_prompt(seq, geo):
    L = WKV_L
    NG = RWKV_HEADS // WKV_HG
    WD = WKV_HG * RWKV_HEAD
    spec = pl.BlockSpec((WKV_NB, L, RWKV_WIDTH), lambda b, c: (b, c, 0))
    y, g = pl.pallas_call(
        functools.partial(_wkv_chunk_kernel, padf=geo.padf),
        grid=(NSLAB // WKV_NB, geo.tp // L),
        in_specs=[spec] * 6,
        out_specs=[spec, pl.BlockSpec((WKV_NB, NG, WD, WD), lambda b, c: (b, 0, 0, 0))],
        out_shape=[jax.ShapeDtypeStruct((NSLAB, geo.slab, RWKV_WIDTH), F32),
                   jax.ShapeDtypeStruct((NSLAB, NG, WD, WD), F32)],
        scratch_shapes=[pltpu.VMEM((WKV_NB * NG, WD, WD), F32)],
        compiler_params=_cp("parallel", "arbitrary"),
        name="wkv_chunk",
    )(*seq)
    g = g.reshape(NSLAB, NG, WKV_HG, RWKV_HEAD, WKV_HG, RWKV_HEAD)
    idx = jnp.arange(WKV_HG)
    s = g[:, :, idx, :, idx, :]
    s = jnp.transpose(s, (1, 2, 0, 3, 4)).reshape(NSLAB, RWKV_HEADS, RWKV_HEAD, RWKV_HEAD)
    return y, s


def _wkv_step_kernel(r_ref, lw_ref, k_ref, v_ref, a_ref, b_ref, s_ref, y_ref, sout_ref):
    H, D = RWKV_HEADS, RWKV_HEAD
    ri = lax.broadcasted_iota(jnp.int32, (H * D, D), 0) % D
    ci = lax.broadcasted_iota(jnp.int32, (H * D, D), 1)
    eye = jnp.where(ri == ci, 1.0, 0.0)

    def expand(x):
        return jnp.concatenate([jnp.broadcast_to(x[:, h * D:(h + 1) * D], (D, D)) for h in range(H)], axis=0)

    js = range(SUB)
    row = lambda ref, j: ref[j:j + 1, :]
    s = [s_ref[j].reshape(H * D, D) for j in js]
    v_col = [jnp.sum(expand(row(v_ref, j)) * eye, axis=1, keepdims=True) for j in js]
    sa = [jnp.sum(s[j] * expand(row(a_ref, j)), axis=1, keepdims=True) for j in js]
    s_new = [s[j] * jnp.exp(expand(row(lw_ref, j))) + sa[j] * expand(row(b_ref, j)) + v_col[j] * expand(row(k_ref, j))
             for j in js]
    y_col = [jnp.sum(s_new[j] * expand(row(r_ref, j)), axis=1, keepdims=True) for j in js]
    for j in js:
        ye = y_col[j] * eye
        y_row = jnp.concatenate([jnp.sum(ye[h * D:(h + 1) * D], axis=0, keepdims=True) for h in range(H)], axis=1)
        y_ref[j:j + 1, :] = y_row
        sout_ref[j] = s_new[j].reshape(H, D, D)


def _wkv_sample(seq, geo, s0, y_prev):
    sb = geo.sample_blk(SUB)
    spec = pl.BlockSpec((None, SUB, RWKV_WIDTH), lambda k: sb(k) + (0,))
    sspec = pl.BlockSpec((SUB, RWKV_HEADS, RWKV_HEAD, RWKV_HEAD), lambda k: (k, 0, 0, 0))
    return pl.pallas_call(
        functools.partial(_drop_arg, _wkv_step_kernel, 7),
        grid=(geo.ns // SUB,),
        in_specs=[spec] * 6 + [sspec, pl.BlockSpec(memory_space=pl.ANY)],
        out_specs=[spec, sspec],
        out_shape=[jax.ShapeDtypeStruct(y_prev.shape, F32), jax.ShapeDtypeStruct(s0.shape, F32)],
        input_output_aliases={7: 0},
        compiler_params=_cp("parallel"),
        name="wkv_step",
    )(*seq, s0, y_prev)


def _rwkv_post_kernel(y_ref, r_ref, k_ref, v_ref, g_ref, gng_ref, gnb_ref, rk_ref, ones_ref, o_ref):
    y = y_ref[...]
    ones = ones_ref[...]
    mu = _head_sum(y, ones) * (1.0 / RWKV_HEAD)
    yc = y - mu
    var = _head_sum(yc * yc, ones) * (1.0 / RWKV_HEAD)
    yn = yc * lax.rsqrt(var + RWKV_GN_EPS) * gng_ref[...] + gnb_ref[...]
    v = v_ref[...]
    bonus = _head_sum(r_ref[...] * k_ref[...] * rk_ref[...], ones) * v
    o_ref[...] = (yn + bonus) * g_ref[...]


def _rwkv_post(y, r, k, v, g, prm, li, tm):
    n = y.shape[0]
    W = RWKV_WIDTH
    hid = jnp.arange(W) // RWKV_HEAD
    ones_bd = (hid[:, None] == hid[None, :]).astype(F32)
    row = pl.BlockSpec((tm, W), lambda i: (i, 0))
    vec = pl.BlockSpec((1, W), lambda i: (0, 0))
    return pl.pallas_call(
        _rwkv_post_kernel,
        grid=(n // tm,),
        in_specs=[row] * 5 + [vec] * 3 + [pl.BlockSpec((W, W), lambda i: (0, 0))],
        out_specs=row,
        out_shape=jax.ShapeDtypeStruct((n, W), F32),
        compiler_params=_cp("parallel"),
        name="rwkv_post",
    )(y, r, k, v, g, prm['rwkv_gn_g'][li].reshape(1, -1), prm['rwkv_gn_b'][li].reshape(1, -1),
      prm['rwkv_r_k'][li].reshape(1, -1), ones_bd)


def _mla_prep_kernel(ckv_ref, qa_ref, kr_ref, cos_ref, sin_ref, qn_ref, kvn_ref, wqb_ref, wuk_ref,
                     qlat_ref, qrope_ref, c_ref, kr8_ref):
    qa = qa_ref[:, :Q_LORA]
    qa = qa * lax.rsqrt(jnp.mean(qa * qa, -1, keepdims=True) + 1e-6) * qn_ref[...]
    q = _bdot(qa, wqb_ref[...])
    nope_w = MLA_HEADS * QK_NOPE
    cos, sin = cos_ref[...], sin_ref[...]
    qrope_ref[...] = _rope(q[:, nope_w:], cos, sin, QK_ROPE // 2)
    hp = LANES // QK_NOPE
    for k in range(MLA_HEADS // hp):
        qlat_ref[:, k * hp * KV_LORA:(k + 1) * hp * KV_LORA] = _bdot(q[:, k * LANES:(k + 1) * LANES], wuk_ref[k])
    ckv = ckv_ref[...]
    c_ref[...] = ckv * lax.rsqrt(jnp.mean(ckv * ckv, -1, keepdims=True) + 1e-6) * kvn_ref[...]
    kr8_ref[...] = _rope(kr_ref[...], cos, sin, QK_ROPE // 2)


def _mla_prep(p, geo, pos_slab, prm, li):
    n = p.shape[0]
    tm, tps = geo.tm, geo.tps
    rw = MLA_HEADS * QK_ROPE
    cos, sin = _rope_tables(pos_slab, QK_ROPE, MLA_HEADS)
    wqb = prm['mla_w_qb'][li].reshape(Q_LORA, MLA_HEADS, QK_NOPE + QK_ROPE)
    wqb = jnp.concatenate([wqb[:, :, :QK_NOPE].reshape(Q_LORA, -1), wqb[:, :, QK_NOPE:].reshape(Q_LORA, -1)], 1)
    eye = jnp.eye(MLA_HEADS, dtype=F32)
    wuk = jnp.einsum('chn,hg->hngc', prm['mla_w_uk'][li], eye).reshape(MLA_HEADS * QK_NOPE, MLA_HEADS * KV_LORA)
    hp = LANES // QK_NOPE
    wuk = jnp.stack([wuk[k * LANES:(k + 1) * LANES, k * hp * KV_LORA:(k + 1) * hp * KV_LORA]
                     for k in range(MLA_HEADS // hp)])
    fix = lambda i: (0, 0)
    c0 = RWKV_PROJ // KV_LORA
    q0 = (RWKV_PROJ + KV_LORA) // 512
    k0 = (RWKV_PROJ + KV_LORA + 512) // rw
    tab = pl.BlockSpec((tm, rw), lambda i: (i % tps, 0))
    return pl.pallas_call(
        _mla_prep_kernel,
        grid=(n // tm,),
        in_specs=[pl.BlockSpec((tm, KV_LORA), lambda i: (i, c0)), pl.BlockSpec((tm, 512), lambda i: (i, q0)),
                  pl.BlockSpec((tm, rw), lambda i: (i, k0)), tab, tab,
                  pl.BlockSpec((1, Q_LORA), fix), pl.BlockSpec((1, KV_LORA), fix),
                  pl.BlockSpec(wqb.shape, fix), pl.BlockSpec(wuk.shape, lambda i: (0, 0, 0))],
        out_specs=[pl.BlockSpec((tm, MLA_HEADS * KV_LORA), lambda i: (i, 0)), pl.BlockSpec((tm, rw), lambda i: (i, 0)),
                   pl.BlockSpec((tm, KV_LORA), lambda i: (i, 0)), pl.BlockSpec((tm, rw), lambda i: (i, 0))],
        out_shape=[jax.ShapeDtypeStruct((n, MLA_HEADS * KV_LORA), F32), jax.ShapeDtypeStruct((n, rw), F32),
                   jax.ShapeDtypeStruct((n, KV_LORA), F32), jax.ShapeDtypeStruct((n, rw), F32)],
        compiler_params=_cp("parallel"),
        name="mla_prep",
    )(p, p, p, cos, sin, prm['mla_q_norm'][li].reshape(1, -1), prm['mla_kv_norm'][li].reshape(1, -1),
      wqb.astype(BF16), wuk.astype(BF16))


def _wuv_blockdiag(w_uv):
    eye = jnp.eye(MLA_HEADS, dtype=F32)
    return jnp.einsum('chv,hg->hcgv', w_uv, eye).reshape(MLA_HEADS * KV_LORA, MLA_WIDTH).astype(BF16)


def _mla_attn_kernel(qi_ref, ki_ref, qlat_ref, qrope_ref, c_ref, kr_ref, wuv_ref, o_ref, qs_scr, qrs_scr, m_scr,
                     l_scr, acc_scr, *, lq, padf):
    qi, ki = qi_ref[pl.program_id(1)], ki_ref[pl.program_id(1)]
    H = MLA_HEADS

    @pl.when(ki == 0)
    def _():
        ql = qlat_ref[...]
        qs_scr[...] = jnp.concatenate([ql[:, h * KV_LORA:(h + 1) * KV_LORA] for h in range(H)], axis=0).astype(BF16)
        qr = qrope_ref[...].astype(BF16)
        lane = lax.broadcasted_iota(jnp.int32, qr.shape, 1) // QK_ROPE
        qrs_scr[...] = jnp.concatenate([jnp.where(lane == h, qr, jnp.zeros_like(qr)) for h in range(H)], axis=0)
        m_scr[...] = jnp.full_like(m_scr, NEG)
        l_scr[...] = jnp.zeros_like(l_scr)
        acc_scr[...] = jnp.zeros_like(acc_scr)

    def tile(kind):
        c2 = MLA_SCALE * math.log2(math.e)
        cb = c_ref[...].astype(BF16)
        st = _dg(cb, qs_scr[...], NT) + _dg(kr_ref[...].astype(BF16), qrs_scr[...], NT)
        if kind == "diagonal":
            kpos = ki * lq + lax.broadcasted_iota(jnp.int32, st.shape, 0)
            qpos = qi * lq + lax.broadcasted_iota(jnp.int32, st.shape, 1) % lq
            st = jnp.where(jnp.logical_and(kpos <= qpos, kpos >= padf), st, NEG)
        elif kind == "first":
            st = jnp.where(lax.broadcasted_iota(jnp.int32, (lq, 1), 0) >= padf, st, NEG)
        m_old = m_scr[...]
        m_new = jnp.maximum(m_old, jnp.max(st, axis=0, keepdims=True))
        alpha = jnp.exp2((m_old - m_new) * c2)
        pt = jnp.exp2((st - m_new) * c2)
        l_scr[...] = alpha * l_scr[...] + jnp.sum(pt, axis=0, keepdims=True)
        acc_scr[...] = alpha * acc_scr[...] + _dg(cb, pt.astype(BF16), TN)
        m_scr[...] = m_new

    assert padf <= lq
    pl.when(ki == qi)(functools.partial(tile, "diagonal"))
    pl.when(jnp.logical_and(ki < qi, ki == 0))(functools.partial(tile, "first"))
    pl.when(jnp.logical_and(ki < qi, ki > 0))(functools.partial(tile, "plain"))

    @pl.when(ki == qi)
    def _():
        ot = (acc_scr[...] * (1.0 / l_scr[...])).astype(BF16)
        y = jnp.zeros(o_ref.shape, F32)
        for h in range(H):
            y = y + _dg(ot[:, h * lq:(h + 1) * lq], wuv_ref[h], TN)
        o_ref[...] = y


def _mla_attn_prompt(qlat3, qrope3, c3, kr3, w_uv, geo):
    lq = _div_tile(geo.tp, SEQ_TILE)
    nq = geo.tp // lq
    wuv = _wuv_blockdiag(w_uv).reshape(MLA_HEADS, KV_LORA, MLA_WIDTH)
    rw = MLA_HEADS * QK_ROPE
    pairs = [(qi, ki) for qi in range(nq) for ki in range(qi + 1)]
    qis = jnp.asarray([p[0] for p in pairs], jnp.int32)
    kis = jnp.asarray([p[1] for p in pairs], jnp.int32)
    qmap = lambda b, p, qr, kr: (b, qr[p], 0)
    kmap = lambda b, p, qr, kr: (b, kr[p], 0)
    grid_spec = pltpu.PrefetchScalarGridSpec(
        num_scalar_prefetch=2,
        grid=(NSLAB, len(pairs)),
        in_specs=[pl.BlockSpec((None, lq, MLA_HEADS * KV_LORA), qmap), pl.BlockSpec((None, lq, rw), qmap),
                  pl.BlockSpec((None, lq, KV_LORA), kmap), pl.BlockSpec((None, lq, rw), kmap),
                  pl.BlockSpec(wuv.shape, lambda b, p, qr, kr: (0, 0, 0))],
        out_specs=pl.BlockSpec((None, lq, MLA_WIDTH), qmap),
        scratch_shapes=[pltpu.VMEM((MLA_HEADS * lq, KV_LORA), BF16), pltpu.VMEM((MLA_HEADS * lq, rw), BF16),
                        pltpu.VMEM((1, MLA_HEADS * lq), F32), pltpu.VMEM((1, MLA_HEADS * lq), F32),
                        pltpu.VMEM((KV_LORA, MLA_HEADS * lq), F32)],
    )
    return pl.pallas_call(
        functools.partial(_mla_attn_kernel, lq=lq, padf=geo.padf),
        grid_spec=grid_spec,
        out_shape=jax.ShapeDtypeStruct((NSLAB, geo.slab, MLA_WIDTH), F32),
        compiler_params=_cp("parallel", "arbitrary"),
        name="mla_attn",
    )(qis, kis, qlat3, qrope3, c3, kr3, wuv)


def _mla_decode_kernel(pt_ref, qlat_ref, qrope_ref, cnew_ref, krnew_ref, kv_hbm, kr_hbm, tile_ref, wuv_ref,
                       yprev_ref, o_ref, kv_buf, kr_buf, kb_scr, s_scr, sem, *, li, n_pages, ch):
    del yprev_ref
    b = pl.program_id(0)
    slot = b % 2
    H = MLA_HEADS

    def page_copies(page, sl, pg):
        return (pltpu.make_async_copy(kv_hbm.at[page, li], kv_buf.at[sl, pg], sem.at[0, sl]),
                pltpu.make_async_copy(kr_hbm.at[page, li], kr_buf.at[sl, pg], sem.at[1, sl]))

    def start_fetch(bb, sl):
        def body(pg, carry):
            for cp in page_copies(pt_ref[bb, pg], sl, pg):
                cp.start()
            return carry
        lax.fori_loop(0, n_pages, body, 0)

    def wait_fetch(sl):
        for pg in range(n_pages):
            for cp in page_copies(0, sl, pg):
                cp.wait()

    @pl.when(b == 0)
    def _():
        start_fetch(0, 0)

    nxt = (b + 1) % pl.num_programs(0)

    j = b % SUB
    rid = lax.broadcasted_iota(jnp.int32, (SUB, 1), 0)

    def pick(ref):
        return jnp.sum(jnp.where(rid == j, ref[...], 0.0), axis=0, keepdims=True)

    ql = pick(qlat_ref)
    qs = jnp.concatenate([ql[:, h * KV_LORA:(h + 1) * KV_LORA] for h in range(H)], axis=0).astype(BF16)
    qr = jnp.broadcast_to(pick(qrope_ref), (H, H * QK_ROPE))
    hl = lax.broadcasted_iota(jnp.int32, qr.shape, 1) // QK_ROPE
    hr = lax.broadcasted_iota(jnp.int32, qr.shape, 0)
    qrs = jnp.where(hl == hr, qr, 0.0).astype(BF16)
    qr32 = _dg(qrs, tile_ref[...], NT).astype(BF16)
    cn = _rb(pick(cnew_ref))
    kn = _rb(pick(krnew_ref))
    s_self = (jnp.sum(qs.astype(F32) * cn, axis=1, keepdims=True)
              + jnp.sum(qrs.astype(F32) * kn, axis=1, keepdims=True)) * MLA_SCALE

    wait_fetch(slot)
    cw = ch * PAGE_SIZE
    for ci in range(n_pages // ch):
        for pg in range(ci * ch, (ci + 1) * ch):
            for cp in page_copies(pt_ref[nxt, pg], 1 - slot, pg):
                cp.start()
        kc = kv_buf[slot, ci * ch:(ci + 1) * ch].reshape(cw, KV_LORA).astype(BF16)
        kb_scr[ci * cw:(ci + 1) * cw, :] = kc
        krc = jnp.concatenate([kr_buf[slot, pg] for pg in range(ci * ch, (ci + 1) * ch)], axis=1).astype(BF16)
        s_scr[:, ci * cw:(ci + 1) * cw] = (_dg(qs, kc, NT) + _dg(qr32, krc, NN)) * MLA_SCALE
    s = s_scr[...]
    m = jnp.maximum(jnp.max(s, axis=1, keepdims=True), s_self)
    e = jnp.exp(s - m)
    e_self = jnp.exp(s_self - m)
    rl = 1.0 / (jnp.sum(e, axis=1, keepdims=True) + e_self)
    p = (e * rl).astype(BF16)
    acc = _rb(e_self * rl) * cn
    for ci in range(n_pages // ch):
        acc = acc + _dg(p[:, ci * cw:(ci + 1) * cw], kb_scr[ci * cw:(ci + 1) * cw, :], NN)
    ol = jnp.concatenate([acc[h:h + 1] for h in range(H)], axis=1)
    y = _bdot(jnp.broadcast_to(ol, (SUB, ol.shape[1])), wuv_ref[...])
    o_ref[pl.ds(j, 1), :] = y[0:1]

    @pl.when(b == pl.num_programs(0) - 1)
    def _():
        wait_fetch(1 - slot)


def _mla_decode(qlat, qrope, c, kr8, cache_kv, cache_kr, page_table, w_uv, li, y_prev, geo):
    n = qlat.shape[0]
    n_pages = page_table.shape[1]
    ch = _div_tile(n_pages, 8, 1)
    rw = MLA_HEADS * QK_ROPE
    wuv = _wuv_blockdiag(w_uv)
    tile = jnp.tile(jnp.eye(QK_ROPE, dtype=F32), (1, MLA_HEADS)).astype(BF16)

    def rowmap(b, pt):
        return ((b // geo.sps) * geo.slab + geo.tp + b % geo.sps) // SUB, 0

    rowblk = lambda w: pl.BlockSpec((SUB, w), rowmap)
    grid_spec = pltpu.PrefetchScalarGridSpec(
        num_scalar_prefetch=1,
        grid=(geo.ns,),
        in_specs=[rowblk(MLA_HEADS * KV_LORA), rowblk(rw), rowblk(KV_LORA), rowblk(rw),
                  pl.BlockSpec(memory_space=pl.ANY), pl.BlockSpec(memory_space=pl.ANY),
                  pl.BlockSpec(tile.shape, lambda b, pt: (0, 0)),
                  pl.BlockSpec(wuv.shape, lambda b, pt: (0, 0)),
                  pl.BlockSpec(memory_space=pl.ANY)],
        out_specs=rowblk(MLA_WIDTH),
        scratch_shapes=[pltpu.VMEM((2, n_pages, PAGE_SIZE, KV_LORA), F32),
                        pltpu.VMEM((2, n_pages, QK_ROPE, PAGE_SIZE), F32),
                        pltpu.VMEM((n_pages * PAGE_SIZE, KV_LORA), BF16),
                        pltpu.VMEM((MLA_HEADS, n_pages * PAGE_SIZE), F32),
                        pltpu.SemaphoreType.DMA((2, 2))],
    )
    return pl.pallas_call(
        functools.partial(_mla_decode_kernel, li=li, n_pages=n_pages, ch=ch),
        grid_spec=grid_spec,
        out_shape=jax.ShapeDtypeStruct((n, MLA_WIDTH), F32),
        input_output_aliases={9: 0},
        compiler_params=_cp("arbitrary"),
        name="mla_decode",
    )(page_table, qlat, qrope, c, kr8, cache_kv, jnp.swapaxes(cache_kr, 2, 3), tile, wuv, y_prev)


def _forward(x_prompt, x_sample, state_ret, state_s5_re, state_s5_im, state_wkv, state_shift, cache_kv,
             cache_krope, page_table, meta_tokens, prm):
    nb, seq, _ = x_prompt.shape
    assert nb == NSLAB and x_sample.shape[1] == 1
    ns = x_sample.shape[0]
    geo = _Geom(seq, ns, page_table.shape[1])
    n, tm, tp, padf, sps = geo.n, geo.tm, geo.tp, geo.padf, geo.sps

    meta = jnp.broadcast_to(meta_tokens[None], (nb, N_META, D_MODEL))
    x = jnp.concatenate([jnp.zeros((nb, padf, D_MODEL), F32), meta, x_prompt, x_sample.reshape(nb, sps, D_MODEL)], 1)
    x = x.reshape(n, D_MODEL)
    pos_slab = jnp.concatenate([jnp.arange(tp, dtype=jnp.int32) - padf, jnp.full((sps,), geo.past, jnp.int32)])

    ret_p, ret_s, s5_p, s5_s, wkv_p, wkv_s, sh_p, sh_s, kv_l, kr_l = [], [], [], [], [], [], [], [], [], []
    for layer in range(DEPTH):
        li = layer // 2
        if layer % 2 == 0:
            w_in = prm['ev_w_in'][li].astype(BF16)
            p = _matmul(x, w_in, tm, _div_tile(w_in.shape[1], 1024, 128))
            p3 = geo.v3(p)
            gng, gnb = prm['ret_gn_g'][li], prm['ret_gn_b'][li]
            ret, s_p = _retention_prompt(p3, geo, gng, gnb)
            ret, s_s = _retention_sample(p3, geo, state_ret[li], gng, gnb, ret)
            s5o, h_p = _s5_prompt(p3, geo, prm, li)
            h0 = jnp.concatenate([state_s5_re[li].reshape(ns, S5_CH), state_s5_im[li].reshape(ns, S5_CH)], 1)
            s5o, h_s = _s5_sample(p3, geo, h0, prm, li, s5o)
            ret_p.append(s_p)
            ret_s.append(s_s)
            s5_p.append(h_p)
            s5_s.append(h_s)
            x = _proj_ln(ret.reshape(n, -1), s5o.reshape(n, -1), x, prm['ev_w_out'][li], prm['ln1_g'][layer],
                         prm['ln1_b'][layer], tm)
        else:
            w = prm['od_w_in'][li]
            o1 = RWKV_PROJ
            o2 = o1 + Q_LORA
            o3 = o2 + KV_LORA
            w_in = jnp.concatenate([w[:, :o1], w[:, o2:o3], w[:, o1:o2], jnp.zeros((D_MODEL, 512 - Q_LORA), F32),
                                    jnp.tile(w[:, o3:], (1, MLA_HEADS))], 1).astype(BF16)
            p = _matmul(x, w_in, tm, _div_tile(w_in.shape[1], 1408, 128))
            p3 = geo.v3(p)
            sh_p.append(p3[:, tp - 1, :RWKV_PROJ])
            sh_s.append(p3[:, tp:, :RWKV_PROJ].reshape(ns, RWKV_PROJ))
            r, lw, k, v, a, b, g = _rwkv_prep(p3, geo, state_shift[li], prm, li)
            seqs = (r, lw, k, v, a, b)
            y, g_p = _wkv_prompt(seqs, geo)
            y, g_s = _wkv_sample(seqs, geo, state_wkv[li], y)
            wkv_p.append(g_p)
            wkv_s.append(g_s)
            flat = lambda t: t.reshape(n, t.shape[-1])
            y_c = _rwkv_post(flat(y), flat(r), flat(k), flat(v), flat(g), prm, li, tm)
            qlat, qrope, c, kr8 = _mla_prep(p, geo, pos_slab, prm, li)
            kv_l.append(geo.v3(c))
            kr_l.append(geo.v3(kr8)[:, :, :QK_ROPE])
            y_d = _mla_attn_prompt(geo.v3(qlat), geo.v3(qrope), geo.v3(c), geo.v3(kr8), prm['mla_w_uv'][li], geo)
            y_d = _mla_decode(qlat, qrope, c, kr8, cache_kv, cache_krope, page_table, prm['mla_w_uv'][li], li,
                              flat(y_d), geo)
            x = _proj_ln(y_c, y_d, x, prm['od_w_out'][li], prm['ln1_g'][layer], prm['ln1_b'][layer], tm)
        x = _moe_ln(x, prm, layer, tm)

    x3 = geo.v3(x)
    y_p = x3[:, padf + N_META:tp]
    y_s = x3[:, tp:].reshape(ns, 1, D_MODEL)

    def s5_state(hs, part):
        return jnp.stack([h[:, part * S5_CH:(part + 1) * S5_CH].reshape(-1, S5_GROUPS, S5_STATE) for h in hs])

    kv_p = jnp.stack([c[:, padf:tp] for c in kv_l], 1)
    kv_s = jnp.stack([c[:, tp:].reshape(ns, 1, KV_LORA) for c in kv_l], 1)
    kr_p = jnp.stack([c[:, padf:tp] for c in kr_l], 1)
    kr_s = jnp.stack([c[:, tp:].reshape(ns, 1, QK_ROPE) for c in kr_l], 1)
    return (y_p, y_s, jnp.stack(ret_p), jnp.stack(ret_s), s5_state(s5_p, 0), s5_state(s5_s, 0),
            s5_state(s5_p, 1), s5_state(s5_s, 1), jnp.stack(wkv_p), jnp.stack(wkv_s),
            jnp.stack(sh_p), jnp.stack(sh_s), kv_p, kv_s, kr_p, kr_s)


def kernel(x_prompt, x_sample, state_ret, state_s5_re, state_s5_im, state_wkv, state_shift, cache_kv, cache_krope, page_table, meta_tokens, ev_w_in, ret_gn_g, ret_gn_b, s5_a_re, s5_a_im, s5_log_dt, s5_b_re, s5_b_im, s5_c_re, s5_c_im, s5_d, s5_w_glu, s5_b_glu, ev_w_out, od_w_in, rwkv_mu, rwkv_w0, rwkv_w2, rwkv_a0, rwkv_a2, rwkv_g2, rwkv_k_k, rwkv_k_a, rwkv_r_k, rwkv_gn_g, rwkv_gn_b, mla_q_norm, mla_w_qb, mla_kv_norm, mla_w_uk, mla_w_uv, od_w_out, ln1_g, ln1_b, ln2_g, ln2_b, moe_w_coarse, moe_b_coarse, moe_w_fine, moe_b_fine, moe_w_gate, moe_w_up, moe_w_down):
    prm = dict(ev_w_in=ev_w_in, ret_gn_g=ret_gn_g, ret_gn_b=ret_gn_b, s5_a_re=s5_a_re, s5_a_im=s5_a_im,
               s5_log_dt=s5_log_dt, s5_b_re=s5_b_re, s5_b_im=s5_b_im, s5_c_re=s5_c_re, s5_c_im=s5_c_im,
               s5_d=s5_d, s5_w_glu=s5_w_glu, s5_b_glu=s5_b_glu, ev_w_out=ev_w_out,
               od_w_in=od_w_in, rwkv_mu=rwkv_mu, rwkv_w0=rwkv_w0, rwkv_w2=rwkv_w2, rwkv_a0=rwkv_a0,
               rwkv_a2=rwkv_a2, rwkv_g2=rwkv_g2, rwkv_k_k=rwkv_k_k, rwkv_k_a=rwkv_k_a, rwkv_r_k=rwkv_r_k,
               rwkv_gn_g=rwkv_gn_g, rwkv_gn_b=rwkv_gn_b, mla_q_norm=mla_q_norm, mla_w_qb=mla_w_qb,
               mla_kv_norm=mla_kv_norm, mla_w_uk=mla_w_uk, mla_w_uv=mla_w_uv, od_w_out=od_w_out,
               ln1_g=ln1_g, ln1_b=ln1_b, ln2_g=ln2_g, ln2_b=ln2_b, moe_w_coarse=moe_w_coarse,
               moe_b_coarse=moe_b_coarse, moe_w_fine=moe_w_fine, moe_b_fine=moe_b_fine,
               moe_w_gate=moe_w_gate, moe_w_up=moe_w_up, moe_w_down=moe_w_down)
    return _forward(x_prompt, x_sample, state_ret, state_s5_re, state_s5_im, state_wkv, state_shift, cache_kv,
                    cache_krope, page_table, meta_tokens, prm)
```

```python
import functools
import math

import jax
import jax.numpy as jnp
from jax import lax
from jax.experimental import pallas as pl
from jax.experimental.pallas import tpu as pltpu

F32 = jnp.float32
BF16 = jnp.bfloat16

D_MODEL = 1024
N_META = 16
PAGE_SIZE = 128
RET_HEADS, RET_DK, RET_DV = 4, 64, 128
S5_WIDTH, S5_GROUP, S5_STATE = 512, 16, 64
S5_GROUPS = S5_WIDTH // S5_GROUP
S5_CH = S5_GROUPS * S5_STATE
S5_GPB = 128 // S5_GROUP
S5_NBLK = S5_GROUPS // S5_GPB
S5_BCH = S5_GPB * S5_STATE
RWKV_HEAD, RWKV_WIDTH = 64, 512
RWKV_HEADS = RWKV_WIDTH // RWKV_HEAD
LORA_W, LORA_A, LORA_G = 64, 64, 128
RWKV_PROJ = 3 * RWKV_WIDTH + LORA_W + LORA_A + LORA_G
RWKV_GN_EPS = 64e-5
MLA_HEADS, Q_LORA, KV_LORA, QK_NOPE, QK_ROPE, V_HEAD = 8, 384, 256, 64, 32, 64
MLA_WIDTH = MLA_HEADS * V_HEAD
MLA_SCALE = (QK_NOPE + QK_ROPE) ** -0.5
ROPE_THETA = 10000.0
N_EGROUPS, E_PER_GROUP, D_EXPERT = 4, 4, 256
N_EXPERTS = N_EGROUPS * E_PER_GROUP
DEPTH = 4
DN_ALPHA = (2 * DEPTH) ** 0.25
NSLAB = 8
SUB = 8
LANES = 128

VMEM_LIMIT = 56 * 1024 * 1024
WKV_L = 32
WKV_HG = 4
WKV_NB = 8
ROW_TILE = 1056
SEQ_TILE = 416
S5_TILE = 80
NEG = -1e30

NT = (((1,), (1,)), ((), ()))
TN = (((0,), (0,)), ((), ()))
NN = (((1,), (0,)), ((), ()))


def _cp(*sem):
    return pltpu.CompilerParams(dimension_semantics=sem, vmem_limit_bytes=VMEM_LIMIT)


def _div_tile(n, cap, mult=8):
    best = None
    for d in range(mult, min(n, cap) + 1, mult):
        if n % d == 0:
            best = d
    assert best is not None, (n, cap, mult)
    return best


def _dg(a, b, dims):
    return lax.dot_general(a, b, dims, preferred_element_type=F32)


def _bdot(a, b, dims=NN):
    return _dg(a.astype(BF16), b.astype(BF16), dims)


def _dot3(a, b, dims=NN):
    ah = a.astype(BF16)
    al = (a - ah.astype(F32)).astype(BF16)
    bh = b.astype(BF16)
    bl = (b - bh.astype(F32)).astype(BF16)
    return _dg(ah, bh, dims) + (_dg(ah, bl, dims) + _dg(al, bh, dims))


def _dotc(a, b, dims=NN):
    return _bdot(a, b, dims)


def _rb(x):
    return x.astype(BF16).astype(F32)


def _layer_norm(z, g, b, eps=1e-5):
    mu = jnp.mean(z, -1, keepdims=True)
    zc = z - mu
    var = jnp.mean(zc * zc, -1, keepdims=True)
    return zc * lax.rsqrt(var + eps) * g + b


def _rope(x, cos, sin_signed, half):
    w = x.shape[-1]
    lane = lax.broadcasted_iota(jnp.int32, x.shape, 1)
    first = (lane % (2 * half)) < half
    other = jnp.where(first, pltpu.roll(x, w - half, 1), pltpu.roll(x, half, 1))
    return x * cos + other * sin_signed


def _rope_tables(pos, dim, reps):
    inv = ROPE_THETA ** (-jnp.arange(0, dim, 2, dtype=F32) / dim)
    ang = pos.astype(F32)[:, None] * inv[None, :]
    cos = jnp.cos(ang)
    sin = jnp.sin(ang)
    cos = jnp.tile(jnp.concatenate([cos, cos], -1), (1, reps))
    sin = jnp.tile(jnp.concatenate([-sin, sin], -1), (1, reps))
    return cos, sin


def _drop_arg(kern, idx, *refs):
    return kern(*(refs[:idx] + refs[idx + 1:]))


class _Geom:
    def __init__(self, seq, ns, n_pages):
        self.tq = seq + N_META
        self.padf = (-self.tq) % WKV_L
        self.tp = self.padf + self.tq
        assert ns % (NSLAB * SUB) == 0
        self.ns = ns
        self.sps = ns // NSLAB
        self.slab = self.tp + self.sps
        self.n = NSLAB * self.slab
        self.past = n_pages * PAGE_SIZE
        self.tm = _div_tile(self.slab, ROW_TILE)
        self.tps = self.slab // self.tm

    def v3(self, x):
        return x.reshape(NSLAB, self.slab, x.shape[-1])

    def sample_blk(self, rows):
        per = self.sps // rows
        return lambda k: (k // per, self.tp // rows + k % per)


def _mm_kernel(x_ref, w_ref, o_ref):
    o_ref[...] = _dg(x_ref[...].astype(BF16), w_ref[...], NN)


def _matmul(x, w, tm, tn):
    n, k = x.shape
    m = w.shape[1]
    return pl.pallas_call(
        _mm_kernel,
        grid=(n // tm, m // tn),
        in_specs=[pl.BlockSpec((tm, k), lambda i, j: (i, 0)), pl.BlockSpec((k, tn), lambda i, j: (0, j))],
        out_specs=pl.BlockSpec((tm, tn), lambda i, j: (i, j)),
        out_shape=jax.ShapeDtypeStruct((n, m), F32),
        compiler_params=_cp("parallel", "arbitrary"),
        name="proj_in",
    )(x, w)


def _proj_ln_kernel(a1_ref, a2_ref, x_ref, w1_ref, w2_ref, g_ref, b_ref, o_ref):
    mix = _dg(a1_ref[...].astype(BF16), w1_ref[...], NN) + _dg(a2_ref[...].astype(BF16), w2_ref[...], NN)
    z = DN_ALPHA * x_ref[...] + mix
    o_ref[...] = _layer_norm(z, g_ref[...], b_ref[...])


def _proj_ln(a1, a2, x, w_out, g, b, tm):
    n = x.shape[0]
    k1, k2 = a1.shape[1], a2.shape[1]
    w1 = w_out[:k1].astype(BF16)
    w2 = w_out[k1:].astype(BF16)
    row = lambda i: (i, 0)
    fix = lambda i: (0, 0)
    return pl.pallas_call(
        _proj_ln_kernel,
        grid=(n // tm,),
        in_specs=[pl.BlockSpec((tm, k1), row), pl.BlockSpec((tm, k2), row), pl.BlockSpec((tm, D_MODEL), row),
                  pl.BlockSpec((k1, D_MODEL), fix), pl.BlockSpec((k2, D_MODEL), fix),
                  pl.BlockSpec((1, D_MODEL), fix), pl.BlockSpec((1, D_MODEL), fix)],
        out_specs=pl.BlockSpec((tm, D_MODEL), row),
        out_shape=jax.ShapeDtypeStruct((n, D_MODEL), F32),
        compiler_params=_cp("parallel"),
        name="proj_out_ln",
    )(a1, a2, x, w1, w2, g.reshape(1, -1), b.reshape(1, -1))


MOE_EPS = 2
ROUTE0 = N_EGROUPS


def _route(lt):
    lane = lax.broadcasted_iota(jnp.int32, lt.shape, 1)
    lanef = lane.astype(F32)
    coarse = lane < N_EGROUPS
    lc = jnp.where(coarse, lt, NEG)
    m = jnp.max(lc, axis=1, keepdims=True)
    gsel = jnp.min(jnp.where(lc == m, lanef, 1e9), axis=1, keepdims=True)
    gate_c = 1.0 / jnp.sum(jnp.where(coarse, jnp.exp(lc - m), 0.0), axis=1, keepdims=True)
    grp = ((lane - ROUTE0) // E_PER_GROUP).astype(F32)
    lf = jnp.where(grp == gsel, lt, NEG)
    t1 = jnp.max(lf, axis=1, keepdims=True)
    i1 = jnp.min(jnp.where(lf == t1, lanef, 1e9), axis=1, keepdims=True)
    lf2 = jnp.where(lanef == i1, NEG, lf)
    t2 = jnp.max(lf2, axis=1, keepdims=True)
    i2 = jnp.min(jnp.where(lf2 == t2, lanef, 1e9), axis=1, keepdims=True)
    e2 = jnp.exp(t2 - t1)
    w1 = gate_c / (1.0 + e2)
    return jnp.where(lanef == i1, w1, 0.0) + jnp.where(lanef == i2, w1 * e2, 0.0)


def _moe_kernel(x_ref, wr_ref, br_ref, wgu_ref, wd_ref, g_ref, b_ref, o_ref, xb_scr, comb_scr, h_scr):
    e = pl.program_id(1)

    @pl.when(e == 0)
    def _():
        xb = x_ref[...].astype(BF16)
        xb_scr[...] = xb
        comb_scr[...] = _route(_dg(xb, wr_ref[...], NN) + br_ref[...])

    xb = xb_scr[...]
    comb = comb_scr[...]
    lane = lax.broadcasted_iota(jnp.int32, comb.shape, 1)
    gus = [_dg(xb, wgu_ref[q], NN) for q in range(MOE_EPS)]
    hb = []
    for q, gu in enumerate(gus):
        hg, hu = gu[:, :D_EXPERT], gu[:, D_EXPERT:]
        ce = jnp.sum(jnp.where(lane == e * MOE_EPS + q + ROUTE0, comb, 0.0), axis=1, keepdims=True)
        hb.append(((hg * jax.nn.sigmoid(hg)) * hu * ce).astype(BF16))
    for ee in range(N_EXPERTS // MOE_EPS):
        @pl.when(e == ee)
        def _():
            for q in range(MOE_EPS):
                x0 = (ee * MOE_EPS + q) * D_EXPERT
                h_scr[:, x0:x0 + D_EXPERT] = hb[q]

    @pl.when(e == pl.num_programs(1) - 1)
    def _():
        z = DN_ALPHA * x_ref[...] + _dg(h_scr[...], wd_ref[...], NN)
        o_ref[...] = _layer_norm(z, g_ref[...], b_ref[...])


def _moe_ln(x, prm, layer, tm):
    n = x.shape[0]
    wr = jnp.concatenate([prm['moe_w_coarse'][layer],
                          jnp.transpose(prm['moe_w_fine'][layer], (1, 0, 2)).reshape(D_MODEL, N_EXPERTS)], 1)
    wr = jnp.pad(wr, ((0, 0), (0, LANES - wr.shape[1]))).astype(BF16)
    br = jnp.pad(jnp.concatenate([prm['moe_b_coarse'][layer], prm['moe_b_fine'][layer].reshape(-1)]),
                 (0, LANES - N_EGROUPS - N_EXPERTS)).reshape(1, LANES)
    row = lambda i, e: (i, 0)
    fix = lambda i, e: (0, 0)
    exp3 = lambda i, e: (e, 0, 0)
    return pl.pallas_call(
        _moe_kernel,
        grid=(n // tm, N_EXPERTS // MOE_EPS),
        in_specs=[pl.BlockSpec((tm, D_MODEL), row), pl.BlockSpec((D_MODEL, LANES), fix), pl.BlockSpec((1, LANES), fix),
                  pl.BlockSpec((MOE_EPS, D_MODEL, 2 * D_EXPERT), exp3),
                  pl.BlockSpec((N_EXPERTS * D_EXPERT, D_MODEL), fix, pipeline_mode=pl.Buffered(1)),
                  pl.BlockSpec((1, D_MODEL), fix), pl.BlockSpec((1, D_MODEL), fix)],
        out_specs=pl.BlockSpec((tm, D_MODEL), row),
        out_shape=jax.ShapeDtypeStruct((n, D_MODEL), F32),
        scratch_shapes=[pltpu.VMEM((tm, D_MODEL), BF16), pltpu.VMEM((tm, LANES), F32),
                        pltpu.VMEM((tm, N_EXPERTS * D_EXPERT), BF16)],
        compiler_params=_cp("parallel", "arbitrary"),
        name="moe_ln",
    )(x, wr, br, jnp.concatenate([prm['moe_w_gate'][layer], prm['moe_w_up'][layer]], -1).astype(BF16),
      prm['moe_w_down'][layer].astype(BF16).reshape(N_EXPERTS * D_EXPERT, D_MODEL),
      prm['ln2_g'][layer].reshape(1, -1), prm['ln2_b'][layer].reshape(1, -1))


def _ret_log_g():
    return [math.log1p(-2.0 ** (-5.0 - h)) for h in range(RET_HEADS)]


def _ret_kernel(q_ref, k_ref, v_ref, gate_ref, cos_ref, sin_ref, dmask_ref, cdec_ref, kdec_ref, sdec_ref,
                gng_ref, gnb_ref, o_ref, sout_ref, s_scr, *, L, padf):
    c = pl.program_id(1)
    HK, HV = RET_HEADS * RET_DK, RET_HEADS * RET_DV

    @pl.when(c == 0)
    def _():
        s_scr[...] = jnp.zeros_like(s_scr)

    cos, sin = cos_ref[...], sin_ref[...]
    q = _rope(q_ref[...], cos, sin, RET_DK // 2) * RET_DK ** -0.5
    k = _rope(k_ref[...], cos, sin, RET_DK // 2)
    real = (lax.broadcasted_iota(jnp.int32, (L, 1), 0) + c * L) >= padf
    k = jnp.where(real, k, 0.0)
    v = jnp.where(real, v_ref[...], 0.0)
    gate = gate_ref[...]
    qb, kb, vb = q.astype(BF16), k.astype(BF16), v.astype(BF16)
    sbd = s_scr[...]
    cross = _dg(qb, sbd.astype(BF16), NN) * cdec_ref[...]
    lane = lax.broadcasted_iota(jnp.int32, (L, HK), 1)
    outs = []
    scs = [(_dg(jnp.where(lane // RET_DK == h, qb, jnp.zeros_like(qb)), kb, NT) * dmask_ref[h]).astype(BF16)
           for h in range(RET_HEADS)]
    for h in range(RET_HEADS):
        sl = slice(h * RET_DV, (h + 1) * RET_DV)
        o = _dg(scs[h], vb[:, sl], NN) + cross[:, sl]
        mu = jnp.mean(o, -1, keepdims=True)
        oc = o - mu
        var = jnp.mean(oc * oc, -1, keepdims=True)
        outs.append(oc * lax.rsqrt(var + 1e-5))
    on = jnp.concatenate(outs, axis=1) * gng_ref[...] + gnb_ref[...]
    o_ref[...] = (gate * jax.nn.sigmoid(gate)) * on
    upd = _dg((k * kdec_ref[...]).astype(BF16), vb, TN)
    ri = lax.broadcasted_iota(jnp.int32, (HK, HV), 0) // RET_DK
    ci = lax.broadcasted_iota(jnp.int32, (HK, HV), 1) // RET_DV
    s_new = sdec_ref[...] * sbd + jnp.where(ri == ci, upd, 0.0)
    s_scr[...] = s_new

    @pl.when(c == pl.num_programs(1) - 1)
    def _():
        for h in range(RET_HEADS):
            sout_ref[h] = s_new[h * RET_DK:(h + 1) * RET_DK, h * RET_DV:(h + 1) * RET_DV]


def _retention_prompt(p3, geo, gn_g, gn_b):
    HK, HV = RET_HEADS * RET_DK, RET_HEADS * RET_DV
    L = _div_tile(geo.tp, SEQ_TILE)
    log_g = jnp.asarray(_ret_log_g(), F32)
    idx = jnp.arange(L, dtype=F32)
    diff = idx[:, None] - idx[None, :]
    dmask = jnp.where(diff >= 0, jnp.exp(log_g[:, None, None] * jnp.maximum(diff, 0.0)), 0.0)
    cdec = jnp.repeat(jnp.exp(log_g[None, :] * (idx + 1.0)[:, None]), RET_DV, axis=1)
    kdec = jnp.repeat(jnp.exp(log_g[None, :] * (L - 1.0 - idx)[:, None]), RET_DK, axis=1)
    sdec = jnp.broadcast_to(jnp.repeat(jnp.exp(log_g * L), RET_DK)[:, None], (HK, HV))
    cos, sin = _rope_tables(jnp.arange(geo.tp, dtype=jnp.int32) - geo.padf, RET_DK, RET_HEADS)
    fix2 = lambda b, c: (0, 0)
    tab = pl.BlockSpec((L, HK), lambda b, c: (c, 0))
    return pl.pallas_call(
        functools.partial(_ret_kernel, L=L, padf=geo.padf),
        grid=(NSLAB, geo.tp // L),
        in_specs=[pl.BlockSpec((None, L, HK), lambda b, c: (b, c, 0)), pl.BlockSpec((None, L, HK), lambda b, c: (b, c, 1)),
                  pl.BlockSpec((None, L, HV), lambda b, c: (b, c, 1)), pl.BlockSpec((None, L, HV), lambda b, c: (b, c, 2)),
                  tab, tab, pl.BlockSpec((RET_HEADS, L, L), lambda b, c: (0, 0, 0)),
                  pl.BlockSpec((L, HV), fix2), pl.BlockSpec((L, HK), fix2), pl.BlockSpec((HK, HV), fix2),
                  pl.BlockSpec((1, HV), fix2), pl.BlockSpec((1, HV), fix2)],
        out_specs=[pl.BlockSpec((None, L, HV), lambda b, c: (b, c, 0)),
                   pl.BlockSpec((None, RET_HEADS, RET_DK, RET_DV), lambda b, c: (b, 0, 0, 0))],
        out_shape=[jax.ShapeDtypeStruct((NSLAB, geo.slab, HV), F32),
                   jax.ShapeDtypeStruct((NSLAB, RET_HEADS, RET_DK, RET_DV), F32)],
        scratch_shapes=[pltpu.VMEM((HK, HV), F32)],
        compiler_params=_cp("parallel", "arbitrary"),
        name="retention",
    )(p3, p3, p3, p3, cos, sin, dmask, cdec, kdec, sdec, gn_g.reshape(1, -1), gn_b.reshape(1, -1))


def _ret_step_kernel(q_ref, k_ref, v_ref, gate_ref, cos_ref, sin_ref, gng_ref, gnb_ref, s_ref, o_ref, sout_ref):
    cos, sin = cos_ref[...], sin_ref[...]
    qb = _rb(_rope(q_ref[...], cos, sin, RET_DK // 2) * RET_DK ** -0.5)
    kb = _rb(_rope(k_ref[...], cos, sin, RET_DK // 2))
    vb = _rb(v_ref[...])
    gate = gate_ref[...]
    ri = lax.broadcasted_iota(jnp.int32, (RET_DK, RET_DK), 0)
    ci = lax.broadcasted_iota(jnp.int32, (RET_DK, RET_DK), 1)
    eye = jnp.where(ri == ci, 1.0, 0.0)

    def col(x):
        return jnp.sum(eye * jnp.broadcast_to(x, (RET_DK, RET_DK)), axis=1, keepdims=True)

    decay = [math.exp(lg) for lg in _ret_log_g()]
    pairs = [(j, h) for j in range(SUB) for h in range(RET_HEADS)]
    qh = {p: qb[p[0]:p[0] + 1, p[1] * RET_DK:(p[1] + 1) * RET_DK] for p in pairs}
    kh = {p: kb[p[0]:p[0] + 1, p[1] * RET_DK:(p[1] + 1) * RET_DK] for p in pairs}
    vh = {p: vb[p[0]:p[0] + 1, p[1] * RET_DV:(p[1] + 1) * RET_DV] for p in pairs}
    qcol = {p: col(qh[p]) for p in pairs}
    kcol = {p: col(kh[p]) for p in pairs}
    qk = {p: _rb(jnp.sum(qh[p] * kh[p], axis=1, keepdims=True)) for p in pairs}
    o = {}
    for p in pairs:
        s0 = s_ref[p[0], p[1]]
        o[p] = qk[p] * vh[p] + decay[p[1]] * jnp.sum(qcol[p] * _rb(s0), axis=0, keepdims=True)
        sout_ref[p[0], p[1]] = decay[p[1]] * s0 + kcol[p] * vh[p]
    mu = {p: jnp.mean(o[p], -1, keepdims=True) for p in pairs}
    oc = {p: o[p] - mu[p] for p in pairs}
    var = {p: jnp.mean(oc[p] * oc[p], -1, keepdims=True) for p in pairs}
    for j in range(SUB):
        on = jnp.concatenate([oc[(j, h)] * lax.rsqrt(var[(j, h)] + 1e-5) for h in range(RET_HEADS)], axis=1)
        on = on * gng_ref[...] + gnb_ref[...]
        gj = gate[j:j + 1]
        o_ref[j:j + 1, :] = (gj * jax.nn.sigmoid(gj)) * on


def _retention_sample(p3, geo, s0, gn_g, gn_b, out_prev):
    HK, HV = RET_HEADS * RET_DK, RET_HEADS * RET_DV
    cos, sin = _rope_tables(jnp.full((1,), geo.past, jnp.int32), RET_DK, RET_HEADS)
    sb = geo.sample_blk(SUB)
    fix = lambda k: (0, 0)
    st = pl.BlockSpec((SUB, RET_HEADS, RET_DK, RET_DV), lambda k: (k, 0, 0, 0))
    return pl.pallas_call(
        functools.partial(_drop_arg, _ret_step_kernel, 9),
        grid=(geo.ns // SUB,),
        in_specs=[pl.BlockSpec((None, SUB, HK), lambda k: sb(k) + (0,)), pl.BlockSpec((None, SUB, HK), lambda k: sb(k) + (1,)),
                  pl.BlockSpec((None, SUB, HV), lambda k: sb(k) + (1,)), pl.BlockSpec((None, SUB, HV), lambda k: sb(k) + (2,)),
                  pl.BlockSpec((1, HK), fix), pl.BlockSpec((1, HK), fix),
                  pl.BlockSpec((1, HV), fix), pl.BlockSpec((1, HV), fix), st, pl.BlockSpec(memory_space=pl.ANY)],
        out_specs=[pl.BlockSpec((None, SUB, HV), lambda k: sb(k) + (0,)), st],
        out_shape=[jax.ShapeDtypeStruct(out_prev.shape, F32), jax.ShapeDtypeStruct(s0.shape, F32)],
        input_output_aliases={9: 0},
        compiler_params=_cp("parallel"),
        name="retention_step",
    )(p3, p3, p3, p3, cos, sin, gn_g.reshape(1, -1), gn_b.reshape(1, -1), s0, out_prev)


def _s5_params(prm, li):
    f32 = F32
    a_re, a_im = prm['s5_a_re'][li].astype(f32), prm['s5_a_im'][li].astype(f32)
    dt = jnp.exp(prm['s5_log_dt'][li].astype(f32))[:, None]
    mag = jnp.exp(dt * a_re)
    ab_re, ab_im = mag * jnp.cos(dt * a_im), mag * jnp.sin(dt * a_im)
    den = a_re * a_re + a_im * a_im
    n_re = ab_re - 1.0
    f_re = (n_re * a_re + ab_im * a_im) / den
    f_im = (ab_im * a_re - n_re * a_im) / den
    b_re, b_im = prm['s5_b_re'][li].astype(f32), prm['s5_b_im'][li].astype(f32)
    bb_re = f_re[..., None] * b_re - f_im[..., None] * b_im
    bb_im = f_re[..., None] * b_im + f_im[..., None] * b_re
    eye = jnp.eye(S5_GPB, dtype=f32)

    def block_in(w):
        w = w.reshape(S5_NBLK, S5_GPB, S5_STATE, S5_GROUP)
        return jnp.einsum('bgpc,gh->bgchp', w, eye).reshape(S5_NBLK, LANES, S5_BCH)

    def block_out(w):
        w = w.reshape(S5_NBLK, S5_GPB, S5_GROUP, S5_STATE)
        return jnp.einsum('bgcp,gh->bgphc', w, eye).reshape(S5_NBLK, S5_BCH, LANES)

    bbw = jnp.concatenate([block_in(bb_re), block_in(bb_im)], 2).astype(BF16)
    ccw = jnp.concatenate([block_out(prm['s5_c_re'][li]), -block_out(prm['s5_c_im'][li])], 1).astype(BF16)
    lam = jnp.concatenate([ab_re.reshape(1, -1), ab_im.reshape(1, -1)], 1)
    return [bbw, ccw, lam, prm['s5_d'][li].reshape(1, -1), prm['s5_w_glu'][li].astype(BF16),
            prm['s5_b_glu'][li].reshape(1, -1)]


def _s5_specs(fix):
    fix3 = lambda *a: (0, 0, 0)
    return [pl.BlockSpec((S5_NBLK, LANES, 2 * S5_BCH), fix3), pl.BlockSpec((S5_NBLK, 2 * S5_BCH, LANES), fix3),
            pl.BlockSpec((1, 2 * S5_CH), fix), pl.BlockSpec((1, S5_WIDTH), fix),
            pl.BlockSpec((S5_WIDTH, S5_WIDTH), fix), pl.BlockSpec((1, S5_WIDTH), fix)]


def _s5_gate(y, u, d_ref, wglu_ref, bglu_ref):
    y = y + d_ref[...] * u
    z = 0.5 * y * (1.0 + jnp.tanh(math.sqrt(2.0 / math.pi) * (y + 0.044715 * (y * y * y))))
    return z * jax.nn.sigmoid(_dg(z.astype(BF16), wglu_ref[...], NN) + bglu_ref[...])


S5_NJ = 2 * S5_CH // LANES
S5_PAIRS = 8


def _s5_prompt_kernel(u_ref, bb_ref, cc_ref, lam_ref, d_ref, wglu_ref, bglu_ref, o_ref, hout_ref, sc, h_scr,
                      *, lt, pitch, padf):
    i = pl.program_id(0)

    @pl.when(i == 0)
    def _():
        h_scr[...] = jnp.zeros_like(h_scr)

    real = (lax.broadcasted_iota(jnp.int32, (lt, 1), 0) + i * lt) >= padf
    half = S5_NJ // 2
    spb = S5_BCH // LANES
    ub = [jnp.where(real, u_ref[b], 0.0).astype(BF16) for b in range(NSLAB)]
    for k in range(S5_NBLK):
        for b in range(NSLAB):
            bu = _dg(ub[b][:, k * LANES:(k + 1) * LANES], bb_ref[k], NN)
            for q in range(spb):
                sc[k * spb + q, b * pitch:b * pitch + lt, :] = bu[:, q * LANES:(q + 1) * LANES]
                sc[half + k * spb + q, b * pitch:b * pitch + lt, :] = bu[:, S5_BCH + q * LANES:S5_BCH + (q + 1) * LANES]
    for j0 in range(0, half, S5_PAIRS):
        js = list(range(j0, j0 + S5_PAIRS))
        lr = [jnp.broadcast_to(lam_ref[:, j * LANES:(j + 1) * LANES], (NSLAB, LANES)) for j in js]
        li = [jnp.broadcast_to(lam_ref[:, S5_CH + j * LANES:S5_CH + (j + 1) * LANES], (NSLAB, LANES)) for j in js]

        def step(t, carry):
            hr, hi = carry
            rows = pl.ds(t, NSLAB, stride=pitch)
            nr, ni = [], []
            for q, j in enumerate(js):
                r = lr[q] * hr[q] - li[q] * hi[q] + sc[j, rows, :]
                m = lr[q] * hi[q] + li[q] * hr[q] + sc[half + j, rows, :]
                sc[j, rows, :] = r
                sc[half + j, rows, :] = m
                nr.append(r)
                ni.append(m)
            return tuple(nr), tuple(ni)

        init = (tuple(h_scr[:, j * LANES:(j + 1) * LANES] for j in js),
                tuple(h_scr[:, S5_CH + j * LANES:S5_CH + (j + 1) * LANES] for j in js))
        hr, hi = lax.fori_loop(0, lt, step, init, unroll=2)
        for q, j in enumerate(js):
            h_scr[:, j * LANES:(j + 1) * LANES] = hr[q]
            h_scr[:, S5_CH + j * LANES:S5_CH + (j + 1) * LANES] = hi[q]
    ys = [[] for _ in range(NSLAB)]
    for k in range(S5_NBLK):
        for b in range(NSLAB):
            rows = slice(b * pitch, b * pitch + lt)
            hk = jnp.concatenate([sc[k * spb + q, rows, :] for q in range(spb)]
                                 + [sc[half + k * spb + q, rows, :] for q in range(spb)], axis=1)
            ys[b].append(_dg(hk.astype(BF16), cc_ref[k], NN))
    for b in range(NSLAB):
        o_ref[b] = _s5_gate(jnp.concatenate(ys[b], axis=1), u_ref[b], d_ref, wglu_ref, bglu_ref)

    @pl.when(i == pl.num_programs(0) - 1)
    def _():
        hout_ref[...] = h_scr[...]


def _s5_prompt(p3, geo, prm, li):
    lt = _div_tile(geo.tp, S5_TILE, 16)
    pitch = lt + SUB
    ucol = p3.shape[-1] // S5_WIDTH - 1
    fix = lambda i: (0, 0)
    return pl.pallas_call(
        functools.partial(_s5_prompt_kernel, lt=lt, pitch=pitch, padf=geo.padf),
        grid=(geo.tp // lt,),
        in_specs=[pl.BlockSpec((NSLAB, lt, S5_WIDTH), lambda i: (0, i, ucol))] + _s5_specs(fix),
        out_specs=[pl.BlockSpec((NSLAB, lt, S5_WIDTH), lambda i: (0, i, 0)), pl.BlockSpec((NSLAB, 2 * S5_CH), fix)],
        out_shape=[jax.ShapeDtypeStruct((NSLAB, geo.slab, S5_WIDTH), F32), jax.ShapeDtypeStruct((NSLAB, 2 * S5_CH), F32)],
        scratch_shapes=[pltpu.VMEM((S5_NJ, NSLAB * pitch, LANES), F32), pltpu.VMEM((NSLAB, 2 * S5_CH), F32)],
        compiler_params=_cp("arbitrary"),
        name="s5",
    )(p3, *_s5_params(prm, li))


def _s5_step_kernel(u_ref, bb_ref, cc_ref, lam_ref, d_ref, wglu_ref, bglu_ref, h0_ref, o_ref, hout_ref):
    u = u_ref[...]
    ub = u.astype(BF16)
    bu = [_dg(ub[:, k * LANES:(k + 1) * LANES], bb_ref[k], NN) for k in range(S5_NBLK)]
    bu_re = jnp.concatenate([x[:, :S5_BCH] for x in bu], axis=1)
    bu_im = jnp.concatenate([x[:, S5_BCH:] for x in bu], axis=1)
    h0 = h0_ref[...]
    lr, li = lam_ref[:, :S5_CH], lam_ref[:, S5_CH:]
    hr, hi = h0[:, :S5_CH], h0[:, S5_CH:]
    nr = lr * hr - li * hi + bu_re
    ni = lr * hi + li * hr + bu_im
    hout_ref[...] = jnp.concatenate([nr, ni], axis=1)
    ys = []
    for k in range(S5_NBLK):
        sl = slice(k * S5_BCH, (k + 1) * S5_BCH)
        ys.append(_dg(jnp.concatenate([nr[:, sl], ni[:, sl]], axis=1).astype(BF16), cc_ref[k], NN))
    o_ref[...] = _s5_gate(jnp.concatenate(ys, axis=1), u, d_ref, wglu_ref, bglu_ref)


def _s5_sample(p3, geo, h0, prm, li, out_prev):
    rows = geo.sps
    sb = geo.sample_blk(rows)
    ucol = p3.shape[-1] // S5_WIDTH - 1
    fix = lambda k: (0, 0)
    st = pl.BlockSpec((rows, 2 * S5_CH), lambda k: (k, 0))
    return pl.pallas_call(
        functools.partial(_drop_arg, _s5_step_kernel, 8),
        grid=(NSLAB,),
        in_specs=[pl.BlockSpec((None, rows, S5_WIDTH), lambda k: sb(k) + (ucol,))] + _s5_specs(fix)
        + [st, pl.BlockSpec(memory_space=pl.ANY)],
        out_specs=[pl.BlockSpec((None, rows, S5_WIDTH), lambda k: sb(k) + (0,)), st],
        out_shape=[jax.ShapeDtypeStruct(out_prev.shape, F32), jax.ShapeDtypeStruct(h0.shape, F32)],
        input_output_aliases={8: 0},
        compiler_params=_cp("parallel"),
        name="s5_step",
    )(p3, *_s5_params(prm, li), h0, out_prev)


def _head_sum(x, ones_bd):
    ones = ones_bd.astype(BF16)
    hi = x.astype(BF16)
    lo = (x - hi.astype(F32)).astype(BF16)
    return _dg(hi, ones, NN) + _dg(lo, ones, NN)


def _rwkv_prep_body(pr, prev, mu_ref, w0_ref, w2_ref, a0_ref, a2_ref, g2_ref, kk_ref, ka_ref, ones_ref,
                    r_ref, lw_ref, k_ref, v_ref, a_ref, b_ref, g_ref):
    W = RWKV_WIDTH
    xm = pr + (prev - pr) * mu_ref[...]
    r = xm[:, :W]
    k = xm[:, W:2 * W]
    v = xm[:, 2 * W:3 * W]
    xwa = xm[:, 3 * W:3 * W + LORA_W + LORA_A]
    xg = xm[:, 3 * W + LORA_W + LORA_A:]
    wl = w0_ref[...] + _bdot(jnp.tanh(xwa), w2_ref[...])
    sp = jnp.maximum(-wl, 0.0) + jnp.log(1.0 + jnp.exp(-jnp.abs(wl)))
    w = -sp - 0.5
    lw = -jnp.exp(w)
    a = jax.nn.sigmoid(a0_ref[...] + _bdot(xwa, a2_ref[...]))
    g = _bdot(jax.nn.sigmoid(xg), g2_ref[...])
    kk = k * kk_ref[...]
    nrm = jnp.sqrt(_head_sum(kk * kk, ones_ref[...]))
    kk = kk / jnp.maximum(nrm, 1e-12)
    k2 = k * (1.0 + (a - 1.0) * ka_ref[...])
    r_ref[...] = r
    lw_ref[...] = lw
    k_ref[...] = k2
    v_ref[...] = v
    a_ref[...] = -kk
    b_ref[...] = kk * a
    g_ref[...] = g


def _rwkv_prep_prompt_kernel(pr_ref, prev8_ref, *rest, tm, padf):
    pr = pr_ref[...]
    t = lax.broadcasted_iota(jnp.int32, (tm, 1), 0)
    last = jnp.broadcast_to(prev8_ref[SUB - 1:SUB, :], pr.shape)
    prev = jnp.where(t == 0, last, pltpu.roll(pr, 1, 0))
    first = (t + pl.program_id(1) * tm) == padf
    prev = jnp.where(first, 0.0, prev)
    _rwkv_prep_body(pr, prev, *rest)


def _rwkv_prep_sample_kernel(pr_ref, prev_ref, *rest):
    _rwkv_prep_body(pr_ref[...], prev_ref[...], *rest[:9], *rest[9 + 7:])


def _rwkv_prep(p3, geo, shift_s, prm, li):
    W = RWKV_WIDTH
    zeros64 = jnp.zeros((LORA_W, W), F32)
    w2p = jnp.concatenate([prm['rwkv_w2'][li], zeros64], 0).astype(BF16)
    a2p = jnp.concatenate([zeros64, prm['rwkv_a2'][li]], 0).astype(BF16)
    hid = jnp.arange(W) // RWKV_HEAD
    ones_bd = (hid[:, None] == hid[None, :]).astype(F32)
    consts = [prm['rwkv_mu'][li].reshape(1, -1), prm['rwkv_w0'][li].reshape(1, -1), w2p,
              prm['rwkv_a0'][li].reshape(1, -1), a2p, prm['rwkv_g2'][li].astype(BF16),
              prm['rwkv_k_k'][li].reshape(1, -1), prm['rwkv_k_a'][li].reshape(1, -1), ones_bd]
    outs_shape = [jax.ShapeDtypeStruct((NSLAB, geo.slab, W), F32)] * 7
    tm = _div_tile(geo.tp, SEQ_TILE)
    outs = pl.pallas_call(
        functools.partial(_rwkv_prep_prompt_kernel, tm=tm, padf=geo.padf),
        grid=(NSLAB, geo.tp // tm),
        in_specs=[pl.BlockSpec((None, tm, RWKV_PROJ), lambda b, c: (b, c, 0)),
                  pl.BlockSpec((None, SUB, RWKV_PROJ), lambda b, c: (b, jnp.maximum(c * (tm // SUB) - 1, 0), 0))]
        + [pl.BlockSpec(x.shape, lambda b, c: (0, 0)) for x in consts],
        out_specs=[pl.BlockSpec((None, tm, W), lambda b, c: (b, c, 0))] * 7,
        out_shape=outs_shape,
        compiler_params=_cp("parallel", "parallel"),
        name="rwkv_prep_prompt",
    )(p3, p3, *consts)
    rows = geo.sps
    sb = geo.sample_blk(rows)
    nin = 2 + len(consts)
    outs = pl.pallas_call(
        _rwkv_prep_sample_kernel,
        grid=(NSLAB,),
        in_specs=[pl.BlockSpec((None, rows, RWKV_PROJ), lambda k: sb(k) + (0,)),
                  pl.BlockSpec((rows, RWKV_PROJ), lambda k: (k, 0))]
        + [pl.BlockSpec(x.shape, lambda k: (0, 0)) for x in consts] + [pl.BlockSpec(memory_space=pl.ANY)] * 7,
        out_specs=[pl.BlockSpec((None, rows, W), lambda k: sb(k) + (0,))] * 7,
        out_shape=outs_shape,
        input_output_aliases={nin + j: j for j in range(7)},
        compiler_params=_cp("parallel"),
        name="rwkv_prep_sample",
    )(p3, shift_s, *consts, *outs)
    return outs


def _wkv_chunk_kernel(r_ref, lw_ref, k_ref, v_ref, a_ref, b_ref, gate_ref, gng_ref, gnb_ref, rk_ref, ones_ref,
                      y_ref, gout_ref, g_scr, *, padf):
    c = pl.program_id(1)
    L, HG = WKV_L, WKV_HG
    NG = RWKV_HEADS // HG
    WD = HG * RWKV_HEAD
    M = HG * L

    @pl.when(c == 0)
    def _():
        g_scr[...] = jnp.zeros_like(g_scr)

    real = (lax.broadcasted_iota(jnp.int32, (L, 1), 0) + c * L) >= padf

    def ld(ref):
        return jnp.where(real, jnp.concatenate([ref[i] for i in range(WKV_NB)], axis=1), 0.0)

    r, lw, k, v, a, b = ld(r_ref), ld(lw_ref), ld(k_ref), ld(v_ref), ld(a_ref), ld(b_ref)
    ti = lax.broadcasted_iota(jnp.int32, (L, L), 0)
    tj = lax.broadcasted_iota(jnp.int32, (L, L), 1)
    tri = jnp.where(ti >= tj, 1.0, 0.0)
    cum = _dot3(tri, lw)
    dec = jnp.exp(cum)
    inv = jnp.exp(-cum)
    at = a * jnp.exp(cum - lw)
    bt = b * inv
    kt = k * inv
    rt = r * dec
    rh = lax.broadcasted_iota(jnp.int32, (M, WD), 0) // L
    lh = lax.broadcasted_iota(jnp.int32, (M, WD), 1) // RWKV_HEAD
    bd = rh == lh
    mi = lax.broadcasted_iota(jnp.int32, (M, M), 0)
    mj = lax.broadcasted_iota(jnp.int32, (M, M), 1)
    same = (mi // L) == (mj // L)
    m_strict = jnp.logical_and(same, (mj % L) < (mi % L))
    m_incl = jnp.logical_and(same, (mj % L) <= (mi % L))
    eye = jnp.where(mi == mj, 1.0, 0.0)

    def stack(x):
        return jnp.where(bd, jnp.concatenate([x] * HG, axis=0), 0.0)

    chains = range(WKV_NB * NG)
    sls = [slice(i * WD, (i + 1) * WD) for i in chains]
    a_s = [stack(at[:, sl]).astype(BF16) for sl in sls]
    r_s = [stack(rt[:, sl]).astype(BF16) for sl in sls]
    v_s = [stack(v[:, sl]).astype(BF16) for sl in sls]
    bk_s = [jnp.concatenate([stack(bt[:, sl]), stack(kt[:, sl])], axis=0).astype(BF16) for sl in sls]
    g = [g_scr[i] for i in chains]
    gb = [x.astype(BF16) for x in g]
    prod = [_dg(jnp.concatenate([a_s[i], r_s[i]], axis=0), bk_s[i], NT) for i in chains]
    ab = [jnp.where(m_strict, p[:M, :M], 0.0) for p in prod]
    ak = [jnp.where(m_strict, p[:M, M:], 0.0).astype(BF16) for p in prod]
    rb = [jnp.where(m_incl, p[M:, :M], 0.0).astype(BF16) for p in prod]
    rk = [jnp.where(m_incl, p[M:, M:], 0.0).astype(BF16) for p in prod]
    tinv = [eye + x for x in ab]
    pw = ab
    for _ in range(int(math.log2(L)) - 1):
        pw = [_dotc(x, x) for x in pw]
        tinv = [t + _dotc(t, x) for t, x in zip(tinv, pw)]
    z = [_dg(a_s[i], gb[i], NT) + _dg(ak[i], v_s[i], NN) for i in chains]
    u = [_dotc(tinv[i], z[i]).astype(BF16) for i in chains]
    ybd = [_dg(r_s[i], gb[i], NT) + _dg(rb[i], u[i], NN) + _dg(rk[i], v_s[i], NN) for i in chains]
    upd = [_dg(jnp.concatenate([u[i], v_s[i]], axis=0), bk_s[i], TN) for i in chains]
    ones = ones_ref[...]
    gate = jnp.concatenate([gate_ref[i] for i in range(WKV_NB)], axis=1)
    cls = [slice((i % NG) * WD, (i % NG + 1) * WD) for i in chains]
    ys = jnp.concatenate([functools.reduce(lambda p, q: p + q, [ybd[i][h * L:(h + 1) * L] for h in range(HG)])
                          for i in chains], axis=0)
    rkr = jnp.concatenate([r[:, sls[i]] * k[:, sls[i]] * rk_ref[:, cls[i]] for i in chains], axis=0)
    yc = ys - _head_sum(ys, ones) * (1.0 / RWKV_HEAD)
    yn = yc * lax.rsqrt(_head_sum(yc * yc, ones) * (1.0 / RWKV_HEAD) + RWKV_GN_EPS)
    bsum = _head_sum(rkr, ones)
    g_last = []
    for i in chains:
        rows = slice(i * L, (i + 1) * L)
        out = yn[rows] * gng_ref[:, cls[i]] + gnb_ref[:, cls[i]] + bsum[rows] * v[:, sls[i]]
        y_ref[i // NG, :, cls[i]] = out * gate[:, sls[i]]
        g_new = (g[i] + upd[i]) * dec[L - 1:L, sls[i]]
        g_scr[i] = g_new
        g_last.append(g_new)

    @pl.when(c == pl.num_programs(1) - 1)
    def _():
        for grp, g_new in enumerate(g_last):
            gout_ref[grp // NG, grp % NG] = g_new


def _wkv_prompt(seq, gate, geo, prm, li):
    L = WKV_L
    NG = RWKV_HEADS // WKV_HG
    WD = WKV_HG * RWKV_HEAD
    spec = pl.BlockSpec((WKV_NB, L, RWKV_WIDTH), lambda b, c: (b, c, 0))
    vec = pl.BlockSpec((1, RWKV_WIDTH), lambda b, c: (0, 0))
    hid = jnp.arange(WD) // RWKV_HEAD
    ones_bd = (hid[:, None] == hid[None, :]).astype(F32)
    y, g = pl.pallas_call(
        functools.partial(_wkv_chunk_kernel, padf=geo.padf),
        grid=(NSLAB // WKV_NB, geo.tp // L),
        in_specs=[spec] * 7 + [vec] * 3 + [pl.BlockSpec((WD, WD), lambda b, c: (0, 0))],
        out_specs=[spec, pl.BlockSpec((WKV_NB, NG, WD, WD), lambda b, c: (b, 0, 0, 0))],
        out_shape=[jax.ShapeDtypeStruct((NSLAB, geo.slab, RWKV_WIDTH), F32),
                   jax.ShapeDtypeStruct((NSLAB, NG, WD, WD), F32)],
        scratch_shapes=[pltpu.VMEM((WKV_NB * NG, WD, WD), F32)],
        compiler_params=_cp("parallel", "arbitrary"),
        name="wkv_chunk",
    )(*seq, gate, prm['rwkv_gn_g'][li].reshape(1, -1), prm['rwkv_gn_b'][li].reshape(1, -1),
      prm['rwkv_r_k'][li].reshape(1, -1), ones_bd)
    g = g.reshape(NSLAB, NG, WKV_HG, RWKV_HEAD, WKV_HG, RWKV_HEAD)
    idx = jnp.arange(WKV_HG)
    s = g[:, :, idx, :, idx, :]
    s = jnp.transpose(s, (1, 2, 0, 3, 4)).reshape(NSLAB, RWKV_HEADS, RWKV_HEAD, RWKV_HEAD)
    return y, s


def _wkv_step_kernel(r_ref, lw_ref, k_ref, v_ref, a_ref, b_ref, s_ref, y_ref, sout_ref):
    H, D = RWKV_HEADS, RWKV_HEAD
    ri = lax.broadcasted_iota(jnp.int32, (H * D, D), 0) % D
    ci = lax.broadcasted_iota(jnp.int32, (H * D, D), 1)
    eye = jnp.where(ri == ci, 1.0, 0.0)

    def expand(x):
        return jnp.concatenate([jnp.broadcast_to(x[:, h * D:(h + 1) * D], (D, D)) for h in range(H)], axis=0)

    js = range(SUB)
    row = lambda ref, j: ref[j:j + 1, :]
    s = [s_ref[j].reshape(H * D, D) for j in js]
    v_col = [jnp.sum(expand(row(v_ref, j)) * eye, axis=1, keepdims=True) for j in js]
    sa = [jnp.sum(s[j] * expand(row(a_ref, j)), axis=1, keepdims=True) for j in js]
    s_new = [s[j] * jnp.exp(expand(row(lw_ref, j))) + sa[j] * expand(row(b_ref, j)) + v_col[j] * expand(row(k_ref, j))
             for j in js]
    y_col = [jnp.sum(s_new[j] * expand(row(r_ref, j)), axis=1, keepdims=True) for j in js]
    for j in js:
        ye = y_col[j] * eye
        y_row = jnp.concatenate([jnp.sum(ye[h * D:(h + 1) * D], axis=0, keepdims=True) for h in range(H)], axis=1)
        y_ref[j:j + 1, :] = y_row
        sout_ref[j] = s_new[j].reshape(H, D, D)


def _wkv_sample(seq, geo, s0, y_prev):
    sb = geo.sample_blk(SUB)
    spec = pl.BlockSpec((None, SUB, RWKV_WIDTH), lambda k: sb(k) + (0,))
    sspec = pl.BlockSpec((SUB, RWKV_HEADS, RWKV_HEAD, RWKV_HEAD), lambda k: (k, 0, 0, 0))
    return pl.pallas_call(
        functools.partial(_drop_arg, _wkv_step_kernel, 7),
        grid=(geo.ns // SUB,),
        in_specs=[spec] * 6 + [sspec, pl.BlockSpec(memory_space=pl.ANY)],
        out_specs=[spec, sspec],
        out_shape=[jax.ShapeDtypeStruct(y_prev.shape, F32), jax.ShapeDtypeStruct(s0.shape, F32)],
        input_output_aliases={7: 0},
        compiler_params=_cp("parallel"),
        name="wkv_step",
    )(*seq, s0, y_prev)


def _rwkv_post_kernel(y_ref, r_ref, k_ref, v_ref, g_ref, gng_ref, gnb_ref, rk_ref, ones_ref, o_ref):
    y = y_ref[...]
    ones = ones_ref[...]
    mu = _head_sum(y, ones) * (1.0 / RWKV_HEAD)
    yc = y - mu
    var = _head_sum(yc * yc, ones) * (1.0 / RWKV_HEAD)
    yn = yc * lax.rsqrt(var + RWKV_GN_EPS) * gng_ref[...] + gnb_ref[...]
    v = v_ref[...]
    bonus = _head_sum(r_ref[...] * k_ref[...] * rk_ref[...], ones) * v
    o_ref[...] = (yn + bonus) * g_ref[...]


def _rwkv_post_sample(y, r, k, v, g, geo, prm, li):
    W = RWKV_WIDTH
    hid = jnp.arange(W) // RWKV_HEAD
    ones_bd = (hid[:, None] == hid[None, :]).astype(F32)
    sb = geo.sample_blk(geo.sps)
    row = pl.BlockSpec((None, geo.sps, W), lambda i: sb(i) + (0,))
    vec = pl.BlockSpec((1, W), lambda i: (0, 0))
    return pl.pallas_call(
        _rwkv_post_kernel,
        grid=(NSLAB,),
        in_specs=[row] * 5 + [vec] * 3 + [pl.BlockSpec((W, W), lambda i: (0, 0))],
        out_specs=row,
        out_shape=jax.ShapeDtypeStruct(y.shape, F32),
        input_output_aliases={0: 0},
        compiler_params=_cp("parallel"),
        name="rwkv_post",
    )(y, r, k, v, g, prm['rwkv_gn_g'][li].reshape(1, -1), prm['rwkv_gn_b'][li].reshape(1, -1),
      prm['rwkv_r_k'][li].reshape(1, -1), ones_bd)


def _mla_prep_kernel(ckv_ref, qa_ref, kr_ref, cos_ref, sin_ref, qn_ref, kvn_ref, wqb_ref, wuk_ref,
                     qlat_ref, qrope_ref, c_ref, kr8_ref):
    qa = qa_ref[:, :Q_LORA]
    qa = qa * lax.rsqrt(jnp.mean(qa * qa, -1, keepdims=True) + 1e-6) * qn_ref[...]
    q = _bdot(qa, wqb_ref[...])
    nope_w = MLA_HEADS * QK_NOPE
    cos, sin = cos_ref[...], sin_ref[...]
    qrope_ref[...] = _rope(q[:, nope_w:], cos, sin, QK_ROPE // 2)
    hp = LANES // QK_NOPE
    for k in range(MLA_HEADS // hp):
        qlat_ref[:, k * hp * KV_LORA:(k + 1) * hp * KV_LORA] = _bdot(q[:, k * LANES:(k + 1) * LANES], wuk_ref[k])
    ckv = ckv_ref[...]
    c_ref[...] = ckv * lax.rsqrt(jnp.mean(ckv * ckv, -1, keepdims=True) + 1e-6) * kvn_ref[...]
    kr8_ref[...] = _rope(kr_ref[...], cos, sin, QK_ROPE // 2)


def _mla_prep(p, geo, pos_slab, prm, li):
    n = p.shape[0]
    tm, tps = geo.tm, geo.tps
    rw = MLA_HEADS * QK_ROPE
    cos, sin = _rope_tables(pos_slab, QK_ROPE, MLA_HEADS)
    wqb = prm['mla_w_qb'][li].reshape(Q_LORA, MLA_HEADS, QK_NOPE + QK_ROPE)
    wqb = jnp.concatenate([wqb[:, :, :QK_NOPE].reshape(Q_LORA, -1), wqb[:, :, QK_NOPE:].reshape(Q_LORA, -1)], 1)
    eye = jnp.eye(MLA_HEADS, dtype=F32)
    wuk = jnp.einsum('chn,hg->hngc', prm['mla_w_uk'][li], eye).reshape(MLA_HEADS * QK_NOPE, MLA_HEADS * KV_LORA)
    hp = LANES // QK_NOPE
    wuk = jnp.stack([wuk[k * LANES:(k + 1) * LANES, k * hp * KV_LORA:(k + 1) * hp * KV_LORA]
                     for k in range(MLA_HEADS // hp)])
    fix = lambda i: (0, 0)
    c0 = RWKV_PROJ // KV_LORA
    q0 = (RWKV_PROJ + KV_LORA) // 512
    k0 = (RWKV_PROJ + KV_LORA + 512) // rw
    tab = pl.BlockSpec((tm, rw), lambda i: (i % tps, 0))
    return pl.pallas_call(
        _mla_prep_kernel,
        grid=(n // tm,),
        in_specs=[pl.BlockSpec((tm, KV_LORA), lambda i: (i, c0)), pl.BlockSpec((tm, 512), lambda i: (i, q0)),
                  pl.BlockSpec((tm, rw), lambda i: (i, k0)), tab, tab,
                  pl.BlockSpec((1, Q_LORA), fix), pl.BlockSpec((1, KV_LORA), fix),
                  pl.BlockSpec(wqb.shape, fix), pl.BlockSpec(wuk.shape, lambda i: (0, 0, 0))],
        out_specs=[pl.BlockSpec((tm, MLA_HEADS * KV_LORA), lambda i: (i, 0)), pl.BlockSpec((tm, rw), lambda i: (i, 0)),
                   pl.BlockSpec((tm, KV_LORA), lambda i: (i, 0)), pl.BlockSpec((tm, rw), lambda i: (i, 0))],
        out_shape=[jax.ShapeDtypeStruct((n, MLA_HEADS * KV_LORA), F32), jax.ShapeDtypeStruct((n, rw), F32),
                   jax.ShapeDtypeStruct((n, KV_LORA), F32), jax.ShapeDtypeStruct((n, rw), F32)],
        compiler_params=_cp("parallel"),
        name="mla_prep",
    )(p, p, p, cos, sin, prm['mla_q_norm'][li].reshape(1, -1), prm['mla_kv_norm'][li].reshape(1, -1),
      wqb.astype(BF16), wuk.astype(BF16))


def _wuv_blockdiag(w_uv):
    eye = jnp.eye(MLA_HEADS, dtype=F32)
    return jnp.einsum('chv,hg->hcgv', w_uv, eye).reshape(MLA_HEADS * KV_LORA, MLA_WIDTH).astype(BF16)


def _mla_attn_kernel(qi_ref, ki_ref, qlat_ref, qrope_ref, c_ref, kr_ref, wuv_ref, o_ref, qs_scr, qrs_scr, m_scr,
                     l_scr, acc_scr, *, lq, padf):
    qi, ki = qi_ref[pl.program_id(1)], ki_ref[pl.program_id(1)]
    H = MLA_HEADS

    @pl.when(ki == 0)
    def _():
        ql = qlat_ref[...]
        qs_scr[...] = jnp.concatenate([ql[:, h * KV_LORA:(h + 1) * KV_LORA] for h in range(H)], axis=0).astype(BF16)
        qr = qrope_ref[...].astype(BF16)
        lane = lax.broadcasted_iota(jnp.int32, qr.shape, 1) // QK_ROPE
        qrs_scr[...] = jnp.concatenate([jnp.where(lane == h, qr, jnp.zeros_like(qr)) for h in range(H)], axis=0)
        m_scr[...] = jnp.full_like(m_scr, NEG)
        l_scr[...] = jnp.zeros_like(l_scr)
        acc_scr[...] = jnp.zeros_like(acc_scr)

    def tile(kind):
        c2 = MLA_SCALE * math.log2(math.e)
        cb = c_ref[...].astype(BF16)
        st = _dg(cb, qs_scr[...], NT) + _dg(kr_ref[...].astype(BF16), qrs_scr[...], NT)
        if kind == "diagonal":
            kpos = ki * lq + lax.broadcasted_iota(jnp.int32, st.shape, 0)
            qpos = qi * lq + lax.broadcasted_iota(jnp.int32, st.shape, 1) % lq
            st = jnp.where(jnp.logical_and(kpos <= qpos, kpos >= padf), st, NEG)
        elif kind == "first":
            st = jnp.where(lax.broadcasted_iota(jnp.int32, (lq, 1), 0) >= padf, st, NEG)
        m_old = m_scr[...]
        m_new = jnp.maximum(m_old, jnp.max(st, axis=0, keepdims=True))
        alpha = jnp.exp2((m_old - m_new) * c2)
        pt = jnp.exp2((st - m_new) * c2)
        l_scr[...] = alpha * l_scr[...] + jnp.sum(pt, axis=0, keepdims=True)
        acc_scr[...] = alpha * acc_scr[...] + _dg(cb, pt.astype(BF16), TN)
        m_scr[...] = m_new

    assert padf <= lq
    pl.when(ki == qi)(functools.partial(tile, "diagonal"))
    pl.when(jnp.logical_and(ki < qi, ki == 0))(functools.partial(tile, "first"))
    pl.when(jnp.logical_and(ki < qi, ki > 0))(functools.partial(tile, "plain"))

    @pl.when(ki == qi)
    def _():
        ot = (acc_scr[...] * (1.0 / l_scr[...])).astype(BF16)
        y = jnp.zeros(o_ref.shape, F32)
        for h in range(H):
            y = y + _dg(ot[:, h * lq:(h + 1) * lq], wuv_ref[h], TN)
        o_ref[...] = y


def _mla_attn_prompt(qlat3, qrope3, c3, kr3, w_uv, geo):
    lq = _div_tile(geo.tp, SEQ_TILE)
    nq = geo.tp // lq
    wuv = _wuv_blockdiag(w_uv).reshape(MLA_HEADS, KV_LORA, MLA_WIDTH)
    rw = MLA_HEADS * QK_ROPE
    pairs = [(qi, ki) for qi in range(nq) for ki in range(qi + 1)]
    qis = jnp.asarray([p[0] for p in pairs], jnp.int32)
    kis = jnp.asarray([p[1] for p in pairs], jnp.int32)
    qmap = lambda b, p, qr, kr: (b, qr[p], 0)
    kmap = lambda b, p, qr, kr: (b, kr[p], 0)
    grid_spec = pltpu.PrefetchScalarGridSpec(
        num_scalar_prefetch=2,
        grid=(NSLAB, len(pairs)),
        in_specs=[pl.BlockSpec((None, lq, MLA_HEADS * KV_LORA), qmap), pl.BlockSpec((None, lq, rw), qmap),
                  pl.BlockSpec((None, lq, KV_LORA), kmap), pl.BlockSpec((None, lq, rw), kmap),
                  pl.BlockSpec(wuv.shape, lambda b, p, qr, kr: (0, 0, 0))],
        out_specs=pl.BlockSpec((None, lq, MLA_WIDTH), qmap),
        scratch_shapes=[pltpu.VMEM((MLA_HEADS * lq, KV_LORA), BF16), pltpu.VMEM((MLA_HEADS * lq, rw), BF16),
                        pltpu.VMEM((1, MLA_HEADS * lq), F32), pltpu.VMEM((1, MLA_HEADS * lq), F32),
                        pltpu.VMEM((KV_LORA, MLA_HEADS * lq), F32)],
    )
    return pl.pallas_call(
        functools.partial(_mla_attn_kernel, lq=lq, padf=geo.padf),
        grid_spec=grid_spec,
        out_shape=jax.ShapeDtypeStruct((NSLAB, geo.slab, MLA_WIDTH), F32),
        compiler_params=_cp("parallel", "arbitrary"),
        name="mla_attn",
    )(qis, kis, qlat3, qrope3, c3, kr3, wuv)


def _mla_decode_kernel(pt_ref, qlat_ref, qrope_ref, cnew_ref, krnew_ref, kv_hbm, kr_hbm, tile_ref, wuv_ref,
                       yprev_ref, o_ref, kv_buf, kr_buf, kb_scr, s_scr, sem, *, li, n_pages, ch):
    del yprev_ref
    b = pl.program_id(0)
    slot = b % 2
    H = MLA_HEADS

    def page_copies(page, sl, pg):
        return (pltpu.make_async_copy(kv_hbm.at[page, li], kv_buf.at[sl, pg], sem.at[0, sl]),
                pltpu.make_async_copy(kr_hbm.at[page, li], kr_buf.at[sl, pg], sem.at[1, sl]))

    def start_fetch(bb, sl):
        def body(pg, carry):
            for cp in page_copies(pt_ref[bb, pg], sl, pg):
                cp.start()
            return carry
        lax.fori_loop(0, n_pages, body, 0)

    def wait_fetch(sl):
        for pg in range(n_pages):
            for cp in page_copies(0, sl, pg):
                cp.wait()

    @pl.when(b == 0)
    def _():
        start_fetch(0, 0)

    nxt = (b + 1) % pl.num_programs(0)

    j = b % SUB
    rid = lax.broadcasted_iota(jnp.int32, (SUB, 1), 0)

    def pick(ref):
        return jnp.sum(jnp.where(rid == j, ref[...], 0.0), axis=0, keepdims=True)

    ql = pick(qlat_ref)
    qs = jnp.concatenate([ql[:, h * KV_LORA:(h + 1) * KV_LORA] for h in range(H)], axis=0).astype(BF16)
    qr = jnp.broadcast_to(pick(qrope_ref), (H, H * QK_ROPE))
    hl = lax.broadcasted_iota(jnp.int32, qr.shape, 1) // QK_ROPE
    hr = lax.broadcasted_iota(jnp.int32, qr.shape, 0)
    qrs = jnp.where(hl == hr, qr, 0.0).astype(BF16)
    qr32 = _dg(qrs, tile_ref[...], NT).astype(BF16)
    cn = _rb(pick(cnew_ref))
    kn = _rb(pick(krnew_ref))
    s_self = (jnp.sum(qs.astype(F32) * cn, axis=1, keepdims=True)
              + jnp.sum(qrs.astype(F32) * kn, axis=1, keepdims=True)) * MLA_SCALE

    wait_fetch(slot)
    cw = ch * PAGE_SIZE
    for ci in range(n_pages // ch):
        for pg in range(ci * ch, (ci + 1) * ch):
            for cp in page_copies(pt_ref[nxt, pg], 1 - slot, pg):
                cp.start()
        kc = kv_buf[slot, ci * ch:(ci + 1) * ch].reshape(cw, KV_LORA).astype(BF16)
        kb_scr[ci * cw:(ci + 1) * cw, :] = kc
        krc = jnp.concatenate([kr_buf[slot, pg] for pg in range(ci * ch, (ci + 1) * ch)], axis=1).astype(BF16)
        s_scr[:, ci * cw:(ci + 1) * cw] = (_dg(qs, kc, NT) + _dg(qr32, krc, NN)) * MLA_SCALE
    s = s_scr[...]
    m = jnp.maximum(jnp.max(s, axis=1, keepdims=True), s_self)
    e = jnp.exp(s - m)
    e_self = jnp.exp(s_self - m)
    rl = 1.0 / (jnp.sum(e, axis=1, keepdims=True) + e_self)
    p = (e * rl).astype(BF16)
    acc = _rb(e_self * rl) * cn
    for ci in range(n_pages // ch):
        acc = acc + _dg(p[:, ci * cw:(ci + 1) * cw], kb_scr[ci * cw:(ci + 1) * cw, :], NN)
    ol = jnp.concatenate([acc[h:h + 1] for h in range(H)], axis=1)
    y = _bdot(jnp.broadcast_to(ol, (SUB, ol.shape[1])), wuv_ref[...])
    o_ref[pl.ds(j, 1), :] = y[0:1]

    @pl.when(b == pl.num_programs(0) - 1)
    def _():
        wait_fetch(1 - slot)


def _mla_decode(qlat, qrope, c, kr8, cache_kv, cache_kr, page_table, w_uv, li, y_prev, geo):
    n = qlat.shape[0]
    n_pages = page_table.shape[1]
    ch = _div_tile(n_pages, 8, 1)
    rw = MLA_HEADS * QK_ROPE
    wuv = _wuv_blockdiag(w_uv)
    tile = jnp.tile(jnp.eye(QK_ROPE, dtype=F32), (1, MLA_HEADS)).astype(BF16)

    def rowmap(b, pt):
        return ((b // geo.sps) * geo.slab + geo.tp + b % geo.sps) // SUB, 0

    rowblk = lambda w: pl.BlockSpec((SUB, w), rowmap)
    grid_spec = pltpu.PrefetchScalarGridSpec(
        num_scalar_prefetch=1,
        grid=(geo.ns,),
        in_specs=[rowblk(MLA_HEADS * KV_LORA), rowblk(rw), rowblk(KV_LORA), rowblk(rw),
                  pl.BlockSpec(memory_space=pl.ANY), pl.BlockSpec(memory_space=pl.ANY),
                  pl.BlockSpec(tile.shape, lambda b, pt: (0, 0)),
                  pl.BlockSpec(wuv.shape, lambda b, pt: (0, 0)),
                  pl.BlockSpec(memory_space=pl.ANY)],
        out_specs=rowblk(MLA_WIDTH),
        scratch_shapes=[pltpu.VMEM((2, n_pages, PAGE_SIZE, KV_LORA), F32),
                        pltpu.VMEM((2, n_pages, QK_ROPE, PAGE_SIZE), F32),
                        pltpu.VMEM((n_pages * PAGE_SIZE, KV_LORA), BF16),
                        pltpu.VMEM((MLA_HEADS, n_pages * PAGE_SIZE), F32),
                        pltpu.SemaphoreType.DMA((2, 2))],
    )
    return pl.pallas_call(
        functools.partial(_mla_decode_kernel, li=li, n_pages=n_pages, ch=ch),
        grid_spec=grid_spec,
        out_shape=jax.ShapeDtypeStruct((n, MLA_WIDTH), F32),
        input_output_aliases={9: 0},
        compiler_params=_cp("arbitrary"),
        name="mla_decode",
    )(page_table, qlat, qrope, c, kr8, cache_kv, jnp.swapaxes(cache_kr, 2, 3), tile, wuv, y_prev)


def _forward(x_prompt, x_sample, state_ret, state_s5_re, state_s5_im, state_wkv, state_shift, cache_kv,
             cache_krope, page_table, meta_tokens, prm):
    nb, seq, _ = x_prompt.shape
    assert nb == NSLAB and x_sample.shape[1] == 1
    ns = x_sample.shape[0]
    geo = _Geom(seq, ns, page_table.shape[1])
    n, tm, tp, padf, sps = geo.n, geo.tm, geo.tp, geo.padf, geo.sps

    meta = jnp.broadcast_to(meta_tokens[None], (nb, N_META, D_MODEL))
    x = jnp.concatenate([jnp.zeros((nb, padf, D_MODEL), F32), meta, x_prompt, x_sample.reshape(nb, sps, D_MODEL)], 1)
    x = x.reshape(n, D_MODEL)
    pos_slab = jnp.concatenate([jnp.arange(tp, dtype=jnp.int32) - padf, jnp.full((sps,), geo.past, jnp.int32)])

    ret_p, ret_s, s5_p, s5_s, wkv_p, wkv_s, sh_p, sh_s, kv_l, kr_l = [], [], [], [], [], [], [], [], [], []
    for layer in range(DEPTH):
        li = layer // 2
        if layer % 2 == 0:
            w_in = prm['ev_w_in'][li].astype(BF16)
            p = _matmul(x, w_in, tm, _div_tile(w_in.shape[1], 1024, 128))
            p3 = geo.v3(p)
            gng, gnb = prm['ret_gn_g'][li], prm['ret_gn_b'][li]
            ret, s_p = _retention_prompt(p3, geo, gng, gnb)
            ret, s_s = _retention_sample(p3, geo, state_ret[li], gng, gnb, ret)
            s5o, h_p = _s5_prompt(p3, geo, prm, li)
            h0 = jnp.concatenate([state_s5_re[li].reshape(ns, S5_CH), state_s5_im[li].reshape(ns, S5_CH)], 1)
            s5o, h_s = _s5_sample(p3, geo, h0, prm, li, s5o)
            ret_p.append(s_p)
            ret_s.append(s_s)
            s5_p.append(h_p)
            s5_s.append(h_s)
            x = _proj_ln(ret.reshape(n, -1), s5o.reshape(n, -1), x, prm['ev_w_out'][li], prm['ln1_g'][layer],
                         prm['ln1_b'][layer], tm)
        else:
            w = prm['od_w_in'][li]
            o1 = RWKV_PROJ
            o2 = o1 + Q_LORA
            o3 = o2 + KV_LORA
            w_in = jnp.concatenate([w[:, :o1], w[:, o2:o3], w[:, o1:o2], jnp.zeros((D_MODEL, 512 - Q_LORA), F32),
                                    jnp.tile(w[:, o3:], (1, MLA_HEADS))], 1).astype(BF16)
            p = _matmul(x, w_in, tm, _div_tile(w_in.shape[1], 1408, 128))
            p3 = geo.v3(p)
            sh_p.append(p3[:, tp - 1, :RWKV_PROJ])
            sh_s.append(p3[:, tp:, :RWKV_PROJ].reshape(ns, RWKV_PROJ))
            r, lw, k, v, a, b, g = _rwkv_prep(p3, geo, state_shift[li], prm, li)
            seqs = (r, lw, k, v, a, b)
            y, g_p = _wkv_prompt(seqs, g, geo, prm, li)
            y, g_s = _wkv_sample(seqs, geo, state_wkv[li], y)
            wkv_p.append(g_p)
            wkv_s.append(g_s)
            flat = lambda t: t.reshape(n, t.shape[-1])
            y_c = flat(_rwkv_post_sample(y, r, k, v, g, geo, prm, li))
            qlat, qrope, c, kr8 = _mla_prep(p, geo, pos_slab, prm, li)
            kv_l.append(geo.v3(c))
            kr_l.append(geo.v3(kr8)[:, :, :QK_ROPE])
            y_d = _mla_attn_prompt(geo.v3(qlat), geo.v3(qrope), geo.v3(c), geo.v3(kr8), prm['mla_w_uv'][li], geo)
            y_d = _mla_decode(qlat, qrope, c, kr8, cache_kv, cache_krope, page_table, prm['mla_w_uv'][li], li,
                              flat(y_d), geo)
            x = _proj_ln(y_c, y_d, x, prm['od_w_out'][li], prm['ln1_g'][layer], prm['ln1_b'][layer], tm)
        x = _moe_ln(x, prm, layer, tm)

    x3 = geo.v3(x)
    y_p = x3[:, padf + N_META:tp]
    y_s = x3[:, tp:].reshape(ns, 1, D_MODEL)

    def s5_state(hs, part):
        return jnp.stack([h[:, part * S5_CH:(part + 1) * S5_CH].reshape(-1, S5_GROUPS, S5_STATE) for h in hs])

    kv_p = jnp.stack([c[:, padf:tp] for c in kv_l], 1)
    kv_s = jnp.stack([c[:, tp:].reshape(ns, 1, KV_LORA) for c in kv_l], 1)
    kr_p = jnp.stack([c[:, padf:tp] for c in kr_l], 1)
    kr_s = jnp.stack([c[:, tp:].reshape(ns, 1, QK_ROPE) for c in kr_l], 1)
    return (y_p, y_s, jnp.stack(ret_p), jnp.stack(ret_s), s5_state(s5_p, 0), s5_state(s5_s, 0),
            s5_state(s5_p, 1), s5_state(s5_s, 1), jnp.stack(wkv_p), jnp.stack(wkv_s),
            jnp.stack(sh_p), jnp.stack(sh_s), kv_p, kv_s, kr_p, kr_s)


def kernel(x_prompt, x_sample, state_ret, state_s5_re, state_s5_im, state_wkv, state_shift, cache_kv, cache_krope, page_table, meta_tokens, ev_w_in, ret_gn_g, ret_gn_b, s5_a_re, s5_a_im, s5_log_dt, s5_b_re, s5_b_im, s5_c_re, s5_c_im, s5_d, s5_w_glu, s5_b_glu, ev_w_out, od_w_in, rwkv_mu, rwkv_w0, rwkv_w2, rwkv_a0, rwkv_a2, rwkv_g2, rwkv_k_k, rwkv_k_a, rwkv_r_k, rwkv_gn_g, rwkv_gn_b, mla_q_norm, mla_w_qb, mla_kv_norm, mla_w_uk, mla_w_uv, od_w_out, ln1_g, ln1_b, ln2_g, ln2_b, moe_w_coarse, moe_b_coarse, moe_w_fine, moe_b_fine, moe_w_gate, moe_w_up, moe_w_down):
    prm = dict(ev_w_in=ev_w_in, ret_gn_g=ret_gn_g, ret_gn_b=ret_gn_b, s5_a_re=s5_a_re, s5_a_im=s5_a_im,
               s5_log_dt=s5_log_dt, s5_b_re=s5_b_re, s5_b_im=s5_b_im, s5_c_re=s5_c_re, s5_c_im=s5_c_im,
               s5_d=s5_d, s5_w_glu=s5_w_glu, s5_b_glu=s5_b_glu, ev_w_out=ev_w_out,
               od_w_in=od_w_in, rwkv_mu=rwkv_mu, rwkv_w0=rwkv_w0, rwkv_w2=rwkv_w2, rwkv_a0=rwkv_a0,
               rwkv_a2=rwkv_a2, rwkv_g2=rwkv_g2, rwkv_k_k=rwkv_k_k, rwkv_k_a=rwkv_k_a, rwkv_r_k=rwkv_r_k,
               rwkv_gn_g=rwkv_gn_g, rwkv_gn_b=rwkv_gn_b, mla_q_norm=mla_q_norm, mla_w_qb=mla_w_qb,
               mla_kv_norm=mla_kv_norm, mla_w_uk=mla_w_uk, mla_w_uv=mla_w_uv, od_w_out=od_w_out,
               ln1_g=ln1_g, ln1_b=ln1_b, ln2_g=ln2_g, ln2_b=ln2_b, moe_w_coarse=moe_w_coarse,
               moe_b_coarse=moe_b_coarse, moe_w_fine=moe_w_fine, moe_b_fine=moe_b_fine,
               moe_w_gate=moe_w_gate, moe_w_up=moe_w_up, moe_w_down=moe_w_down)
    return _forward(x_prompt, x_sample, state_ret, state_s5_re, state_s5_im, state_wkv, state_shift, cache_kv,
                    cache_krope, page_table, meta_tokens, prm)
```

```python
import functools
import math

import jax
import jax.numpy as jnp
from jax import lax
from jax.experimental import pallas as pl
from jax.experimental.pallas import tpu as pltpu

F32 = jnp.float32
BF16 = jnp.bfloat16

D_MODEL = 1024
N_META = 16
PAGE_SIZE = 128
RET_HEADS, RET_DK, RET_DV = 4, 64, 128
S5_WIDTH, S5_GROUP, S5_STATE = 512, 16, 64
S5_GROUPS = S5_WIDTH // S5_GROUP
S5_CH = S5_GROUPS * S5_STATE
S5_GPB = 128 // S5_GROUP
S5_NBLK = S5_GROUPS // S5_GPB
S5_BCH = S5_GPB * S5_STATE
RWKV_HEAD, RWKV_WIDTH = 64, 512
RWKV_HEADS = RWKV_WIDTH // RWKV_HEAD
LORA_W, LORA_A, LORA_G = 64, 64, 128
RWKV_PROJ = 3 * RWKV_WIDTH + LORA_W + LORA_A + LORA_G
RWKV_GN_EPS = 64e-5
MLA_HEADS, Q_LORA, KV_LORA, QK_NOPE, QK_ROPE, V_HEAD = 8, 384, 256, 64, 32, 64
MLA_WIDTH = MLA_HEADS * V_HEAD
MLA_SCALE = (QK_NOPE + QK_ROPE) ** -0.5
ROPE_THETA = 10000.0
N_EGROUPS, E_PER_GROUP, D_EXPERT = 4, 4, 256
N_EXPERTS = N_EGROUPS * E_PER_GROUP
DEPTH = 4
DN_ALPHA = (2 * DEPTH) ** 0.25
NSLAB = 8
SUB = 8
LANES = 128

VMEM_LIMIT = 56 * 1024 * 1024
WKV_L = 32
WKV_HG = 4
WKV_NB = 8
ROW_TILE = 1056
SEQ_TILE = 416
S5_TILE = 80
NEG = -1e30

NT = (((1,), (1,)), ((), ()))
TN = (((0,), (0,)), ((), ()))
NN = (((1,), (0,)), ((), ()))


def _cp(*sem):
    return pltpu.CompilerParams(dimension_semantics=sem, vmem_limit_bytes=VMEM_LIMIT)


def _div_tile(n, cap, mult=8):
    best = None
    for d in range(mult, min(n, cap) + 1, mult):
        if n % d == 0:
            best = d
    assert best is not None, (n, cap, mult)
    return best


def _dg(a, b, dims):
    return lax.dot_general(a, b, dims, preferred_element_type=F32)


def _bdot(a, b, dims=NN):
    return _dg(a.astype(BF16), b.astype(BF16), dims)


def _dot3(a, b, dims=NN):
    ah = a.astype(BF16)
    al = (a - ah.astype(F32)).astype(BF16)
    bh = b.astype(BF16)
    bl = (b - bh.astype(F32)).astype(BF16)
    return _dg(ah, bh, dims) + (_dg(ah, bl, dims) + _dg(al, bh, dims))


def _dotc(a, b, dims=NN):
    return _bdot(a, b, dims)


def _rb(x):
    return x.astype(BF16).astype(F32)


def _layer_norm(z, g, b, eps=1e-5):
    mu = jnp.mean(z, -1, keepdims=True)
    zc = z - mu
    var = jnp.mean(zc * zc, -1, keepdims=True)
    return zc * lax.rsqrt(var + eps) * g + b


def _rope(x, cos, sin_signed, half):
    w = x.shape[-1]
    lane = lax.broadcasted_iota(jnp.int32, x.shape, 1)
    first = (lane % (2 * half)) < half
    other = jnp.where(first, pltpu.roll(x, w - half, 1), pltpu.roll(x, half, 1))
    return x * cos + other * sin_signed


def _rope_tables(pos, dim, reps):
    inv = ROPE_THETA ** (-jnp.arange(0, dim, 2, dtype=F32) / dim)
    ang = pos.astype(F32)[:, None] * inv[None, :]
    cos = jnp.cos(ang)
    sin = jnp.sin(ang)
    cos = jnp.tile(jnp.concatenate([cos, cos], -1), (1, reps))
    sin = jnp.tile(jnp.concatenate([-sin, sin], -1), (1, reps))
    return cos, sin


def _drop_arg(kern, idx, *refs):
    return kern(*(refs[:idx] + refs[idx + 1:]))


class _Geom:
    def __init__(self, seq, ns, n_pages):
        self.tq = seq + N_META
        self.padf = (-self.tq) % WKV_L
        self.tp = self.padf + self.tq
        assert ns % (NSLAB * SUB) == 0
        self.ns = ns
        self.sps = ns // NSLAB
        self.slab = self.tp + self.sps
        self.n = NSLAB * self.slab
        self.past = n_pages * PAGE_SIZE
        self.tm = _div_tile(self.slab, ROW_TILE)
        self.tps = self.slab // self.tm

    def v3(self, x):
        return x.reshape(NSLAB, self.slab, x.shape[-1])

    def sample_blk(self, rows):
        per = self.sps // rows
        return lambda k: (k // per, self.tp // rows + k % per)


def _mm_kernel(x_ref, w_ref, o_ref):
    o_ref[...] = _dg(x_ref[...].astype(BF16), w_ref[...], NN)


def _matmul(x, w, tm, tn):
    n, k = x.shape
    m = w.shape[1]
    return pl.pallas_call(
        _mm_kernel,
        grid=(n // tm, m // tn),
        in_specs=[pl.BlockSpec((tm, k), lambda i, j: (i, 0)), pl.BlockSpec((k, tn), lambda i, j: (0, j))],
        out_specs=pl.BlockSpec((tm, tn), lambda i, j: (i, j)),
        out_shape=jax.ShapeDtypeStruct((n, m), F32),
        compiler_params=_cp("parallel", "arbitrary"),
        name="proj_in",
    )(x, w)


def _proj_ln_kernel(a1_ref, a2_ref, x_ref, w1_ref, w2_ref, g_ref, b_ref, o_ref):
    mix = _dg(a1_ref[...].astype(BF16), w1_ref[...], NN) + _dg(a2_ref[...].astype(BF16), w2_ref[...], NN)
    z = DN_ALPHA * x_ref[...] + mix
    o_ref[...] = _layer_norm(z, g_ref[...], b_ref[...])


def _proj_ln(a1, a2, x, w_out, g, b, tm):
    n = x.shape[0]
    k1, k2 = a1.shape[1], a2.shape[1]
    w1 = w_out[:k1].astype(BF16)
    w2 = w_out[k1:].astype(BF16)
    row = lambda i: (i, 0)
    fix = lambda i: (0, 0)
    return pl.pallas_call(
        _proj_ln_kernel,
        grid=(n // tm,),
        in_specs=[pl.BlockSpec((tm, k1), row), pl.BlockSpec((tm, k2), row), pl.BlockSpec((tm, D_MODEL), row),
                  pl.BlockSpec((k1, D_MODEL), fix), pl.BlockSpec((k2, D_MODEL), fix),
                  pl.BlockSpec((1, D_MODEL), fix), pl.BlockSpec((1, D_MODEL), fix)],
        out_specs=pl.BlockSpec((tm, D_MODEL), row),
        out_shape=jax.ShapeDtypeStruct((n, D_MODEL), F32),
        compiler_params=_cp("parallel"),
        name="proj_out_ln",
    )(a1, a2, x, w1, w2, g.reshape(1, -1), b.reshape(1, -1))


MOE_EPS = 2
ROUTE0 = N_EGROUPS


def _route(lt):
    lane = lax.broadcasted_iota(jnp.int32, lt.shape, 1)
    lanef = lane.astype(F32)
    coarse = lane < N_EGROUPS
    lc = jnp.where(coarse, lt, NEG)
    m = jnp.max(lc, axis=1, keepdims=True)
    gsel = jnp.min(jnp.where(lc == m, lanef, 1e9), axis=1, keepdims=True)
    gate_c = 1.0 / jnp.sum(jnp.where(coarse, jnp.exp(lc - m), 0.0), axis=1, keepdims=True)
    grp = ((lane - ROUTE0) // E_PER_GROUP).astype(F32)
    lf = jnp.where(grp == gsel, lt, NEG)
    t1 = jnp.max(lf, axis=1, keepdims=True)
    i1 = jnp.min(jnp.where(lf == t1, lanef, 1e9), axis=1, keepdims=True)
    lf2 = jnp.where(lanef == i1, NEG, lf)
    t2 = jnp.max(lf2, axis=1, keepdims=True)
    i2 = jnp.min(jnp.where(lf2 == t2, lanef, 1e9), axis=1, keepdims=True)
    e2 = jnp.exp(t2 - t1)
    w1 = gate_c / (1.0 + e2)
    return jnp.where(lanef == i1, w1, 0.0) + jnp.where(lanef == i2, w1 * e2, 0.0)


def _moe_kernel(x_ref, wr_ref, br_ref, wgu_ref, wd_ref, g_ref, b_ref, o_ref, xb_scr, comb_scr, h_scr):
    e = pl.program_id(1)

    @pl.when(e == 0)
    def _():
        xb = x_ref[...].astype(BF16)
        xb_scr[...] = xb
        comb_scr[...] = _route(_dg(xb, wr_ref[...], NN) + br_ref[...])

    xb = xb_scr[...]
    comb = comb_scr[...]
    lane = lax.broadcasted_iota(jnp.int32, comb.shape, 1)
    gus = [_dg(xb, wgu_ref[q], NN) for q in range(MOE_EPS)]
    hb = []
    for q, gu in enumerate(gus):
        hg, hu = gu[:, :D_EXPERT], gu[:, D_EXPERT:]
        ce = jnp.sum(jnp.where(lane == e * MOE_EPS + q + ROUTE0, comb, 0.0), axis=1, keepdims=True)
        hb.append(((hg * jax.nn.sigmoid(hg)) * hu * ce).astype(BF16))
    for ee in range(N_EXPERTS // MOE_EPS):
        @pl.when(e == ee)
        def _():
            for q in range(MOE_EPS):
                x0 = (ee * MOE_EPS + q) * D_EXPERT
                h_scr[:, x0:x0 + D_EXPERT] = hb[q]

    @pl.when(e == pl.num_programs(1) - 1)
    def _():
        z = DN_ALPHA * x_ref[...] + _dg(h_scr[...], wd_ref[...], NN)
        o_ref[...] = _layer_norm(z, g_ref[...], b_ref[...])


def _moe_ln(x, prm, layer, tm):
    n = x.shape[0]
    wr = jnp.concatenate([prm['moe_w_coarse'][layer],
                          jnp.transpose(prm['moe_w_fine'][layer], (1, 0, 2)).reshape(D_MODEL, N_EXPERTS)], 1)
    wr = jnp.pad(wr, ((0, 0), (0, LANES - wr.shape[1]))).astype(BF16)
    br = jnp.pad(jnp.concatenate([prm['moe_b_coarse'][layer], prm['moe_b_fine'][layer].reshape(-1)]),
                 (0, LANES - N_EGROUPS - N_EXPERTS)).reshape(1, LANES)
    row = lambda i, e: (i, 0)
    fix = lambda i, e: (0, 0)
    exp3 = lambda i, e: (e, 0, 0)
    return pl.pallas_call(
        _moe_kernel,
        grid=(n // tm, N_EXPERTS // MOE_EPS),
        in_specs=[pl.BlockSpec((tm, D_MODEL), row), pl.BlockSpec((D_MODEL, LANES), fix), pl.BlockSpec((1, LANES), fix),
                  pl.BlockSpec((MOE_EPS, D_MODEL, 2 * D_EXPERT), exp3),
                  pl.BlockSpec((N_EXPERTS * D_EXPERT, D_MODEL), fix, pipeline_mode=pl.Buffered(1)),
                  pl.BlockSpec((1, D_MODEL), fix), pl.BlockSpec((1, D_MODEL), fix)],
        out_specs=pl.BlockSpec((tm, D_MODEL), row),
        out_shape=jax.ShapeDtypeStruct((n, D_MODEL), F32),
        scratch_shapes=[pltpu.VMEM((tm, D_MODEL), BF16), pltpu.VMEM((tm, LANES), F32),
                        pltpu.VMEM((tm, N_EXPERTS * D_EXPERT), BF16)],
        compiler_params=_cp("parallel", "arbitrary"),
        name="moe_ln",
    )(x, wr, br, jnp.concatenate([prm['moe_w_gate'][layer], prm['moe_w_up'][layer]], -1).astype(BF16),
      prm['moe_w_down'][layer].astype(BF16).reshape(N_EXPERTS * D_EXPERT, D_MODEL),
      prm['ln2_g'][layer].reshape(1, -1), prm['ln2_b'][layer].reshape(1, -1))


def _ret_log_g():
    return [math.log1p(-2.0 ** (-5.0 - h)) for h in range(RET_HEADS)]


def _ret_kernel(q_ref, k_ref, v_ref, gate_ref, cos_ref, sin_ref, dmask_ref, cdec_ref, kdec_ref, sdec_ref,
                gng_ref, gnb_ref, o_ref, sout_ref, s_scr, *, L, padf):
    c = pl.program_id(1)
    HK, HV = RET_HEADS * RET_DK, RET_HEADS * RET_DV

    @pl.when(c == 0)
    def _():
        s_scr[...] = jnp.zeros_like(s_scr)

    cos, sin = cos_ref[...], sin_ref[...]
    q = _rope(q_ref[...], cos, sin, RET_DK // 2) * RET_DK ** -0.5
    k = _rope(k_ref[...], cos, sin, RET_DK // 2)
    real = (lax.broadcasted_iota(jnp.int32, (L, 1), 0) + c * L) >= padf
    k = jnp.where(real, k, 0.0)
    v = jnp.where(real, v_ref[...], 0.0)
    gate = gate_ref[...]
    qb, kb, vb = q.astype(BF16), k.astype(BF16), v.astype(BF16)
    sbd = s_scr[...]
    cross = _dg(qb, sbd.astype(BF16), NN) * cdec_ref[...]
    lane = lax.broadcasted_iota(jnp.int32, (L, HK), 1)
    outs = []
    scs = [(_dg(jnp.where(lane // RET_DK == h, qb, jnp.zeros_like(qb)), kb, NT) * dmask_ref[h]).astype(BF16)
           for h in range(RET_HEADS)]
    for h in range(RET_HEADS):
        sl = slice(h * RET_DV, (h + 1) * RET_DV)
        o = _dg(scs[h], vb[:, sl], NN) + cross[:, sl]
        mu = jnp.mean(o, -1, keepdims=True)
        oc = o - mu
        var = jnp.mean(oc * oc, -1, keepdims=True)
        outs.append(oc * lax.rsqrt(var + 1e-5))
    on = jnp.concatenate(outs, axis=1) * gng_ref[...] + gnb_ref[...]
    o_ref[...] = (gate * jax.nn.sigmoid(gate)) * on
    upd = _dg((k * kdec_ref[...]).astype(BF16), vb, TN)
    ri = lax.broadcasted_iota(jnp.int32, (HK, HV), 0) // RET_DK
    ci = lax.broadcasted_iota(jnp.int32, (HK, HV), 1) // RET_DV
    s_new = sdec_ref[...] * sbd + jnp.where(ri == ci, upd, 0.0)
    s_scr[...] = s_new

    @pl.when(c == pl.num_programs(1) - 1)
    def _():
        for h in range(RET_HEADS):
            sout_ref[h] = s_new[h * RET_DK:(h + 1) * RET_DK, h * RET_DV:(h + 1) * RET_DV]


def _retention_prompt(p3, geo, gn_g, gn_b):
    HK, HV = RET_HEADS * RET_DK, RET_HEADS * RET_DV
    L = _div_tile(geo.tp, SEQ_TILE)
    log_g = jnp.asarray(_ret_log_g(), F32)
    idx = jnp.arange(L, dtype=F32)
    diff = idx[:, None] - idx[None, :]
    dmask = jnp.where(diff >= 0, jnp.exp(log_g[:, None, None] * jnp.maximum(diff, 0.0)), 0.0)
    cdec = jnp.repeat(jnp.exp(log_g[None, :] * (idx + 1.0)[:, None]), RET_DV, axis=1)
    kdec = jnp.repeat(jnp.exp(log_g[None, :] * (L - 1.0 - idx)[:, None]), RET_DK, axis=1)
    sdec = jnp.broadcast_to(jnp.repeat(jnp.exp(log_g * L), RET_DK)[:, None], (HK, HV))
    cos, sin = _rope_tables(jnp.arange(geo.tp, dtype=jnp.int32) - geo.padf, RET_DK, RET_HEADS)
    fix2 = lambda b, c: (0, 0)
    tab = pl.BlockSpec((L, HK), lambda b, c: (c, 0))
    return pl.pallas_call(
        functools.partial(_ret_kernel, L=L, padf=geo.padf),
        grid=(NSLAB, geo.tp // L),
        in_specs=[pl.BlockSpec((None, L, HK), lambda b, c: (b, c, 0)), pl.BlockSpec((None, L, HK), lambda b, c: (b, c, 1)),
                  pl.BlockSpec((None, L, HV), lambda b, c: (b, c, 1)), pl.BlockSpec((None, L, HV), lambda b, c: (b, c, 2)),
                  tab, tab, pl.BlockSpec((RET_HEADS, L, L), lambda b, c: (0, 0, 0)),
                  pl.BlockSpec((L, HV), fix2), pl.BlockSpec((L, HK), fix2), pl.BlockSpec((HK, HV), fix2),
                  pl.BlockSpec((1, HV), fix2), pl.BlockSpec((1, HV), fix2)],
        out_specs=[pl.BlockSpec((None, L, HV), lambda b, c: (b, c, 0)),
                   pl.BlockSpec((None, RET_HEADS, RET_DK, RET_DV), lambda b, c: (b, 0, 0, 0))],
        out_shape=[jax.ShapeDtypeStruct((NSLAB, geo.slab, HV), F32),
                   jax.ShapeDtypeStruct((NSLAB, RET_HEADS, RET_DK, RET_DV), F32)],
        scratch_shapes=[pltpu.VMEM((HK, HV), F32)],
        compiler_params=_cp("parallel", "arbitrary"),
        name="retention",
    )(p3, p3, p3, p3, cos, sin, dmask, cdec, kdec, sdec, gn_g.reshape(1, -1), gn_b.reshape(1, -1))


def _ret_step_kernel(q_ref, k_ref, v_ref, gate_ref, cos_ref, sin_ref, gng_ref, gnb_ref, s_ref, o_ref, sout_ref):
    cos, sin = cos_ref[...], sin_ref[...]
    qb = _rb(_rope(q_ref[...], cos, sin, RET_DK // 2) * RET_DK ** -0.5)
    kb = _rb(_rope(k_ref[...], cos, sin, RET_DK // 2))
    vb = _rb(v_ref[...])
    gate = gate_ref[...]
    ri = lax.broadcasted_iota(jnp.int32, (RET_DK, RET_DK), 0)
    ci = lax.broadcasted_iota(jnp.int32, (RET_DK, RET_DK), 1)
    eye = jnp.where(ri == ci, 1.0, 0.0)

    def col(x):
        return jnp.sum(eye * jnp.broadcast_to(x, (RET_DK, RET_DK)), axis=1, keepdims=True)

    decay = [math.exp(lg) for lg in _ret_log_g()]
    pairs = [(j, h) for j in range(SUB) for h in range(RET_HEADS)]
    qh = {p: qb[p[0]:p[0] + 1, p[1] * RET_DK:(p[1] + 1) * RET_DK] for p in pairs}
    kh = {p: kb[p[0]:p[0] + 1, p[1] * RET_DK:(p[1] + 1) * RET_DK] for p in pairs}
    vh = {p: vb[p[0]:p[0] + 1, p[1] * RET_DV:(p[1] + 1) * RET_DV] for p in pairs}
    qcol = {p: col(qh[p]) for p in pairs}
    kcol = {p: col(kh[p]) for p in pairs}
    qk = {p: _rb(jnp.sum(qh[p] * kh[p], axis=1, keepdims=True)) for p in pairs}
    o = {}
    for p in pairs:
        s0 = s_ref[p[0], p[1]]
        o[p] = qk[p] * vh[p] + decay[p[1]] * jnp.sum(qcol[p] * _rb(s0), axis=0, keepdims=True)
        sout_ref[p[0], p[1]] = decay[p[1]] * s0 + kcol[p] * vh[p]
    mu = {p: jnp.mean(o[p], -1, keepdims=True) for p in pairs}
    oc = {p: o[p] - mu[p] for p in pairs}
    var = {p: jnp.mean(oc[p] * oc[p], -1, keepdims=True) for p in pairs}
    for j in range(SUB):
        on = jnp.concatenate([oc[(j, h)] * lax.rsqrt(var[(j, h)] + 1e-5) for h in range(RET_HEADS)], axis=1)
        on = on * gng_ref[...] + gnb_ref[...]
        gj = gate[j:j + 1]
        o_ref[j:j + 1, :] = (gj * jax.nn.sigmoid(gj)) * on


def _retention_sample(p3, geo, s0, gn_g, gn_b, out_prev):
    HK, HV = RET_HEADS * RET_DK, RET_HEADS * RET_DV
    cos, sin = _rope_tables(jnp.full((1,), geo.past, jnp.int32), RET_DK, RET_HEADS)
    sb = geo.sample_blk(SUB)
    fix = lambda k: (0, 0)
    st = pl.BlockSpec((SUB, RET_HEADS, RET_DK, RET_DV), lambda k: (k, 0, 0, 0))
    return pl.pallas_call(
        functools.partial(_drop_arg, _ret_step_kernel, 9),
        grid=(geo.ns // SUB,),
        in_specs=[pl.BlockSpec((None, SUB, HK), lambda k: sb(k) + (0,)), pl.BlockSpec((None, SUB, HK), lambda k: sb(k) + (1,)),
                  pl.BlockSpec((None, SUB, HV), lambda k: sb(k) + (1,)), pl.BlockSpec((None, SUB, HV), lambda k: sb(k) + (2,)),
                  pl.BlockSpec((1, HK), fix), pl.BlockSpec((1, HK), fix),
                  pl.BlockSpec((1, HV), fix), pl.BlockSpec((1, HV), fix), st, pl.BlockSpec(memory_space=pl.ANY)],
        out_specs=[pl.BlockSpec((None, SUB, HV), lambda k: sb(k) + (0,)), st],
        out_shape=[jax.ShapeDtypeStruct(out_prev.shape, F32), jax.ShapeDtypeStruct(s0.shape, F32)],
        input_output_aliases={9: 0},
        compiler_params=_cp("parallel"),
        name="retention_step",
    )(p3, p3, p3, p3, cos, sin, gn_g.reshape(1, -1), gn_b.reshape(1, -1), s0, out_prev)


def _s5_params(prm, li):
    f32 = F32
    a_re, a_im = prm['s5_a_re'][li].astype(f32), prm['s5_a_im'][li].astype(f32)
    dt = jnp.exp(prm['s5_log_dt'][li].astype(f32))[:, None]
    mag = jnp.exp(dt * a_re)
    ab_re, ab_im = mag * jnp.cos(dt * a_im), mag * jnp.sin(dt * a_im)
    den = a_re * a_re + a_im * a_im
    n_re = ab_re - 1.0
    f_re = (n_re * a_re + ab_im * a_im) / den
    f_im = (ab_im * a_re - n_re * a_im) / den
    b_re, b_im = prm['s5_b_re'][li].astype(f32), prm['s5_b_im'][li].astype(f32)
    bb_re = f_re[..., None] * b_re - f_im[..., None] * b_im
    bb_im = f_re[..., None] * b_im + f_im[..., None] * b_re
    eye = jnp.eye(S5_GPB, dtype=f32)

    def block_in(w):
        w = w.reshape(S5_NBLK, S5_GPB, S5_STATE, S5_GROUP)
        return jnp.einsum('bgpc,gh->bgchp', w, eye).reshape(S5_NBLK, LANES, S5_BCH)

    def block_out(w):
        w = w.reshape(S5_NBLK, S5_GPB, S5_GROUP, S5_STATE)
        return jnp.einsum('bgcp,gh->bgphc', w, eye).reshape(S5_NBLK, S5_BCH, LANES)

    bbw = jnp.concatenate([block_in(bb_re), block_in(bb_im)], 2).astype(BF16)
    ccw = jnp.concatenate([block_out(prm['s5_c_re'][li]), -block_out(prm['s5_c_im'][li])], 1).astype(BF16)
    lam = jnp.concatenate([ab_re.reshape(1, -1), ab_im.reshape(1, -1)], 1)
    return [bbw, ccw, lam, prm['s5_d'][li].reshape(1, -1), prm['s5_w_glu'][li].astype(BF16),
            prm['s5_b_glu'][li].reshape(1, -1)]


def _s5_specs(fix):
    fix3 = lambda *a: (0, 0, 0)
    return [pl.BlockSpec((S5_NBLK, LANES, 2 * S5_BCH), fix3), pl.BlockSpec((S5_NBLK, 2 * S5_BCH, LANES), fix3),
            pl.BlockSpec((1, 2 * S5_CH), fix), pl.BlockSpec((1, S5_WIDTH), fix),
            pl.BlockSpec((S5_WIDTH, S5_WIDTH), fix), pl.BlockSpec((1, S5_WIDTH), fix)]


def _s5_gate(y, u, d_ref, wglu_ref, bglu_ref):
    y = y + d_ref[...] * u
    z = 0.5 * y * (1.0 + jnp.tanh(math.sqrt(2.0 / math.pi) * (y + 0.044715 * (y * y * y))))
    return z * jax.nn.sigmoid(_dg(z.astype(BF16), wglu_ref[...], NN) + bglu_ref[...])


S5_NJ = 2 * S5_CH // LANES
S5_PAIRS = 8


def _s5_prompt_kernel(u_ref, bb_ref, cc_ref, lam_ref, d_ref, wglu_ref, bglu_ref, o_ref, hout_ref, sc, h_scr,
                      *, lt, pitch, padf):
    i = pl.program_id(0)

    @pl.when(i == 0)
    def _():
        h_scr[...] = jnp.zeros_like(h_scr)

    real = (lax.broadcasted_iota(jnp.int32, (lt, 1), 0) + i * lt) >= padf
    half = S5_NJ // 2
    spb = S5_BCH // LANES
    ub = [jnp.where(real, u_ref[b], 0.0).astype(BF16) for b in range(NSLAB)]
    for k in range(S5_NBLK):
        for b in range(NSLAB):
            bu = _dg(ub[b][:, k * LANES:(k + 1) * LANES], bb_ref[k], NN)
            for q in range(spb):
                sc[k * spb + q, b * pitch:b * pitch + lt, :] = bu[:, q * LANES:(q + 1) * LANES]
                sc[half + k * spb + q, b * pitch:b * pitch + lt, :] = bu[:, S5_BCH + q * LANES:S5_BCH + (q + 1) * LANES]
    for j0 in range(0, half, S5_PAIRS):
        js = list(range(j0, j0 + S5_PAIRS))
        lr = [jnp.broadcast_to(lam_ref[:, j * LANES:(j + 1) * LANES], (NSLAB, LANES)) for j in js]
        li = [jnp.broadcast_to(lam_ref[:, S5_CH + j * LANES:S5_CH + (j + 1) * LANES], (NSLAB, LANES)) for j in js]

        def step(t, carry):
            hr, hi = carry
            rows = pl.ds(t, NSLAB, stride=pitch)
            nr, ni = [], []
            for q, j in enumerate(js):
                r = lr[q] * hr[q] - li[q] * hi[q] + sc[j, rows, :]
                m = lr[q] * hi[q] + li[q] * hr[q] + sc[half + j, rows, :]
                sc[j, rows, :] = r
                sc[half + j, rows, :] = m
                nr.append(r)
                ni.append(m)
            return tuple(nr), tuple(ni)

        init = (tuple(h_scr[:, j * LANES:(j + 1) * LANES] for j in js),
                tuple(h_scr[:, S5_CH + j * LANES:S5_CH + (j + 1) * LANES] for j in js))
        hr, hi = lax.fori_loop(0, lt, step, init, unroll=2)
        for q, j in enumerate(js):
            h_scr[:, j * LANES:(j + 1) * LANES] = hr[q]
            h_scr[:, S5_CH + j * LANES:S5_CH + (j + 1) * LANES] = hi[q]
    ys = [[] for _ in range(NSLAB)]
    for k in range(S5_NBLK):
        for b in range(NSLAB):
            rows = slice(b * pitch, b * pitch + lt)
            hk = jnp.concatenate([sc[k * spb + q, rows, :] for q in range(spb)]
                                 + [sc[half + k * spb + q, rows, :] for q in range(spb)], axis=1)
            ys[b].append(_dg(hk.astype(BF16), cc_ref[k], NN))
    for b in range(NSLAB):
        o_ref[b] = _s5_gate(jnp.concatenate(ys[b], axis=1), u_ref[b], d_ref, wglu_ref, bglu_ref)

    @pl.when(i == pl.num_programs(0) - 1)
    def _():
        hout_ref[...] = h_scr[...]


def _s5_prompt(p3, geo, prm, li):
    lt = _div_tile(geo.tp, S5_TILE, 16)
    pitch = lt + SUB
    ucol = p3.shape[-1] // S5_WIDTH - 1
    fix = lambda i: (0, 0)
    return pl.pallas_call(
        functools.partial(_s5_prompt_kernel, lt=lt, pitch=pitch, padf=geo.padf),
        grid=(geo.tp // lt,),
        in_specs=[pl.BlockSpec((NSLAB, lt, S5_WIDTH), lambda i: (0, i, ucol))] + _s5_specs(fix),
        out_specs=[pl.BlockSpec((NSLAB, lt, S5_WIDTH), lambda i: (0, i, 0)), pl.BlockSpec((NSLAB, 2 * S5_CH), fix)],
        out_shape=[jax.ShapeDtypeStruct((NSLAB, geo.slab, S5_WIDTH), F32), jax.ShapeDtypeStruct((NSLAB, 2 * S5_CH), F32)],
        scratch_shapes=[pltpu.VMEM((S5_NJ, NSLAB * pitch, LANES), F32), pltpu.VMEM((NSLAB, 2 * S5_CH), F32)],
        compiler_params=_cp("arbitrary"),
        name="s5",
    )(p3, *_s5_params(prm, li))


def _s5_step_kernel(u_ref, bb_ref, cc_ref, lam_ref, d_ref, wglu_ref, bglu_ref, h0_ref, o_ref, hout_ref):
    u = u_ref[...]
    ub = u.astype(BF16)
    bu = [_dg(ub[:, k * LANES:(k + 1) * LANES], bb_ref[k], NN) for k in range(S5_NBLK)]
    bu_re = jnp.concatenate([x[:, :S5_BCH] for x in bu], axis=1)
    bu_im = jnp.concatenate([x[:, S5_BCH:] for x in bu], axis=1)
    h0 = h0_ref[...]
    lr, li = lam_ref[:, :S5_CH], lam_ref[:, S5_CH:]
    hr, hi = h0[:, :S5_CH], h0[:, S5_CH:]
    nr = lr * hr - li * hi + bu_re
    ni = lr * hi + li * hr + bu_im
    hout_ref[...] = jnp.concatenate([nr, ni], axis=1)
    ys = []
    for k in range(S5_NBLK):
        sl = slice(k * S5_BCH, (k + 1) * S5_BCH)
        ys.append(_dg(jnp.concatenate([nr[:, sl], ni[:, sl]], axis=1).astype(BF16), cc_ref[k], NN))
    o_ref[...] = _s5_gate(jnp.concatenate(ys, axis=1), u, d_ref, wglu_ref, bglu_ref)


def _s5_sample(p3, geo, h0, prm, li, out_prev):
    rows = geo.sps
    sb = geo.sample_blk(rows)
    ucol = p3.shape[-1] // S5_WIDTH - 1
    fix = lambda k: (0, 0)
    st = pl.BlockSpec((rows, 2 * S5_CH), lambda k: (k, 0))
    return pl.pallas_call(
        functools.partial(_drop_arg, _s5_step_kernel, 8),
        grid=(NSLAB,),
        in_specs=[pl.BlockSpec((None, rows, S5_WIDTH), lambda k: sb(k) + (ucol,))] + _s5_specs(fix)
        + [st, pl.BlockSpec(memory_space=pl.ANY)],
        out_specs=[pl.BlockSpec((None, rows, S5_WIDTH), lambda k: sb(k) + (0,)), st],
        out_shape=[jax.ShapeDtypeStruct(out_prev.shape, F32), jax.ShapeDtypeStruct(h0.shape, F32)],
        input_output_aliases={8: 0},
        compiler_params=_cp("parallel"),
        name="s5_step",
    )(p3, *_s5_params(prm, li), h0, out_prev)


def _head_sum(x, ones_bd):
    ones = ones_bd.astype(BF16)
    hi = x.astype(BF16)
    lo = (x - hi.astype(F32)).astype(BF16)
    return _dg(hi, ones, NN) + _dg(lo, ones, NN)


def _rwkv_prep_body(pr, prev, mu_ref, w0_ref, w2_ref, a0_ref, a2_ref, g2_ref, kk_ref, ka_ref, ones_ref,
                    r_ref, lw_ref, k_ref, v_ref, a_ref, b_ref, g_ref):
    W = RWKV_WIDTH
    xm = pr + (prev - pr) * mu_ref[...]
    r = xm[:, :W]
    k = xm[:, W:2 * W]
    v = xm[:, 2 * W:3 * W]
    xwa = xm[:, 3 * W:3 * W + LORA_W + LORA_A]
    xg = xm[:, 3 * W + LORA_W + LORA_A:]
    wl = w0_ref[...] + _bdot(jnp.tanh(xwa), w2_ref[...])
    sp = jnp.maximum(-wl, 0.0) + jnp.log(1.0 + jnp.exp(-jnp.abs(wl)))
    w = -sp - 0.5
    lw = -jnp.exp(w)
    a = jax.nn.sigmoid(a0_ref[...] + _bdot(xwa, a2_ref[...]))
    g = _bdot(jax.nn.sigmoid(xg), g2_ref[...])
    kk = k * kk_ref[...]
    nrm = jnp.sqrt(_head_sum(kk * kk, ones_ref[...]))
    kk = kk / jnp.maximum(nrm, 1e-12)
    k2 = k * (1.0 + (a - 1.0) * ka_ref[...])
    r_ref[...] = r
    lw_ref[...] = lw
    k_ref[...] = k2
    v_ref[...] = v
    a_ref[...] = -kk
    b_ref[...] = kk * a
    g_ref[...] = g


def _rwkv_prep_prompt_kernel(pr_ref, prev8_ref, *rest, tm, padf):
    pr = pr_ref[...]
    t = lax.broadcasted_iota(jnp.int32, (tm, 1), 0)
    last = jnp.broadcast_to(prev8_ref[SUB - 1:SUB, :], pr.shape)
    prev = jnp.where(t == 0, last, pltpu.roll(pr, 1, 0))
    first = (t + pl.program_id(1) * tm) == padf
    prev = jnp.where(first, 0.0, prev)
    _rwkv_prep_body(pr, prev, *rest)


def _rwkv_prep_sample_kernel(pr_ref, prev_ref, *rest):
    _rwkv_prep_body(pr_ref[...], prev_ref[...], *rest[:9], *rest[9 + 7:])


def _rwkv_prep(p3, geo, shift_s, prm, li):
    W = RWKV_WIDTH
    zeros64 = jnp.zeros((LORA_W, W), F32)
    w2p = jnp.concatenate([prm['rwkv_w2'][li], zeros64], 0).astype(BF16)
    a2p = jnp.concatenate([zeros64, prm['rwkv_a2'][li]], 0).astype(BF16)
    hid = jnp.arange(W) // RWKV_HEAD
    ones_bd = (hid[:, None] == hid[None, :]).astype(F32)
    consts = [prm['rwkv_mu'][li].reshape(1, -1), prm['rwkv_w0'][li].reshape(1, -1), w2p,
              prm['rwkv_a0'][li].reshape(1, -1), a2p, prm['rwkv_g2'][li].astype(BF16),
              prm['rwkv_k_k'][li].reshape(1, -1), prm['rwkv_k_a'][li].reshape(1, -1), ones_bd]
    outs_shape = [jax.ShapeDtypeStruct((NSLAB, geo.slab, W), F32)] * 7
    tm = _div_tile(geo.tp, SEQ_TILE)
    outs = pl.pallas_call(
        functools.partial(_rwkv_prep_prompt_kernel, tm=tm, padf=geo.padf),
        grid=(NSLAB, geo.tp // tm),
        in_specs=[pl.BlockSpec((None, tm, RWKV_PROJ), lambda b, c: (b, c, 0)),
                  pl.BlockSpec((None, SUB, RWKV_PROJ), lambda b, c: (b, jnp.maximum(c * (tm // SUB) - 1, 0), 0))]
        + [pl.BlockSpec(x.shape, lambda b, c: (0, 0)) for x in consts],
        out_specs=[pl.BlockSpec((None, tm, W), lambda b, c: (b, c, 0))] * 7,
        out_shape=outs_shape,
        compiler_params=_cp("parallel", "parallel"),
        name="rwkv_prep_prompt",
    )(p3, p3, *consts)
    rows = geo.sps
    sb = geo.sample_blk(rows)
    nin = 2 + len(consts)
    outs = pl.pallas_call(
        _rwkv_prep_sample_kernel,
        grid=(NSLAB,),
        in_specs=[pl.BlockSpec((None, rows, RWKV_PROJ), lambda k: sb(k) + (0,)),
                  pl.BlockSpec((rows, RWKV_PROJ), lambda k: (k, 0))]
        + [pl.BlockSpec(x.shape, lambda k: (0, 0)) for x in consts] + [pl.BlockSpec(memory_space=pl.ANY)] * 7,
        out_specs=[pl.BlockSpec((None, rows, W), lambda k: sb(k) + (0,))] * 7,
        out_shape=outs_shape,
        input_output_aliases={nin + j: j for j in range(7)},
        compiler_params=_cp("parallel"),
        name="rwkv_prep_sample",
    )(p3, shift_s, *consts, *outs)
    return outs


def _wkv_chunk_kernel(r_ref, lw_ref, k_ref, v_ref, a_ref, b_ref, gate_ref, gng_ref, gnb_ref, rk_ref, ones_ref,
                      y_ref, gout_ref, g_scr, *, padf):
    c = pl.program_id(1)
    L, HG = WKV_L, WKV_HG
    NG = RWKV_HEADS // HG
    WD = HG * RWKV_HEAD
    M = HG * L

    @pl.when(c == 0)
    def _():
        g_scr[...] = jnp.zeros_like(g_scr)

    real = (lax.broadcasted_iota(jnp.int32, (L, 1), 0) + c * L) >= padf

    def ld(ref):
        return jnp.where(real, jnp.concatenate([ref[i] for i in range(WKV_NB)], axis=1), 0.0)

    r, lw, k, v, a, b = ld(r_ref), ld(lw_ref), ld(k_ref), ld(v_ref), ld(a_ref), ld(b_ref)
    ti = lax.broadcasted_iota(jnp.int32, (L, L), 0)
    tj = lax.broadcasted_iota(jnp.int32, (L, L), 1)
    tri = jnp.where(ti >= tj, 1.0, 0.0)
    cum = _dot3(tri, lw)
    dec = jnp.exp(cum)
    inv = jnp.exp(-cum)
    at = a * jnp.exp(cum - lw)
    bt = b * inv
    kt = k * inv
    rt = r * dec
    rh = lax.broadcasted_iota(jnp.int32, (M, WD), 0) // L
    lh = lax.broadcasted_iota(jnp.int32, (M, WD), 1) // RWKV_HEAD
    bd = rh == lh
    mi = lax.broadcasted_iota(jnp.int32, (M, M), 0)
    mj = lax.broadcasted_iota(jnp.int32, (M, M), 1)
    same = (mi // L) == (mj // L)
    m_strict = jnp.logical_and(same, (mj % L) < (mi % L))
    m_incl = jnp.logical_and(same, (mj % L) <= (mi % L))
    eye = jnp.where(mi == mj, 1.0, 0.0)

    def stack(x):
        return jnp.where(bd, jnp.concatenate([x] * HG, axis=0), 0.0)

    chains = range(WKV_NB * NG)
    sls = [slice(i * WD, (i + 1) * WD) for i in chains]
    a_s = [stack(at[:, sl]).astype(BF16) for sl in sls]
    r_s = [stack(rt[:, sl]).astype(BF16) for sl in sls]
    v_s = [stack(v[:, sl]).astype(BF16) for sl in sls]
    bk_s = [jnp.concatenate([stack(bt[:, sl]), stack(kt[:, sl])], axis=0).astype(BF16) for sl in sls]
    g = [g_scr[i] for i in chains]
    gb = [x.astype(BF16) for x in g]
    prod = [_dg(jnp.concatenate([a_s[i], r_s[i]], axis=0), bk_s[i], NT) for i in chains]
    ab = [jnp.where(m_strict, p[:M, :M], 0.0) for p in prod]
    ak = [jnp.where(m_strict, p[:M, M:], 0.0).astype(BF16) for p in prod]
    rb = [jnp.where(m_incl, p[M:, :M], 0.0).astype(BF16) for p in prod]
    rk = [jnp.where(m_incl, p[M:, M:], 0.0).astype(BF16) for p in prod]
    tinv = [eye + x for x in ab]
    pw = ab
    for _ in range(int(math.log2(L)) - 1):
        pw = [_dotc(x, x) for x in pw]
        tinv = [t + _dotc(t, x) for t, x in zip(tinv, pw)]
    z = [_dg(a_s[i], gb[i], NT) + _dg(ak[i], v_s[i], NN) for i in chains]
    u = [_dotc(tinv[i], z[i]).astype(BF16) for i in chains]
    ybd = [_dg(r_s[i], gb[i], NT) + _dg(rb[i], u[i], NN) + _dg(rk[i], v_s[i], NN) for i in chains]
    upd = [_dg(jnp.concatenate([u[i], v_s[i]], axis=0), bk_s[i], TN) for i in chains]
    ones = ones_ref[...]
    gate = jnp.concatenate([gate_ref[i] for i in range(WKV_NB)], axis=1)
    cls = [slice((i % NG) * WD, (i % NG + 1) * WD) for i in chains]
    ys = jnp.concatenate([functools.reduce(lambda p, q: p + q, [ybd[i][h * L:(h + 1) * L] for h in range(HG)])
                          for i in chains], axis=0)
    rkr = jnp.concatenate([r[:, sls[i]] * k[:, sls[i]] * rk_ref[:, cls[i]] for i in chains], axis=0)
    yc = ys - _head_sum(ys, ones) * (1.0 / RWKV_HEAD)
    yn = yc * lax.rsqrt(_head_sum(yc * yc, ones) * (1.0 / RWKV_HEAD) + RWKV_GN_EPS)
    bsum = _head_sum(rkr, ones)
    g_last = []
    for i in chains:
        rows = slice(i * L, (i + 1) * L)
        out = yn[rows] * gng_ref[:, cls[i]] + gnb_ref[:, cls[i]] + bsum[rows] * v[:, sls[i]]
        y_ref[i // NG, :, cls[i]] = out * gate[:, sls[i]]
        g_new = (g[i] + upd[i]) * dec[L - 1:L, sls[i]]
        g_scr[i] = g_new
        g_last.append(g_new)

    @pl.when(c == pl.num_programs(1) - 1)
    def _():
        for grp, g_new in enumerate(g_last):
            gout_ref[grp // NG, grp % NG] = g_new


def _wkv_prompt(seq, gate, geo, prm, li):
    L = WKV_L
    NG = RWKV_HEADS // WKV_HG
    WD = WKV_HG * RWKV_HEAD
    spec = pl.BlockSpec((WKV_NB, L, RWKV_WIDTH), lambda b, c: (b, c, 0))
    vec = pl.BlockSpec((1, RWKV_WIDTH), lambda b, c: (0, 0))
    hid = jnp.arange(WD) // RWKV_HEAD
    ones_bd = (hid[:, None] == hid[None, :]).astype(F32)
    y, g = pl.pallas_call(
        functools.partial(_wkv_chunk_kernel, padf=geo.padf),
        grid=(NSLAB // WKV_NB, geo.tp // L),
        in_specs=[spec] * 7 + [vec] * 3 + [pl.BlockSpec((WD, WD), lambda b, c: (0, 0))],
        out_specs=[spec, pl.BlockSpec((WKV_NB, NG, WD, WD), lambda b, c: (b, 0, 0, 0))],
        out_shape=[jax.ShapeDtypeStruct((NSLAB, geo.slab, RWKV_WIDTH), F32),
                   jax.ShapeDtypeStruct((NSLAB, NG, WD, WD), F32)],
        scratch_shapes=[pltpu.VMEM((WKV_NB * NG, WD, WD), F32)],
        compiler_params=_cp("parallel", "arbitrary"),
        name="wkv_chunk",
    )(*seq, gate, prm['rwkv_gn_g'][li].reshape(1, -1), prm['rwkv_gn_b'][li].reshape(1, -1),
      prm['rwkv_r_k'][li].reshape(1, -1), ones_bd)
    g = g.reshape(NSLAB, NG, WKV_HG, RWKV_HEAD, WKV_HG, RWKV_HEAD)
    idx = jnp.arange(WKV_HG)
    s = g[:, :, idx, :, idx, :]
    s = jnp.transpose(s, (1, 2, 0, 3, 4)).reshape(NSLAB, RWKV_HEADS, RWKV_HEAD, RWKV_HEAD)
    return y, s


def _wkv_step_kernel(r_ref, lw_ref, k_ref, v_ref, a_ref, b_ref, s_ref, y_ref, sout_ref):
    H, D = RWKV_HEADS, RWKV_HEAD
    ri = lax.broadcasted_iota(jnp.int32, (H * D, D), 0) % D
    ci = lax.broadcasted_iota(jnp.int32, (H * D, D), 1)
    eye = jnp.where(ri == ci, 1.0, 0.0)

    def expand(x):
        return jnp.concatenate([jnp.broadcast_to(x[:, h * D:(h + 1) * D], (D, D)) for h in range(H)], axis=0)

    js = range(SUB)
    row = lambda ref, j: ref[j:j + 1, :]
    s = [s_ref[j].reshape(H * D, D) for j in js]
    v_col = [jnp.sum(expand(row(v_ref, j)) * eye, axis=1, keepdims=True) for j in js]
    sa = [jnp.sum(s[j] * expand(row(a_ref, j)), axis=1, keepdims=True) for j in js]
    s_new = [s[j] * jnp.exp(expand(row(lw_ref, j))) + sa[j] * expand(row(b_ref, j)) + v_col[j] * expand(row(k_ref, j))
             for j in js]
    y_col = [jnp.sum(s_new[j] * expand(row(r_ref, j)), axis=1, keepdims=True) for j in js]
    for j in js:
        ye = y_col[j] * eye
        y_row = jnp.concatenate([jnp.sum(ye[h * D:(h + 1) * D], axis=0, keepdims=True) for h in range(H)], axis=1)
        y_ref[j:j + 1, :] = y_row
        sout_ref[j] = s_new[j].reshape(H, D, D)


def _wkv_sample(seq, geo, s0, y_prev):
    sb = geo.sample_blk(SUB)
    spec = pl.BlockSpec((None, SUB, RWKV_WIDTH), lambda k: sb(k) + (0,))
    sspec = pl.BlockSpec((SUB, RWKV_HEADS, RWKV_HEAD, RWKV_HEAD), lambda k: (k, 0, 0, 0))
    return pl.pallas_call(
        functools.partial(_drop_arg, _wkv_step_kernel, 7),
        grid=(geo.ns // SUB,),
        in_specs=[spec] * 6 + [sspec, pl.BlockSpec(memory_space=pl.ANY)],
        out_specs=[spec, sspec],
        out_shape=[jax.ShapeDtypeStruct(y_prev.shape, F32), jax.ShapeDtypeStruct(s0.shape, F32)],
        input_output_aliases={7: 0},
        compiler_params=_cp("parallel"),
        name="wkv_step",
    )(*seq, s0, y_prev)


def _rwkv_post_kernel(y_ref, r_ref, k_ref, v_ref, g_ref, gng_ref, gnb_ref, rk_ref, ones_ref, o_ref):
    y = y_ref[...]
    ones = ones_ref[...]
    mu = _head_sum(y, ones) * (1.0 / RWKV_HEAD)
    yc = y - mu
    var = _head_sum(yc * yc, ones) * (1.0 / RWKV_HEAD)
    yn = yc * lax.rsqrt(var + RWKV_GN_EPS) * gng_ref[...] + gnb_ref[...]
    v = v_ref[...]
    bonus = _head_sum(r_ref[...] * k_ref[...] * rk_ref[...], ones) * v
    o_ref[...] = (yn + bonus) * g_ref[...]


def _rwkv_post_sample(y, r, k, v, g, geo, prm, li):
    W = RWKV_WIDTH
    hid = jnp.arange(W) // RWKV_HEAD
    ones_bd = (hid[:, None] == hid[None, :]).astype(F32)
    sb = geo.sample_blk(geo.sps)
    row = pl.BlockSpec((None, geo.sps, W), lambda i: sb(i) + (0,))
    vec = pl.BlockSpec((1, W), lambda i: (0, 0))
    return pl.pallas_call(
        _rwkv_post_kernel,
        grid=(NSLAB,),
        in_specs=[row] * 5 + [vec] * 3 + [pl.BlockSpec((W, W), lambda i: (0, 0))],
        out_specs=row,
        out_shape=jax.ShapeDtypeStruct(y.shape, F32),
        input_output_aliases={0: 0},
        compiler_params=_cp("parallel"),
        name="rwkv_post",
    )(y, r, k, v, g, prm['rwkv_gn_g'][li].reshape(1, -1), prm['rwkv_gn_b'][li].reshape(1, -1),
      prm['rwkv_r_k'][li].reshape(1, -1), ones_bd)


def _mla_prep_kernel(ckv_ref, qa_ref, kr_ref, cos_ref, sin_ref, qn_ref, kvn_ref, wqb_ref, wuk_ref,
                     qlat_ref, qrope_ref, c_ref, kr8_ref):
    qa = qa_ref[:, :Q_LORA]
    qa = qa * lax.rsqrt(jnp.mean(qa * qa, -1, keepdims=True) + 1e-6) * qn_ref[...]
    q = _bdot(qa, wqb_ref[...])
    nope_w = MLA_HEADS * QK_NOPE
    cos, sin = cos_ref[...], sin_ref[...]
    qrope_ref[...] = _rope(q[:, nope_w:], cos, sin, QK_ROPE // 2)
    hp = LANES // QK_NOPE
    for k in range(MLA_HEADS // hp):
        qlat_ref[:, k * hp * KV_LORA:(k + 1) * hp * KV_LORA] = _bdot(q[:, k * LANES:(k + 1) * LANES], wuk_ref[k])
    ckv = ckv_ref[...]
    c_ref[...] = ckv * lax.rsqrt(jnp.mean(ckv * ckv, -1, keepdims=True) + 1e-6) * kvn_ref[...]
    kr8_ref[...] = _rope(kr_ref[...], cos, sin, QK_ROPE // 2)


def _mla_prep(p, geo, pos_slab, prm, li):
    n = p.shape[0]
    tm, tps = geo.tm, geo.tps
    rw = MLA_HEADS * QK_ROPE
    cos, sin = _rope_tables(pos_slab, QK_ROPE, MLA_HEADS)
    wqb = prm['mla_w_qb'][li].reshape(Q_LORA, MLA_HEADS, QK_NOPE + QK_ROPE)
    wqb = jnp.concatenate([wqb[:, :, :QK_NOPE].reshape(Q_LORA, -1), wqb[:, :, QK_NOPE:].reshape(Q_LORA, -1)], 1)
    eye = jnp.eye(MLA_HEADS, dtype=F32)
    wuk = jnp.einsum('chn,hg->hngc', prm['mla_w_uk'][li], eye).reshape(MLA_HEADS * QK_NOPE, MLA_HEADS * KV_LORA)
    hp = LANES // QK_NOPE
    wuk = jnp.stack([wuk[k * LANES:(k + 1) * LANES, k * hp * KV_LORA:(k + 1) * hp * KV_LORA]
                     for k in range(MLA_HEADS // hp)])
    fix = lambda i: (0, 0)
    c0 = RWKV_PROJ // KV_LORA
    q0 = (RWKV_PROJ + KV_LORA) // 512
    k0 = (RWKV_PROJ + KV_LORA + 512) // rw
    tab = pl.BlockSpec((tm, rw), lambda i: (i % tps, 0))
    return pl.pallas_call(
        _mla_prep_kernel,
        grid=(n // tm,),
        in_specs=[pl.BlockSpec((tm, KV_LORA), lambda i: (i, c0)), pl.BlockSpec((tm, 512), lambda i: (i, q0)),
                  pl.BlockSpec((tm, rw), lambda i: (i, k0)), tab, tab,
                  pl.BlockSpec((1, Q_LORA), fix), pl.BlockSpec((1, KV_LORA), fix),
                  pl.BlockSpec(wqb.shape, fix), pl.BlockSpec(wuk.shape, lambda i: (0, 0, 0))],
        out_specs=[pl.BlockSpec((tm, MLA_HEADS * KV_LORA), lambda i: (i, 0)), pl.BlockSpec((tm, rw), lambda i: (i, 0)),
                   pl.BlockSpec((tm, KV_LORA), lambda i: (i, 0)), pl.BlockSpec((tm, rw), lambda i: (i, 0))],
        out_shape=[jax.ShapeDtypeStruct((n, MLA_HEADS * KV_LORA), F32), jax.ShapeDtypeStruct((n, rw), F32),
                   jax.ShapeDtypeStruct((n, KV_LORA), F32), jax.ShapeDtypeStruct((n, rw), F32)],
        compiler_params=_cp("parallel"),
        name="mla_prep",
    )(p, p, p, cos, sin, prm['mla_q_norm'][li].reshape(1, -1), prm['mla_kv_norm'][li].reshape(1, -1),
      wqb.astype(BF16), wuk.astype(BF16))


def _wuv_blockdiag(w_uv):
    eye = jnp.eye(MLA_HEADS, dtype=F32)
    return jnp.einsum('chv,hg->hcgv', w_uv, eye).reshape(MLA_HEADS * KV_LORA, MLA_WIDTH).astype(BF16)


def _mla_attn_kernel(qi_ref, ki_ref, qlat_ref, qrope_ref, c_ref, kr_ref, wuv_ref, o_ref, qs_scr, qrs_scr, m_scr,
                     l_scr, acc_scr, *, lq, padf):
    qi, ki = qi_ref[pl.program_id(1)], ki_ref[pl.program_id(1)]
    H = MLA_HEADS

    @pl.when(ki == 0)
    def _():
        ql = qlat_ref[...]
        qs_scr[...] = jnp.concatenate([ql[:, h * KV_LORA:(h + 1) * KV_LORA] for h in range(H)], axis=0).astype(BF16)
        qr = qrope_ref[...].astype(BF16)
        lane = lax.broadcasted_iota(jnp.int32, qr.shape, 1) // QK_ROPE
        qrs_scr[...] = jnp.concatenate([jnp.where(lane == h, qr, jnp.zeros_like(qr)) for h in range(H)], axis=0)
        m_scr[...] = jnp.full_like(m_scr, NEG)
        l_scr[...] = jnp.zeros_like(l_scr)
        acc_scr[...] = jnp.zeros_like(acc_scr)

    def tile(kind):
        c2 = MLA_SCALE * math.log2(math.e)
        cb = c_ref[...].astype(BF16)
        st = _dg(cb, qs_scr[...], NT) + _dg(kr_ref[...].astype(BF16), qrs_scr[...], NT)
        if kind == "diagonal":
            kpos = ki * lq + lax.broadcasted_iota(jnp.int32, st.shape, 0)
            qpos = qi * lq + lax.broadcasted_iota(jnp.int32, st.shape, 1) % lq
            st = jnp.where(jnp.logical_and(kpos <= qpos, kpos >= padf), st, NEG)
        elif kind == "first":
            st = jnp.where(lax.broadcasted_iota(jnp.int32, (lq, 1), 0) >= padf, st, NEG)
        m_old = m_scr[...]
        m_new = jnp.maximum(m_old, jnp.max(st, axis=0, keepdims=True))
        alpha = jnp.exp2((m_old - m_new) * c2)
        pt = jnp.exp2((st - m_new) * c2)
        l_scr[...] = alpha * l_scr[...] + jnp.sum(pt, axis=0, keepdims=True)
        acc_scr[...] = alpha * acc_scr[...] + _dg(cb, pt.astype(BF16), TN)
        m_scr[...] = m_new

    assert padf <= lq
    pl.when(ki == qi)(functools.partial(tile, "diagonal"))
    pl.when(jnp.logical_and(ki < qi, ki == 0))(functools.partial(tile, "first"))
    pl.when(jnp.logical_and(ki < qi, ki > 0))(functools.partial(tile, "plain"))

    @pl.when(ki == qi)
    def _():
        ot = (acc_scr[...] * (1.0 / l_scr[...])).astype(BF16)
        y = jnp.zeros(o_ref.shape, F32)
        for h in range(H):
            y = y + _dg(ot[:, h * lq:(h + 1) * lq], wuv_ref[h], TN)
        o_ref[...] = y


def _mla_attn_prompt(qlat3, qrope3, c3, kr3, w_uv, geo):
    lq = _div_tile(geo.tp, SEQ_TILE)
    nq = geo.tp // lq
    wuv = _wuv_blockdiag(w_uv).reshape(MLA_HEADS, KV_LORA, MLA_WIDTH)
    rw = MLA_HEADS * QK_ROPE
    pairs = [(qi, ki) for qi in range(nq) for ki in range(qi + 1)]
    qis = jnp.asarray([p[0] for p in pairs], jnp.int32)
    kis = jnp.asarray([p[1] for p in pairs], jnp.int32)
    qmap = lambda b, p, qr, kr: (b, qr[p], 0)
    kmap = lambda b, p, qr, kr: (b, kr[p], 0)
    grid_spec = pltpu.PrefetchScalarGridSpec(
        num_scalar_prefetch=2,
        grid=(NSLAB, len(pairs)),
        in_specs=[pl.BlockSpec((None, lq, MLA_HEADS * KV_LORA), qmap), pl.BlockSpec((None, lq, rw), qmap),
                  pl.BlockSpec((None, lq, KV_LORA), kmap), pl.BlockSpec((None, lq, rw), kmap),
                  pl.BlockSpec(wuv.shape, lambda b, p, qr, kr: (0, 0, 0))],
        out_specs=pl.BlockSpec((None, lq, MLA_WIDTH), qmap),
        scratch_shapes=[pltpu.VMEM((MLA_HEADS * lq, KV_LORA), BF16), pltpu.VMEM((MLA_HEADS * lq, rw), BF16),
                        pltpu.VMEM((1, MLA_HEADS * lq), F32), pltpu.VMEM((1, MLA_HEADS * lq), F32),
                        pltpu.VMEM((KV_LORA, MLA_HEADS * lq), F32)],
    )
    return pl.pallas_call(
        functools.partial(_mla_attn_kernel, lq=lq, padf=geo.padf),
        grid_spec=grid_spec,
        out_shape=jax.ShapeDtypeStruct((NSLAB, geo.slab, MLA_WIDTH), F32),
        compiler_params=_cp("parallel", "arbitrary"),
        name="mla_attn",
    )(qis, kis, qlat3, qrope3, c3, kr3, wuv)


def _mla_decode_kernel(pt_ref, qlat_ref, qrope_ref, cnew_ref, krnew_ref, kv_hbm, kr_hbm, tile_ref, wuv_ref,
                       yprev_ref, o_ref, kv_buf, kr_buf, kb_scr, s_scr, sem, *, li, n_pages, ch):
    del yprev_ref
    b = pl.program_id(0)
    slot = b % 2
    H = MLA_HEADS

    def page_copies(page, sl, pg):
        return (pltpu.make_async_copy(kv_hbm.at[page, li], kv_buf.at[sl, pg], sem.at[0, sl]),
                pltpu.make_async_copy(kr_hbm.at[page, li], kr_buf.at[sl, pg], sem.at[1, sl]))

    def start_fetch(bb, sl):
        def body(pg, carry):
            for cp in page_copies(pt_ref[bb, pg], sl, pg):
                cp.start()
            return carry
        lax.fori_loop(0, n_pages, body, 0)

    def wait_fetch(sl):
        for pg in range(n_pages):
            for cp in page_copies(0, sl, pg):
                cp.wait()

    @pl.when(b == 0)
    def _():
        start_fetch(0, 0)

    nxt = (b + 1) % pl.num_programs(0)

    j = b % SUB
    rid = lax.broadcasted_iota(jnp.int32, (SUB, 1), 0)

    def pick(ref):
        return jnp.sum(jnp.where(rid == j, ref[...], 0.0), axis=0, keepdims=True)

    ql = pick(qlat_ref)
    qs = jnp.concatenate([ql[:, h * KV_LORA:(h + 1) * KV_LORA] for h in range(H)], axis=0).astype(BF16)
    qr = jnp.broadcast_to(pick(qrope_ref), (H, H * QK_ROPE))
    hl = lax.broadcasted_iota(jnp.int32, qr.shape, 1) // QK_ROPE
    hr = lax.broadcasted_iota(jnp.int32, qr.shape, 0)
    qrs = jnp.where(hl == hr, qr, 0.0).astype(BF16)
    qr32 = _dg(qrs, tile_ref[...], NT).astype(BF16)
    cn = _rb(pick(cnew_ref))
    kn = _rb(pick(krnew_ref))
    s_self = (jnp.sum(qs.astype(F32) * cn, axis=1, keepdims=True)
              + jnp.sum(qrs.astype(F32) * kn, axis=1, keepdims=True)) * MLA_SCALE

    wait_fetch(slot)
    cw = ch * PAGE_SIZE
    for ci in range(n_pages // ch):
        for pg in range(ci * ch, (ci + 1) * ch):
            for cp in page_copies(pt_ref[nxt, pg], 1 - slot, pg):
                cp.start()
        kc = kv_buf[slot, ci * ch:(ci + 1) * ch].reshape(cw, KV_LORA).astype(BF16)
        kb_scr[ci * cw:(ci + 1) * cw, :] = kc
        krc = jnp.concatenate([kr_buf[slot, pg] for pg in range(ci * ch, (ci + 1) * ch)], axis=1).astype(BF16)
        s_scr[:, ci * cw:(ci + 1) * cw] = (_dg(qs, kc, NT) + _dg(qr32, krc, NN)) * MLA_SCALE
    s = s_scr[...]
    m = jnp.maximum(jnp.max(s, axis=1, keepdims=True), s_self)
    e = jnp.exp(s - m)
    e_self = jnp.exp(s_self - m)
    rl = 1.0 / (jnp.sum(e, axis=1, keepdims=True) + e_self)
    p = (e * rl).astype(BF16)
    parts = [_dg(p[:, ci * cw:(ci + 1) * cw], kb_scr[ci * cw:(ci + 1) * cw, :], NN) for ci in range(n_pages // ch)]
    while len(parts) > 1:
        parts = [parts[i] + parts[i + 1] for i in range(0, len(parts) - 1, 2)] + parts[len(parts) & ~1:]
    acc = _rb(e_self * rl) * cn + parts[0]
    ol = jnp.concatenate([acc[h:h + 1] for h in range(H)], axis=1)
    y = _bdot(jnp.broadcast_to(ol, (SUB, ol.shape[1])), wuv_ref[...])
    o_ref[pl.ds(j, 1), :] = y[0:1]

    @pl.when(b == pl.num_programs(0) - 1)
    def _():
        wait_fetch(1 - slot)


def _mla_decode(qlat, qrope, c, kr8, cache_kv, cache_kr, page_table, w_uv, li, y_prev, geo):
    n = qlat.shape[0]
    n_pages = page_table.shape[1]
    ch = _div_tile(n_pages, 4, 1)
    rw = MLA_HEADS * QK_ROPE
    wuv = _wuv_blockdiag(w_uv)
    tile = jnp.tile(jnp.eye(QK_ROPE, dtype=F32), (1, MLA_HEADS)).astype(BF16)

    def rowmap(b, pt):
        return ((b // geo.sps) * geo.slab + geo.tp + b % geo.sps) // SUB, 0

    rowblk = lambda w: pl.BlockSpec((SUB, w), rowmap)
    grid_spec = pltpu.PrefetchScalarGridSpec(
        num_scalar_prefetch=1,
        grid=(geo.ns,),
        in_specs=[rowblk(MLA_HEADS * KV_LORA), rowblk(rw), rowblk(KV_LORA), rowblk(rw),
                  pl.BlockSpec(memory_space=pl.ANY), pl.BlockSpec(memory_space=pl.ANY),
                  pl.BlockSpec(tile.shape, lambda b, pt: (0, 0)),
                  pl.BlockSpec(wuv.shape, lambda b, pt: (0, 0)),
                  pl.BlockSpec(memory_space=pl.ANY)],
        out_specs=rowblk(MLA_WIDTH),
        scratch_shapes=[pltpu.VMEM((2, n_pages, PAGE_SIZE, KV_LORA), F32),
                        pltpu.VMEM((2, n_pages, QK_ROPE, PAGE_SIZE), F32),
                        pltpu.VMEM((n_pages * PAGE_SIZE, KV_LORA), BF16),
                        pltpu.VMEM((MLA_HEADS, n_pages * PAGE_SIZE), F32),
                        pltpu.SemaphoreType.DMA((2, 2))],
    )
    return pl.pallas_call(
        functools.partial(_mla_decode_kernel, li=li, n_pages=n_pages, ch=ch),
        grid_spec=grid_spec,
        out_shape=jax.ShapeDtypeStruct((n, MLA_WIDTH), F32),
        input_output_aliases={9: 0},
        compiler_params=_cp("arbitrary"),
        name="mla_decode",
    )(page_table, qlat, qrope, c, kr8, cache_kv, jnp.swapaxes(cache_kr, 2, 3), tile, wuv, y_prev)


def _forward(x_prompt, x_sample, state_ret, state_s5_re, state_s5_im, state_wkv, state_shift, cache_kv,
             cache_krope, page_table, meta_tokens, prm):
    nb, seq, _ = x_prompt.shape
    assert nb == NSLAB and x_sample.shape[1] == 1
    ns = x_sample.shape[0]
    geo = _Geom(seq, ns, page_table.shape[1])
    n, tm, tp, padf, sps = geo.n, geo.tm, geo.tp, geo.padf, geo.sps

    meta = jnp.broadcast_to(meta_tokens[None], (nb, N_META, D_MODEL))
    x = jnp.concatenate([jnp.zeros((nb, padf, D_MODEL), F32), meta, x_prompt, x_sample.reshape(nb, sps, D_MODEL)], 1)
    x = x.reshape(n, D_MODEL)
    pos_slab = jnp.concatenate([jnp.arange(tp, dtype=jnp.int32) - padf, jnp.full((sps,), geo.past, jnp.int32)])

    ret_p, ret_s, s5_p, s5_s, wkv_p, wkv_s, sh_p, sh_s, kv_l, kr_l = [], [], [], [], [], [], [], [], [], []
    for layer in range(DEPTH):
        li = layer // 2
        if layer % 2 == 0:
            w_in = prm['ev_w_in'][li].astype(BF16)
            p = _matmul(x, w_in, tm, _div_tile(w_in.shape[1], 1024, 128))
            p3 = geo.v3(p)
            gng, gnb = prm['ret_gn_g'][li], prm['ret_gn_b'][li]
            ret, s_p = _retention_prompt(p3, geo, gng, gnb)
            ret, s_s = _retention_sample(p3, geo, state_ret[li], gng, gnb, ret)
            s5o, h_p = _s5_prompt(p3, geo, prm, li)
            h0 = jnp.concatenate([state_s5_re[li].reshape(ns, S5_CH), state_s5_im[li].reshape(ns, S5_CH)], 1)
            s5o, h_s = _s5_sample(p3, geo, h0, prm, li, s5o)
            ret_p.append(s_p)
            ret_s.append(s_s)
            s5_p.append(h_p)
            s5_s.append(h_s)
            x = _proj_ln(ret.reshape(n, -1), s5o.reshape(n, -1), x, prm['ev_w_out'][li], prm['ln1_g'][layer],
                         prm['ln1_b'][layer], tm)
        else:
            w = prm['od_w_in'][li]
            o1 = RWKV_PROJ
            o2 = o1 + Q_LORA
            o3 = o2 + KV_LORA
            w_in = jnp.concatenate([w[:, :o1], w[:, o2:o3], w[:, o1:o2], jnp.zeros((D_MODEL, 512 - Q_LORA), F32),
                                    jnp.tile(w[:, o3:], (1, MLA_HEADS))], 1).astype(BF16)
            p = _matmul(x, w_in, tm, _div_tile(w_in.shape[1], 1408, 128))
            p3 = geo.v3(p)
            sh_p.append(p3[:, tp - 1, :RWKV_PROJ])
            sh_s.append(p3[:, tp:, :RWKV_PROJ].reshape(ns, RWKV_PROJ))
            r, lw, k, v, a, b, g = _rwkv_prep(p3, geo, state_shift[li], prm, li)
            seqs = (r, lw, k, v, a, b)
            y, g_p = _wkv_prompt(seqs, g, geo, prm, li)
            y, g_s = _wkv_sample(seqs, geo, state_wkv[li], y)
            wkv_p.append(g_p)
            wkv_s.append(g_s)
            flat = lambda t: t.reshape(n, t.shape[-1])
            y_c = flat(_rwkv_post_sample(y, r, k, v, g, geo, prm, li))
            qlat, qrope, c, kr8 = _mla_prep(p, geo, pos_slab, prm, li)
            kv_l.append(geo.v3(c))
            kr_l.append(geo.v3(kr8)[:, :, :QK_ROPE])
            y_d = _mla_attn_prompt(geo.v3(qlat), geo.v3(qrope), geo.v3(c), geo.v3(kr8), prm['mla_w_uv'][li], geo)
            y_d = _mla_decode(qlat, qrope, c, kr8, cache_kv, cache_krope, page_table, prm['mla_w_uv'][li], li,
                              flat(y_d), geo)
            x = _proj_ln(y_c, y_d, x, prm['od_w_out'][li], prm['ln1_g'][layer], prm['ln1_b'][layer], tm)
        x = _moe_ln(x, prm, layer, tm)

    x3 = geo.v3(x)
    y_p = x3[:, padf + N_META:tp]
    y_s = x3[:, tp:].reshape(ns, 1, D_MODEL)

    def s5_state(hs, part):
        return jnp.stack([h[:, part * S5_CH:(part + 1) * S5_CH].reshape(-1, S5_GROUPS, S5_STATE) for h in hs])

    kv_p = jnp.stack([c[:, padf:tp] for c in kv_l], 1)
    kv_s = jnp.stack([c[:, tp:].reshape(ns, 1, KV_LORA) for c in kv_l], 1)
    kr_p = jnp.stack([c[:, padf:tp] for c in kr_l], 1)
    kr_s = jnp.stack([c[:, tp:].reshape(ns, 1, QK_ROPE) for c in kr_l], 1)
    return (y_p, y_s, jnp.stack(ret_p), jnp.stack(ret_s), s5_state(s5_p, 0), s5_state(s5_s, 0),
            s5_state(s5_p, 1), s5_state(s5_s, 1), jnp.stack(wkv_p), jnp.stack(wkv_s),
            jnp.stack(sh_p), jnp.stack(sh_s), kv_p, kv_s, kr_p, kr_s)


def kernel(x_prompt, x_sample, state_ret, state_s5_re, state_s5_im, state_wkv, state_shift, cache_kv, cache_krope, page_table, meta_tokens, ev_w_in, ret_gn_g, ret_gn_b, s5_a_re, s5_a_im, s5_log_dt, s5_b_re, s5_b_im, s5_c_re, s5_c_im, s5_d, s5_w_glu, s5_b_glu, ev_w_out, od_w_in, rwkv_mu, rwkv_w0, rwkv_w2, rwkv_a0, rwkv_a2, rwkv_g2, rwkv_k_k, rwkv_k_a, rwkv_r_k, rwkv_gn_g, rwkv_gn_b, mla_q_norm, mla_w_qb, mla_kv_norm, mla_w_uk, mla_w_uv, od_w_out, ln1_g, ln1_b, ln2_g, ln2_b, moe_w_coarse, moe_b_coarse, moe_w_fine, moe_b_fine, moe_w_gate, moe_w_up, moe_w_down):
    prm = dict(ev_w_in=ev_w_in, ret_gn_g=ret_gn_g, ret_gn_b=ret_gn_b, s5_a_re=s5_a_re, s5_a_im=s5_a_im,
               s5_log_dt=s5_log_dt, s5_b_re=s5_b_re, s5_b_im=s5_b_im, s5_c_re=s5_c_re, s5_c_im=s5_c_im,
               s5_d=s5_d, s5_w_glu=s5_w_glu, s5_b_glu=s5_b_glu, ev_w_out=ev_w_out,
               od_w_in=od_w_in, rwkv_mu=rwkv_mu, rwkv_w0=rwkv_w0, rwkv_w2=rwkv_w2, rwkv_a0=rwkv_a0,
               rwkv_a2=rwkv_a2, rwkv_g2=rwkv_g2, rwkv_k_k=rwkv_k_k, rwkv_k_a=rwkv_k_a, rwkv_r_k=rwkv_r_k,
               rwkv_gn_g=rwkv_gn_g, rwkv_gn_b=rwkv_gn_b, mla_q_norm=mla_q_norm, mla_w_qb=mla_w_qb,
               mla_kv_norm=mla_kv_norm, mla_w_uk=mla_w_uk, mla_w_uv=mla_w_uv, od_w_out=od_w_out,
               ln1_g=ln1_g, ln1_b=ln1_b, ln2_g=ln2_g, ln2_b=ln2_b, moe_w_coarse=moe_w_coarse,
               moe_b_coarse=moe_b_coarse, moe_w_fine=moe_w_fine, moe_b_fine=moe_b_fine,
               moe_w_gate=moe_w_gate, moe_w_up=moe_w_up, moe_w_down=moe_w_down)
    return _forward(x_prompt, x_sample, state_ret, state_s5_re, state_s5_im, state_wkv, state_shift, cache_kv,
                    cache_krope, page_table, meta_tokens, prm)
```
